```python
import math
import jax, jax.numpy as jnp
from jax import lax
import numpy as np

D_MODEL = 1024
BATCH = 4
SEQ = 8192
DEPTH = 2

HEAD_DIM = 64
BLOCK = 128
A_GROUPS = ((128, 1), (512, 4), (2048, 16))
A_HEADS = 8
A_TOTAL_HEADS = len(A_GROUPS) * A_HEADS
B_Q_HEADS = 8
B_KV_HEADS = 2
B_WINDOW = 128
C_HEADS = 8
C_Q_RANK = 256
C_KV_RANK = 128
C_NOPE = 64
C_ROPE = 32
C_V = 64
ROPE_BASE = 10000.0
REL_BUCKETS = 32
REL_MAX_DIST = 2048
N_REL_HEADS = A_TOTAL_HEADS + B_Q_HEADS
N_BRANCH = 3
BRANCH_WIDTH = 512
D_FF = 2816
CONV_WIDTH = 3
ALPHA = (2 * DEPTH) ** 0.25
BETA = (8 * DEPTH) ** -0.25
LN_EPS = 1e-5
RMS_EPS = 1e-6
NEG = -1e30

A_QKV_COLS = len(A_GROUPS) * 3 * A_HEADS * HEAD_DIM
B_Q_COLS = B_Q_HEADS * HEAD_DIM
B_KV_COLS = 2 * B_KV_HEADS * HEAD_DIM
C_DQ_COLS = C_Q_RANK
C_DKV_COLS = C_KV_RANK + C_ROPE
GATE_COLS = N_BRANCH * D_MODEL
IN_SPLITS = (A_QKV_COLS,
             A_QKV_COLS + B_Q_COLS,
             A_QKV_COLS + B_Q_COLS + B_KV_COLS,
             A_QKV_COLS + B_Q_COLS + B_KV_COLS + C_DQ_COLS,
             A_QKV_COLS + B_Q_COLS + B_KV_COLS + C_DQ_COLS + C_DKV_COLS)
D_IN = IN_SPLITS[-1] + GATE_COLS

kernel_name = "hybrid_dilated_swa_mla_convglu_block"


def layer_norm(x, g, b):
    xf = x.astype(jnp.float32)
    mu = xf.mean(-1, keepdims=True)
    var = jnp.square(xf - mu).mean(-1, keepdims=True)
    y = (xf - mu) * lax.rsqrt(var + LN_EPS) * g.astype(jnp.float32) + b.astype(jnp.float32)
    return y.astype(x.dtype)


def rms_norm(x, g):
    xf = x.astype(jnp.float32)
    y = xf * lax.rsqrt(jnp.mean(xf * xf, -1, keepdims=True) + RMS_EPS) * g.astype(jnp.float32)
    return y.astype(x.dtype)


def t5_bucket(dist):
    n = jnp.maximum(dist, 0)
    max_exact = REL_BUCKETS // 2
    scaled = jnp.log(jnp.maximum(n, 1).astype(jnp.float32) / max_exact) / math.log(REL_MAX_DIST / max_exact)
    large = max_exact + (scaled * (REL_BUCKETS - max_exact)).astype(jnp.int32)
    return jnp.where(n < max_exact, n, jnp.minimum(large, REL_BUCKETS - 1))


def apply_rope(x, cos, sin):
    x1, x2 = jnp.split(x, 2, axis=-1)
    c = cos[:, None, :].astype(x.dtype)
    s = sin[:, None, :].astype(x.dtype)
    return jnp.concatenate([x1 * c - x2 * s, x1 * s + x2 * c], axis=-1)


def dilated_group_attention(q, k, v, bias_table, window, dilation):
    b, s, h, dh = q.shape
    w = window // dilation
    span = w * dilation
    s_pad = -(-s // span) * span
    n_sub = s_pad // dilation
    nb = n_sub // w

    def to_blocks(t):
        t = jnp.pad(t, ((0, 0), (0, s_pad - s), (0, 0), (0, 0)))
        t = t.reshape(b, n_sub, dilation, h, dh).transpose(0, 2, 1, 3, 4)
        return t.reshape(b, dilation, nb, w, h, dh)

    def with_prev(t):
        prev = jnp.pad(t[:, :, :-1], ((0, 0), (0, 0), (1, 0), (0, 0), (0, 0), (0, 0)))
        return jnp.concatenate([prev, t], axis=3)

    qb = to_blocks(q)
    kb = with_prev(to_blocks(k))
    vb = with_prev(to_blocks(v))
    logits = jnp.einsum('brnqhd,brnkhd->brnhqk', qb, kb).astype(jnp.float32) * (dh ** -0.5)
    qi = jnp.arange(w)[:, None]
    ki = jnp.arange(2 * w)[None, :]
    step = w + qi - ki
    band = (step >= 0) & (step <= w)
    valid = band[None] & ((jnp.arange(nb)[:, None, None] > 0) | (ki >= w)[None])
    bias = bias_table[t5_bucket(step * dilation)].astype(jnp.float32).transpose(2, 0, 1)
    logits = jnp.where(valid[None, None, :, None], logits + bias, NEG)
    m = logits.max(-1, keepdims=True)
    p = jnp.exp(logits - m)
    l = p.sum(-1, keepdims=True)
    o = jnp.einsum('brnhqk,brnkhd->brnqhd', (p / l).astype(v.dtype), vb)
    lse = (m + jnp.log(l))[..., 0]
    o = o.reshape(b, dilation, n_sub, h, dh).transpose(0, 2, 1, 3, 4).reshape(b, s_pad, h, dh)[:, :s]
    lse = lse.transpose(0, 1, 2, 4, 3).reshape(b, dilation, n_sub, h).transpose(0, 2, 1, 3)
    lse = lse.reshape(b, s_pad, h)[:, :s]
    return o, lse


def sliding_window_sink_attention(q, k, v, sinks, bias_table):
    b, s, hq, dh = q.shape
    hkv = k.shape[2]
    g = hq // hkv
    nb = s // BLOCK
    qb = q.reshape(b, nb, BLOCK, hkv, g, dh)

    def with_prev(t):
        t = t.reshape(b, nb, BLOCK, hkv, dh)
        prev = jnp.pad(t[:, :-1], ((0, 0), (1, 0), (0, 0), (0, 0), (0, 0)))
        return jnp.concatenate([prev, t], axis=2)

    kb = with_prev(k)
    vb = with_prev(v)
    logits = jnp.einsum('bnqhgd,bnchd->bnhgqc', qb, kb).astype(jnp.float32) * (dh ** -0.5)
    qi = jnp.arange(BLOCK)[:, None]
    ci = jnp.arange(2 * BLOCK)[None, :]
    dist = BLOCK + qi - ci
    valid = ((dist >= 0) & (dist < B_WINDOW))[None] & ((jnp.arange(nb)[:, None, None] > 0) | (ci >= BLOCK)[None])
    bias = bias_table[t5_bucket(dist)].astype(jnp.float32).transpose(2, 0, 1).reshape(hkv, g, BLOCK, 2 * BLOCK)
    logits = jnp.where(valid[None, :, None, None], logits + bias, NEG)
    sink = sinks.astype(jnp.float32).reshape(1, 1, hkv, g, 1, 1)
    m = jnp.maximum(logits.max(-1, keepdims=True), sink)
    p = jnp.exp(logits - m)
    denom = p.sum(-1, keepdims=True) + jnp.exp(sink - m)
    o = jnp.einsum('bnhgqc,bnchd->bnqhgd', (p / denom).astype(v.dtype), vb)
    return o.reshape(b, s, hq * dh)


def mla_attention(cq, c_kv, k_rope, q_norm_g, kv_norm_g, w_uq, w_ukv, cos, sin):
    b, s, _ = cq.shape
    q = (rms_norm(cq, q_norm_g) @ w_uq).reshape(b, s, C_HEADS, C_NOPE + C_ROPE)
    q_nope = q[..., :C_NOPE]
    q_rope = apply_rope(q[..., C_NOPE:], cos, sin)
    kv = (rms_norm(c_kv, kv_norm_g) @ w_ukv).reshape(b, s, C_HEADS, C_NOPE + C_V)
    k_nope = kv[..., :C_NOPE]
    v = kv[..., C_NOPE:]
    k_r = apply_rope(k_rope[:, :, None, :], cos, sin)[:, :, 0]
    nb = s // BLOCK
    qn_b = q_nope.reshape(b, nb, BLOCK, C_HEADS, C_NOPE).swapaxes(0, 1)
    qr_b = q_rope.reshape(b, nb, BLOCK, C_HEADS, C_ROPE).swapaxes(0, 1)
    kpos = jnp.arange(s)
    scale = (C_NOPE + C_ROPE) ** -0.5

    def one_block(args):
        i, qn, qr = args
        logits = (jnp.einsum('bqhd,bkhd->bhqk', qn, k_nope)
                  + jnp.einsum('bqhr,bkr->bhqk', qr, k_r)).astype(jnp.float32) * scale
        qpos = i * BLOCK + jnp.arange(BLOCK)
        logits = jnp.where(kpos[None, :] <= qpos[:, None], logits, NEG)
        p = jax.nn.softmax(logits, axis=-1).astype(v.dtype)
        return jnp.einsum('bhqk,bkhd->bqhd', p, v)

    o = lax.map(one_block, (jnp.arange(nb), qn_b, qr_b))
    return o.swapaxes(0, 1).reshape(b, s, C_HEADS * C_V)


def hybrid_mixer(x, w_in, b_gate, sinks, q_norm_g, kv_norm_g, w_uq, w_ukv, w_branch, w_out,
                 rel_table, cos, sin):
    b, s, _ = x.shape
    proj = x @ w_in
    a_qkv, b_q, b_kv, c_q, c_dkv, gate_pre = jnp.split(proj, IN_SPLITS, axis=-1)

    a_qkv = a_qkv.reshape(b, s, len(A_GROUPS), 3, A_HEADS, HEAD_DIM)
    outs, lses = [], []
    for gi, (window, dil) in enumerate(A_GROUPS):
        o_g, lse_g = dilated_group_attention(a_qkv[:, :, gi, 0], a_qkv[:, :, gi, 1], a_qkv[:, :, gi, 2],
                                             rel_table[:, gi * A_HEADS:(gi + 1) * A_HEADS], window, dil)
        outs.append(o_g)
        lses.append(lse_g)
    wts = jax.nn.softmax(jnp.stack(lses, axis=0), axis=0)
    o_a = jnp.einsum('gbsh,gbshd->bshd', wts, jnp.stack(outs, axis=0).astype(jnp.float32))
    o_a = o_a.astype(x.dtype).reshape(b, s, A_HEADS * HEAD_DIM)

    b_k, b_v = jnp.split(b_kv, 2, axis=-1)
    o_b = sliding_window_sink_attention(b_q.reshape(b, s, B_Q_HEADS, HEAD_DIM),
                                        b_k.reshape(b, s, B_KV_HEADS, HEAD_DIM),
                                        b_v.reshape(b, s, B_KV_HEADS, HEAD_DIM),
                                        sinks, rel_table[:, A_TOTAL_HEADS:])

    c_kv, k_rope = jnp.split(c_dkv, [C_KV_RANK], axis=-1)
    o_c = mla_attention(c_q, c_kv, k_rope, q_norm_g, kv_norm_g, w_uq, w_ukv, cos, sin)

    gates = jax.nn.sigmoid((gate_pre + b_gate).astype(jnp.float32)).astype(x.dtype)
    gates = gates.reshape(b, s, N_BRANCH, D_MODEL)
    merged = (gates[:, :, 0] * (o_a @ w_branch[0])
              + gates[:, :, 1] * (o_b @ w_branch[1])
              + gates[:, :, 2] * (o_c @ w_branch[2]))
    return merged @ w_out


def conv_glu_ffn(x, w_up, conv_w, conv_b, w_down):
    u = x @ w_up
    c = u.shape[-1]
    u = lax.conv_general_dilated(u, conv_w[:, None, :].astype(u.dtype), window_strides=(1,),
                                 padding=[(CONV_WIDTH - 1, 0)],
                                 dimension_numbers=('NWC', 'WIO', 'NWC'),
                                 feature_group_count=c) + conv_b
    gate, val = jnp.split(u, 2, axis=-1)
    return (jax.nn.silu(gate) * val) @ w_down


def setup_inputs(seed: int = 0) -> dict:
    key = jax.random.key(seed)
    ks = jax.random.split(key, 20)
    f32 = jnp.float32
    nrm = lambda k, shape, scale: jax.random.normal(k, shape, f32) * scale
    return {
        'x': jax.random.normal(ks[0], (BATCH, SEQ, D_MODEL), f32),
        'rel_table': nrm(ks[1], (REL_BUCKETS, N_REL_HEADS), 0.2),
        'w_in': nrm(ks[2], (DEPTH, D_MODEL, D_IN), D_MODEL ** -0.5),
        'b_gate': nrm(ks[3], (DEPTH, GATE_COLS), 0.01),
        'sinks': nrm(ks[4], (DEPTH, B_Q_HEADS), 0.5),
        'q_norm_g': 1.0 + nrm(ks[5], (DEPTH, C_Q_RANK), 0.05),
        'kv_norm_g': 1.0 + nrm(ks[6], (DEPTH, C_KV_RANK), 0.05),
        'w_uq': nrm(ks[7], (DEPTH, C_Q_RANK, C_HEADS * (C_NOPE + C_ROPE)), C_Q_RANK ** -0.5),
        'w_ukv': nrm(ks[8], (DEPTH, C_KV_RANK, C_HEADS * (C_NOPE + C_V)), C_KV_RANK ** -0.5),
        'w_branch': nrm(ks[9], (DEPTH, N_BRANCH, BRANCH_WIDTH, D_MODEL), BRANCH_WIDTH ** -0.5 * BETA),
        'w_out': nrm(ks[10], (DEPTH, D_MODEL, D_MODEL), D_MODEL ** -0.5 * BETA),
        'ln1_g': 1.0 + nrm(ks[11], (DEPTH, D_MODEL), 0.05),
        'ln1_b': nrm(ks[12], (DEPTH, D_MODEL), 0.01),
        'w_ffn_up': nrm(ks[13], (DEPTH, D_MODEL, 2 * D_FF), D_MODEL ** -0.5),
        'conv_w': nrm(ks[14], (DEPTH, CONV_WIDTH, 2 * D_FF), CONV_WIDTH ** -0.5),
        'conv_b': nrm(ks[15], (DEPTH, 2 * D_FF), 0.01),
        'w_ffn_down': nrm(ks[16], (DEPTH, D_FF, D_MODEL), D_FF ** -0.5 * BETA),
        'ln2_g': 1.0 + nrm(ks[17], (DEPTH, D_MODEL), 0.05),
        'ln2_b': nrm(ks[18], (DEPTH, D_MODEL), 0.01),
    }


def reference(x, rel_table, w_in, b_gate, sinks, q_norm_g, kv_norm_g, w_uq, w_ukv, w_branch, w_out,
              ln1_g, ln1_b, w_ffn_up, conv_w, conv_b, w_ffn_down, ln2_g, ln2_b):
    s = x.shape[1]
    pos = jnp.arange(s, dtype=jnp.float32)
    inv_freq = ROPE_BASE ** (-jnp.arange(0, C_ROPE, 2, dtype=jnp.float32) / C_ROPE)
    ang = pos[:, None] * inv_freq[None, :]
    cos, sin = jnp.cos(ang), jnp.sin(ang)
    for l in range(DEPTH):
        mix = hybrid_mixer(x, w_in[l], b_gate[l], sinks[l], q_norm_g[l], kv_norm_g[l], w_uq[l], w_ukv[l],
                           w_branch[l], w_out[l], rel_table, cos, sin)
        x = layer_norm(ALPHA * x + mix, ln1_g[l], ln1_b[l])
        ff = conv_glu_ffn(x, w_ffn_up[l], conv_w[l], conv_b[l], w_ffn_down[l])
        x = layer_norm(ALPHA * x + ff, ln2_g[l], ln2_b[l])
    return x
```

```python
import functools
import math

import jax
import jax.numpy as jnp
import numpy as np
from jax import lax
from jax.experimental import pallas as pl
from jax.experimental.pallas import tpu as pltpu

F32 = jnp.float32
BF16 = jnp.bfloat16

D_MODEL = 1024
HEAD_DIM = 64
BLOCK = 128
A_GROUPS = ((128, 1), (512, 4), (2048, 16))
A_HEADS = 8
B_Q_HEADS = 8
B_WINDOW = 128
C_HEADS = 8
C_Q_RANK = 256
C_KV_RANK = 128
C_NOPE = 64
C_ROPE = 32
C_V = 64
ROPE_BASE = 10000.0
REL_BUCKETS = 32
REL_MAX_DIST = 2048
D_FF = 2816
LN_EPS = 1e-5
RMS_EPS = 1e-6
NEG = -1e30

COL_GATE = 0
COL_A = 3072
COL_BQ = 7680
COL_C = 8192
COL_BKV = 8704
NP = 9216

LANE = 128
VMEM_LIMIT = 56 * 1024 * 1024

N_KINDS = 4


def _cparams(n_axes):
    return pltpu.CompilerParams(dimension_semantics=("arbitrary",) * n_axes,
                                vmem_limit_bytes=VMEM_LIMIT)


def _proj_kernel(x_ref, w_ref, o_ref, xb_ref):
    @pl.when(pl.program_id(1) == 0)
    def _():
        xb_ref[...] = x_ref[...].astype(BF16)

    o_ref[...] = jnp.dot(xb_ref[...], w_ref[...], preferred_element_type=F32).astype(o_ref.dtype)


def _proj(x2d, w, tm=1024, tn=1536):
    t, k = x2d.shape
    n = w.shape[1]
    return pl.pallas_call(
        _proj_kernel,
        grid=(t // tm, n // tn),
        in_specs=[pl.BlockSpec((tm, k), lambda i, j: (i, 0)),
                  pl.BlockSpec((k, tn), lambda i, j: (0, j))],
        out_specs=pl.BlockSpec((tm, tn), lambda i, j: (i, j)),
        out_shape=jax.ShapeDtypeStruct((t, n), BF16),
        scratch_shapes=[pltpu.VMEM((tm, k), BF16)],
        compiler_params=_cparams(2),
        name="proj",
    )(x2d, w)


def _t5_bucket(dist):
    n = jnp.maximum(dist, 0)
    max_exact = REL_BUCKETS // 2
    scaled = jnp.log(jnp.maximum(n, 1).astype(F32) / max_exact) / math.log(REL_MAX_DIST / max_exact)
    large = max_exact + (scaled * (REL_BUCKETS - max_exact)).astype(jnp.int32)
    return jnp.where(n < max_exact, n, jnp.minimum(large, REL_BUCKETS - 1))


def _bias_codes():
    qi = jnp.arange(BLOCK)[:, None]
    ki = jnp.arange(2 * BLOCK)[None, :]
    step = BLOCK + qi - ki
    has_prev = ki >= BLOCK
    codes = []
    for kind in range(N_KINDS):
        if kind < len(A_GROUPS):
            dil = A_GROUPS[kind][1]
            band = (step >= 0) & (step <= BLOCK)
        else:
            dil = 1
            band = (step >= 0) & (step < B_WINDOW)
        bucket = _t5_bucket(step * dil)
        codes.append(jnp.stack([jnp.where(band & has_prev, bucket, -1),
                                jnp.where(band, bucket, -1)]))
    return jnp.stack(codes).astype(jnp.int32)


def _bias_kernel(rel_ref, code_ref, o_ref):
    kind = pl.program_id(0)
    code = code_ref[0, 0]
    for slot in range(A_HEADS):
        b_head = len(A_GROUPS) * A_HEADS + slot // 2 + 4 * (slot % 2)
        col = jnp.where(kind < len(A_GROUPS), kind * A_HEADS + slot, b_head)
        acc = jnp.full(code.shape, NEG, F32)
        for b in range(REL_BUCKETS):
            acc = jnp.where(code == b, rel_ref[b, col], acc)
        o_ref[0, 0, slot] = acc


def _bias_tables(rel_table):
    codes = _bias_codes()
    return pl.pallas_call(
        _bias_kernel,
        grid=(N_KINDS, 2),
        in_specs=[pl.BlockSpec(memory_space=pltpu.SMEM),
                  pl.BlockSpec((1, 1, BLOCK, 2 * BLOCK), lambda k, v: (k, v, 0, 0))],
        out_specs=pl.BlockSpec((1, 1, A_HEADS, BLOCK, 2 * BLOCK), lambda k, v: (k, v, 0, 0, 0)),
        out_shape=jax.ShapeDtypeStruct((N_KINDS, 2, A_HEADS, BLOCK, 2 * BLOCK), F32),
        compiler_params=_cparams(2),
        name="bias_tables",
    )(rel_table, codes)


def _band_kernel(sink_ref, q_ref, kp_ref, kc_ref, vp_ref, vc_ref, bias_ref, o_ref, *lse_refs,
                 shared_kv, with_sink):
    n = pl.program_id(2)
    variant = jnp.where(n == 0, 0, 1)
    lane = lax.broadcasted_iota(jnp.int32, (BLOCK, LANE), 1)
    low = lane < HEAD_DIM
    lse_tile = jnp.zeros((BLOCK, LANE), F32)
    for pair in range(4):
        cols = slice(pair * LANE, (pair + 1) * LANE)
        kv_cols = slice(0, LANE) if shared_kv else cols
        qp = q_ref[0, :, cols] * jnp.asarray(HEAD_DIM ** -0.5, BF16)
        kcat = jnp.concatenate([kp_ref[0, :, kv_cols], kc_ref[0, :, kv_cols]], axis=0)
        vcat = jnp.concatenate([vp_ref[0, :, kv_cols], vc_ref[0, :, kv_cols]], axis=0)
        halves = []
        for sub in range(2):
            slot = 2 * pair + sub
            qh = jnp.where(low if sub == 0 else jnp.logical_not(low), qp, jnp.zeros_like(qp))
            s = lax.dot_general(qh, kcat, (((1,), (1,)), ((), ())), preferred_element_type=F32)
            s = s + bias_ref[0, variant, slot]
            m = jnp.max(s, axis=-1, keepdims=True)
            if with_sink:
                sink = sink_ref[slot]
                m = jnp.maximum(m, sink)
            p = jnp.exp(s - m)
            l = jnp.sum(p, axis=-1, keepdims=True)
            if with_sink:
                l = l + jnp.exp(sink - m)
            o = jnp.dot(p.astype(BF16), vcat, preferred_element_type=F32)
            halves.append(o * (1.0 / l))
            if lse_refs:
                lse_tile = jnp.where(lane // (LANE // A_HEADS) == slot, m + jnp.log(l), lse_tile)
        o_ref[0, :, cols] = jnp.where(low, halves[0], halves[1]).astype(o_ref.dtype)
    if lse_refs:
        lse_refs[0][0] = lse_tile


def _band_attention(p3, bias, sinks, *, kind, batch, seq):
    is_b = kind == len(A_GROUPS)
    dil = 1 if is_b else A_GROUPS[kind][1]
    n_sub = seq // dil
    nb = n_sub // BLOCK
    if is_b:
        view = p3
        q_blk, k_blk, v_blk, kv_w = COL_BQ // 512, COL_BKV // LANE, COL_BKV // LANE + 1, LANE
    elif dil == 1:
        view = p3
        q_blk = (COL_A + kind * 1536) // 512
        k_blk, v_blk, kv_w = q_blk + 1, q_blk + 2, 512
    else:
        qkv = lax.slice_in_dim(p3, COL_A + kind * 1536, COL_A + (kind + 1) * 1536, axis=2)
        view = qkv.reshape(batch, n_sub, dil * 1536)
        q_blk, k_blk, v_blk, kv_w = 0, 1, 2, 512
    per_q = view.shape[2] // dil // 512
    per_kv = view.shape[2] // dil // kv_w

    def cur(blk, per):
        return lambda b, r, n: (b, n, r * per + blk)

    def prev(blk, per):
        return lambda b, r, n: (b, jnp.maximum(n - 1, 0), r * per + blk)

    in_specs = [
        pl.BlockSpec(memory_space=pltpu.SMEM),
        pl.BlockSpec((1, BLOCK, 512), cur(q_blk, per_q)),
        pl.BlockSpec((1, BLOCK, kv_w), prev(k_blk, per_kv)),
        pl.BlockSpec((1, BLOCK, kv_w), cur(k_blk, per_kv)),
        pl.BlockSpec((1, BLOCK, kv_w), prev(v_blk, per_kv)),
        pl.BlockSpec((1, BLOCK, kv_w), cur(v_blk, per_kv)),
        pl.BlockSpec((1, 2, A_HEADS, BLOCK, 2 * BLOCK), lambda b, r, n: (kind, 0, 0, 0, 0)),
    ]
    out_specs = [pl.BlockSpec((1, BLOCK, 512), lambda b, r, n: (b, n, r))]
    out_shape = [jax.ShapeDtypeStruct((batch, n_sub, dil * 512), BF16)]
    if not is_b:
        out_specs.append(pl.BlockSpec((1, BLOCK, LANE), lambda b, r, n: (b, n, r)))
        out_shape.append(jax.ShapeDtypeStruct((batch, n_sub, dil * LANE), F32))
    outs = pl.pallas_call(
        functools.partial(_band_kernel, shared_kv=is_b, with_sink=is_b),
        grid=(batch, dil, nb),
        in_specs=in_specs,
        out_specs=out_specs,
        out_shape=out_shape,
        compiler_params=_cparams(3),
        name="band_b" if is_b else f"band_a{kind}",
    )(sinks, view, view, view, view, view, bias)
    o = outs[0].reshape(batch * seq, 512)
    if is_b:
        return o
    return o, outs[1].reshape(batch * seq, LANE)


def _mla_prep_kernel(c_ref, cos_ref, sin_ref, gq_ref, gkv_ref, wq1_ref, wq2_ref, wk_ref, wv_ref,
                     sela_ref, selb_ref, q_ref, k_ref, v_ref):
    c = c_ref[...]
    cq = c[:, :C_Q_RANK].astype(F32)
    ckv = c[:, C_Q_RANK:C_Q_RANK + C_KV_RANK].astype(F32)
    nq = cq * lax.rsqrt(jnp.mean(cq * cq, axis=-1, keepdims=True) + RMS_EPS) * gq_ref[...]
    nkv = ckv * lax.rsqrt(jnp.mean(ckv * ckv, axis=-1, keepdims=True) + RMS_EPS) * gkv_ref[...]
    nq = nq.astype(BF16)
    nkv = nkv.astype(BF16)
    cos = cos_ref[...]
    sin = sin_ref[...]
    cos8 = jnp.tile(cos, (1, C_HEADS))
    sin8 = jnp.tile(sin, (1, C_HEADS))
    q = (jnp.dot(nq, wq1_ref[...], preferred_element_type=F32) * cos8
         + jnp.dot(nq, wq2_ref[...], preferred_element_type=F32) * sin8)
    q_ref[...] = (q * ((C_NOPE + C_ROPE) ** -0.5)).astype(q_ref.dtype)
    kr = (jnp.dot(c, sela_ref[...], preferred_element_type=F32) * cos
          + jnp.dot(c, selb_ref[...], preferred_element_type=F32) * sin)
    k = jnp.dot(nkv, wk_ref[...], preferred_element_type=F32) + jnp.tile(kr, (1, C_HEADS))
    k_ref[...] = k.astype(k_ref.dtype)
    v_ref[...] = jnp.dot(nkv, wv_ref[...], preferred_element_type=F32).astype(v_ref.dtype)


def _mla_prep(p2, cos_t, sin_t, lw, *, seq, tm=512):
    t = p2.shape[0]
    per_seq = seq // tm
    full = lambda a: pl.BlockSpec(a.shape, lambda i: (0,) * a.ndim)
    consts = (lw["gq"], lw["gkv"], lw["wq1"], lw["wq2"], lw["wk"], lw["wv"], lw["sela"], lw["selb"])
    return pl.pallas_call(
        _mla_prep_kernel,
        grid=(t // tm,),
        in_specs=[pl.BlockSpec((tm, 512), lambda i: (i, COL_C // 512)),
                  pl.BlockSpec((tm, LANE), lambda i: (i % per_seq, 0)),
                  pl.BlockSpec((tm, LANE), lambda i: (i % per_seq, 0))] + [full(a) for a in consts],
        out_specs=[pl.BlockSpec((tm, C_HEADS * LANE), lambda i: (i, 0)),
                   pl.BlockSpec((tm, C_HEADS * LANE), lambda i: (i, 0)),
                   pl.BlockSpec((tm, C_HEADS * C_V), lambda i: (i, 0))],
        out_shape=[jax.ShapeDtypeStruct((t, C_HEADS * LANE), BF16),
                   jax.ShapeDtypeStruct((t, C_HEADS * LANE), BF16),
                   jax.ShapeDtypeStruct((t, C_HEADS * C_V), BF16)],
        compiler_params=_cparams(1),
        name="mla_prep",
    )(p2, cos_t, sin_t, *consts)


def _mla_flash_kernel(q_ref, k_ref, v_ref, o_ref, m_ref, l_ref, acc_ref, *, tq, tk):
    qi = pl.program_id(1)
    ki = pl.program_id(2)

    @pl.when(ki == 0)
    def _():
        m_ref[...] = jnp.full(m_ref.shape, NEG, F32)
        l_ref[...] = jnp.zeros(l_ref.shape, F32)
        acc_ref[...] = jnp.zeros(acc_ref.shape, F32)

    def step(masked):
        if masked:
            row = lax.broadcasted_iota(jnp.int32, (tq, tk), 0)
            col = lax.broadcasted_iota(jnp.int32, (tq, tk), 1)
            keep = row >= col
        for h in range(C_HEADS):
            hc = slice(h * LANE, (h + 1) * LANE)
            vc = slice((h // 2) * LANE, (h // 2 + 1) * LANE)
            s = lax.dot_general(q_ref[0, :, hc], k_ref[0, :, hc], (((1,), (1,)), ((), ())),
                                preferred_element_type=F32)
            if masked:
                s = jnp.where(keep, s, NEG)
            m_prev = m_ref[h]
            m_new = jnp.maximum(m_prev, jnp.max(s, axis=-1, keepdims=True))
            alpha = jnp.exp(m_prev - m_new)
            p = jnp.exp(s - jnp.tile(m_new, (1, tk // LANE)))
            l_ref[h] = alpha * l_ref[h] + jnp.sum(p, axis=-1, keepdims=True)
            acc_ref[h] = alpha * acc_ref[h] + jnp.dot(p.astype(BF16), v_ref[0, :, vc],
                                                      preferred_element_type=F32)
            m_ref[h] = m_new

    @pl.when(ki < qi)
    def _():
        step(False)

    @pl.when(ki == qi)
    def _():
        step(True)
        low = lax.broadcasted_iota(jnp.int32, (tq, LANE), 1) < C_V
        for pair in range(C_HEADS // 2):
            oa = acc_ref[2 * pair] * (1.0 / l_ref[2 * pair])
            ob = acc_ref[2 * pair + 1] * (1.0 / l_ref[2 * pair + 1])
            o_ref[0, :, pair * LANE:(pair + 1) * LANE] = jnp.where(low, oa, ob).astype(o_ref.dtype)


def _mla_flash(q, k, v, *, batch, seq, tq=512):
    tk = tq
    nq = seq // tq
    q3 = q.reshape(batch, seq, C_HEADS * LANE)
    k3 = k.reshape(batch, seq, C_HEADS * LANE)
    v3 = v.reshape(batch, seq, C_HEADS * C_V)
    kv_map = lambda b, i, j: (b, jnp.minimum(i, j), 0)
    out = pl.pallas_call(
        functools.partial(_mla_flash_kernel, tq=tq, tk=tk),
        grid=(batch, nq, nq),
        in_specs=[pl.BlockSpec((1, tq, C_HEADS * LANE), lambda b, i, j: (b, i, 0)),
                  pl.BlockSpec((1, tk, C_HEADS * LANE), kv_map),
                  pl.BlockSpec((1, tk, C_HEADS * C_V), kv_map)],
        out_specs=pl.BlockSpec((1, tq, C_HEADS * C_V), lambda b, i, j: (b, i, 0)),
        out_shape=jax.ShapeDtypeStruct((batch, seq, C_HEADS * C_V), BF16),
        scratch_shapes=[pltpu.VMEM((C_HEADS, tq, LANE), F32),
                        pltpu.VMEM((C_HEADS, tq, LANE), F32),
                        pltpu.VMEM((C_HEADS, tq, LANE), F32)],
        compiler_params=_cparams(3),
        name="mla_flash",
    )(q3, k3, v3)
    return out.reshape(batch * seq, C_HEADS * C_V)


def _layer_norm(y, g, b):
    mu = jnp.mean(y, axis=-1, keepdims=True)
    d = y - mu
    var = jnp.mean(d * d, axis=-1, keepdims=True)
    return d * lax.rsqrt(var + LN_EPS) * g + b


def _merge_kernel(oa0_ref, oa1_ref, oa2_ref, l0_ref, l1_ref, l2_ref, ob_ref, oc_ref, gate_ref, x_ref,
                  e_ref, wb_ref, wo_ref, bg_ref, g_ref, b_ref, o_ref, *, alpha):
    lses = (l0_ref[...], l1_ref[...], l2_ref[...])
    top = jnp.maximum(jnp.maximum(lses[0], lses[1]), lses[2])
    es = [jnp.exp(v - top) for v in lses]
    inv = 1.0 / (es[0] + es[1] + es[2])
    o_a = None
    for e, oa_ref in zip(es, (oa0_ref, oa1_ref, oa2_ref)):
        w = e * inv
        hi = w.astype(BF16)
        lo = (w - hi.astype(F32)).astype(BF16)
        wide = (jnp.dot(hi, e_ref[...], preferred_element_type=F32)
                + jnp.dot(lo, e_ref[...], preferred_element_type=F32))
        term = wide * oa_ref[...].astype(F32)
        o_a = term if o_a is None else o_a + term
    branches = (o_a.astype(BF16), ob_ref[...], oc_ref[...])
    merged = None
    for i, br in enumerate(branches):
        gate = jax.nn.sigmoid(gate_ref[:, i * D_MODEL:(i + 1) * D_MODEL].astype(F32)
                              + bg_ref[:, i * D_MODEL:(i + 1) * D_MODEL])
        term = gate * jnp.dot(br, wb_ref[i], preferred_element_type=F32)
        merged = term if merged is None else merged + term
    mix = jnp.dot(merged.astype(BF16), wo_ref[...], preferred_element_type=F32)
    o_ref[...] = _layer_norm(alpha * x_ref[...] + mix, g_ref[...], b_ref[...])


def _merge(oa, lse, ob, oc, p2, x2d, expand, lw, *, alpha, tm=512):
    t = x2d.shape[0]
    row = lambda w: pl.BlockSpec((tm, w), lambda i: (i, 0))
    full = lambda a: pl.BlockSpec(a.shape, lambda i: (0,) * a.ndim)
    consts = (expand, lw["wb"], lw["wo"], lw["bg"], lw["ln1_g"], lw["ln1_b"])
    return pl.pallas_call(
        functools.partial(_merge_kernel, alpha=alpha),
        grid=(t // tm,),
        in_specs=[row(512)] * 3 + [row(LANE)] * 3 + [row(512), row(512),
                  pl.BlockSpec((tm, 3 * D_MODEL), lambda i: (i, COL_GATE // (3 * D_MODEL))),
                  row(D_MODEL)] + [full(a) for a in consts],
        out_specs=row(D_MODEL),
        out_shape=jax.ShapeDtypeStruct((t, D_MODEL), F32),
        compiler_params=_cparams(1),
        name="merge",
    )(*oa, *lse, ob, oc, p2, x2d, *consts)


HALO = 8
FF_CHUNK = 512


def _ffn_kernel(halo_ref, x_ref, wup_ref, cw_ref, cb_ref, wdn_ref, g_ref, b_ref, o_ref,
                ug_ref, uv_ref, *, alpha, tm, per_seq):
    i = pl.program_id(0)
    x = x_ref[...]
    halo = jnp.where(i % per_seq == 0, jnp.zeros_like(halo_ref[...]), halo_ref[...])
    xh = jnp.concatenate([halo, x], axis=0).astype(BF16)
    acc = None
    for c0 in range(0, D_FF, FF_CHUNK):
        cf = min(FF_CHUNK, D_FF - c0)
        parts = []
        for u_ref, base in ((ug_ref, c0), (uv_ref, D_FF + c0)):
            u_ref[:, :cf] = jnp.dot(xh, wup_ref[:, base:base + cf], preferred_element_type=F32)
            y = cb_ref[:, base:base + cf]
            for tap in range(3):
                y = y + u_ref[pl.ds(HALO - 2 + tap, tm), :cf] * cw_ref[tap:tap + 1, base:base + cf]
            parts.append(y)
        act = (parts[0] * jax.nn.sigmoid(parts[0]) * parts[1]).astype(BF16)
        term = jnp.dot(act, wdn_ref[c0:c0 + cf, :], preferred_element_type=F32)
        acc = term if acc is None else acc + term
    o_ref[...] = _layer_norm(alpha * x + acc, g_ref[...], b_ref[...])


def _ffn(x2d, lw, *, alpha, seq, tm=512):
    t = x2d.shape[0]
    per_seq = seq // tm
    full = lambda a: pl.BlockSpec(a.shape, lambda i: (0,) * a.ndim, pipeline_mode=pl.Buffered(1))
    consts = (lw["wup"], lw["cw"], lw["cb"], lw["wdn"], lw["ln2_g"], lw["ln2_b"])
    return pl.pallas_call(
        functools.partial(_ffn_kernel, alpha=alpha, tm=tm, per_seq=per_seq),
        grid=(t // tm,),
        in_specs=[pl.BlockSpec((HALO, D_MODEL), lambda i: (jnp.maximum(i * (tm // HALO) - 1, 0), 0)),
                  pl.BlockSpec((tm, D_MODEL), lambda i: (i, 0))] + [full(a) for a in consts],
        out_specs=pl.BlockSpec((tm, D_MODEL), lambda i: (i, 0)),
        out_shape=jax.ShapeDtypeStruct((t, D_MODEL), F32),
        scratch_shapes=[pltpu.VMEM((tm + HALO, FF_CHUNK), F32),
                        pltpu.VMEM((tm + HALO, FF_CHUNK), F32)],
        compiler_params=_cparams(1),
        name="ffn",
    )(x2d, x2d, *consts)


def _layer_weights(l, w_in, b_gate, q_norm_g, kv_norm_g, w_uq, w_ukv, w_branch, w_out,
                   ln1_g, ln1_b, w_ffn_up, conv_w, conv_b, w_ffn_down, ln2_g, ln2_b):
    wi = w_in[l]
    a_end = 4608
    bq = wi[:, a_end:a_end + 512].reshape(D_MODEL, B_Q_HEADS, HEAD_DIM)
    bq = jnp.stack([bq[:, :4], bq[:, 4:]], axis=2).reshape(D_MODEL, 512)
    bkv = wi[:, 5120:5376]
    cdq = wi[:, 5376:5632]
    ckv = wi[:, 5632:5760]
    kr = wi[:, 5760:5792]
    kr_rot = jnp.concatenate([-kr[:, C_ROPE // 2:], kr[:, :C_ROPE // 2]], axis=1)
    gate = wi[:, 5792:]
    zeros = lambda n: jnp.zeros((D_MODEL, n), wi.dtype)
    wp = jnp.concatenate([gate, wi[:, :a_end], bq, cdq, ckv, kr, kr_rot, zeros(64), bkv, zeros(256)],
                         axis=1).astype(BF16)

    uq = w_uq[l].reshape(C_Q_RANK, C_HEADS, C_NOPE + C_ROPE)
    zq = lambda n: jnp.zeros((C_Q_RANK, C_HEADS, n), uq.dtype)
    half = C_ROPE // 2
    wq1 = jnp.concatenate([uq, zq(LANE - C_NOPE - C_ROPE)], axis=2)
    wq2 = jnp.concatenate([zq(C_NOPE), -uq[:, :, C_NOPE + half:], uq[:, :, C_NOPE:C_NOPE + half],
                           zq(LANE - C_NOPE - C_ROPE)], axis=2)
    ukv = w_ukv[l].reshape(C_KV_RANK, C_HEADS, C_NOPE + C_V)
    wk = jnp.concatenate([ukv[:, :, :C_NOPE], jnp.zeros((C_KV_RANK, C_HEADS, LANE - C_NOPE), ukv.dtype)],
                         axis=2)
    wv = ukv[:, :, C_NOPE:]

    wb = w_branch[l]
    wb1 = wb[1].reshape(B_Q_HEADS, HEAD_DIM, D_MODEL)
    wb1 = jnp.stack([wb1[:4], wb1[4:]], axis=1).reshape(512, D_MODEL)
    wb = jnp.stack([wb[0], wb1, wb[2]])

    sel_rows = np.zeros((2, 512, LANE), np.float32)
    for j in range(C_ROPE):
        sel_rows[0, C_Q_RANK + C_KV_RANK + j, C_NOPE + j] = 1.0
        sel_rows[1, C_Q_RANK + C_KV_RANK + C_ROPE + j, C_NOPE + j] = 1.0

    return dict(
        wp=wp,
        gq=q_norm_g[l].reshape(1, -1), gkv=kv_norm_g[l].reshape(1, -1),
        wq1=wq1.reshape(C_Q_RANK, -1).astype(BF16), wq2=wq2.reshape(C_Q_RANK, -1).astype(BF16),
        wk=wk.reshape(C_KV_RANK, -1).astype(BF16), wv=wv.reshape(C_KV_RANK, -1).astype(BF16),
        sela=jnp.asarray(sel_rows[0], BF16), selb=jnp.asarray(sel_rows[1], BF16),
        wb=wb.astype(BF16), wo=w_out[l].astype(BF16), bg=b_gate[l].reshape(1, -1),
        ln1_g=ln1_g[l].reshape(1, -1), ln1_b=ln1_b[l].reshape(1, -1),
        wup=w_ffn_up[l].astype(BF16), cw=conv_w[l], cb=conv_b[l].reshape(1, -1),
        wdn=w_ffn_down[l].astype(BF16),
        ln2_g=ln2_g[l].reshape(1, -1), ln2_b=ln2_b[l].reshape(1, -1),
    )


def _rope_tables(seq):
    pos = jnp.arange(seq, dtype=F32)
    inv_freq = ROPE_BASE ** (-jnp.arange(0, C_ROPE, 2, dtype=F32) / C_ROPE)
    ang = pos[:, None] * inv_freq[None, :]
    cos, sin = jnp.cos(ang), jnp.sin(ang)
    pad = jnp.zeros((seq, LANE - C_NOPE - C_ROPE), F32)
    cos_t = jnp.concatenate([jnp.ones((seq, C_NOPE), F32), cos, cos, pad], axis=1)
    sin_t = jnp.concatenate([jnp.zeros((seq, C_NOPE), F32), sin, sin, pad], axis=1)
    return cos_t, sin_t


def _expand_matrix():
    e = np.zeros((LANE, A_HEADS * HEAD_DIM), np.float32)
    for c in range(A_HEADS * HEAD_DIM):
        e[(LANE // A_HEADS) * (c // HEAD_DIM), c] = 1.0
    return jnp.asarray(e, BF16)


def kernel(x, rel_table, w_in, b_gate, sinks, q_norm_g, kv_norm_g, w_uq, w_ukv, w_branch, w_out,
           ln1_g, ln1_b, w_ffn_up, conv_w, conv_b, w_ffn_down, ln2_g, ln2_b):
    batch, seq, d = x.shape
    depth = w_in.shape[0]
    alpha = (2 * depth) ** 0.25
    cos_t, sin_t = _rope_tables(seq)
    expand = _expand_matrix()
    bias = _bias_tables(rel_table)
    x2d = x.reshape(batch * seq, d)
    for l in range(depth):
        lw = _layer_weights(l, w_in, b_gate, q_norm_g, kv_norm_g, w_uq, w_ukv, w_branch, w_out,
                            ln1_g, ln1_b, w_ffn_up, conv_w, conv_b, w_ffn_down, ln2_g, ln2_b)
        p2 = _proj(x2d, lw["wp"])
        p3 = p2.reshape(batch, seq, NP)
        sink_slots = sinks[l].reshape(2, 4).T.reshape(-1)
        oa, lse = [], []
        for kind in range(len(A_GROUPS)):
            o_g, lse_g = _band_attention(p3, bias, sink_slots, kind=kind, batch=batch, seq=seq)
            oa.append(o_g)
            lse.append(lse_g)
        ob = _band_attention(p3, bias, sink_slots, kind=len(A_GROUPS), batch=batch, seq=seq)
        q, k, v = _mla_prep(p2, cos_t, sin_t, lw, seq=seq)
        oc = _mla_flash(q, k, v, batch=batch, seq=seq)
        x2d = _merge(oa, lse, ob, oc, p2, x2d, expand, lw, alpha=alpha)
        x2d = _ffn(x2d, lw, alpha=alpha, seq=seq)
    return x2d.reshape(batch, seq, d)
```

```python
import functools
import math

import jax
import jax.numpy as jnp
import numpy as np
from jax import lax
from jax.experimental import pallas as pl
from jax.experimental.pallas import tpu as pltpu

F32 = jnp.float32
BF16 = jnp.bfloat16

D_MODEL = 1024
HEAD_DIM = 64
BLOCK = 128
A_GROUPS = ((128, 1), (512, 4), (2048, 16))
A_HEADS = 8
B_Q_HEADS = 8
B_WINDOW = 128
C_HEADS = 8
C_Q_RANK = 256
C_KV_RANK = 128
C_NOPE = 64
C_ROPE = 32
C_V = 64
ROPE_BASE = 10000.0
REL_BUCKETS = 32
REL_MAX_DIST = 2048
D_FF = 2816
LN_EPS = 1e-5
RMS_EPS = 1e-6
NEG = -1e30
LOG2E = math.log2(math.e)

COL_GATE = 0
COL_A = 3072
COL_BQ = 7680
COL_C = 8192
COL_BKV = 8704
NP = 9216

LANE = 128
VMEM_LIMIT = 56 * 1024 * 1024

N_KINDS = 4


def _cparams(n_axes):
    return pltpu.CompilerParams(dimension_semantics=("arbitrary",) * n_axes,
                                vmem_limit_bytes=VMEM_LIMIT)


PERM_ROWS = 2048
PROJ_TN = 1536


def _xperm_kernel(*refs):
    n_slab = D_MODEL // LANE
    x_refs, o_refs = refs[:n_slab], refs[n_slab:]
    for c, x_ref in enumerate(x_refs):
        for o_ref, (_, dil) in zip(o_refs, A_GROUPS[1:]):
            span = BLOCK * dil
            for s0 in range(0, PERM_ROWS, span):
                for r in range(dil):
                    rows = x_ref[pl.ds(s0 + r, BLOCK, stride=dil), :]
                    o_ref[s0 + r * BLOCK:s0 + (r + 1) * BLOCK, c * LANE:(c + 1) * LANE] = rows.astype(BF16)


def _xperm(x2d):
    t, k = x2d.shape
    n_out = len(A_GROUPS) - 1
    slab = lambda c: pl.BlockSpec((PERM_ROWS, LANE), lambda i: (i, c))
    return pl.pallas_call(
        _xperm_kernel,
        grid=(t // PERM_ROWS,),
        in_specs=[slab(c) for c in range(k // LANE)],
        out_specs=[pl.BlockSpec((PERM_ROWS, k), lambda i: (i, 0))] * n_out,
        out_shape=[jax.ShapeDtypeStruct((t, k), BF16)] * n_out,
        compiler_params=_cparams(1),
        name="xperm",
    )(*([x2d] * (k // LANE)))


def _proj_kernel(x_ref, x4_ref, x16_ref, w_ref, o_ref, xb_ref):
    j = pl.program_id(1)

    @pl.when(j == 0)
    def _():
        xb_ref[...] = x_ref[...].astype(BF16)

    def emit(lhs_ref):
        o_ref[...] = jnp.dot(lhs_ref[...], w_ref[...], preferred_element_type=F32).astype(o_ref.dtype)

    tile_a1 = COL_A // PROJ_TN + 1
    pl.when(j == tile_a1)(lambda: emit(x4_ref))
    pl.when(j == tile_a1 + 1)(lambda: emit(x16_ref))
    pl.when((j != tile_a1) & (j != tile_a1 + 1))(lambda: emit(xb_ref))


def _proj(x2d, x4, x16, w, tm=1024):
    t, k = x2d.shape
    n = w.shape[1]
    row = pl.BlockSpec((tm, k), lambda i, j: (i, 0))
    return pl.pallas_call(
        _proj_kernel,
        grid=(t // tm, n // PROJ_TN),
        in_specs=[row, row, row, pl.BlockSpec((k, PROJ_TN), lambda i, j: (0, j))],
        out_specs=pl.BlockSpec((tm, PROJ_TN), lambda i, j: (i, j)),
        out_shape=jax.ShapeDtypeStruct((t, n), BF16),
        scratch_shapes=[pltpu.VMEM((tm, k), BF16)],
        compiler_params=_cparams(2),
        name="proj",
    )(x2d, x4, x16, w)


def _t5_bucket(dist):
    n = jnp.maximum(dist, 0)
    max_exact = REL_BUCKETS // 2
    scaled = jnp.log(jnp.maximum(n, 1).astype(F32) / max_exact) / math.log(REL_MAX_DIST / max_exact)
    large = max_exact + (scaled * (REL_BUCKETS - max_exact)).astype(jnp.int32)
    return jnp.where(n < max_exact, n, jnp.minimum(large, REL_BUCKETS - 1))


def _bias_codes():
    qi = jnp.arange(BLOCK)[:, None]
    ki = jnp.arange(2 * BLOCK)[None, :]
    step = BLOCK + qi - ki
    has_prev = ki >= BLOCK
    codes = []
    for kind in range(N_KINDS):
        if kind < len(A_GROUPS):
            dil = A_GROUPS[kind][1]
            band = (step >= 0) & (step <= BLOCK)
        else:
            dil = 1
            band = (step >= 0) & (step < B_WINDOW)
        bucket = _t5_bucket(step * dil)
        codes.append(jnp.stack([jnp.where(band & has_prev, bucket, -1),
                                jnp.where(band, bucket, -1)]))
    return jnp.stack(codes).astype(jnp.int32)


def _bias_kernel(rel_ref, code_ref, o_ref):
    kind = pl.program_id(0)
    code = code_ref[0, 0]
    for slot in range(A_HEADS):
        b_head = len(A_GROUPS) * A_HEADS + slot // 2 + 4 * (slot % 2)
        col = jnp.where(kind < len(A_GROUPS), kind * A_HEADS + slot, b_head)
        acc = jnp.full(code.shape, NEG, F32)
        for b in range(REL_BUCKETS):
            acc = jnp.where(code == b, rel_ref[b, col], acc)
        o_ref[0, 0, slot] = acc


def _bias_tables(rel_table):
    codes = _bias_codes()
    return pl.pallas_call(
        _bias_kernel,
        grid=(N_KINDS, 2),
        in_specs=[pl.BlockSpec(memory_space=pltpu.SMEM),
                  pl.BlockSpec((1, 1, BLOCK, 2 * BLOCK), lambda k, v: (k, v, 0, 0))],
        out_specs=pl.BlockSpec((1, 1, A_HEADS, BLOCK, 2 * BLOCK), lambda k, v: (k, v, 0, 0, 0)),
        out_shape=jax.ShapeDtypeStruct((N_KINDS, 2, A_HEADS, BLOCK, 2 * BLOCK), F32),
        compiler_params=_cparams(2),
        name="bias_tables",
    )(rel_table, codes)


BAND_QB = 4


def _band_kernel(sink_ref, q_ref, kp_ref, kc_ref, vp_ref, vc_ref, bias_ref, o_ref, *lse_refs,
                 shared_kv, with_sink):
    first = pl.program_id(2) == 0
    lane = lax.broadcasted_iota(jnp.int32, (BLOCK, LANE), 1)
    low = lane < HEAD_DIM
    for qb in range(BAND_QB):
        variant = jnp.where(first, 0, 1) if qb == 0 else 1
        lse_tile = jnp.zeros((BLOCK, LANE), F32)
        for pair in range(4):
            cols = slice(pair * LANE, (pair + 1) * LANE)
            kv_cols = slice(0, LANE) if shared_kv else cols
            qp = q_ref[qb, :, cols] * jnp.asarray(HEAD_DIM ** -0.5, BF16)
            k_prev = kp_ref[:, kv_cols] if qb == 0 else kc_ref[qb - 1, :, kv_cols]
            v_prev = vp_ref[:, kv_cols] if qb == 0 else vc_ref[qb - 1, :, kv_cols]
            kcat = jnp.concatenate([k_prev, kc_ref[qb, :, kv_cols]], axis=0)
            vcat = jnp.concatenate([v_prev, vc_ref[qb, :, kv_cols]], axis=0)
            halves = []
            for sub in range(2):
                slot = 2 * pair + sub
                qh = jnp.where(low if sub == 0 else jnp.logical_not(low), qp, jnp.zeros_like(qp))
                s = lax.dot_general(qh, kcat, (((1,), (1,)), ((), ())), preferred_element_type=F32)
                s = s + bias_ref[variant, slot]
                m = jnp.max(s, axis=-1, keepdims=True)
                if with_sink:
                    sink = sink_ref[slot]
                    m = jnp.maximum(m, sink)
                p = jnp.exp(s - m)
                l = jnp.sum(p, axis=-1, keepdims=True)
                if with_sink:
                    l = l + jnp.exp(sink - m)
                o = jnp.dot(p.astype(BF16), vcat, preferred_element_type=F32)
                halves.append(o * (1.0 / l))
                if lse_refs:
                    lse_tile = jnp.where(lane // (LANE // A_HEADS) == slot, m + jnp.log(l), lse_tile)
            o_ref[qb, :, cols] = jnp.where(low, halves[0], halves[1]).astype(o_ref.dtype)
        if lse_refs:
            lse_refs[0][qb] = lse_tile


def _band_attention(p2, bias, sinks, *, kind, batch, seq):
    is_b = kind == len(A_GROUPS)
    dil = 1 if is_b else A_GROUPS[kind][1]
    n_span = seq // (BLOCK * dil)
    view = p2.reshape(batch, n_span, dil, BLOCK, NP)
    if is_b:
        q_blk, k_blk, v_blk, kv_w = COL_BQ // 512, COL_BKV // LANE, COL_BKV // LANE + 1, LANE
    else:
        q_blk = (COL_A + kind * 1536) // 512
        k_blk, v_blk, kv_w = q_blk + 1, q_blk + 2, 512

    def cur(blk, width):
        return pl.BlockSpec((None, BAND_QB, None, BLOCK, width), lambda b, r, n: (b, n, r, 0, blk))

    def prev(blk, width):
        return pl.BlockSpec((None, None, None, BLOCK, width),
                            lambda b, r, n: (b, jnp.maximum(n * BAND_QB - 1, 0), r, 0, blk))

    in_specs = [
        pl.BlockSpec(memory_space=pltpu.SMEM),
        cur(q_blk, 512), prev(k_blk, kv_w), cur(k_blk, kv_w), prev(v_blk, kv_w), cur(v_blk, kv_w),
        pl.BlockSpec((None, 2, A_HEADS, BLOCK, 2 * BLOCK), lambda b, r, n: (kind, 0, 0, 0, 0)),
    ]
    out_specs = [pl.BlockSpec((None, BAND_QB, None, BLOCK, 512), lambda b, r, n: (b, n, r, 0, 0))]
    out_shape = [jax.ShapeDtypeStruct((batch, n_span, dil, BLOCK, 512), BF16)]
    if not is_b:
        out_specs.append(pl.BlockSpec((None, BAND_QB, None, BLOCK, LANE), lambda b, r, n: (b, n, r, 0, 0)))
        out_shape.append(jax.ShapeDtypeStruct((batch, n_span, dil, BLOCK, LANE), F32))
    outs = pl.pallas_call(
        functools.partial(_band_kernel, shared_kv=is_b, with_sink=is_b),
        grid=(batch, dil, n_span // BAND_QB),
        in_specs=in_specs,
        out_specs=out_specs,
        out_shape=out_shape,
        compiler_params=_cparams(3),
        name="band_b" if is_b else f"band_a{kind}",
    )(sinks, view, view, view, view, view, bias)
    return outs[0] if is_b else outs


def _mla_prep_kernel(c_ref, cos_ref, sin_ref, gq_ref, gkv_ref, wq1_ref, wq2_ref, wk_ref, wv_ref,
                     sela_ref, selb_ref, q_ref, k_ref, v_ref):
    c = c_ref[...]
    cq = c[:, :C_Q_RANK].astype(F32)
    ckv = c[:, C_Q_RANK:C_Q_RANK + C_KV_RANK].astype(F32)
    nq = cq * lax.rsqrt(jnp.mean(cq * cq, axis=-1, keepdims=True) + RMS_EPS) * gq_ref[...]
    nkv = ckv * lax.rsqrt(jnp.mean(ckv * ckv, axis=-1, keepdims=True) + RMS_EPS) * gkv_ref[...]
    nq = nq.astype(BF16)
    nkv = nkv.astype(BF16)
    cos = cos_ref[...]
    sin = sin_ref[...]
    cos8 = jnp.tile(cos, (1, C_HEADS))
    sin8 = jnp.tile(sin, (1, C_HEADS))
    q = (jnp.dot(nq, wq1_ref[...], preferred_element_type=F32) * cos8
         + jnp.dot(nq, wq2_ref[...], preferred_element_type=F32) * sin8)
    q_ref[...] = (q * ((C_NOPE + C_ROPE) ** -0.5 * LOG2E)).astype(q_ref.dtype)
    kr = (jnp.dot(c, sela_ref[...], preferred_element_type=F32) * cos
          + jnp.dot(c, selb_ref[...], preferred_element_type=F32) * sin)
    k = jnp.dot(nkv, wk_ref[...], preferred_element_type=F32) + jnp.tile(kr, (1, C_HEADS))
    k_ref[...] = k.astype(k_ref.dtype)
    lane = lax.broadcasted_iota(jnp.int32, (1, C_HEADS * LANE), 1)
    ones_col = jnp.where(lane % LANE == C_V, 1.0, 0.0).astype(F32)
    v_ref[...] = (jnp.dot(nkv, wv_ref[...], preferred_element_type=F32) + ones_col).astype(v_ref.dtype)


def _mla_prep(p2, cos_t, sin_t, lw, *, seq, tm=512):
    t = p2.shape[0]
    per_seq = seq // tm
    full = lambda a: pl.BlockSpec(a.shape, lambda i: (0,) * a.ndim)
    consts = (lw["gq"], lw["gkv"], lw["wq1"], lw["wq2"], lw["wk"], lw["wv"], lw["sela"], lw["selb"])
    wide = C_HEADS * LANE
    return pl.pallas_call(
        _mla_prep_kernel,
        grid=(t // tm,),
        in_specs=[pl.BlockSpec((tm, 512), lambda i: (i, COL_C // 512)),
                  pl.BlockSpec((tm, LANE), lambda i: (i % per_seq, 0)),
                  pl.BlockSpec((tm, LANE), lambda i: (i % per_seq, 0))] + [full(a) for a in consts],
        out_specs=[pl.BlockSpec((tm, wide), lambda i: (i, 0))] * 3,
        out_shape=[jax.ShapeDtypeStruct((t, wide), BF16)] * 3,
        compiler_params=_cparams(1),
        name="mla_prep",
    )(p2, cos_t, sin_t, *consts)


def _mla_flash_kernel(q_ref, k_ref, v_ref, o_ref, m_ref, acc_ref, *, tq, tk):
    qi = pl.program_id(1)
    ki = pl.program_id(2)

    @pl.when(ki == 0)
    def _():
        m_ref[...] = jnp.full(m_ref.shape, NEG, F32)
        acc_ref[...] = jnp.zeros(acc_ref.shape, F32)

    def step(masked):
        if masked:
            row = lax.broadcasted_iota(jnp.int32, (tq, tk), 0)
            col = lax.broadcasted_iota(jnp.int32, (tq, tk), 1)
            keep = row >= col
        for h in range(C_HEADS):
            hc = slice(h * LANE, (h + 1) * LANE)
            s = lax.dot_general(q_ref[0, :, hc], k_ref[0, :, hc], (((1,), (1,)), ((), ())),
                                preferred_element_type=F32)
            if masked:
                s = jnp.where(keep, s, NEG)
            m_prev = m_ref[h]
            m_new = jnp.maximum(m_prev, jnp.max(s, axis=-1, keepdims=True))
            alpha = jnp.exp2(m_prev - m_new)
            p = jnp.exp2(s - jnp.tile(m_new, (1, tk // LANE)))
            acc_ref[h] = alpha * acc_ref[h] + jnp.dot(p.astype(BF16), v_ref[0, :, hc],
                                                      preferred_element_type=F32)
            m_ref[h] = m_new

    @pl.when(ki < qi)
    def _():
        step(False)

    @pl.when(ki == qi)
    def _():
        step(True)
        low = lax.broadcasted_iota(jnp.int32, (tq, LANE), 1) < C_V
        for pair in range(C_HEADS // 2):
            halves = []
            for h in (2 * pair, 2 * pair + 1):
                acc = acc_ref[h]
                halves.append(acc * (1.0 / acc[:, C_V:C_V + 1]))
            odd = pltpu.roll(halves[1], C_V, axis=1)
            o_ref[0, :, pair * LANE:(pair + 1) * LANE] = jnp.where(low, halves[0], odd).astype(o_ref.dtype)


def _mla_flash(q, k, v, *, batch, seq, tq=512):
    tk = tq
    nq = seq // tq
    wide = C_HEADS * LANE
    q3 = q.reshape(batch, seq, wide)
    k3 = k.reshape(batch, seq, wide)
    v3 = v.reshape(batch, seq, wide)
    kv_map = lambda b, i, j: (b, jnp.minimum(i, j), 0)
    out = pl.pallas_call(
        functools.partial(_mla_flash_kernel, tq=tq, tk=tk),
        grid=(batch, nq, nq),
        in_specs=[pl.BlockSpec((1, tq, wide), lambda b, i, j: (b, i, 0)),
                  pl.BlockSpec((1, tk, wide), kv_map),
                  pl.BlockSpec((1, tk, wide), kv_map)],
        out_specs=pl.BlockSpec((1, tq, C_HEADS * C_V), lambda b, i, j: (b, i, 0)),
        out_shape=jax.ShapeDtypeStruct((batch, seq, C_HEADS * C_V), BF16),
        scratch_shapes=[pltpu.VMEM((C_HEADS, tq, LANE), F32),
                        pltpu.VMEM((C_HEADS, tq, LANE), F32)],
        compiler_params=_cparams(3),
        name="mla_flash",
    )(q3, k3, v3)
    return out.reshape(batch * seq, C_HEADS * C_V)


def _layer_norm(y, g, b):
    mu = jnp.mean(y, axis=-1, keepdims=True)
    d = y - mu
    var = jnp.mean(d * d, axis=-1, keepdims=True)
    return d * lax.rsqrt(var + LN_EPS) * g + b


def _token_order(src_ref, tmp_ref, dil):
    if dil == 1:
        return src_ref[...].astype(F32)
    n = src_ref.shape[1]
    n_slab = src_ref.shape[2] // LANE
    for r in range(dil):
        rows = src_ref[r].astype(F32)
        for c in range(n_slab):
            tmp_ref[c, pl.ds(r, n, stride=dil), :] = rows[:, c * LANE:(c + 1) * LANE]
    return jnp.concatenate([tmp_ref[c] for c in range(n_slab)], axis=1)


def _merge_kernel(oa0_ref, oa1_ref, oa2_ref, l0_ref, l1_ref, l2_ref, ob_ref, oc_ref, gate_ref, x_ref,
                  e_ref, wb_ref, wo_ref, bg_ref, g_ref, b_ref, o_ref, ot1_ref, ot2_ref, lt1_ref, lt2_ref,
                  *, alpha):
    dils = [d for _, d in A_GROUPS]
    lses = [_token_order(ref, tmp, d)
            for ref, tmp, d in zip((l0_ref, l1_ref, l2_ref), (None, lt1_ref, lt2_ref), dils)]
    top = jnp.maximum(jnp.maximum(lses[0], lses[1]), lses[2])
    es = [jnp.exp(v - top) for v in lses]
    inv = 1.0 / (es[0] + es[1] + es[2])
    o_a = None
    for e, oa_ref, tmp, d in zip(es, (oa0_ref, oa1_ref, oa2_ref), (None, ot1_ref, ot2_ref), dils):
        w = e * inv
        hi = w.astype(BF16)
        lo = (w - hi.astype(F32)).astype(BF16)
        wide = (jnp.dot(hi, e_ref[...], preferred_element_type=F32)
                + jnp.dot(lo, e_ref[...], preferred_element_type=F32))
        term = wide * _token_order(oa_ref, tmp, d)
        o_a = term if o_a is None else o_a + term
    branches = (o_a.astype(BF16), ob_ref[...], oc_ref[...])
    merged = None
    for i, br in enumerate(branches):
        gate = jax.nn.sigmoid(gate_ref[:, i * D_MODEL:(i + 1) * D_MODEL].astype(F32)
                              + bg_ref[:, i * D_MODEL:(i + 1) * D_MODEL])
        term = gate * jnp.dot(br, wb_ref[i], preferred_element_type=F32)
        merged = term if merged is None else merged + term
    mix = jnp.dot(merged.astype(BF16), wo_ref[...], preferred_element_type=F32)
    o_ref[...] = _layer_norm(alpha * x_ref[...] + mix, g_ref[...], b_ref[...])


def _merge(oa, lse, ob, oc, p2, x2d, expand, lw, *, alpha, tm=512):
    t = x2d.shape[0]
    row = lambda w: pl.BlockSpec((tm, w), lambda i: (i, 0))
    full = lambda a: pl.BlockSpec(a.shape, lambda i: (0,) * a.ndim)

    def grouped(arrs):
        views, specs = [], []
        for a, (_, dil) in zip(arrs, A_GROUPS):
            w = a.shape[-1]
            if dil == 1:
                views.append(a.reshape(t, w))
                specs.append(row(w))
                continue
            per_span = BLOCK * dil // tm
            views.append(a.reshape(-1, dil, BLOCK, w))
            specs.append(pl.BlockSpec((None, dil, tm // dil, w),
                                      lambda i, per_span=per_span: (i // per_span, 0, i % per_span, 0)))
        return views, specs

    oa_v, oa_s = grouped(oa)
    lse_v, lse_s = grouped(lse)
    consts = (expand, lw["wb"], lw["wo"], lw["bg"], lw["ln1_g"], lw["ln1_b"])
    slabs = lambda w: pltpu.VMEM((w // LANE, tm, LANE), F32)
    return pl.pallas_call(
        functools.partial(_merge_kernel, alpha=alpha),
        grid=(t // tm,),
        in_specs=oa_s + lse_s + [row(512), row(512),
                  pl.BlockSpec((tm, 3 * D_MODEL), lambda i: (i, COL_GATE // (3 * D_MODEL))),
                  row(D_MODEL)] + [full(a) for a in consts],
        out_specs=row(D_MODEL),
        out_shape=jax.ShapeDtypeStruct((t, D_MODEL), F32),
        scratch_shapes=[slabs(512), slabs(512), slabs(LANE), slabs(LANE)],
        compiler_params=_cparams(1),
        name="merge",
    )(*oa_v, *lse_v, ob.reshape(t, 512), oc, p2, x2d, *consts)


HALO = 8
FF_CHUNK = 512


def _ffn_kernel(halo_ref, x_ref, wup_ref, cw_ref, cb_ref, wdn_ref, g_ref, b_ref, o_ref,
                ug_ref, uv_ref, *, alpha, tm, per_seq):
    i = pl.program_id(0)
    x = x_ref[...]
    halo = jnp.where(i % per_seq == 0, jnp.zeros_like(halo_ref[...]), halo_ref[...])
    xh = jnp.concatenate([halo, x], axis=0).astype(BF16)
    acc = None
    for c0 in range(0, D_FF, FF_CHUNK):
        cf = min(FF_CHUNK, D_FF - c0)
        parts = []
        for u_ref, base in ((ug_ref, c0), (uv_ref, D_FF + c0)):
            u_ref[:, :cf] = jnp.dot(xh, wup_ref[:, base:base + cf], preferred_element_type=F32)
            y = cb_ref[:, base:base + cf]
            for tap in range(3):
                y = y + u_ref[pl.ds(HALO - 2 + tap, tm), :cf] * cw_ref[tap:tap + 1, base:base + cf]
            parts.append(y)
        act = (parts[0] * jax.nn.sigmoid(parts[0]) * parts[1]).astype(BF16)
        term = jnp.dot(act, wdn_ref[c0:c0 + cf, :], preferred_element_type=F32)
        acc = term if acc is None else acc + term
    o_ref[...] = _layer_norm(alpha * x + acc, g_ref[...], b_ref[...])


def _ffn(x2d, lw, *, alpha, seq, tm=512):
    t = x2d.shape[0]
    per_seq = seq // tm
    full = lambda a: pl.BlockSpec(a.shape, lambda i: (0,) * a.ndim, pipeline_mode=pl.Buffered(1))
    consts = (lw["wup"], lw["cw"], lw["cb"], lw["wdn"], lw["ln2_g"], lw["ln2_b"])
    return pl.pallas_call(
        functools.partial(_ffn_kernel, alpha=alpha, tm=tm, per_seq=per_seq),
        grid=(t // tm,),
        in_specs=[pl.BlockSpec((HALO, D_MODEL), lambda i: (jnp.maximum(i * (tm // HALO) - 1, 0), 0)),
                  pl.BlockSpec((tm, D_MODEL), lambda i: (i, 0))] + [full(a) for a in consts],
        out_specs=pl.BlockSpec((tm, D_MODEL), lambda i: (i, 0)),
        out_shape=jax.ShapeDtypeStruct((t, D_MODEL), F32),
        scratch_shapes=[pltpu.VMEM((tm + HALO, FF_CHUNK), F32),
                        pltpu.VMEM((tm + HALO, FF_CHUNK), F32)],
        compiler_params=_cparams(1),
        name="ffn",
    )(x2d, x2d, *consts)


def _layer_weights(l, w_in, b_gate, q_norm_g, kv_norm_g, w_uq, w_ukv, w_branch, w_out,
                   ln1_g, ln1_b, w_ffn_up, conv_w, conv_b, w_ffn_down, ln2_g, ln2_b):
    wi = w_in[l]
    a_end = 4608
    bq = wi[:, a_end:a_end + 512].reshape(D_MODEL, B_Q_HEADS, HEAD_DIM)
    bq = jnp.stack([bq[:, :4], bq[:, 4:]], axis=2).reshape(D_MODEL, 512)
    bkv = wi[:, 5120:5376]
    cdq = wi[:, 5376:5632]
    ckv = wi[:, 5632:5760]
    kr = wi[:, 5760:5792]
    kr_rot = jnp.concatenate([-kr[:, C_ROPE // 2:], kr[:, :C_ROPE // 2]], axis=1)
    gate = wi[:, 5792:]
    zeros = lambda n: jnp.zeros((D_MODEL, n), wi.dtype)
    wp = jnp.concatenate([gate, wi[:, :a_end], bq, cdq, ckv, kr, kr_rot, zeros(64), bkv, zeros(256)],
                         axis=1).astype(BF16)

    uq = w_uq[l].reshape(C_Q_RANK, C_HEADS, C_NOPE + C_ROPE)
    zq = lambda n: jnp.zeros((C_Q_RANK, C_HEADS, n), uq.dtype)
    half = C_ROPE // 2
    wq1 = jnp.concatenate([uq, zq(LANE - C_NOPE - C_ROPE)], axis=2)
    wq2 = jnp.concatenate([zq(C_NOPE), -uq[:, :, C_NOPE + half:], uq[:, :, C_NOPE:C_NOPE + half],
                           zq(LANE - C_NOPE - C_ROPE)], axis=2)
    ukv = w_ukv[l].reshape(C_KV_RANK, C_HEADS, C_NOPE + C_V)
    wk = jnp.concatenate([ukv[:, :, :C_NOPE], jnp.zeros((C_KV_RANK, C_HEADS, LANE - C_NOPE), ukv.dtype)],
                         axis=2)
    wv = jnp.concatenate([ukv[:, :, C_NOPE:], jnp.zeros((C_KV_RANK, C_HEADS, LANE - C_V), ukv.dtype)],
                         axis=2)

    wb = w_branch[l]
    wb1 = wb[1].reshape(B_Q_HEADS, HEAD_DIM, D_MODEL)
    wb1 = jnp.stack([wb1[:4], wb1[4:]], axis=1).reshape(512, D_MODEL)
    wb = jnp.stack([wb[0], wb1, wb[2]])

    sel_rows = np.zeros((2, 512, LANE), np.float32)
    for j in range(C_ROPE):
        sel_rows[0, C_Q_RANK + C_KV_RANK + j, C_NOPE + j] = 1.0
        sel_rows[1, C_Q_RANK + C_KV_RANK + C_ROPE + j, C_NOPE + j] = 1.0

    return dict(
        wp=wp,
        gq=q_norm_g[l].reshape(1, -1), gkv=kv_norm_g[l].reshape(1, -1),
        wq1=wq1.reshape(C_Q_RANK, -1).astype(BF16), wq2=wq2.reshape(C_Q_RANK, -1).astype(BF16),
        wk=wk.reshape(C_KV_RANK, -1).astype(BF16), wv=wv.reshape(C_KV_RANK, -1).astype(BF16),
        sela=jnp.asarray(sel_rows[0], BF16), selb=jnp.asarray(sel_rows[1], BF16),
        wb=wb.astype(BF16), wo=w_out[l].astype(BF16), bg=b_gate[l].reshape(1, -1),
        ln1_g=ln1_g[l].reshape(1, -1), ln1_b=ln1_b[l].reshape(1, -1),
        wup=w_ffn_up[l].astype(BF16), cw=conv_w[l], cb=conv_b[l].reshape(1, -1),
        wdn=w_ffn_down[l].astype(BF16),
        ln2_g=ln2_g[l].reshape(1, -1), ln2_b=ln2_b[l].reshape(1, -1),
    )


def _rope_tables(seq):
    pos = jnp.arange(seq, dtype=F32)
    inv_freq = ROPE_BASE ** (-jnp.arange(0, C_ROPE, 2, dtype=F32) / C_ROPE)
    ang = pos[:, None] * inv_freq[None, :]
    cos, sin = jnp.cos(ang), jnp.sin(ang)
    pad = jnp.zeros((seq, LANE - C_NOPE - C_ROPE), F32)
    cos_t = jnp.concatenate([jnp.ones((seq, C_NOPE), F32), cos, cos, pad], axis=1)
    sin_t = jnp.concatenate([jnp.zeros((seq, C_NOPE), F32), sin, sin, pad], axis=1)
    return cos_t, sin_t


def _expand_matrix():
    e = np.zeros((LANE, A_HEADS * HEAD_DIM), np.float32)
    for c in range(A_HEADS * HEAD_DIM):
        e[(LANE // A_HEADS) * (c // HEAD_DIM), c] = 1.0
    return jnp.asarray(e, BF16)


def kernel(x, rel_table, w_in, b_gate, sinks, q_norm_g, kv_norm_g, w_uq, w_ukv, w_branch, w_out,
           ln1_g, ln1_b, w_ffn_up, conv_w, conv_b, w_ffn_down, ln2_g, ln2_b):
    batch, seq, d = x.shape
    depth = w_in.shape[0]
    alpha = (2 * depth) ** 0.25
    cos_t, sin_t = _rope_tables(seq)
    expand = _expand_matrix()
    bias = _bias_tables(rel_table)
    x2d = x.reshape(batch * seq, d)
    for l in range(depth):
        lw = _layer_weights(l, w_in, b_gate, q_norm_g, kv_norm_g, w_uq, w_ukv, w_branch, w_out,
                            ln1_g, ln1_b, w_ffn_up, conv_w, conv_b, w_ffn_down, ln2_g, ln2_b)
        x4, x16 = _xperm(x2d)
        p2 = _proj(x2d, x4, x16, lw["wp"])
        sink_slots = sinks[l].reshape(2, 4).T.reshape(-1)
        oa, lse = [], []
        for kind in range(len(A_GROUPS)):
            o_g, lse_g = _band_attention(p2, bias, sink_slots, kind=kind, batch=batch, seq=seq)
            oa.append(o_g)
            lse.append(lse_g)
        ob = _band_attention(p2, bias, sink_slots, kind=len(A_GROUPS), batch=batch, seq=seq)
        q, k, v = _mla_prep(p2, cos_t, sin_t, lw, seq=seq)
        oc = _mla_flash(q, k, v, batch=batch, seq=seq)
        x2d = _merge(oa, lse, ob, oc, p2, x2d, expand, lw, alpha=alpha)
        x2d = _ffn(x2d, lw, alpha=alpha, seq=seq)
    return x2d.reshape(batch, seq, d)
```

```python
import functools
import math

import jax
import jax.numpy as jnp
import numpy as np
from jax import lax
from jax.experimental import pallas as pl
from jax.experimental.pallas import tpu as pltpu

F32 = jnp.float32
BF16 = jnp.bfloat16

D_MODEL = 1024
HEAD_DIM = 64
BLOCK = 128
A_GROUPS = ((128, 1), (512, 4), (2048, 16))
A_HEADS = 8
B_Q_HEADS = 8
B_WINDOW = 128
C_HEADS = 8
C_Q_RANK = 256
C_KV_RANK = 128
C_NOPE = 64
C_ROPE = 32
C_V = 64
ROPE_BASE = 10000.0
REL_BUCKETS = 32
REL_MAX_DIST = 2048
D_FF = 2816
LN_EPS = 1e-5
RMS_EPS = 1e-6
NEG = -1e30
LOG2E = math.log2(math.e)
V_ROWS = 80

COL_GATE = 0
COL_A = 3072
COL_BQ = 7680
COL_C = 8192
COL_BKV = 8704
NP = 9216

LANE = 128
VMEM_LIMIT = 56 * 1024 * 1024

N_KINDS = 4


def _cparams(n_axes):
    return pltpu.CompilerParams(dimension_semantics=("arbitrary",) * n_axes,
                                vmem_limit_bytes=VMEM_LIMIT)


PERM_ROWS = 2048
PROJ_TN = 1536


def _xperm_kernel(*refs):
    n_slab = D_MODEL // LANE
    x_refs, o_refs = refs[:n_slab], refs[n_slab:]
    for c, x_ref in enumerate(x_refs):
        for o_ref, (_, dil) in zip(o_refs, A_GROUPS[1:]):
            span = BLOCK * dil
            for s0 in range(0, PERM_ROWS, span):
                for r in range(dil):
                    rows = x_ref[pl.ds(s0 + r, BLOCK, stride=dil), :]
                    o_ref[s0 + r * BLOCK:s0 + (r + 1) * BLOCK, c * LANE:(c + 1) * LANE] = rows.astype(BF16)


def _xperm(x2d):
    t, k = x2d.shape
    n_out = len(A_GROUPS) - 1
    slab = lambda c: pl.BlockSpec((PERM_ROWS, LANE), lambda i: (i, c))
    return pl.pallas_call(
        _xperm_kernel,
        grid=(t // PERM_ROWS,),
        in_specs=[slab(c) for c in range(k // LANE)],
        out_specs=[pl.BlockSpec((PERM_ROWS, k), lambda i: (i, 0))] * n_out,
        out_shape=[jax.ShapeDtypeStruct((t, k), BF16)] * n_out,
        compiler_params=_cparams(1),
        name="xperm",
    )(*([x2d] * (k // LANE)))


def _proj_kernel(x_ref, x4_ref, x16_ref, w_ref, o_ref, xb_ref):
    j = pl.program_id(1)

    @pl.when(j == 0)
    def _():
        xb_ref[...] = x_ref[...].astype(BF16)

    def emit(lhs_ref):
        o_ref[...] = jnp.dot(lhs_ref[...], w_ref[...], preferred_element_type=F32).astype(o_ref.dtype)

    tile_a1 = COL_A // PROJ_TN + 1
    pl.when(j == tile_a1)(lambda: emit(x4_ref))
    pl.when(j == tile_a1 + 1)(lambda: emit(x16_ref))
    pl.when((j != tile_a1) & (j != tile_a1 + 1))(lambda: emit(xb_ref))


def _proj(x2d, x4, x16, w, tm=1024):
    t, k = x2d.shape
    n = w.shape[1]
    row = pl.BlockSpec((tm, k), lambda i, j: (i, 0))
    return pl.pallas_call(
        _proj_kernel,
        grid=(t // tm, n // PROJ_TN),
        in_specs=[row, row, row, pl.BlockSpec((k, PROJ_TN), lambda i, j: (0, j))],
        out_specs=pl.BlockSpec((tm, PROJ_TN), lambda i, j: (i, j)),
        out_shape=jax.ShapeDtypeStruct((t, n), BF16),
        scratch_shapes=[pltpu.VMEM((tm, k), BF16)],
        compiler_params=_cparams(2),
        name="proj",
    )(x2d, x4, x16, w)


def _t5_bucket(dist):
    n = jnp.maximum(dist, 0)
    max_exact = REL_BUCKETS // 2
    scaled = jnp.log(jnp.maximum(n, 1).astype(F32) / max_exact) / math.log(REL_MAX_DIST / max_exact)
    large = max_exact + (scaled * (REL_BUCKETS - max_exact)).astype(jnp.int32)
    return jnp.where(n < max_exact, n, jnp.minimum(large, REL_BUCKETS - 1))


def _bias_codes():
    qi = jnp.arange(BLOCK)[:, None]
    ki = jnp.arange(2 * BLOCK)[None, :]
    step = BLOCK + qi - ki
    has_prev = ki >= BLOCK
    codes = []
    for kind in range(N_KINDS):
        if kind < len(A_GROUPS):
            dil = A_GROUPS[kind][1]
            band = (step >= 0) & (step <= BLOCK)
        else:
            dil = 1
            band = (step >= 0) & (step < B_WINDOW)
        bucket = _t5_bucket(step * dil)
        codes.append(jnp.stack([jnp.where(band & has_prev, bucket, -1),
                                jnp.where(band, bucket, -1)]))
    return jnp.stack(codes).astype(jnp.int32)


def _bias_kernel(rel_ref, code_ref, o_ref):
    kind = pl.program_id(0)
    code = code_ref[0, 0]
    for slot in range(A_HEADS):
        b_head = len(A_GROUPS) * A_HEADS + slot // 2 + 4 * (slot % 2)
        col = jnp.where(kind < len(A_GROUPS), kind * A_HEADS + slot, b_head)
        acc = jnp.full(code.shape, NEG, F32)
        for b in range(REL_BUCKETS):
            acc = jnp.where(code == b, rel_ref[b, col], acc)
        o_ref[0, 0, slot] = acc


def _bias_tables(rel_table):
    codes = _bias_codes()
    return pl.pallas_call(
        _bias_kernel,
        grid=(N_KINDS, 2),
        in_specs=[pl.BlockSpec(memory_space=pltpu.SMEM),
                  pl.BlockSpec((1, 1, BLOCK, 2 * BLOCK), lambda k, v: (k, v, 0, 0))],
        out_specs=pl.BlockSpec((1, 1, A_HEADS, BLOCK, 2 * BLOCK), lambda k, v: (k, v, 0, 0, 0)),
        out_shape=jax.ShapeDtypeStruct((N_KINDS, 2, A_HEADS, BLOCK, 2 * BLOCK), F32),
        compiler_params=_cparams(2),
        name="bias_tables",
    )(rel_table, codes)


BAND_QB = 4


def _band_kernel(sink_ref, q_ref, kp_ref, kc_ref, vp_ref, vc_ref, bias_ref, o_ref, *lse_refs,
                 shared_kv, with_sink):
    first = pl.program_id(2) == 0
    lane = lax.broadcasted_iota(jnp.int32, (BLOCK, LANE), 1)
    low = lane < HEAD_DIM
    for qb in range(BAND_QB):
        variant = jnp.where(first, 0, 1) if qb == 0 else 1
        lse_tile = jnp.zeros((BLOCK, LANE), F32)
        for pair in range(4):
            cols = slice(pair * LANE, (pair + 1) * LANE)
            kv_cols = slice(0, LANE) if shared_kv else cols
            qp = q_ref[qb, :, cols] * jnp.asarray(HEAD_DIM ** -0.5, BF16)
            k_prev = kp_ref[:, kv_cols] if qb == 0 else kc_ref[qb - 1, :, kv_cols]
            v_prev = vp_ref[:, kv_cols] if qb == 0 else vc_ref[qb - 1, :, kv_cols]
            kcat = jnp.concatenate([k_prev, kc_ref[qb, :, kv_cols]], axis=0)
            vcat = jnp.concatenate([v_prev, vc_ref[qb, :, kv_cols]], axis=0)
            halves = []
            for sub in range(2):
                slot = 2 * pair + sub
                qh = jnp.where(low if sub == 0 else jnp.logical_not(low), qp, jnp.zeros_like(qp))
                s = lax.dot_general(qh, kcat, (((1,), (1,)), ((), ())), preferred_element_type=F32)
                s = s + bias_ref[variant, slot]
                m = jnp.max(s, axis=-1, keepdims=True)
                if with_sink:
                    sink = sink_ref[slot]
                    m = jnp.maximum(m, sink)
                p = jnp.exp(s - m)
                l = jnp.sum(p, axis=-1, keepdims=True)
                if with_sink:
                    l = l + jnp.exp(sink - m)
                o = jnp.dot(p.astype(BF16), vcat, preferred_element_type=F32)
                halves.append(o * (1.0 / l))
                if lse_refs:
                    lse_tile = jnp.where(lane // (LANE // A_HEADS) == slot, m + jnp.log(l), lse_tile)
            o_ref[qb, :, cols] = jnp.where(low, halves[0], halves[1]).astype(o_ref.dtype)
        if lse_refs:
            lse_refs[0][qb] = lse_tile


def _band_attention(p2, bias, sinks, *, kind, batch, seq):
    is_b = kind == len(A_GROUPS)
    dil = 1 if is_b else A_GROUPS[kind][1]
    n_span = seq // (BLOCK * dil)
    view = p2.reshape(batch, n_span, dil, BLOCK, NP)
    if is_b:
        q_blk, k_blk, v_blk, kv_w = COL_BQ // 512, COL_BKV // LANE, COL_BKV // LANE + 1, LANE
    else:
        q_blk = (COL_A + kind * 1536) // 512
        k_blk, v_blk, kv_w = q_blk + 1, q_blk + 2, 512

    def cur(blk, width):
        return pl.BlockSpec((None, BAND_QB, None, BLOCK, width), lambda b, r, n: (b, n, r, 0, blk))

    def prev(blk, width):
        return pl.BlockSpec((None, None, None, BLOCK, width),
                            lambda b, r, n: (b, jnp.maximum(n * BAND_QB - 1, 0), r, 0, blk))

    in_specs = [
        pl.BlockSpec(memory_space=pltpu.SMEM),
        cur(q_blk, 512), prev(k_blk, kv_w), cur(k_blk, kv_w), prev(v_blk, kv_w), cur(v_blk, kv_w),
        pl.BlockSpec((None, 2, A_HEADS, BLOCK, 2 * BLOCK), lambda b, r, n: (kind, 0, 0, 0, 0)),
    ]
    out_specs = [pl.BlockSpec((None, BAND_QB, None, BLOCK, 512), lambda b, r, n: (b, n, r, 0, 0))]
    out_shape = [jax.ShapeDtypeStruct((batch, n_span, dil, BLOCK, 512), BF16)]
    if not is_b:
        out_specs.append(pl.BlockSpec((None, BAND_QB, None, BLOCK, LANE), lambda b, r, n: (b, n, r, 0, 0)))
        out_shape.append(jax.ShapeDtypeStruct((batch, n_span, dil, BLOCK, LANE), F32))
    outs = pl.pallas_call(
        functools.partial(_band_kernel, shared_kv=is_b, with_sink=is_b),
        grid=(batch, dil, n_span // BAND_QB),
        in_specs=in_specs,
        out_specs=out_specs,
        out_shape=out_shape,
        compiler_params=_cparams(3),
        name="band_b" if is_b else f"band_a{kind}",
    )(sinks, view, view, view, view, view, bias)
    return outs[0] if is_b else outs


def _mla_prep_kernel(c_ref, cos_ref, sin_ref, gq_ref, gkv_ref, wq1_ref, wq2_ref, wk_ref, wv_ref,
                     sela_ref, selb_ref, q_ref, k_ref, v_ref):
    c = c_ref[...]
    cq = c[:, :C_Q_RANK].astype(F32)
    ckv = c[:, C_Q_RANK:C_Q_RANK + C_KV_RANK].astype(F32)
    nq = cq * lax.rsqrt(jnp.mean(cq * cq, axis=-1, keepdims=True) + RMS_EPS) * gq_ref[...]
    nkv = ckv * lax.rsqrt(jnp.mean(ckv * ckv, axis=-1, keepdims=True) + RMS_EPS) * gkv_ref[...]
    nq = nq.astype(BF16)
    nkv = nkv.astype(BF16)
    cos = cos_ref[...]
    sin = sin_ref[...]
    cos8 = jnp.tile(cos, (1, C_HEADS))
    sin8 = jnp.tile(sin, (1, C_HEADS))
    q = (jnp.dot(nq, wq1_ref[...], preferred_element_type=F32) * cos8
         + jnp.dot(nq, wq2_ref[...], preferred_element_type=F32) * sin8)
    q_ref[...] = (q * ((C_NOPE + C_ROPE) ** -0.5 * LOG2E)).astype(q_ref.dtype)
    kr = (jnp.dot(c, sela_ref[...], preferred_element_type=F32) * cos
          + jnp.dot(c, selb_ref[...], preferred_element_type=F32) * sin)
    k = jnp.dot(nkv, wk_ref[...], preferred_element_type=F32) + jnp.tile(kr, (1, C_HEADS))
    k_ref[...] = k.astype(k_ref.dtype)
    vt = lax.dot_general(wv_ref[...], nkv, (((1,), (1,)), ((), ())), preferred_element_type=F32)
    row = lax.broadcasted_iota(jnp.int32, vt.shape, 0)
    v_ref[...] = jnp.where(row % V_ROWS == C_V, 1.0, vt).astype(v_ref.dtype)


def _mla_prep(p2, cos_t, sin_t, lw, *, batch, seq, tm=512):
    t = p2.shape[0]
    per_seq = seq // tm
    full = lambda a: pl.BlockSpec(a.shape, lambda i: (0,) * a.ndim)
    consts = (lw["gq"], lw["gkv"], lw["wq1"], lw["wq2"], lw["wk"], lw["wv"], lw["sela"], lw["selb"])
    wide = C_HEADS * LANE
    return pl.pallas_call(
        _mla_prep_kernel,
        grid=(t // tm,),
        in_specs=[pl.BlockSpec((tm, 512), lambda i: (i, COL_C // 512)),
                  pl.BlockSpec((tm, LANE), lambda i: (i % per_seq, 0)),
                  pl.BlockSpec((tm, LANE), lambda i: (i % per_seq, 0))] + [full(a) for a in consts],
        out_specs=[pl.BlockSpec((tm, wide), lambda i: (i, 0)),
                   pl.BlockSpec((tm, wide), lambda i: (i, 0)),
                   pl.BlockSpec((None, C_HEADS * V_ROWS, tm), lambda i: (i // per_seq, 0, i % per_seq))],
        out_shape=[jax.ShapeDtypeStruct((t, wide), BF16),
                   jax.ShapeDtypeStruct((t, wide), BF16),
                   jax.ShapeDtypeStruct((batch, C_HEADS * V_ROWS, seq), BF16)],
        compiler_params=_cparams(1),
        name="mla_prep",
    )(p2, cos_t, sin_t, *consts)


SUBLANES = 8
FLASH_KC = 512


def _mla_flash_kernel(q_ref, k_ref, vt_ref, o_ref, m_ref, acc_ref, *, tq, tk):
    qi = pl.program_id(1)
    ki = pl.program_id(2)

    @pl.when(ki == 0)
    def _():
        m_ref[...] = jnp.full(m_ref.shape, NEG, F32)
        acc_ref[...] = jnp.zeros(acc_ref.shape, F32)

    def step(masked):
        for h in range(C_HEADS):
            hc = slice(h * LANE, (h + 1) * LANE)
            for c0 in range(0, tk, FLASH_KC):
                kc = slice(c0, c0 + FLASH_KC)
                st = lax.dot_general(k_ref[kc, hc], q_ref[:, hc], (((1,), (1,)), ((), ())),
                                     preferred_element_type=F32)
                if masked:
                    key = lax.broadcasted_iota(jnp.int32, (FLASH_KC, tq), 0) + c0
                    qry = lax.broadcasted_iota(jnp.int32, (FLASH_KC, tq), 1)
                    st = jnp.where(qry >= key, st, NEG)
                m_prev = m_ref[h]
                m_new = jnp.maximum(m_prev, jnp.max(st, axis=0, keepdims=True))
                alpha = jnp.exp2(m_prev - m_new)
                p = jnp.exp2(st - jnp.tile(m_new, (FLASH_KC // SUBLANES, 1)))
                pv = jnp.dot(vt_ref[h * V_ROWS:(h + 1) * V_ROWS, kc], p.astype(BF16),
                             preferred_element_type=F32)
                acc_ref[h] = jnp.tile(alpha, (V_ROWS // SUBLANES, 1)) * acc_ref[h] + pv
                m_ref[h] = m_new

    @pl.when(ki < qi)
    def _():
        step(False)

    @pl.when(ki == qi)
    def _():
        step(True)
        for pair in range(C_HEADS // 2):
            halves = []
            for h in (2 * pair, 2 * pair + 1):
                acc = acc_ref[h]
                halves.append(acc[:C_V] * (1.0 / acc[C_V:C_V + 1]))
            o_pair = jnp.concatenate(halves, axis=0).T
            o_ref[:, pair * LANE:(pair + 1) * LANE] = o_pair.astype(o_ref.dtype)


def _mla_flash(q, k, vt, *, batch, seq, tq=512):
    tk = tq
    nq = seq // tq
    wide = C_HEADS * LANE
    q3 = q.reshape(batch, seq, wide)
    k3 = k.reshape(batch, seq, wide)
    out = pl.pallas_call(
        functools.partial(_mla_flash_kernel, tq=tq, tk=tk),
        grid=(batch, nq, nq),
        in_specs=[pl.BlockSpec((None, tq, wide), lambda b, i, j: (b, i, 0)),
                  pl.BlockSpec((None, tk, wide), lambda b, i, j: (b, jnp.minimum(i, j), 0)),
                  pl.BlockSpec((None, C_HEADS * V_ROWS, tk), lambda b, i, j: (b, 0, jnp.minimum(i, j)))],
        out_specs=pl.BlockSpec((None, tq, C_HEADS * C_V), lambda b, i, j: (b, i, 0)),
        out_shape=jax.ShapeDtypeStruct((batch, seq, C_HEADS * C_V), BF16),
        scratch_shapes=[pltpu.VMEM((C_HEADS, SUBLANES, tq), F32),
                        pltpu.VMEM((C_HEADS, V_ROWS, tq), F32)],
        compiler_params=_cparams(3),
        name="mla_flash",
    )(q3, k3, vt)
    return out.reshape(batch * seq, C_HEADS * C_V)


def _layer_norm(y, g, b):
    mu = jnp.mean(y, axis=-1, keepdims=True)
    d = y - mu
    var = jnp.mean(d * d, axis=-1, keepdims=True)
    return d * lax.rsqrt(var + LN_EPS) * g + b


def _token_order(src_ref, tmp_ref, dil):
    if dil == 1:
        return src_ref[...].astype(F32)
    n = src_ref.shape[1]
    n_slab = src_ref.shape[2] // LANE
    for r in range(dil):
        rows = src_ref[r].astype(F32)
        for c in range(n_slab):
            tmp_ref[c, pl.ds(r, n, stride=dil), :] = rows[:, c * LANE:(c + 1) * LANE]
    return jnp.concatenate([tmp_ref[c] for c in range(n_slab)], axis=1)


def _merge_kernel(oa0_ref, oa1_ref, oa2_ref, l0_ref, l1_ref, l2_ref, ob_ref, oc_ref, gate_ref, x_ref,
                  e_ref, wb_ref, wo_ref, bg_ref, g_ref, b_ref, o_ref, ot1_ref, ot2_ref, lt1_ref, lt2_ref,
                  *, alpha):
    dils = [d for _, d in A_GROUPS]
    lses = [_token_order(ref, tmp, d)
            for ref, tmp, d in zip((l0_ref, l1_ref, l2_ref), (None, lt1_ref, lt2_ref), dils)]
    top = jnp.maximum(jnp.maximum(lses[0], lses[1]), lses[2])
    es = [jnp.exp(v - top) for v in lses]
    inv = 1.0 / (es[0] + es[1] + es[2])
    o_a = None
    for e, oa_ref, tmp, d in zip(es, (oa0_ref, oa1_ref, oa2_ref), (None, ot1_ref, ot2_ref), dils):
        w = e * inv
        hi = w.astype(BF16)
        lo = (w - hi.astype(F32)).astype(BF16)
        wide = (jnp.dot(hi, e_ref[...], preferred_element_type=F32)
                + jnp.dot(lo, e_ref[...], preferred_element_type=F32))
        term = wide * _token_order(oa_ref, tmp, d)
        o_a = term if o_a is None else o_a + term
    branches = (o_a.astype(BF16), ob_ref[...], oc_ref[...])
    merged = None
    for i, br in enumerate(branches):
        gate = jax.nn.sigmoid(gate_ref[:, i * D_MODEL:(i + 1) * D_MODEL].astype(F32)
                              + bg_ref[:, i * D_MODEL:(i + 1) * D_MODEL])
        term = gate * jnp.dot(br, wb_ref[i], preferred_element_type=F32)
        merged = term if merged is None else merged + term
    mix = jnp.dot(merged.astype(BF16), wo_ref[...], preferred_element_type=F32)
    o_ref[...] = _layer_norm(alpha * x_ref[...] + mix, g_ref[...], b_ref[...])


def _merge(oa, lse, ob, oc, p2, x2d, expand, lw, *, alpha, tm=512):
    t = x2d.shape[0]
    row = lambda w: pl.BlockSpec((tm, w), lambda i: (i, 0))
    full = lambda a: pl.BlockSpec(a.shape, lambda i: (0,) * a.ndim)

    def grouped(arrs):
        views, specs = [], []
        for a, (_, dil) in zip(arrs, A_GROUPS):
            w = a.shape[-1]
            if dil == 1:
                views.append(a.reshape(t, w))
                specs.append(row(w))
                continue
            per_span = BLOCK * dil // tm
            views.append(a.reshape(-1, dil, BLOCK, w))
            specs.append(pl.BlockSpec((None, dil, tm // dil, w),
                                      lambda i, per_span=per_span: (i // per_span, 0, i % per_span, 0)))
        return views, specs

    oa_v, oa_s = grouped(oa)
    lse_v, lse_s = grouped(lse)
    consts = (expand, lw["wb"], lw["wo"], lw["bg"], lw["ln1_g"], lw["ln1_b"])
    slabs = lambda w: pltpu.VMEM((w // LANE, tm, LANE), F32)
    return pl.pallas_call(
        functools.partial(_merge_kernel, alpha=alpha),
        grid=(t // tm,),
        in_specs=oa_s + lse_s + [row(512), row(512),
                  pl.BlockSpec((tm, 3 * D_MODEL), lambda i: (i, COL_GATE // (3 * D_MODEL))),
                  row(D_MODEL)] + [full(a) for a in consts],
        out_specs=row(D_MODEL),
        out_shape=jax.ShapeDtypeStruct((t, D_MODEL), F32),
        scratch_shapes=[slabs(512), slabs(512), slabs(LANE), slabs(LANE)],
        compiler_params=_cparams(1),
        name="merge",
    )(*oa_v, *lse_v, ob.reshape(t, 512), oc, p2, x2d, *consts)


HALO = 8
FF_CHUNK = 512


def _ffn_kernel(halo_ref, x_ref, wup_ref, cw_ref, cb_ref, wdn_ref, g_ref, b_ref, o_ref,
                ug_ref, uv_ref, *, alpha, tm, per_seq):
    i = pl.program_id(0)
    x = x_ref[...]
    halo = jnp.where(i % per_seq == 0, jnp.zeros_like(halo_ref[...]), halo_ref[...])
    xh = jnp.concatenate([halo, x], axis=0).astype(BF16)
    chunks = [(c0, min(FF_CHUNK, D_FF - c0)) for c0 in range(0, D_FF, FF_CHUNK)]

    def up(c0, cf):
        return [jnp.dot(xh, wup_ref[:, base:base + cf], preferred_element_type=F32)
                for base in (c0, D_FF + c0)]

    acc = None
    u_next = up(*chunks[0])
    for ci, (c0, cf) in enumerate(chunks):
        u_cur = u_next
        if ci + 1 < len(chunks):
            u_next = up(*chunks[ci + 1])
        parts = []
        for u, base in zip(u_cur, (c0, D_FF + c0)):
            y = cb_ref[:, base:base + cf]
            for tap in range(3):
                y = y + u[HALO - 2 + tap:HALO - 2 + tap + tm] * cw_ref[tap:tap + 1, base:base + cf]
            parts.append(y)
        act = (parts[0] * jax.nn.sigmoid(parts[0]) * parts[1]).astype(BF16)
        term = jnp.dot(act, wdn_ref[c0:c0 + cf, :], preferred_element_type=F32)
        acc = term if acc is None else acc + term
    o_ref[...] = _layer_norm(alpha * x + acc, g_ref[...], b_ref[...])


def _ffn(x2d, lw, *, alpha, seq, tm=512):
    t = x2d.shape[0]
    per_seq = seq // tm
    full = lambda a: pl.BlockSpec(a.shape, lambda i: (0,) * a.ndim, pipeline_mode=pl.Buffered(1))
    consts = (lw["wup"], lw["cw"], lw["cb"], lw["wdn"], lw["ln2_g"], lw["ln2_b"])
    return pl.pallas_call(
        functools.partial(_ffn_kernel, alpha=alpha, tm=tm, per_seq=per_seq),
        grid=(t // tm,),
        in_specs=[pl.BlockSpec((HALO, D_MODEL), lambda i: (jnp.maximum(i * (tm // HALO) - 1, 0), 0)),
                  pl.BlockSpec((tm, D_MODEL), lambda i: (i, 0))] + [full(a) for a in consts],
        out_specs=pl.BlockSpec((tm, D_MODEL), lambda i: (i, 0)),
        out_shape=jax.ShapeDtypeStruct((t, D_MODEL), F32),
        scratch_shapes=[pltpu.VMEM((tm + HALO, FF_CHUNK), F32),
                        pltpu.VMEM((tm + HALO, FF_CHUNK), F32)],
        compiler_params=_cparams(1),
        name="ffn",
    )(x2d, x2d, *consts)


def _layer_weights(l, w_in, b_gate, q_norm_g, kv_norm_g, w_uq, w_ukv, w_branch, w_out,
                   ln1_g, ln1_b, w_ffn_up, conv_w, conv_b, w_ffn_down, ln2_g, ln2_b):
    wi = w_in[l]
    a_end = 4608
    bq = wi[:, a_end:a_end + 512].reshape(D_MODEL, B_Q_HEADS, HEAD_DIM)
    bq = jnp.stack([bq[:, :4], bq[:, 4:]], axis=2).reshape(D_MODEL, 512)
    bkv = wi[:, 5120:5376]
    cdq = wi[:, 5376:5632]
    ckv = wi[:, 5632:5760]
    kr = wi[:, 5760:5792]
    kr_rot = jnp.concatenate([-kr[:, C_ROPE // 2:], kr[:, :C_ROPE // 2]], axis=1)
    gate = wi[:, 5792:]
    zeros = lambda n: jnp.zeros((D_MODEL, n), wi.dtype)
    wp = jnp.concatenate([gate, wi[:, :a_end], bq, cdq, ckv, kr, kr_rot, zeros(64), bkv, zeros(256)],
                         axis=1).astype(BF16)

    uq = w_uq[l].reshape(C_Q_RANK, C_HEADS, C_NOPE + C_ROPE)
    zq = lambda n: jnp.zeros((C_Q_RANK, C_HEADS, n), uq.dtype)
    half = C_ROPE // 2
    wq1 = jnp.concatenate([uq, zq(LANE - C_NOPE - C_ROPE)], axis=2)
    wq2 = jnp.concatenate([zq(C_NOPE), -uq[:, :, C_NOPE + half:], uq[:, :, C_NOPE:C_NOPE + half],
                           zq(LANE - C_NOPE - C_ROPE)], axis=2)
    ukv = w_ukv[l].reshape(C_KV_RANK, C_HEADS, C_NOPE + C_V)
    wk = jnp.concatenate([ukv[:, :, :C_NOPE], jnp.zeros((C_KV_RANK, C_HEADS, LANE - C_NOPE), ukv.dtype)],
                         axis=2)
    wv = jnp.concatenate([ukv[:, :, C_NOPE:], jnp.zeros((C_KV_RANK, C_HEADS, V_ROWS - C_V), ukv.dtype)],
                         axis=2).reshape(C_KV_RANK, -1).T

    wb = w_branch[l]
    wb1 = wb[1].reshape(B_Q_HEADS, HEAD_DIM, D_MODEL)
    wb1 = jnp.stack([wb1[:4], wb1[4:]], axis=1).reshape(512, D_MODEL)
    wb = jnp.stack([wb[0], wb1, wb[2]])

    sel_rows = np.zeros((2, 512, LANE), np.float32)
    for j in range(C_ROPE):
        sel_rows[0, C_Q_RANK + C_KV_RANK + j, C_NOPE + j] = 1.0
        sel_rows[1, C_Q_RANK + C_KV_RANK + C_ROPE + j, C_NOPE + j] = 1.0

    return dict(
        wp=wp,
        gq=q_norm_g[l].reshape(1, -1), gkv=kv_norm_g[l].reshape(1, -1),
        wq1=wq1.reshape(C_Q_RANK, -1).astype(BF16), wq2=wq2.reshape(C_Q_RANK, -1).astype(BF16),
        wk=wk.reshape(C_KV_RANK, -1).astype(BF16), wv=wv.astype(BF16),
        sela=jnp.asarray(sel_rows[0], BF16), selb=jnp.asarray(sel_rows[1], BF16),
        wb=wb.astype(BF16), wo=w_out[l].astype(BF16), bg=b_gate[l].reshape(1, -1),
        ln1_g=ln1_g[l].reshape(1, -1), ln1_b=ln1_b[l].reshape(1, -1),
        wup=w_ffn_up[l].astype(BF16), cw=conv_w[l], cb=conv_b[l].reshape(1, -1),
        wdn=w_ffn_down[l].astype(BF16),
        ln2_g=ln2_g[l].reshape(1, -1), ln2_b=ln2_b[l].reshape(1, -1),
    )


def _rope_tables(seq):
    pos = jnp.arange(seq, dtype=F32)
    inv_freq = ROPE_BASE ** (-jnp.arange(0, C_ROPE, 2, dtype=F32) / C_ROPE)
    ang = pos[:, None] * inv_freq[None, :]
    cos, sin = jnp.cos(ang), jnp.sin(ang)
    pad = jnp.zeros((seq, LANE - C_NOPE - C_ROPE), F32)
    cos_t = jnp.concatenate([jnp.ones((seq, C_NOPE), F32), cos, cos, pad], axis=1)
    sin_t = jnp.concatenate([jnp.zeros((seq, C_NOPE), F32), sin, sin, pad], axis=1)
    return cos_t, sin_t


def _expand_matrix():
    e = np.zeros((LANE, A_HEADS * HEAD_DIM), np.float32)
    for c in range(A_HEADS * HEAD_DIM):
        e[(LANE // A_HEADS) * (c // HEAD_DIM), c] = 1.0
    return jnp.asarray(e, BF16)


def kernel(x, rel_table, w_in, b_gate, sinks, q_norm_g, kv_norm_g, w_uq, w_ukv, w_branch, w_out,
           ln1_g, ln1_b, w_ffn_up, conv_w, conv_b, w_ffn_down, ln2_g, ln2_b):
    batch, seq, d = x.shape
    depth = w_in.shape[0]
    alpha = (2 * depth) ** 0.25
    cos_t, sin_t = _rope_tables(seq)
    expand = _expand_matrix()
    bias = _bias_tables(rel_table)
    x2d = x.reshape(batch * seq, d)
    for l in range(depth):
        lw = _layer_weights(l, w_in, b_gate, q_norm_g, kv_norm_g, w_uq, w_ukv, w_branch, w_out,
                            ln1_g, ln1_b, w_ffn_up, conv_w, conv_b, w_ffn_down, ln2_g, ln2_b)
        x4, x16 = _xperm(x2d)
        p2 = _proj(x2d, x4, x16, lw["wp"])
        sink_slots = sinks[l].reshape(2, 4).T.reshape(-1)
        oa, lse = [], []
        for kind in range(len(A_GROUPS)):
            o_g, lse_g = _band_attention(p2, bias, sink_slots, kind=kind, batch=batch, seq=seq)
            oa.append(o_g)
            lse.append(lse_g)
        ob = _band_attention(p2, bias, sink_slots, kind=len(A_GROUPS), batch=batch, seq=seq)
        q, k, vt = _mla_prep(p2, cos_t, sin_t, lw, batch=batch, seq=seq)
        oc = _mla_flash(q, k, vt, batch=batch, seq=seq)
        x2d = _merge(oa, lse, ob, oc, p2, x2d, expand, lw, alpha=alpha)
        x2d = _ffn(x2d, lw, alpha=alpha, seq=seq)
    return x2d.reshape(batch, seq, d)
```

```python
import functools
import math

import jax
import jax.numpy as jnp
import numpy as np
from jax import lax
from jax.experimental import pallas as pl
from jax.experimental.pallas import tpu as pltpu

F32 = jnp.float32
BF16 = jnp.bfloat16

D_MODEL = 1024
HEAD_DIM = 64
BLOCK = 128
A_GROUPS = ((128, 1), (512, 4), (2048, 16))
A_HEADS = 8
B_Q_HEADS = 8
B_WINDOW = 128
C_HEADS = 8
C_Q_RANK = 256
C_KV_RANK = 128
C_NOPE = 64
C_ROPE = 32
C_V = 64
ROPE_BASE = 10000.0
REL_BUCKETS = 32
REL_MAX_DIST = 2048
D_FF = 2816
LN_EPS = 1e-5
RMS_EPS = 1e-6
NEG = -1e30
LOG2E = math.log2(math.e)
FLASH_TQ = 1024
FLASH_TK = 512

COL_GATE = 0
COL_A = 3072
COL_BQ = 4608
COL_C = 5120
COL_BKV = 5632
NP = 6144
A_COLS = 1536

LANE = 128
VMEM_LIMIT = 56 * 1024 * 1024

N_KINDS = 4


def _cparams(n_axes):
    return pltpu.CompilerParams(dimension_semantics=("arbitrary",) * n_axes,
                                vmem_limit_bytes=VMEM_LIMIT)


def _proj_kernel(*refs, dil):
    *x_refs, w_ref, o_ref, xb_ref = refs

    @pl.when(pl.program_id(1) == 0)
    def _():
        if dil == 1:
            xb_ref[...] = x_refs[0][...].astype(BF16)
            return
        span = BLOCK * dil
        for c, x_ref in enumerate(x_refs):
            for s0 in range(0, xb_ref.shape[0], span):
                for r in range(dil):
                    rows = x_ref[pl.ds(s0 + r, BLOCK, stride=dil), :]
                    xb_ref[s0 + r * BLOCK:s0 + (r + 1) * BLOCK, c * LANE:(c + 1) * LANE] = rows.astype(BF16)

    o_ref[...] = jnp.dot(xb_ref[...], w_ref[...], preferred_element_type=F32).astype(o_ref.dtype)


def _proj(x2d, w, *, dil, tm, tn, name):
    t, k = x2d.shape
    n = w.shape[1]
    if dil == 1:
        x_specs = [pl.BlockSpec((tm, k), lambda i, j: (i, 0))]
    else:
        x_specs = [pl.BlockSpec((tm, LANE), lambda i, j, c=c: (i, c)) for c in range(k // LANE)]
    return pl.pallas_call(
        functools.partial(_proj_kernel, dil=dil),
        grid=(t // tm, n // tn),
        in_specs=x_specs + [pl.BlockSpec((k, tn), lambda i, j: (0, j))],
        out_specs=pl.BlockSpec((tm, tn), lambda i, j: (i, j)),
        out_shape=jax.ShapeDtypeStruct((t, n), BF16),
        scratch_shapes=[pltpu.VMEM((tm, k), BF16)],
        compiler_params=_cparams(2),
        name=name,
    )(*([x2d] * len(x_specs)), w)


def _t5_bucket(dist):
    n = jnp.maximum(dist, 0)
    max_exact = REL_BUCKETS // 2
    scaled = jnp.log(jnp.maximum(n, 1).astype(F32) / max_exact) / math.log(REL_MAX_DIST / max_exact)
    large = max_exact + (scaled * (REL_BUCKETS - max_exact)).astype(jnp.int32)
    return jnp.where(n < max_exact, n, jnp.minimum(large, REL_BUCKETS - 1))


def _bias_codes():
    qi = jnp.arange(BLOCK)[:, None]
    ki = jnp.arange(2 * BLOCK)[None, :]
    step = BLOCK + qi - ki
    has_prev = ki >= BLOCK
    codes = []
    for kind in range(N_KINDS):
        if kind < len(A_GROUPS):
            dil = A_GROUPS[kind][1]
            band = (step >= 0) & (step <= BLOCK)
        else:
            dil = 1
            band = (step >= 0) & (step < B_WINDOW)
        bucket = _t5_bucket(step * dil)
        codes.append(jnp.stack([jnp.where(band & has_prev, bucket, -1),
                                jnp.where(band, bucket, -1)]))
    return jnp.stack(codes).astype(jnp.int32)


def _bias_kernel(rel_ref, code_ref, o_ref):
    kind = pl.program_id(0)
    code = code_ref[0, 0]
    for slot in range(A_HEADS):
        b_head = len(A_GROUPS) * A_HEADS + slot // 2 + 4 * (slot % 2)
        col = jnp.where(kind < len(A_GROUPS), kind * A_HEADS + slot, b_head)
        acc = jnp.full(code.shape, NEG, F32)
        for b in range(REL_BUCKETS):
            acc = jnp.where(code == b, rel_ref[b, col], acc)
        o_ref[0, 0, slot] = acc


def _bias_tables(rel_table):
    codes = _bias_codes()
    return pl.pallas_call(
        _bias_kernel,
        grid=(N_KINDS, 2),
        in_specs=[pl.BlockSpec(memory_space=pltpu.SMEM),
                  pl.BlockSpec((1, 1, BLOCK, 2 * BLOCK), lambda k, v: (k, v, 0, 0))],
        out_specs=pl.BlockSpec((1, 1, A_HEADS, BLOCK, 2 * BLOCK), lambda k, v: (k, v, 0, 0, 0)),
        out_shape=jax.ShapeDtypeStruct((N_KINDS, 2, A_HEADS, BLOCK, 2 * BLOCK), F32),
        compiler_params=_cparams(2),
        name="bias_tables",
    )(rel_table, codes)


BAND_QB = 4


def _band_kernel(sink_ref, q_ref, kp_ref, kc_ref, vp_ref, vc_ref, bias_ref, o_ref, *lse_refs,
                 shared_kv, with_sink):
    first = pl.program_id(2) == 0
    lane = lax.broadcasted_iota(jnp.int32, (BLOCK, LANE), 1)
    low = lane < HEAD_DIM
    for qb in range(BAND_QB):
        variant = jnp.where(first, 0, 1) if qb == 0 else 1
        lse_tile = jnp.zeros((BLOCK, LANE), F32)
        for pair in range(4):
            cols = slice(pair * LANE, (pair + 1) * LANE)
            kv_cols = slice(0, LANE) if shared_kv else cols
            qp = q_ref[qb, :, cols] * jnp.asarray(HEAD_DIM ** -0.5, BF16)
            k_prev = kp_ref[:, kv_cols] if qb == 0 else kc_ref[qb - 1, :, kv_cols]
            v_prev = vp_ref[:, kv_cols] if qb == 0 else vc_ref[qb - 1, :, kv_cols]
            kcat = jnp.concatenate([k_prev, kc_ref[qb, :, kv_cols]], axis=0)
            vcat = jnp.concatenate([v_prev, vc_ref[qb, :, kv_cols]], axis=0)
            halves = []
            for sub in range(2):
                slot = 2 * pair + sub
                qh = jnp.where(low if sub == 0 else jnp.logical_not(low), qp, jnp.zeros_like(qp))
                s = lax.dot_general(qh, kcat, (((1,), (1,)), ((), ())), preferred_element_type=F32)
                s = s + bias_ref[variant, slot]
                m = jnp.max(s, axis=-1, keepdims=True)
                if with_sink:
                    sink = sink_ref[slot]
                    m = jnp.maximum(m, sink)
                p = jnp.exp(s - m)
                l = jnp.sum(p, axis=-1, keepdims=True)
                if with_sink:
                    l = l + jnp.exp(sink - m)
                o = jnp.dot(p.astype(BF16), vcat, preferred_element_type=F32)
                halves.append(o * (1.0 / l))
                if lse_refs:
                    lse_tile = jnp.where(lane // (LANE // A_HEADS) == slot, m + jnp.log(l), lse_tile)
            o_ref[qb, :, cols] = jnp.where(low, halves[0], halves[1]).astype(o_ref.dtype)
        if lse_refs:
            lse_refs[0][qb] = lse_tile


def _band_attention(p2, bias, sinks, *, kind, batch, seq):
    is_b = kind == len(A_GROUPS)
    dil = 1 if is_b else A_GROUPS[kind][1]
    n_span = seq // (BLOCK * dil)
    view = p2.reshape(batch, n_span, dil, BLOCK, p2.shape[-1])
    if is_b:
        q_blk, k_blk, v_blk, kv_w = COL_BQ // 512, COL_BKV // LANE, COL_BKV // LANE + 1, LANE
    else:
        q_blk = COL_A // 512 if dil == 1 else 0
        k_blk, v_blk, kv_w = q_blk + 1, q_blk + 2, 512

    def cur(blk, width):
        return pl.BlockSpec((None, BAND_QB, None, BLOCK, width), lambda b, r, n: (b, n, r, 0, blk))

    def prev(blk, width):
        return pl.BlockSpec((None, None, None, BLOCK, width),
                            lambda b, r, n: (b, jnp.maximum(n * BAND_QB - 1, 0), r, 0, blk))

    in_specs = [
        pl.BlockSpec(memory_space=pltpu.SMEM),
        cur(q_blk, 512), prev(k_blk, kv_w), cur(k_blk, kv_w), prev(v_blk, kv_w), cur(v_blk, kv_w),
        pl.BlockSpec((None, 2, A_HEADS, BLOCK, 2 * BLOCK), lambda b, r, n: (kind, 0, 0, 0, 0)),
    ]
    out_specs = [pl.BlockSpec((None, BAND_QB, None, BLOCK, 512), lambda b, r, n: (b, n, r, 0, 0))]
    out_shape = [jax.ShapeDtypeStruct((batch, n_span, dil, BLOCK, 512), BF16)]
    if not is_b:
        out_specs.append(pl.BlockSpec((None, BAND_QB, None, BLOCK, LANE), lambda b, r, n: (b, n, r, 0, 0)))
        out_shape.append(jax.ShapeDtypeStruct((batch, n_span, dil, BLOCK, LANE), F32))
    outs = pl.pallas_call(
        functools.partial(_band_kernel, shared_kv=is_b, with_sink=is_b),
        grid=(batch, dil, n_span // BAND_QB),
        in_specs=in_specs,
        out_specs=out_specs,
        out_shape=out_shape,
        compiler_params=_cparams(3),
        name="band_b" if is_b else f"band_a{kind}",
    )(sinks, view, view, view, view, view, bias)
    return outs[0] if is_b else outs


def _mla_prep_kernel(c_ref, cos_ref, sin_ref, gq_ref, gkv_ref, wq1_ref, wq2_ref, wk_ref, wv_ref,
                     sela_ref, selb_ref, q_ref, k_ref, v_ref):
    c = c_ref[...]
    cq = c[:, :C_Q_RANK].astype(F32)
    ckv = c[:, C_Q_RANK:C_Q_RANK + C_KV_RANK].astype(F32)
    nq = cq * lax.rsqrt(jnp.mean(cq * cq, axis=-1, keepdims=True) + RMS_EPS) * gq_ref[...]
    nkv = ckv * lax.rsqrt(jnp.mean(ckv * ckv, axis=-1, keepdims=True) + RMS_EPS) * gkv_ref[...]
    nq = nq.astype(BF16)
    nkv = nkv.astype(BF16)
    cos = cos_ref[...]
    sin = sin_ref[...]
    cos8 = jnp.tile(cos, (1, C_HEADS))
    sin8 = jnp.tile(sin, (1, C_HEADS))
    q = (jnp.dot(nq, wq1_ref[...], preferred_element_type=F32) * cos8
         + jnp.dot(nq, wq2_ref[...], preferred_element_type=F32) * sin8)
    q_ref[...] = (q * ((C_NOPE + C_ROPE) ** -0.5 * LOG2E)).astype(q_ref.dtype)
    kr = (jnp.dot(c, sela_ref[...], preferred_element_type=F32) * cos
          + jnp.dot(c, selb_ref[...], preferred_element_type=F32) * sin)
    k = jnp.dot(nkv, wk_ref[...], preferred_element_type=F32) + jnp.tile(kr, (1, C_HEADS))
    k_ref[...] = k.astype(k_ref.dtype)
    lane = lax.broadcasted_iota(jnp.int32, (1, C_HEADS * LANE), 1)
    ones_col = jnp.where(lane % LANE == C_V, 1.0, 0.0).astype(F32)
    v_ref[...] = (jnp.dot(nkv, wv_ref[...], preferred_element_type=F32) + ones_col).astype(v_ref.dtype)


def _mla_prep(p2, cos_t, sin_t, lw, *, seq, tm=512):
    t = p2.shape[0]
    per_seq = seq // tm
    full = lambda a: pl.BlockSpec(a.shape, lambda i: (0,) * a.ndim)
    consts = (lw["gq"], lw["gkv"], lw["wq1"], lw["wq2"], lw["wk"], lw["wv"], lw["sela"], lw["selb"])
    wide = C_HEADS * LANE
    return pl.pallas_call(
        _mla_prep_kernel,
        grid=(t // tm,),
        in_specs=[pl.BlockSpec((tm, 512), lambda i: (i, COL_C // 512)),
                  pl.BlockSpec((tm, LANE), lambda i: (i % per_seq, 0)),
                  pl.BlockSpec((tm, LANE), lambda i: (i % per_seq, 0))] + [full(a) for a in consts],
        out_specs=[pl.BlockSpec((tm, wide), lambda i: (i, 0))] * 3,
        out_shape=[jax.ShapeDtypeStruct((t, wide), BF16)] * 3,
        compiler_params=_cparams(1),
        name="mla_prep",
    )(p2, cos_t, sin_t, *consts)


def _mla_flash_kernel(q_ref, k_ref, v_ref, o_ref, m_ref, acc_ref, *, tq, tk):
    qi = pl.program_id(1)
    ki = pl.program_id(2)
    ratio = tq // tk

    @pl.when(ki == 0)
    def _():
        m_ref[...] = jnp.full(m_ref.shape, NEG, F32)
        acc_ref[...] = jnp.zeros(acc_ref.shape, F32)

    def step(masked):
        if masked:
            row = lax.broadcasted_iota(jnp.int32, (tq, tk), 0)
            col = lax.broadcasted_iota(jnp.int32, (tq, tk), 1)
            keep = row - col >= ki * tk - qi * tq
        for h in range(C_HEADS):
            hc = slice(h * LANE, (h + 1) * LANE)
            s = lax.dot_general(q_ref[:, hc], k_ref[:, hc], (((1,), (1,)), ((), ())),
                                preferred_element_type=F32)
            if masked:
                s = jnp.where(keep, s, NEG)
            m_prev = m_ref[h]
            m_new = jnp.maximum(m_prev, jnp.max(s, axis=-1, keepdims=True))
            alpha = jnp.exp2(m_prev - m_new)
            p = jnp.exp2(s - jnp.tile(m_new, (1, tk // LANE)))
            acc_ref[h] = alpha * acc_ref[h] + jnp.dot(p.astype(BF16), v_ref[:, hc],
                                                      preferred_element_type=F32)
            m_ref[h] = m_new

    @pl.when(ki < qi * ratio)
    def _():
        step(False)

    @pl.when((ki >= qi * ratio) & (ki < (qi + 1) * ratio))
    def _():
        step(True)

    @pl.when(ki == (qi + 1) * ratio - 1)
    def _():
        low = lax.broadcasted_iota(jnp.int32, (tq, LANE), 1) < C_V
        for pair in range(C_HEADS // 2):
            halves = []
            for h in (2 * pair, 2 * pair + 1):
                acc = acc_ref[h]
                halves.append(acc * (1.0 / acc[:, C_V:C_V + 1]))
            odd = pltpu.roll(halves[1], C_V, axis=1)
            o_ref[:, pair * LANE:(pair + 1) * LANE] = jnp.where(low, halves[0], odd).astype(o_ref.dtype)


def _mla_flash(q, k, v, *, batch, seq, tq=FLASH_TQ, tk=FLASH_TK):
    ratio = tq // tk
    wide = C_HEADS * LANE
    q3 = q.reshape(batch, seq, wide)
    k3 = k.reshape(batch, seq, wide)
    v3 = v.reshape(batch, seq, wide)
    kv_map = lambda b, i, j: (b, jnp.minimum(j, (i + 1) * ratio - 1), 0)
    out = pl.pallas_call(
        functools.partial(_mla_flash_kernel, tq=tq, tk=tk),
        grid=(batch, seq // tq, seq // tk),
        in_specs=[pl.BlockSpec((None, tq, wide), lambda b, i, j: (b, i, 0)),
                  pl.BlockSpec((None, tk, wide), kv_map),
                  pl.BlockSpec((None, tk, wide), kv_map)],
        out_specs=pl.BlockSpec((None, tq, C_HEADS * C_V), lambda b, i, j: (b, i, 0)),
        out_shape=jax.ShapeDtypeStruct((batch, seq, C_HEADS * C_V), BF16),
        scratch_shapes=[pltpu.VMEM((C_HEADS, tq, LANE), F32),
                        pltpu.VMEM((C_HEADS, tq, LANE), F32)],
        compiler_params=_cparams(3),
        name="mla_flash",
    )(q3, k3, v3)
    return out.reshape(batch * seq, C_HEADS * C_V)


def _layer_norm(y, g, b):
    mu = jnp.mean(y, axis=-1, keepdims=True)
    d = y - mu
    var = jnp.mean(d * d, axis=-1, keepdims=True)
    return d * lax.rsqrt(var + LN_EPS) * g + b


def _token_order(src_ref, tmp_ref, dil):
    if dil == 1:
        return src_ref[...].astype(F32)
    n = src_ref.shape[1]
    n_slab = src_ref.shape[2] // LANE
    for r in range(dil):
        rows = src_ref[r].astype(F32)
        for c in range(n_slab):
            tmp_ref[c, pl.ds(r, n, stride=dil), :] = rows[:, c * LANE:(c + 1) * LANE]
    return jnp.concatenate([tmp_ref[c] for c in range(n_slab)], axis=1)


def _merge_kernel(oa0_ref, oa1_ref, oa2_ref, l0_ref, l1_ref, l2_ref, ob_ref, oc_ref, gate_ref, x_ref,
                  e_ref, wb_ref, wo_ref, bg_ref, g_ref, b_ref, o_ref, ot1_ref, ot2_ref, lt1_ref, lt2_ref,
                  *, alpha):
    dils = [d for _, d in A_GROUPS]
    lses = [_token_order(ref, tmp, d)
            for ref, tmp, d in zip((l0_ref, l1_ref, l2_ref), (None, lt1_ref, lt2_ref), dils)]
    top = jnp.maximum(jnp.maximum(lses[0], lses[1]), lses[2])
    es = [jnp.exp(v - top) for v in lses]
    inv = 1.0 / (es[0] + es[1] + es[2])
    o_a = None
    for e, oa_ref, tmp, d in zip(es, (oa0_ref, oa1_ref, oa2_ref), (None, ot1_ref, ot2_ref), dils):
        w = e * inv
        hi = w.astype(BF16)
        lo = (w - hi.astype(F32)).astype(BF16)
        wide = (jnp.dot(hi, e_ref[...], preferred_element_type=F32)
                + jnp.dot(lo, e_ref[...], preferred_element_type=F32))
        term = wide * _token_order(oa_ref, tmp, d)
        o_a = term if o_a is None else o_a + term
    branches = (o_a.astype(BF16), ob_ref[...], oc_ref[...])
    merged = None
    for i, br in enumerate(branches):
        gate = jax.nn.sigmoid(gate_ref[:, i * D_MODEL:(i + 1) * D_MODEL].astype(F32)
                              + bg_ref[:, i * D_MODEL:(i + 1) * D_MODEL])
        term = gate * jnp.dot(br, wb_ref[i], preferred_element_type=F32)
        merged = term if merged is None else merged + term
    mix = jnp.dot(merged.astype(BF16), wo_ref[...], preferred_element_type=F32)
    o_ref[...] = _layer_norm(alpha * x_ref[...] + mix, g_ref[...], b_ref[...])


def _merge(oa, lse, ob, oc, p2, x2d, expand, lw, *, alpha, tm=512):
    t = x2d.shape[0]
    row = lambda w: pl.BlockSpec((tm, w), lambda i: (i, 0))
    full = lambda a: pl.BlockSpec(a.shape, lambda i: (0,) * a.ndim)

    def grouped(arrs):
        views, specs = [], []
        for a, (_, dil) in zip(arrs, A_GROUPS):
            w = a.shape[-1]
            if dil == 1:
                views.append(a.reshape(t, w))
                specs.append(row(w))
                continue
            per_span = BLOCK * dil // tm
            views.append(a.reshape(-1, dil, BLOCK, w))
            specs.append(pl.BlockSpec((None, dil, tm // dil, w),
                                      lambda i, per_span=per_span: (i // per_span, 0, i % per_span, 0)))
        return views, specs

    oa_v, oa_s = grouped(oa)
    lse_v, lse_s = grouped(lse)
    consts = (expand, lw["wb"], lw["wo"], lw["bg"], lw["ln1_g"], lw["ln1_b"])
    slabs = lambda w: pltpu.VMEM((w // LANE, tm, LANE), F32)
    return pl.pallas_call(
        functools.partial(_merge_kernel, alpha=alpha),
        grid=(t // tm,),
        in_specs=oa_s + lse_s + [row(512), row(512),
                  pl.BlockSpec((tm, 3 * D_MODEL), lambda i: (i, COL_GATE // (3 * D_MODEL))),
                  row(D_MODEL)] + [full(a) for a in consts],
        out_specs=row(D_MODEL),
        out_shape=jax.ShapeDtypeStruct((t, D_MODEL), F32),
        scratch_shapes=[slabs(512), slabs(512), slabs(LANE), slabs(LANE)],
        compiler_params=_cparams(1),
        name="merge",
    )(*oa_v, *lse_v, ob.reshape(t, 512), oc, p2, x2d, *consts)


HALO = 8
FF_CHUNK = 512


def _ffn_kernel(halo_ref, x_ref, wup_ref, cw_ref, cb_ref, wdn_ref, g_ref, b_ref, o_ref,
                ug_ref, uv_ref, *, alpha, tm, per_seq):
    i = pl.program_id(0)
    x = x_ref[...]
    halo = jnp.where(i % per_seq == 0, jnp.zeros_like(halo_ref[...]), halo_ref[...])
    xh = jnp.concatenate([halo, x], axis=0).astype(BF16)
    chunks = [(c0, min(FF_CHUNK, D_FF - c0)) for c0 in range(0, D_FF, FF_CHUNK)]

    def up(c0, cf):
        return [jnp.dot(xh, wup_ref[:, base:base + cf], preferred_element_type=F32)
                for base in (c0, D_FF + c0)]

    acc = None
    u_next = up(*chunks[0])
    for ci, (c0, cf) in enumerate(chunks):
        u_cur = u_next
        if ci + 1 < len(chunks):
            u_next = up(*chunks[ci + 1])
        parts = []
        for u, base in zip(u_cur, (c0, D_FF + c0)):
            y = cb_ref[:, base:base + cf]
            for tap in range(3):
                y = y + u[HALO - 2 + tap:HALO - 2 + tap + tm] * cw_ref[tap:tap + 1, base:base + cf]
            parts.append(y)
        act = (parts[0] * jax.nn.sigmoid(parts[0]) * parts[1]).astype(BF16)
        term = jnp.dot(act, wdn_ref[c0:c0 + cf, :], preferred_element_type=F32)
        acc = term if acc is None else acc + term
    o_ref[...] = _layer_norm(alpha * x + acc, g_ref[...], b_ref[...])


def _ffn(x2d, lw, *, alpha, seq, tm=512):
    t = x2d.shape[0]
    per_seq = seq // tm
    full = lambda a: pl.BlockSpec(a.shape, lambda i: (0,) * a.ndim, pipeline_mode=pl.Buffered(1))
    consts = (lw["wup"], lw["cw"], lw["cb"], lw["wdn"], lw["ln2_g"], lw["ln2_b"])
    return pl.pallas_call(
        functools.partial(_ffn_kernel, alpha=alpha, tm=tm, per_seq=per_seq),
        grid=(t // tm,),
        in_specs=[pl.BlockSpec((HALO, D_MODEL), lambda i: (jnp.maximum(i * (tm // HALO) - 1, 0), 0)),
                  pl.BlockSpec((tm, D_MODEL), lambda i: (i, 0))] + [full(a) for a in consts],
        out_specs=pl.BlockSpec((tm, D_MODEL), lambda i: (i, 0)),
        out_shape=jax.ShapeDtypeStruct((t, D_MODEL), F32),
        scratch_shapes=[pltpu.VMEM((tm + HALO, FF_CHUNK), F32),
                        pltpu.VMEM((tm + HALO, FF_CHUNK), F32)],
        compiler_params=_cparams(1),
        name="ffn",
    )(x2d, x2d, *consts)


def _layer_weights(l, w_in, b_gate, q_norm_g, kv_norm_g, w_uq, w_ukv, w_branch, w_out,
                   ln1_g, ln1_b, w_ffn_up, conv_w, conv_b, w_ffn_down, ln2_g, ln2_b):
    wi = w_in[l]
    a_end = 4608
    bq = wi[:, a_end:a_end + 512].reshape(D_MODEL, B_Q_HEADS, HEAD_DIM)
    bq = jnp.stack([bq[:, :4], bq[:, 4:]], axis=2).reshape(D_MODEL, 512)
    bkv = wi[:, 5120:5376]
    cdq = wi[:, 5376:5632]
    ckv = wi[:, 5632:5760]
    kr = wi[:, 5760:5792]
    kr_rot = jnp.concatenate([-kr[:, C_ROPE // 2:], kr[:, :C_ROPE // 2]], axis=1)
    gate = wi[:, 5792:]
    zeros = lambda n: jnp.zeros((D_MODEL, n), wi.dtype)
    wp = jnp.concatenate([gate, wi[:, :A_COLS], bq, cdq, ckv, kr, kr_rot, zeros(64), bkv, zeros(256)],
                         axis=1).astype(BF16)
    wa = [wi[:, g * A_COLS:(g + 1) * A_COLS].astype(BF16) for g in range(1, len(A_GROUPS))]

    uq = w_uq[l].reshape(C_Q_RANK, C_HEADS, C_NOPE + C_ROPE)
    zq = lambda n: jnp.zeros((C_Q_RANK, C_HEADS, n), uq.dtype)
    half = C_ROPE // 2
    wq1 = jnp.concatenate([uq, zq(LANE - C_NOPE - C_ROPE)], axis=2)
    wq2 = jnp.concatenate([zq(C_NOPE), -uq[:, :, C_NOPE + half:], uq[:, :, C_NOPE:C_NOPE + half],
                           zq(LANE - C_NOPE - C_ROPE)], axis=2)
    ukv = w_ukv[l].reshape(C_KV_RANK, C_HEADS, C_NOPE + C_V)
    wk = jnp.concatenate([ukv[:, :, :C_NOPE], jnp.zeros((C_KV_RANK, C_HEADS, LANE - C_NOPE), ukv.dtype)],
                         axis=2)
    wv = jnp.concatenate([ukv[:, :, C_NOPE:], jnp.zeros((C_KV_RANK, C_HEADS, LANE - C_V), ukv.dtype)],
                         axis=2)

    wb = w_branch[l]
    wb1 = wb[1].reshape(B_Q_HEADS, HEAD_DIM, D_MODEL)
    wb1 = jnp.stack([wb1[:4], wb1[4:]], axis=1).reshape(512, D_MODEL)
    wb = jnp.stack([wb[0], wb1, wb[2]])

    sel_rows = np.zeros((2, 512, LANE), np.float32)
    for j in range(C_ROPE):
        sel_rows[0, C_Q_RANK + C_KV_RANK + j, C_NOPE + j] = 1.0
        sel_rows[1, C_Q_RANK + C_KV_RANK + C_ROPE + j, C_NOPE + j] = 1.0

    return dict(
        wp=wp, wa=wa,
        gq=q_norm_g[l].reshape(1, -1), gkv=kv_norm_g[l].reshape(1, -1),
        wq1=wq1.reshape(C_Q_RANK, -1).astype(BF16), wq2=wq2.reshape(C_Q_RANK, -1).astype(BF16),
        wk=wk.reshape(C_KV_RANK, -1).astype(BF16), wv=wv.reshape(C_KV_RANK, -1).astype(BF16),
        sela=jnp.asarray(sel_rows[0], BF16), selb=jnp.asarray(sel_rows[1], BF16),
        wb=wb.astype(BF16), wo=w_out[l].astype(BF16), bg=b_gate[l].reshape(1, -1),
        ln1_g=ln1_g[l].reshape(1, -1), ln1_b=ln1_b[l].reshape(1, -1),
        wup=w_ffn_up[l].astype(BF16), cw=conv_w[l], cb=conv_b[l].reshape(1, -1),
        wdn=w_ffn_down[l].astype(BF16),
        ln2_g=ln2_g[l].reshape(1, -1), ln2_b=ln2_b[l].reshape(1, -1),
    )


def _rope_tables(seq):
    pos = jnp.arange(seq, dtype=F32)
    inv_freq = ROPE_BASE ** (-jnp.arange(0, C_ROPE, 2, dtype=F32) / C_ROPE)
    ang = pos[:, None] * inv_freq[None, :]
    cos, sin = jnp.cos(ang), jnp.sin(ang)
    pad = jnp.zeros((seq, LANE - C_NOPE - C_ROPE), F32)
    cos_t = jnp.concatenate([jnp.ones((seq, C_NOPE), F32), cos, cos, pad], axis=1)
    sin_t = jnp.concatenate([jnp.zeros((seq, C_NOPE), F32), sin, sin, pad], axis=1)
    return cos_t, sin_t


def _expand_matrix():
    e = np.zeros((LANE, A_HEADS * HEAD_DIM), np.float32)
    for c in range(A_HEADS * HEAD_DIM):
        e[(LANE // A_HEADS) * (c // HEAD_DIM), c] = 1.0
    return jnp.asarray(e, BF16)


def kernel(x, rel_table, w_in, b_gate, sinks, q_norm_g, kv_norm_g, w_uq, w_ukv, w_branch, w_out,
           ln1_g, ln1_b, w_ffn_up, conv_w, conv_b, w_ffn_down, ln2_g, ln2_b):
    batch, seq, d = x.shape
    depth = w_in.shape[0]
    alpha = (2 * depth) ** 0.25
    cos_t, sin_t = _rope_tables(seq)
    expand = _expand_matrix()
    bias = _bias_tables(rel_table)
    x2d = x.reshape(batch * seq, d)
    for l in range(depth):
        lw = _layer_weights(l, w_in, b_gate, q_norm_g, kv_norm_g, w_uq, w_ukv, w_branch, w_out,
                            ln1_g, ln1_b, w_ffn_up, conv_w, conv_b, w_ffn_down, ln2_g, ln2_b)
        p2 = _proj(x2d, lw["wp"], dil=1, tm=1024, tn=A_COLS, name="proj")
        p_dil = []
        for g, (w, (_, dil)) in enumerate(zip(lw["wa"], A_GROUPS[1:])):
            tm = max(1024, BLOCK * dil)
            tn = A_COLS * 1024 // tm
            p_dil.append(_proj(x2d, w, dil=dil, tm=tm, tn=tn, name=f"proj_a{g + 1}"))
        sink_slots = sinks[l].reshape(2, 4).T.reshape(-1)
        oa, lse = [], []
        for kind in range(len(A_GROUPS)):
            src = p2 if kind == 0 else p_dil[kind - 1]
            o_g, lse_g = _band_attention(src, bias, sink_slots, kind=kind, batch=batch, seq=seq)
            oa.append(o_g)
            lse.append(lse_g)
        ob = _band_attention(p2, bias, sink_slots, kind=len(A_GROUPS), batch=batch, seq=seq)
        q, k, v = _mla_prep(p2, cos_t, sin_t, lw, seq=seq)
        oc = _mla_flash(q, k, v, batch=batch, seq=seq)
        x2d = _merge(oa, lse, ob, oc, p2, x2d, expand, lw, alpha=alpha)
        x2d = _ffn(x2d, lw, alpha=alpha, seq=seq)
    return x2d.reshape(batch, seq, d)
```

```python
import functools
import math

import jax
import jax.numpy as jnp
import numpy as np
from jax import lax
from jax.experimental import pallas as pl
from jax.experimental.pallas import tpu as pltpu

F32 = jnp.float32
BF16 = jnp.bfloat16

D_MODEL = 1024
HEAD_DIM = 64
BLOCK = 128
A_GROUPS = ((128, 1), (512, 4), (2048, 16))
A_HEADS = 8
B_Q_HEADS = 8
B_WINDOW = 128
C_HEADS = 8
C_Q_RANK = 256
C_KV_RANK = 128
C_NOPE = 64
C_ROPE = 32
C_V = 64
ROPE_BASE = 10000.0
REL_BUCKETS = 32
REL_MAX_DIST = 2048
D_FF = 2816
LN_EPS = 1e-5
RMS_EPS = 1e-6
NEG = -1e30
LOG2E = math.log2(math.e)
FLASH_TQ = 1024
FLASH_TK = 1024

COL_GATE = 0
COL_A = 3072
COL_BQ = 4608
COL_C = 5120
COL_BKV = 5632
NP = 6144
A_COLS = 1536

LANE = 128
VMEM_LIMIT = 56 * 1024 * 1024

N_KINDS = 4


def _cparams(n_axes):
    return pltpu.CompilerParams(dimension_semantics=("arbitrary",) * n_axes,
                                vmem_limit_bytes=VMEM_LIMIT)


def _proj_kernel(*refs, dil):
    *x_refs, w_ref, o_ref, xb_ref = refs

    @pl.when(pl.program_id(1) == 0)
    def _():
        if dil == 1:
            xb_ref[...] = x_refs[0][...].astype(BF16)
            return
        span = BLOCK * dil
        for c, x_ref in enumerate(x_refs):
            for s0 in range(0, xb_ref.shape[0], span):
                for r in range(dil):
                    rows = x_ref[pl.ds(s0 + r, BLOCK, stride=dil), :]
                    xb_ref[s0 + r * BLOCK:s0 + (r + 1) * BLOCK, c * LANE:(c + 1) * LANE] = rows.astype(BF16)

    o_ref[...] = jnp.dot(xb_ref[...], w_ref[...], preferred_element_type=F32).astype(o_ref.dtype)


def _proj(x2d, w, *, dil, tm, tn, name):
    t, k = x2d.shape
    n = w.shape[1]
    if dil == 1:
        x_specs = [pl.BlockSpec((tm, k), lambda i, j: (i, 0))]
    else:
        x_specs = [pl.BlockSpec((tm, LANE), lambda i, j, c=c: (i, c)) for c in range(k // LANE)]
    return pl.pallas_call(
        functools.partial(_proj_kernel, dil=dil),
        grid=(t // tm, n // tn),
        in_specs=x_specs + [pl.BlockSpec((k, tn), lambda i, j: (0, j))],
        out_specs=pl.BlockSpec((tm, tn), lambda i, j: (i, j)),
        out_shape=jax.ShapeDtypeStruct((t, n), BF16),
        scratch_shapes=[pltpu.VMEM((tm, k), BF16)],
        compiler_params=_cparams(2),
        name=name,
    )(*([x2d] * len(x_specs)), w)


def _t5_bucket(dist):
    n = jnp.maximum(dist, 0)
    max_exact = REL_BUCKETS // 2
    scaled = jnp.log(jnp.maximum(n, 1).astype(F32) / max_exact) / math.log(REL_MAX_DIST / max_exact)
    large = max_exact + (scaled * (REL_BUCKETS - max_exact)).astype(jnp.int32)
    return jnp.where(n < max_exact, n, jnp.minimum(large, REL_BUCKETS - 1))


def _bias_codes():
    qi = jnp.arange(BLOCK)[:, None]
    ki = jnp.arange(2 * BLOCK)[None, :]
    step = BLOCK + qi - ki
    has_prev = ki >= BLOCK
    codes = []
    for kind in range(N_KINDS):
        if kind < len(A_GROUPS):
            dil = A_GROUPS[kind][1]
            band = (step >= 0) & (step <= BLOCK)
        else:
            dil = 1
            band = (step >= 0) & (step < B_WINDOW)
        bucket = _t5_bucket(step * dil)
        codes.append(jnp.stack([jnp.where(band & has_prev, bucket, -1),
                                jnp.where(band, bucket, -1)]))
    return jnp.stack(codes).astype(jnp.int32)


def _bias_kernel(rel_ref, code_ref, o_ref):
    kind = pl.program_id(0)
    code = code_ref[0, 0]
    for slot in range(A_HEADS):
        b_head = len(A_GROUPS) * A_HEADS + slot // 2 + 4 * (slot % 2)
        col = jnp.where(kind < len(A_GROUPS), kind * A_HEADS + slot, b_head)
        acc = jnp.full(code.shape, NEG, F32)
        for b in range(REL_BUCKETS):
            acc = jnp.where(code == b, rel_ref[b, col], acc)
        o_ref[0, 0, slot] = acc


def _bias_tables(rel_table):
    codes = _bias_codes()
    return pl.pallas_call(
        _bias_kernel,
        grid=(N_KINDS, 2),
        in_specs=[pl.BlockSpec(memory_space=pltpu.SMEM),
                  pl.BlockSpec((1, 1, BLOCK, 2 * BLOCK), lambda k, v: (k, v, 0, 0))],
        out_specs=pl.BlockSpec((1, 1, A_HEADS, BLOCK, 2 * BLOCK), lambda k, v: (k, v, 0, 0, 0)),
        out_shape=jax.ShapeDtypeStruct((N_KINDS, 2, A_HEADS, BLOCK, 2 * BLOCK), F32),
        compiler_params=_cparams(2),
        name="bias_tables",
    )(rel_table, codes)


BAND_QB = 4


def _band_kernel(sink_ref, q_ref, kp_ref, kc_ref, vp_ref, vc_ref, bias_ref, o_ref, *lse_refs,
                 shared_kv, with_sink):
    first = pl.program_id(2) == 0
    lane = lax.broadcasted_iota(jnp.int32, (BLOCK, LANE), 1)
    low = lane < HEAD_DIM
    for qb in range(BAND_QB):
        variant = jnp.where(first, 0, 1) if qb == 0 else 1
        lse_tile = jnp.zeros((BLOCK, LANE), F32)
        for pair in range(4):
            cols = slice(pair * LANE, (pair + 1) * LANE)
            kv_cols = slice(0, LANE) if shared_kv else cols
            qp = q_ref[qb, :, cols] * jnp.asarray(HEAD_DIM ** -0.5, BF16)
            k_prev = kp_ref[:, kv_cols] if qb == 0 else kc_ref[qb - 1, :, kv_cols]
            v_prev = vp_ref[:, kv_cols] if qb == 0 else vc_ref[qb - 1, :, kv_cols]
            kcat = jnp.concatenate([k_prev, kc_ref[qb, :, kv_cols]], axis=0)
            vcat = jnp.concatenate([v_prev, vc_ref[qb, :, kv_cols]], axis=0)
            zero = jnp.zeros_like(qp)
            q2 = jnp.concatenate([jnp.where(low, qp, zero), jnp.where(low, zero, qp)], axis=0)
            s = lax.dot_general(q2, kcat, (((1,), (1,)), ((), ())), preferred_element_type=F32)
            s = s + bias_ref[variant, pair]
            m = jnp.max(s, axis=-1, keepdims=True)
            if with_sink:
                head_a = lax.broadcasted_iota(jnp.int32, (2 * BLOCK, 1), 0) < BLOCK
                sink = jnp.where(head_a, sink_ref[2 * pair], sink_ref[2 * pair + 1])
                m = jnp.maximum(m, sink)
            p = jnp.exp(s - m)
            l = jnp.sum(p, axis=-1, keepdims=True)
            if with_sink:
                l = l + jnp.exp(sink - m)
            o = jnp.dot(p.astype(BF16), vcat, preferred_element_type=F32) * (1.0 / l)
            o_ref[qb, :, cols] = jnp.where(low, o[:BLOCK], o[BLOCK:]).astype(o_ref.dtype)
            if lse_refs:
                lse = m + jnp.log(l)
                for sub in range(2):
                    lse_tile = jnp.where(lane // (LANE // A_HEADS) == 2 * pair + sub,
                                         lse[sub * BLOCK:(sub + 1) * BLOCK], lse_tile)
        if lse_refs:
            lse_refs[0][qb] = lse_tile


def _band_attention(p2, bias, sinks, *, kind, batch, seq):
    is_b = kind == len(A_GROUPS)
    dil = 1 if is_b else A_GROUPS[kind][1]
    n_span = seq // (BLOCK * dil)
    view = p2.reshape(batch, n_span, dil, BLOCK, p2.shape[-1])
    if is_b:
        q_blk, k_blk, v_blk, kv_w = COL_BQ // 512, COL_BKV // LANE, COL_BKV // LANE + 1, LANE
    else:
        q_blk = COL_A // 512 if dil == 1 else 0
        k_blk, v_blk, kv_w = q_blk + 1, q_blk + 2, 512

    def cur(blk, width):
        return pl.BlockSpec((None, BAND_QB, None, BLOCK, width), lambda b, r, n: (b, n, r, 0, blk))

    def prev(blk, width):
        return pl.BlockSpec((None, None, None, BLOCK, width),
                            lambda b, r, n: (b, jnp.maximum(n * BAND_QB - 1, 0), r, 0, blk))

    in_specs = [
        pl.BlockSpec(memory_space=pltpu.SMEM),
        cur(q_blk, 512), prev(k_blk, kv_w), cur(k_blk, kv_w), prev(v_blk, kv_w), cur(v_blk, kv_w),
        pl.BlockSpec((None, 2, A_HEADS // 2, 2 * BLOCK, 2 * BLOCK), lambda b, r, n: (kind, 0, 0, 0, 0)),
    ]
    bias = bias.reshape(N_KINDS, 2, A_HEADS // 2, 2 * BLOCK, 2 * BLOCK)
    out_specs = [pl.BlockSpec((None, BAND_QB, None, BLOCK, 512), lambda b, r, n: (b, n, r, 0, 0))]
    out_shape = [jax.ShapeDtypeStruct((batch, n_span, dil, BLOCK, 512), BF16)]
    if not is_b:
        out_specs.append(pl.BlockSpec((None, BAND_QB, None, BLOCK, LANE), lambda b, r, n: (b, n, r, 0, 0)))
        out_shape.append(jax.ShapeDtypeStruct((batch, n_span, dil, BLOCK, LANE), F32))
    outs = pl.pallas_call(
        functools.partial(_band_kernel, shared_kv=is_b, with_sink=is_b),
        grid=(batch, dil, n_span // BAND_QB),
        in_specs=in_specs,
        out_specs=out_specs,
        out_shape=out_shape,
        compiler_params=_cparams(3),
        name="band_b" if is_b else f"band_a{kind}",
    )(sinks, view, view, view, view, view, bias)
    return outs[0] if is_b else outs


def _mla_prep_kernel(c_ref, cos_ref, sin_ref, gq_ref, gkv_ref, wq1_ref, wq2_ref, wk_ref, wv_ref,
                     sela_ref, selb_ref, q_ref, k_ref, v_ref):
    c = c_ref[...]
    cq = c[:, :C_Q_RANK].astype(F32)
    ckv = c[:, C_Q_RANK:C_Q_RANK + C_KV_RANK].astype(F32)
    nq = cq * lax.rsqrt(jnp.mean(cq * cq, axis=-1, keepdims=True) + RMS_EPS) * gq_ref[...]
    nkv = ckv * lax.rsqrt(jnp.mean(ckv * ckv, axis=-1, keepdims=True) + RMS_EPS) * gkv_ref[...]
    nq = nq.astype(BF16)
    nkv = nkv.astype(BF16)
    cos = cos_ref[...]
    sin = sin_ref[...]
    cos8 = jnp.tile(cos, (1, C_HEADS))
    sin8 = jnp.tile(sin, (1, C_HEADS))
    q = (jnp.dot(nq, wq1_ref[...], preferred_element_type=F32) * cos8
         + jnp.dot(nq, wq2_ref[...], preferred_element_type=F32) * sin8)
    q_ref[...] = (q * ((C_NOPE + C_ROPE) ** -0.5 * LOG2E)).astype(q_ref.dtype)
    kr = (jnp.dot(c, sela_ref[...], preferred_element_type=F32) * cos
          + jnp.dot(c, selb_ref[...], preferred_element_type=F32) * sin)
    k = jnp.dot(nkv, wk_ref[...], preferred_element_type=F32) + jnp.tile(kr, (1, C_HEADS))
    k_ref[...] = k.astype(k_ref.dtype)
    lane = lax.broadcasted_iota(jnp.int32, (1, C_HEADS * LANE), 1)
    ones_col = jnp.where(lane % LANE == C_V, 1.0, 0.0).astype(F32)
    v_ref[...] = (jnp.dot(nkv, wv_ref[...], preferred_element_type=F32) + ones_col).astype(v_ref.dtype)


def _mla_prep(p2, cos_t, sin_t, lw, *, seq, tm=512):
    t = p2.shape[0]
    per_seq = seq // tm
    full = lambda a: pl.BlockSpec(a.shape, lambda i: (0,) * a.ndim)
    consts = (lw["gq"], lw["gkv"], lw["wq1"], lw["wq2"], lw["wk"], lw["wv"], lw["sela"], lw["selb"])
    wide = C_HEADS * LANE
    return pl.pallas_call(
        _mla_prep_kernel,
        grid=(t // tm,),
        in_specs=[pl.BlockSpec((tm, 512), lambda i: (i, COL_C // 512)),
                  pl.BlockSpec((tm, LANE), lambda i: (i % per_seq, 0)),
                  pl.BlockSpec((tm, LANE), lambda i: (i % per_seq, 0))] + [full(a) for a in consts],
        out_specs=[pl.BlockSpec((tm, wide), lambda i: (i, 0))] * 3,
        out_shape=[jax.ShapeDtypeStruct((t, wide), BF16)] * 3,
        compiler_params=_cparams(1),
        name="mla_prep",
    )(p2, cos_t, sin_t, *consts)


def _mla_flash_kernel(q_ref, k_ref, v_ref, o_ref, m_ref, acc_ref, *, tq, tk):
    qi = pl.program_id(1)
    ki = pl.program_id(2)
    ratio = tq // tk

    @pl.when(ki == 0)
    def _():
        m_ref[...] = jnp.full(m_ref.shape, NEG, F32)
        acc_ref[...] = jnp.zeros(acc_ref.shape, F32)

    def step(masked):
        if masked:
            row = lax.broadcasted_iota(jnp.int32, (tq, tk), 0)
            col = lax.broadcasted_iota(jnp.int32, (tq, tk), 1)
            keep = row - col >= ki * tk - qi * tq
        for h in range(C_HEADS):
            hc = slice(h * LANE, (h + 1) * LANE)
            s = lax.dot_general(q_ref[:, hc], k_ref[:, hc], (((1,), (1,)), ((), ())),
                                preferred_element_type=F32)
            if masked:
                s = jnp.where(keep, s, NEG)
            m_prev = m_ref[h]
            m_new = jnp.maximum(m_prev, jnp.max(s, axis=-1, keepdims=True))
            alpha = jnp.exp2(m_prev - m_new)
            p = jnp.exp2(s - jnp.tile(m_new, (1, tk // LANE)))
            acc_ref[h] = alpha * acc_ref[h] + jnp.dot(p.astype(BF16), v_ref[:, hc],
                                                      preferred_element_type=F32)
            m_ref[h] = m_new

    @pl.when(ki < qi * ratio)
    def _():
        step(False)

    @pl.when((ki >= qi * ratio) & (ki < (qi + 1) * ratio))
    def _():
        step(True)

    @pl.when(ki == (qi + 1) * ratio - 1)
    def _():
        low = lax.broadcasted_iota(jnp.int32, (tq, LANE), 1) < C_V
        for pair in range(C_HEADS // 2):
            halves = []
            for h in (2 * pair, 2 * pair + 1):
                acc = acc_ref[h]
                halves.append(acc * (1.0 / acc[:, C_V:C_V + 1]))
            odd = pltpu.roll(halves[1], C_V, axis=1)
            o_ref[:, pair * LANE:(pair + 1) * LANE] = jnp.where(low, halves[0], odd).astype(o_ref.dtype)


def _mla_flash(q, k, v, *, batch, seq, tq=FLASH_TQ, tk=FLASH_TK):
    ratio = tq // tk
    wide = C_HEADS * LANE
    q3 = q.reshape(batch, seq, wide)
    k3 = k.reshape(batch, seq, wide)
    v3 = v.reshape(batch, seq, wide)
    kv_map = lambda b, i, j: (b, jnp.minimum(j, (i + 1) * ratio - 1), 0)
    out = pl.pallas_call(
        functools.partial(_mla_flash_kernel, tq=tq, tk=tk),
        grid=(batch, seq // tq, seq // tk),
        in_specs=[pl.BlockSpec((None, tq, wide), lambda b, i, j: (b, i, 0)),
                  pl.BlockSpec((None, tk, wide), kv_map),
                  pl.BlockSpec((None, tk, wide), kv_map)],
        out_specs=pl.BlockSpec((None, tq, C_HEADS * C_V), lambda b, i, j: (b, i, 0)),
        out_shape=jax.ShapeDtypeStruct((batch, seq, C_HEADS * C_V), BF16),
        scratch_shapes=[pltpu.VMEM((C_HEADS, tq, LANE), F32),
                        pltpu.VMEM((C_HEADS, tq, LANE), F32)],
        compiler_params=_cparams(3),
        name="mla_flash",
    )(q3, k3, v3)
    return out.reshape(batch * seq, C_HEADS * C_V)


def _layer_norm(y, g, b):
    mu = jnp.mean(y, axis=-1, keepdims=True)
    d = y - mu
    var = jnp.mean(d * d, axis=-1, keepdims=True)
    return d * lax.rsqrt(var + LN_EPS) * g + b


def _token_order(src_ref, tmp_ref, dil):
    if dil == 1:
        return src_ref[...].astype(F32)
    n = src_ref.shape[1]
    n_slab = src_ref.shape[2] // LANE
    for r in range(dil):
        rows = src_ref[r].astype(F32)
        for c in range(n_slab):
            tmp_ref[c, pl.ds(r, n, stride=dil), :] = rows[:, c * LANE:(c + 1) * LANE]
    return jnp.concatenate([tmp_ref[c] for c in range(n_slab)], axis=1)


def _merge_kernel(oa0_ref, oa1_ref, oa2_ref, l0_ref, l1_ref, l2_ref, ob_ref, oc_ref, gate_ref, x_ref,
                  e_ref, wb_ref, wo_ref, bg_ref, g_ref, b_ref, o_ref, ot1_ref, ot2_ref, lt1_ref, lt2_ref,
                  *, alpha):
    dils = [d for _, d in A_GROUPS]
    lses = [_token_order(ref, tmp, d)
            for ref, tmp, d in zip((l0_ref, l1_ref, l2_ref), (None, lt1_ref, lt2_ref), dils)]
    top = jnp.maximum(jnp.maximum(lses[0], lses[1]), lses[2])
    es = [jnp.exp(v - top) for v in lses]
    inv = 1.0 / (es[0] + es[1] + es[2])
    o_a = None
    for e, oa_ref, tmp, d in zip(es, (oa0_ref, oa1_ref, oa2_ref), (None, ot1_ref, ot2_ref), dils):
        w = e * inv
        hi = w.astype(BF16)
        lo = (w - hi.astype(F32)).astype(BF16)
        wide = (jnp.dot(hi, e_ref[...], preferred_element_type=F32)
                + jnp.dot(lo, e_ref[...], preferred_element_type=F32))
        term = wide * _token_order(oa_ref, tmp, d)
        o_a = term if o_a is None else o_a + term
    branches = (o_a.astype(BF16), ob_ref[...], oc_ref[...])
    merged = None
    for i, br in enumerate(branches):
        gate = jax.nn.sigmoid(gate_ref[:, i * D_MODEL:(i + 1) * D_MODEL].astype(F32)
                              + bg_ref[:, i * D_MODEL:(i + 1) * D_MODEL])
        term = gate * jnp.dot(br, wb_ref[i], preferred_element_type=F32)
        merged = term if merged is None else merged + term
    mix = jnp.dot(merged.astype(BF16), wo_ref[...], preferred_element_type=F32)
    o_ref[...] = _layer_norm(alpha * x_ref[...] + mix, g_ref[...], b_ref[...])


def _merge(oa, lse, ob, oc, p2, x2d, expand, lw, *, alpha, tm=512):
    t = x2d.shape[0]
    row = lambda w: pl.BlockSpec((tm, w), lambda i: (i, 0))
    full = lambda a: pl.BlockSpec(a.shape, lambda i: (0,) * a.ndim)

    def grouped(arrs):
        views, specs = [], []
        for a, (_, dil) in zip(arrs, A_GROUPS):
            w = a.shape[-1]
            if dil == 1:
                views.append(a.reshape(t, w))
                specs.append(row(w))
                continue
            per_span = BLOCK * dil // tm
            views.append(a.reshape(-1, dil, BLOCK, w))
            specs.append(pl.BlockSpec((None, dil, tm // dil, w),
                                      lambda i, per_span=per_span: (i // per_span, 0, i % per_span, 0)))
        return views, specs

    oa_v, oa_s = grouped(oa)
    lse_v, lse_s = grouped(lse)
    consts = (expand, lw["wb"], lw["wo"], lw["bg"], lw["ln1_g"], lw["ln1_b"])
    slabs = lambda w: pltpu.VMEM((w // LANE, tm, LANE), F32)
    return pl.pallas_call(
        functools.partial(_merge_kernel, alpha=alpha),
        grid=(t // tm,),
        in_specs=oa_s + lse_s + [row(512), row(512),
                  pl.BlockSpec((tm, 3 * D_MODEL), lambda i: (i, COL_GATE // (3 * D_MODEL))),
                  row(D_MODEL)] + [full(a) for a in consts],
        out_specs=row(D_MODEL),
        out_shape=jax.ShapeDtypeStruct((t, D_MODEL), F32),
        scratch_shapes=[slabs(512), slabs(512), slabs(LANE), slabs(LANE)],
        compiler_params=_cparams(1),
        name="merge",
    )(*oa_v, *lse_v, ob.reshape(t, 512), oc, p2, x2d, *consts)


HALO = 8
FF_CHUNK = 512


def _ffn_kernel(halo_ref, x_ref, wup_ref, cw_ref, cb_ref, wdn_ref, g_ref, b_ref, o_ref,
                ug_ref, uv_ref, *, alpha, tm, per_seq):
    i = pl.program_id(0)
    x = x_ref[...]
    halo = jnp.where(i % per_seq == 0, jnp.zeros_like(halo_ref[...]), halo_ref[...])
    xh = jnp.concatenate([halo, x], axis=0).astype(BF16)
    chunks = [(c0, min(FF_CHUNK, D_FF - c0)) for c0 in range(0, D_FF, FF_CHUNK)]

    def up(c0, cf):
        return [jnp.dot(xh, wup_ref[:, base:base + cf], preferred_element_type=F32)
                for base in (c0, D_FF + c0)]

    acc = None
    u_next = up(*chunks[0])
    for ci, (c0, cf) in enumerate(chunks):
        u_cur = u_next
        if ci + 1 < len(chunks):
            u_next = up(*chunks[ci + 1])
        parts = []
        for u, base in zip(u_cur, (c0, D_FF + c0)):
            y = cb_ref[:, base:base + cf]
            for tap in range(3):
                y = y + u[HALO - 2 + tap:HALO - 2 + tap + tm] * cw_ref[tap:tap + 1, base:base + cf]
            parts.append(y)
        act = (parts[0] * jax.nn.sigmoid(parts[0]) * parts[1]).astype(BF16)
        term = jnp.dot(act, wdn_ref[c0:c0 + cf, :], preferred_element_type=F32)
        acc = term if acc is None else acc + term
    o_ref[...] = _layer_norm(alpha * x + acc, g_ref[...], b_ref[...])


def _ffn(x2d, lw, *, alpha, seq, tm=512):
    t = x2d.shape[0]
    per_seq = seq // tm
    full = lambda a: pl.BlockSpec(a.shape, lambda i: (0,) * a.ndim, pipeline_mode=pl.Buffered(1))
    consts = (lw["wup"], lw["cw"], lw["cb"], lw["wdn"], lw["ln2_g"], lw["ln2_b"])
    return pl.pallas_call(
        functools.partial(_ffn_kernel, alpha=alpha, tm=tm, per_seq=per_seq),
        grid=(t // tm,),
        in_specs=[pl.BlockSpec((HALO, D_MODEL), lambda i: (jnp.maximum(i * (tm // HALO) - 1, 0), 0)),
                  pl.BlockSpec((tm, D_MODEL), lambda i: (i, 0))] + [full(a) for a in consts],
        out_specs=pl.BlockSpec((tm, D_MODEL), lambda i: (i, 0)),
        out_shape=jax.ShapeDtypeStruct((t, D_MODEL), F32),
        scratch_shapes=[pltpu.VMEM((tm + HALO, FF_CHUNK), F32),
                        pltpu.VMEM((tm + HALO, FF_CHUNK), F32)],
        compiler_params=_cparams(1),
        name="ffn",
    )(x2d, x2d, *consts)


def _layer_weights(l, w_in, b_gate, q_norm_g, kv_norm_g, w_uq, w_ukv, w_branch, w_out,
                   ln1_g, ln1_b, w_ffn_up, conv_w, conv_b, w_ffn_down, ln2_g, ln2_b):
    wi = w_in[l]
    a_end = 4608
    bq = wi[:, a_end:a_end + 512].reshape(D_MODEL, B_Q_HEADS, HEAD_DIM)
    bq = jnp.stack([bq[:, :4], bq[:, 4:]], axis=2).reshape(D_MODEL, 512)
    bkv = wi[:, 5120:5376]
    cdq = wi[:, 5376:5632]
    ckv = wi[:, 5632:5760]
    kr = wi[:, 5760:5792]
    kr_rot = jnp.concatenate([-kr[:, C_ROPE // 2:], kr[:, :C_ROPE // 2]], axis=1)
    gate = wi[:, 5792:]
    zeros = lambda n: jnp.zeros((D_MODEL, n), wi.dtype)
    wp = jnp.concatenate([gate, wi[:, :A_COLS], bq, cdq, ckv, kr, kr_rot, zeros(64), bkv, zeros(256)],
                         axis=1).astype(BF16)
    wa = [wi[:, g * A_COLS:(g + 1) * A_COLS].astype(BF16) for g in range(1, len(A_GROUPS))]

    uq = w_uq[l].reshape(C_Q_RANK, C_HEADS, C_NOPE + C_ROPE)
    zq = lambda n: jnp.zeros((C_Q_RANK, C_HEADS, n), uq.dtype)
    half = C_ROPE // 2
    wq1 = jnp.concatenate([uq, zq(LANE - C_NOPE - C_ROPE)], axis=2)
    wq2 = jnp.concatenate([zq(C_NOPE), -uq[:, :, C_NOPE + half:], uq[:, :, C_NOPE:C_NOPE + half],
                           zq(LANE - C_NOPE - C_ROPE)], axis=2)
    ukv = w_ukv[l].reshape(C_KV_RANK, C_HEADS, C_NOPE + C_V)
    wk = jnp.concatenate([ukv[:, :, :C_NOPE], jnp.zeros((C_KV_RANK, C_HEADS, LANE - C_NOPE), ukv.dtype)],
                         axis=2)
    wv = jnp.concatenate([ukv[:, :, C_NOPE:], jnp.zeros((C_KV_RANK, C_HEADS, LANE - C_V), ukv.dtype)],
                         axis=2)

    wb = w_branch[l]
    wb1 = wb[1].reshape(B_Q_HEADS, HEAD_DIM, D_MODEL)
    wb1 = jnp.stack([wb1[:4], wb1[4:]], axis=1).reshape(512, D_MODEL)
    wb = jnp.stack([wb[0], wb1, wb[2]])

    sel_rows = np.zeros((2, 512, LANE), np.float32)
    for j in range(C_ROPE):
        sel_rows[0, C_Q_RANK + C_KV_RANK + j, C_NOPE + j] = 1.0
        sel_rows[1, C_Q_RANK + C_KV_RANK + C_ROPE + j, C_NOPE + j] = 1.0

    return dict(
        wp=wp, wa=wa,
        gq=q_norm_g[l].reshape(1, -1), gkv=kv_norm_g[l].reshape(1, -1),
        wq1=wq1.reshape(C_Q_RANK, -1).astype(BF16), wq2=wq2.reshape(C_Q_RANK, -1).astype(BF16),
        wk=wk.reshape(C_KV_RANK, -1).astype(BF16), wv=wv.reshape(C_KV_RANK, -1).astype(BF16),
        sela=jnp.asarray(sel_rows[0], BF16), selb=jnp.asarray(sel_rows[1], BF16),
        wb=wb.astype(BF16), wo=w_out[l].astype(BF16), bg=b_gate[l].reshape(1, -1),
        ln1_g=ln1_g[l].reshape(1, -1), ln1_b=ln1_b[l].reshape(1, -1),
        wup=w_ffn_up[l].astype(BF16), cw=conv_w[l], cb=conv_b[l].reshape(1, -1),
        wdn=w_ffn_down[l].astype(BF16),
        ln2_g=ln2_g[l].reshape(1, -1), ln2_b=ln2_b[l].reshape(1, -1),
    )


def _rope_tables(seq):
    pos = jnp.arange(seq, dtype=F32)
    inv_freq = ROPE_BASE ** (-jnp.arange(0, C_ROPE, 2, dtype=F32) / C_ROPE)
    ang = pos[:, None] * inv_freq[None, :]
    cos, sin = jnp.cos(ang), jnp.sin(ang)
    pad = jnp.zeros((seq, LANE - C_NOPE - C_ROPE), F32)
    cos_t = jnp.concatenate([jnp.ones((seq, C_NOPE), F32), cos, cos, pad], axis=1)
    sin_t = jnp.concatenate([jnp.zeros((seq, C_NOPE), F32), sin, sin, pad], axis=1)
    return cos_t, sin_t


def _expand_matrix():
    e = np.zeros((LANE, A_HEADS * HEAD_DIM), np.float32)
    for c in range(A_HEADS * HEAD_DIM):
        e[(LANE // A_HEADS) * (c // HEAD_DIM), c] = 1.0
    return jnp.asarray(e, BF16)


def kernel(x, rel_table, w_in, b_gate, sinks, q_norm_g, kv_norm_g, w_uq, w_ukv, w_branch, w_out,
           ln1_g, ln1_b, w_ffn_up, conv_w, conv_b, w_ffn_down, ln2_g, ln2_b):
    batch, seq, d = x.shape
    depth = w_in.shape[0]
    alpha = (2 * depth) ** 0.25
    cos_t, sin_t = _rope_tables(seq)
    expand = _expand_matrix()
    bias = _bias_tables(rel_table)
    x2d = x.reshape(batch * seq, d)
    for l in range(depth):
        lw = _layer_weights(l, w_in, b_gate, q_norm_g, kv_norm_g, w_uq, w_ukv, w_branch, w_out,
                            ln1_g, ln1_b, w_ffn_up, conv_w, conv_b, w_ffn_down, ln2_g, ln2_b)
        p2 = _proj(x2d, lw["wp"], dil=1, tm=1024, tn=A_COLS, name="proj")
        p_dil = []
        for g, (w, (_, dil)) in enumerate(zip(lw["wa"], A_GROUPS[1:])):
            tm = max(1024, BLOCK * dil)
            tn = A_COLS * 1024 // tm
            p_dil.append(_proj(x2d, w, dil=dil, tm=tm, tn=tn, name=f"proj_a{g + 1}"))
        sink_slots = sinks[l].reshape(2, 4).T.reshape(-1)
        oa, lse = [], []
        for kind in range(len(A_GROUPS)):
            src = p2 if kind == 0 else p_dil[kind - 1]
            o_g, lse_g = _band_attention(src, bias, sink_slots, kind=kind, batch=batch, seq=seq)
            oa.append(o_g)
            lse.append(lse_g)
        ob = _band_attention(p2, bias, sink_slots, kind=len(A_GROUPS), batch=batch, seq=seq)
        q, k, v = _mla_prep(p2, cos_t, sin_t, lw, seq=seq)
        oc = _mla_flash(q, k, v, batch=batch, seq=seq)
        x2d = _merge(oa, lse, ob, oc, p2, x2d, expand, lw, alpha=alpha)
        x2d = _ffn(x2d, lw, alpha=alpha, seq=seq)
    return x2d.reshape(batch, seq, d)
```

```python
import functools
import math

import jax
import jax.numpy as jnp
import numpy as np
from jax import lax
from jax.experimental import pallas as pl
from jax.experimental.pallas import tpu as pltpu

F32 = jnp.float32
BF16 = jnp.bfloat16

D_MODEL = 1024
HEAD_DIM = 64
BLOCK = 128
A_GROUPS = ((128, 1), (512, 4), (2048, 16))
A_HEADS = 8
B_Q_HEADS = 8
B_WINDOW = 128
C_HEADS = 8
C_Q_RANK = 256
C_KV_RANK = 128
C_NOPE = 64
C_ROPE = 32
C_V = 64
ROPE_BASE = 10000.0
REL_BUCKETS = 32
REL_MAX_DIST = 2048
D_FF = 2816
LN_EPS = 1e-5
RMS_EPS = 1e-6
NEG = -1e30
LOG2E = math.log2(math.e)
FLASH_BLK = 1024
FLASH_DIAG = 512

COL_GATE = 0
COL_A = 3072
COL_BQ = 4608
COL_C = 5120
COL_BKV = 5632
NP = 6144
A_COLS = 1536

LANE = 128
VMEM_LIMIT = 56 * 1024 * 1024

N_KINDS = 4


def _cparams(n_axes):
    return pltpu.CompilerParams(dimension_semantics=("arbitrary",) * n_axes,
                                vmem_limit_bytes=VMEM_LIMIT)


def _proj_kernel(*refs, dil):
    *x_refs, w_ref, o_ref, xb_ref = refs

    @pl.when(pl.program_id(1) == 0)
    def _():
        if dil == 1:
            xb_ref[...] = x_refs[0][...].astype(BF16)
            return
        span = BLOCK * dil
        for c, x_ref in enumerate(x_refs):
            for s0 in range(0, xb_ref.shape[0], span):
                for r in range(dil):
                    rows = x_ref[pl.ds(s0 + r, BLOCK, stride=dil), :]
                    xb_ref[s0 + r * BLOCK:s0 + (r + 1) * BLOCK, c * LANE:(c + 1) * LANE] = rows.astype(BF16)

    o_ref[...] = jnp.dot(xb_ref[...], w_ref[...], preferred_element_type=F32).astype(o_ref.dtype)


def _proj(x2d, w, *, dil, tm, tn, name):
    t, k = x2d.shape
    n = w.shape[1]
    if dil == 1:
        x_specs = [pl.BlockSpec((tm, k), lambda i, j: (i, 0))]
    else:
        x_specs = [pl.BlockSpec((tm, LANE), lambda i, j, c=c: (i, c)) for c in range(k // LANE)]
    return pl.pallas_call(
        functools.partial(_proj_kernel, dil=dil),
        grid=(t // tm, n // tn),
        in_specs=x_specs + [pl.BlockSpec((k, tn), lambda i, j: (0, j))],
        out_specs=pl.BlockSpec((tm, tn), lambda i, j: (i, j)),
        out_shape=jax.ShapeDtypeStruct((t, n), BF16),
        scratch_shapes=[pltpu.VMEM((tm, k), BF16)],
        compiler_params=_cparams(2),
        name=name,
    )(*([x2d] * len(x_specs)), w)


def _t5_bucket(dist):
    n = jnp.maximum(dist, 0)
    max_exact = REL_BUCKETS // 2
    scaled = jnp.log(jnp.maximum(n, 1).astype(F32) / max_exact) / math.log(REL_MAX_DIST / max_exact)
    large = max_exact + (scaled * (REL_BUCKETS - max_exact)).astype(jnp.int32)
    return jnp.where(n < max_exact, n, jnp.minimum(large, REL_BUCKETS - 1))


def _bias_codes():
    qi = jnp.arange(BLOCK)[:, None]
    ki = jnp.arange(2 * BLOCK)[None, :]
    step = BLOCK + qi - ki
    has_prev = ki >= BLOCK
    codes = []
    for kind in range(N_KINDS):
        if kind < len(A_GROUPS):
            dil = A_GROUPS[kind][1]
            band = (step >= 0) & (step <= BLOCK)
        else:
            dil = 1
            band = (step >= 0) & (step < B_WINDOW)
        bucket = _t5_bucket(step * dil)
        codes.append(jnp.stack([jnp.where(band & has_prev, bucket, -1),
                                jnp.where(band, bucket, -1)]))
    return jnp.stack(codes).astype(jnp.int32)


def _bias_kernel(rel_ref, code_ref, o_ref):
    kind = pl.program_id(0)
    code = code_ref[0, 0]
    for slot in range(A_HEADS):
        b_head = len(A_GROUPS) * A_HEADS + slot // 2 + 4 * (slot % 2)
        col = jnp.where(kind < len(A_GROUPS), kind * A_HEADS + slot, b_head)
        acc = jnp.full(code.shape, NEG, F32)
        for b in range(REL_BUCKETS):
            acc = jnp.where(code == b, rel_ref[b, col], acc)
        o_ref[0, 0, slot] = acc


def _bias_tables(rel_table):
    codes = _bias_codes()
    return pl.pallas_call(
        _bias_kernel,
        grid=(N_KINDS, 2),
        in_specs=[pl.BlockSpec(memory_space=pltpu.SMEM),
                  pl.BlockSpec((1, 1, BLOCK, 2 * BLOCK), lambda k, v: (k, v, 0, 0))],
        out_specs=pl.BlockSpec((1, 1, A_HEADS, BLOCK, 2 * BLOCK), lambda k, v: (k, v, 0, 0, 0)),
        out_shape=jax.ShapeDtypeStruct((N_KINDS, 2, A_HEADS, BLOCK, 2 * BLOCK), F32),
        compiler_params=_cparams(2),
        name="bias_tables",
    )(rel_table, codes)


BAND_QB = 8


def _band_kernel(sink_ref, q_ref, kp_ref, kc_ref, vp_ref, vc_ref, bias_ref, o_ref, *lse_refs,
                 shared_kv, with_sink):
    first = pl.program_id(2) == 0
    lane = lax.broadcasted_iota(jnp.int32, (BLOCK, LANE), 1)
    low = lane < HEAD_DIM
    for qb in range(q_ref.shape[0]):
        variant = jnp.where(first, 0, 1) if qb == 0 else 1
        lse_tile = jnp.zeros((BLOCK, LANE), F32)
        for pair in range(4):
            cols = slice(pair * LANE, (pair + 1) * LANE)
            kv_cols = slice(0, LANE) if shared_kv else cols
            qp = q_ref[qb, :, cols] * jnp.asarray(HEAD_DIM ** -0.5, BF16)
            k_prev = kp_ref[:, kv_cols] if qb == 0 else kc_ref[qb - 1, :, kv_cols]
            v_prev = vp_ref[:, kv_cols] if qb == 0 else vc_ref[qb - 1, :, kv_cols]
            kcat = jnp.concatenate([k_prev, kc_ref[qb, :, kv_cols]], axis=0)
            vcat = jnp.concatenate([v_prev, vc_ref[qb, :, kv_cols]], axis=0)
            zero = jnp.zeros_like(qp)
            q2 = jnp.concatenate([jnp.where(low, qp, zero), jnp.where(low, zero, qp)], axis=0)
            s = lax.dot_general(q2, kcat, (((1,), (1,)), ((), ())), preferred_element_type=F32)
            s = s + bias_ref[variant, pair]
            m = jnp.max(s, axis=-1, keepdims=True)
            if with_sink:
                head_a = lax.broadcasted_iota(jnp.int32, (2 * BLOCK, 1), 0) < BLOCK
                sink = jnp.where(head_a, sink_ref[2 * pair], sink_ref[2 * pair + 1])
                m = jnp.maximum(m, sink)
            p = jnp.exp(s - m)
            l = jnp.sum(p, axis=-1, keepdims=True)
            if with_sink:
                l = l + jnp.exp(sink - m)
            o = jnp.dot(p.astype(BF16), vcat, preferred_element_type=F32) * (1.0 / l)
            o_ref[qb, :, cols] = jnp.where(low, o[:BLOCK], o[BLOCK:]).astype(o_ref.dtype)
            if lse_refs:
                lse = m + jnp.log(l)
                for sub in range(2):
                    lse_tile = jnp.where(lane // (LANE // A_HEADS) == 2 * pair + sub,
                                         lse[sub * BLOCK:(sub + 1) * BLOCK], lse_tile)
        if lse_refs:
            lse_refs[0][qb] = lse_tile


def _band_attention(p2, bias, sinks, *, kind, batch, seq):
    is_b = kind == len(A_GROUPS)
    dil = 1 if is_b else A_GROUPS[kind][1]
    n_span = seq // (BLOCK * dil)
    view = p2.reshape(batch, n_span, dil, BLOCK, p2.shape[-1])
    if is_b:
        q_blk, k_blk, v_blk, kv_w = COL_BQ // 512, COL_BKV // LANE, COL_BKV // LANE + 1, LANE
    else:
        q_blk = COL_A // 512 if dil == 1 else 0
        k_blk, v_blk, kv_w = q_blk + 1, q_blk + 2, 512

    n_qb = min(BAND_QB, n_span)

    def cur(blk, width):
        return pl.BlockSpec((None, n_qb, None, BLOCK, width), lambda b, r, n: (b, n, r, 0, blk))

    def prev(blk, width):
        return pl.BlockSpec((None, None, None, BLOCK, width),
                            lambda b, r, n: (b, jnp.maximum(n * n_qb - 1, 0), r, 0, blk))

    in_specs = [
        pl.BlockSpec(memory_space=pltpu.SMEM),
        cur(q_blk, 512), prev(k_blk, kv_w), cur(k_blk, kv_w), prev(v_blk, kv_w), cur(v_blk, kv_w),
        pl.BlockSpec((None, 2, A_HEADS // 2, 2 * BLOCK, 2 * BLOCK), lambda b, r, n: (kind, 0, 0, 0, 0)),
    ]
    bias = bias.reshape(N_KINDS, 2, A_HEADS // 2, 2 * BLOCK, 2 * BLOCK)
    out_specs = [pl.BlockSpec((None, n_qb, None, BLOCK, 512), lambda b, r, n: (b, n, r, 0, 0))]
    out_shape = [jax.ShapeDtypeStruct((batch, n_span, dil, BLOCK, 512), BF16)]
    if not is_b:
        out_specs.append(pl.BlockSpec((None, n_qb, None, BLOCK, LANE), lambda b, r, n: (b, n, r, 0, 0)))
        out_shape.append(jax.ShapeDtypeStruct((batch, n_span, dil, BLOCK, LANE), F32))
    outs = pl.pallas_call(
        functools.partial(_band_kernel, shared_kv=is_b, with_sink=is_b),
        grid=(batch, dil, n_span // n_qb),
        in_specs=in_specs,
        out_specs=out_specs,
        out_shape=out_shape,
        compiler_params=_cparams(3),
        name="band_b" if is_b else f"band_a{kind}",
    )(sinks, view, view, view, view, view, bias)
    return outs[0] if is_b else outs


def _mla_prep_kernel(c_ref, cos_ref, sin_ref, gq_ref, gkv_ref, wq1_ref, wq2_ref, wk_ref, wv_ref,
                     sela_ref, selb_ref, q_ref, k_ref, v_ref):
    c = c_ref[...]
    cq = c[:, :C_Q_RANK].astype(F32)
    ckv = c[:, C_Q_RANK:C_Q_RANK + C_KV_RANK].astype(F32)
    nq = cq * lax.rsqrt(jnp.mean(cq * cq, axis=-1, keepdims=True) + RMS_EPS) * gq_ref[...]
    nkv = ckv * lax.rsqrt(jnp.mean(ckv * ckv, axis=-1, keepdims=True) + RMS_EPS) * gkv_ref[...]
    nq = nq.astype(BF16)
    nkv = nkv.astype(BF16)
    cos = cos_ref[...]
    sin = sin_ref[...]
    cos8 = jnp.tile(cos, (1, C_HEADS))
    sin8 = jnp.tile(sin, (1, C_HEADS))
    q = (jnp.dot(nq, wq1_ref[...], preferred_element_type=F32) * cos8
         + jnp.dot(nq, wq2_ref[...], preferred_element_type=F32) * sin8)
    q_ref[...] = (q * ((C_NOPE + C_ROPE) ** -0.5 * LOG2E)).astype(q_ref.dtype)
    kr = (jnp.dot(c, sela_ref[...], preferred_element_type=F32) * cos
          + jnp.dot(c, selb_ref[...], preferred_element_type=F32) * sin)
    k = jnp.dot(nkv, wk_ref[...], preferred_element_type=F32) + jnp.tile(kr, (1, C_HEADS))
    k_ref[...] = k.astype(k_ref.dtype)
    lane = lax.broadcasted_iota(jnp.int32, (1, C_HEADS * LANE), 1)
    ones_col = jnp.where(lane % LANE == C_V, 1.0, 0.0).astype(F32)
    v_ref[...] = (jnp.dot(nkv, wv_ref[...], preferred_element_type=F32) + ones_col).astype(v_ref.dtype)


def _mla_prep(p2, cos_t, sin_t, lw, *, seq, tm=512):
    t = p2.shape[0]
    per_seq = seq // tm
    full = lambda a: pl.BlockSpec(a.shape, lambda i: (0,) * a.ndim)
    consts = (lw["gq"], lw["gkv"], lw["wq1"], lw["wq2"], lw["wk"], lw["wv"], lw["sela"], lw["selb"])
    wide = C_HEADS * LANE
    return pl.pallas_call(
        _mla_prep_kernel,
        grid=(t // tm,),
        in_specs=[pl.BlockSpec((tm, 512), lambda i: (i, COL_C // 512)),
                  pl.BlockSpec((tm, LANE), lambda i: (i % per_seq, 0)),
                  pl.BlockSpec((tm, LANE), lambda i: (i % per_seq, 0))] + [full(a) for a in consts],
        out_specs=[pl.BlockSpec((tm, wide), lambda i: (i, 0))] * 3,
        out_shape=[jax.ShapeDtypeStruct((t, wide), BF16)] * 3,
        compiler_params=_cparams(1),
        name="mla_prep",
    )(p2, cos_t, sin_t, *consts)


def _mla_flash_kernel(q_ref, k_ref, v_ref, o_ref, m_ref, acc_ref, *, blk):
    qi = pl.program_id(1)
    ki = pl.program_id(2)

    @pl.when(ki == 0)
    def _():
        m_ref[...] = jnp.full(m_ref.shape, NEG, F32)
        acc_ref[...] = jnp.zeros(acc_ref.shape, F32)

    def update(h, rows, n_keys, diagonal):
        hc = slice(h * LANE, (h + 1) * LANE)
        s = lax.dot_general(q_ref[rows, hc], k_ref[:n_keys, hc], (((1,), (1,)), ((), ())),
                            preferred_element_type=F32)
        if diagonal:
            row = lax.broadcasted_iota(jnp.int32, s.shape, 0) + rows.start
            col = lax.broadcasted_iota(jnp.int32, s.shape, 1)
            s = jnp.where(row >= col, s, NEG)
        m_prev = m_ref[h, rows]
        m_new = jnp.maximum(m_prev, jnp.max(s, axis=-1, keepdims=True))
        alpha = jnp.exp2(m_prev - m_new)
        p = jnp.exp2(s - jnp.tile(m_new, (1, n_keys // LANE)))
        acc_ref[h, rows] = alpha * acc_ref[h, rows] + jnp.dot(p.astype(BF16), v_ref[:n_keys, hc],
                                                              preferred_element_type=F32)
        m_ref[h, rows] = m_new

    @pl.when(ki < qi)
    def _():
        for h in range(C_HEADS):
            update(h, slice(0, blk), blk, False)

    @pl.when(ki == qi)
    def _():
        for h in range(C_HEADS):
            for r0 in range(0, blk, FLASH_DIAG):
                update(h, slice(r0, r0 + FLASH_DIAG), r0 + FLASH_DIAG, True)
        low = lax.broadcasted_iota(jnp.int32, (blk, LANE), 1) < C_V
        for pair in range(C_HEADS // 2):
            halves = []
            for h in (2 * pair, 2 * pair + 1):
                acc = acc_ref[h]
                halves.append(acc * (1.0 / acc[:, C_V:C_V + 1]))
            odd = pltpu.roll(halves[1], C_V, axis=1)
            o_ref[:, pair * LANE:(pair + 1) * LANE] = jnp.where(low, halves[0], odd).astype(o_ref.dtype)


def _mla_flash(q, k, v, *, batch, seq, blk=FLASH_BLK):
    wide = C_HEADS * LANE
    q3 = q.reshape(batch, seq, wide)
    k3 = k.reshape(batch, seq, wide)
    v3 = v.reshape(batch, seq, wide)
    kv_map = lambda b, i, j: (b, jnp.minimum(j, i), 0)
    out = pl.pallas_call(
        functools.partial(_mla_flash_kernel, blk=blk),
        grid=(batch, seq // blk, seq // blk),
        in_specs=[pl.BlockSpec((None, blk, wide), lambda b, i, j: (b, i, 0)),
                  pl.BlockSpec((None, blk, wide), kv_map),
                  pl.BlockSpec((None, blk, wide), kv_map)],
        out_specs=pl.BlockSpec((None, blk, C_HEADS * C_V), lambda b, i, j: (b, i, 0)),
        out_shape=jax.ShapeDtypeStruct((batch, seq, C_HEADS * C_V), BF16),
        scratch_shapes=[pltpu.VMEM((C_HEADS, blk, LANE), F32),
                        pltpu.VMEM((C_HEADS, blk, LANE), F32)],
        compiler_params=_cparams(3),
        name="mla_flash",
    )(q3, k3, v3)
    return out.reshape(batch * seq, C_HEADS * C_V)


def _layer_norm(y, g, b):
    mu = jnp.mean(y, axis=-1, keepdims=True)
    d = y - mu
    var = jnp.mean(d * d, axis=-1, keepdims=True)
    return d * lax.rsqrt(var + LN_EPS) * g + b


def _token_order(src_ref, tmp_ref, dil):
    if dil == 1:
        return src_ref[...].astype(F32)
    n = src_ref.shape[1]
    n_slab = src_ref.shape[2] // LANE
    for r in range(dil):
        rows = src_ref[r].astype(F32)
        for c in range(n_slab):
            tmp_ref[c, pl.ds(r, n, stride=dil), :] = rows[:, c * LANE:(c + 1) * LANE]
    return jnp.concatenate([tmp_ref[c] for c in range(n_slab)], axis=1)


def _merge_kernel(oa0_ref, oa1_ref, oa2_ref, l0_ref, l1_ref, l2_ref, ob_ref, oc_ref, gate_ref, x_ref,
                  e_ref, wb_ref, wo_ref, bg_ref, g_ref, b_ref, o_ref, ot1_ref, ot2_ref, lt1_ref, lt2_ref,
                  *, alpha):
    dils = [d for _, d in A_GROUPS]
    lses = [_token_order(ref, tmp, d)
            for ref, tmp, d in zip((l0_ref, l1_ref, l2_ref), (None, lt1_ref, lt2_ref), dils)]
    top = jnp.maximum(jnp.maximum(lses[0], lses[1]), lses[2])
    es = [jnp.exp(v - top) for v in lses]
    inv = 1.0 / (es[0] + es[1] + es[2])
    o_a = None
    for e, oa_ref, tmp, d in zip(es, (oa0_ref, oa1_ref, oa2_ref), (None, ot1_ref, ot2_ref), dils):
        w = e * inv
        hi = w.astype(BF16)
        lo = (w - hi.astype(F32)).astype(BF16)
        wide = (jnp.dot(hi, e_ref[...], preferred_element_type=F32)
                + jnp.dot(lo, e_ref[...], preferred_element_type=F32))
        term = wide * _token_order(oa_ref, tmp, d)
        o_a = term if o_a is None else o_a + term
    branches = (o_a.astype(BF16), ob_ref[...], oc_ref[...])
    merged = None
    for i, br in enumerate(branches):
        gate = jax.nn.sigmoid(gate_ref[:, i * D_MODEL:(i + 1) * D_MODEL].astype(F32)
                              + bg_ref[:, i * D_MODEL:(i + 1) * D_MODEL])
        term = gate * jnp.dot(br, wb_ref[i], preferred_element_type=F32)
        merged = term if merged is None else merged + term
    mix = jnp.dot(merged.astype(BF16), wo_ref[...], preferred_element_type=F32)
    o_ref[...] = _layer_norm(alpha * x_ref[...] + mix, g_ref[...], b_ref[...])


def _merge(oa, lse, ob, oc, p2, x2d, expand, lw, *, alpha, tm=512):
    t = x2d.shape[0]
    row = lambda w: pl.BlockSpec((tm, w), lambda i: (i, 0))
    full = lambda a: pl.BlockSpec(a.shape, lambda i: (0,) * a.ndim)

    def grouped(arrs):
        views, specs = [], []
        for a, (_, dil) in zip(arrs, A_GROUPS):
            w = a.shape[-1]
            if dil == 1:
                views.append(a.reshape(t, w))
                specs.append(row(w))
                continue
            per_span = BLOCK * dil // tm
            views.append(a.reshape(-1, dil, BLOCK, w))
            specs.append(pl.BlockSpec((None, dil, tm // dil, w),
                                      lambda i, per_span=per_span: (i // per_span, 0, i % per_span, 0)))
        return views, specs

    oa_v, oa_s = grouped(oa)
    lse_v, lse_s = grouped(lse)
    consts = (expand, lw["wb"], lw["wo"], lw["bg"], lw["ln1_g"], lw["ln1_b"])
    slabs = lambda w: pltpu.VMEM((w // LANE, tm, LANE), F32)
    return pl.pallas_call(
        functools.partial(_merge_kernel, alpha=alpha),
        grid=(t // tm,),
        in_specs=oa_s + lse_s + [row(512), row(512),
                  pl.BlockSpec((tm, 3 * D_MODEL), lambda i: (i, COL_GATE // (3 * D_MODEL))),
                  row(D_MODEL)] + [full(a) for a in consts],
        out_specs=row(D_MODEL),
        out_shape=jax.ShapeDtypeStruct((t, D_MODEL), F32),
        scratch_shapes=[slabs(512), slabs(512), slabs(LANE), slabs(LANE)],
        compiler_params=_cparams(1),
        name="merge",
    )(*oa_v, *lse_v, ob.reshape(t, 512), oc, p2, x2d, *consts)


HALO = 8
FF_CHUNK = 512


def _ffn_kernel(halo_ref, x_ref, wup_ref, cw_ref, cb_ref, wdn_ref, g_ref, b_ref, o_ref,
                ug_ref, uv_ref, *, alpha, tm, per_seq):
    i = pl.program_id(0)
    x = x_ref[...]
    halo = jnp.where(i % per_seq == 0, jnp.zeros_like(halo_ref[...]), halo_ref[...])
    xh = jnp.concatenate([halo, x], axis=0).astype(BF16)
    chunks = [(c0, min(FF_CHUNK, D_FF - c0)) for c0 in range(0, D_FF, FF_CHUNK)]

    def up(c0, cf):
        return [jnp.dot(xh, wup_ref[:, base:base + cf], preferred_element_type=F32)
                for base in (c0, D_FF + c0)]

    acc = None
    u_next = up(*chunks[0])
    for ci, (c0, cf) in enumerate(chunks):
        u_cur = u_next
        if ci + 1 < len(chunks):
            u_next = up(*chunks[ci + 1])
        parts = []
        for u, base in zip(u_cur, (c0, D_FF + c0)):
            y = cb_ref[:, base:base + cf]
            for tap in range(3):
                y = y + u[HALO - 2 + tap:HALO - 2 + tap + tm] * cw_ref[tap:tap + 1, base:base + cf]
            parts.append(y)
        act = (parts[0] * jax.nn.sigmoid(parts[0]) * parts[1]).astype(BF16)
        term = jnp.dot(act, wdn_ref[c0:c0 + cf, :], preferred_element_type=F32)
        acc = term if acc is None else acc + term
    o_ref[...] = _layer_norm(alpha * x + acc, g_ref[...], b_ref[...])


def _ffn(x2d, lw, *, alpha, seq, tm=512):
    t = x2d.shape[0]
    per_seq = seq // tm
    full = lambda a: pl.BlockSpec(a.shape, lambda i: (0,) * a.ndim, pipeline_mode=pl.Buffered(1))
    consts = (lw["wup"], lw["cw"], lw["cb"], lw["wdn"], lw["ln2_g"], lw["ln2_b"])
    return pl.pallas_call(
        functools.partial(_ffn_kernel, alpha=alpha, tm=tm, per_seq=per_seq),
        grid=(t // tm,),
        in_specs=[pl.BlockSpec((HALO, D_MODEL), lambda i: (jnp.maximum(i * (tm // HALO) - 1, 0), 0)),
                  pl.BlockSpec((tm, D_MODEL), lambda i: (i, 0))] + [full(a) for a in consts],
        out_specs=pl.BlockSpec((tm, D_MODEL), lambda i: (i, 0)),
        out_shape=jax.ShapeDtypeStruct((t, D_MODEL), F32),
        scratch_shapes=[pltpu.VMEM((tm + HALO, FF_CHUNK), F32),
                        pltpu.VMEM((tm + HALO, FF_CHUNK), F32)],
        compiler_params=_cparams(1),
        name="ffn",
    )(x2d, x2d, *consts)


def _layer_weights(l, w_in, b_gate, q_norm_g, kv_norm_g, w_uq, w_ukv, w_branch, w_out,
                   ln1_g, ln1_b, w_ffn_up, conv_w, conv_b, w_ffn_down, ln2_g, ln2_b):
    wi = w_in[l].astype(BF16)
    a_end = 4608
    bq = wi[:, a_end:a_end + 512].reshape(D_MODEL, B_Q_HEADS, HEAD_DIM)
    bq = jnp.stack([bq[:, :4], bq[:, 4:]], axis=2).reshape(D_MODEL, 512)
    bkv = wi[:, 5120:5376]
    cdq = wi[:, 5376:5632]
    ckv = wi[:, 5632:5760]
    kr = wi[:, 5760:5792]
    kr_rot = jnp.concatenate([-kr[:, C_ROPE // 2:], kr[:, :C_ROPE // 2]], axis=1)
    gate = wi[:, 5792:]
    zeros = lambda n: jnp.zeros((D_MODEL, n), wi.dtype)
    wp = jnp.concatenate([gate, wi[:, :A_COLS], bq, cdq, ckv, kr, kr_rot, zeros(64), bkv, zeros(256)],
                         axis=1)
    wa = [wi[:, g * A_COLS:(g + 1) * A_COLS] for g in range(1, len(A_GROUPS))]

    uq = w_uq[l].reshape(C_Q_RANK, C_HEADS, C_NOPE + C_ROPE)
    zq = lambda n: jnp.zeros((C_Q_RANK, C_HEADS, n), uq.dtype)
    half = C_ROPE // 2
    wq1 = jnp.concatenate([uq, zq(LANE - C_NOPE - C_ROPE)], axis=2)
    wq2 = jnp.concatenate([zq(C_NOPE), -uq[:, :, C_NOPE + half:], uq[:, :, C_NOPE:C_NOPE + half],
                           zq(LANE - C_NOPE - C_ROPE)], axis=2)
    ukv = w_ukv[l].reshape(C_KV_RANK, C_HEADS, C_NOPE + C_V)
    wk = jnp.concatenate([ukv[:, :, :C_NOPE], jnp.zeros((C_KV_RANK, C_HEADS, LANE - C_NOPE), ukv.dtype)],
                         axis=2)
    wv = jnp.concatenate([ukv[:, :, C_NOPE:], jnp.zeros((C_KV_RANK, C_HEADS, LANE - C_V), ukv.dtype)],
                         axis=2)

    wb = w_branch[l]
    wb1 = wb[1].reshape(B_Q_HEADS, HEAD_DIM, D_MODEL)
    wb1 = jnp.stack([wb1[:4], wb1[4:]], axis=1).reshape(512, D_MODEL)
    wb = jnp.stack([wb[0], wb1, wb[2]])

    sel_rows = np.zeros((2, 512, LANE), np.float32)
    for j in range(C_ROPE):
        sel_rows[0, C_Q_RANK + C_KV_RANK + j, C_NOPE + j] = 1.0
        sel_rows[1, C_Q_RANK + C_KV_RANK + C_ROPE + j, C_NOPE + j] = 1.0

    return dict(
        wp=wp, wa=wa,
        gq=q_norm_g[l].reshape(1, -1), gkv=kv_norm_g[l].reshape(1, -1),
        wq1=wq1.reshape(C_Q_RANK, -1).astype(BF16), wq2=wq2.reshape(C_Q_RANK, -1).astype(BF16),
        wk=wk.reshape(C_KV_RANK, -1).astype(BF16), wv=wv.reshape(C_KV_RANK, -1).astype(BF16),
        sela=jnp.asarray(sel_rows[0], BF16), selb=jnp.asarray(sel_rows[1], BF16),
        wb=wb.astype(BF16), wo=w_out[l].astype(BF16), bg=b_gate[l].reshape(1, -1),
        ln1_g=ln1_g[l].reshape(1, -1), ln1_b=ln1_b[l].reshape(1, -1),
        wup=w_ffn_up[l].astype(BF16), cw=conv_w[l], cb=conv_b[l].reshape(1, -1),
        wdn=w_ffn_down[l].astype(BF16),
        ln2_g=ln2_g[l].reshape(1, -1), ln2_b=ln2_b[l].reshape(1, -1),
    )


def _rope_tables(seq):
    pos = jnp.arange(seq, dtype=F32)
    inv_freq = ROPE_BASE ** (-jnp.arange(0, C_ROPE, 2, dtype=F32) / C_ROPE)
    ang = pos[:, None] * inv_freq[None, :]
    cos, sin = jnp.cos(ang), jnp.sin(ang)
    pad = jnp.zeros((seq, LANE - C_NOPE - C_ROPE), F32)
    cos_t = jnp.concatenate([jnp.ones((seq, C_NOPE), F32), cos, cos, pad], axis=1)
    sin_t = jnp.concatenate([jnp.zeros((seq, C_NOPE), F32), sin, sin, pad], axis=1)
    return cos_t, sin_t


def _expand_matrix():
    e = np.zeros((LANE, A_HEADS * HEAD_DIM), np.float32)
    for c in range(A_HEADS * HEAD_DIM):
        e[(LANE // A_HEADS) * (c // HEAD_DIM), c] = 1.0
    return jnp.asarray(e, BF16)


def kernel(x, rel_table, w_in, b_gate, sinks, q_norm_g, kv_norm_g, w_uq, w_ukv, w_branch, w_out,
           ln1_g, ln1_b, w_ffn_up, conv_w, conv_b, w_ffn_down, ln2_g, ln2_b):
    batch, seq, d = x.shape
    depth = w_in.shape[0]
    alpha = (2 * depth) ** 0.25
    cos_t, sin_t = _rope_tables(seq)
    expand = _expand_matrix()
    bias = _bias_tables(rel_table)
    x2d = x.reshape(batch * seq, d)
    for l in range(depth):
        lw = _layer_weights(l, w_in, b_gate, q_norm_g, kv_norm_g, w_uq, w_ukv, w_branch, w_out,
                            ln1_g, ln1_b, w_ffn_up, conv_w, conv_b, w_ffn_down, ln2_g, ln2_b)
        p2 = _proj(x2d, lw["wp"], dil=1, tm=2048, tn=A_COLS // 2, name="proj")
        p_dil = []
        for g, (w, (_, dil)) in enumerate(zip(lw["wa"], A_GROUPS[1:])):
            tm = max(1024, BLOCK * dil)
            tn = A_COLS * 1024 // tm
            p_dil.append(_proj(x2d, w, dil=dil, tm=tm, tn=tn, name=f"proj_a{g + 1}"))
        sink_slots = sinks[l].reshape(2, 4).T.reshape(-1)
        oa, lse = [], []
        for kind in range(len(A_GROUPS)):
            src = p2 if kind == 0 else p_dil[kind - 1]
            o_g, lse_g = _band_attention(src, bias, sink_slots, kind=kind, batch=batch, seq=seq)
            oa.append(o_g)
            lse.append(lse_g)
        ob = _band_attention(p2, bias, sink_slots, kind=len(A_GROUPS), batch=batch, seq=seq)
        q, k, v = _mla_prep(p2, cos_t, sin_t, lw, seq=seq)
        oc = _mla_flash(q, k, v, batch=batch, seq=seq)
        x2d = _merge(oa, lse, ob, oc, p2, x2d, expand, lw, alpha=alpha)
        x2d = _ffn(x2d, lw, alpha=alpha, seq=seq)
    return x2d.reshape(batch, seq, d)
```

```python
import functools
import math

import jax
import jax.numpy as jnp
import numpy as np
from jax import lax
from jax.experimental import pallas as pl
from jax.experimental.pallas import tpu as pltpu

F32 = jnp.float32
BF16 = jnp.bfloat16

D_MODEL = 1024
HEAD_DIM = 64
BLOCK = 128
A_GROUPS = ((128, 1), (512, 4), (2048, 16))
A_HEADS = 8
B_Q_HEADS = 8
B_WINDOW = 128
C_HEADS = 8
C_Q_RANK = 256
C_KV_RANK = 128
C_NOPE = 64
C_ROPE = 32
C_V = 64
ROPE_BASE = 10000.0
REL_BUCKETS = 32
REL_MAX_DIST = 2048
D_FF = 2816
LN_EPS = 1e-5
RMS_EPS = 1e-6
NEG = -1e30
LOG2E = math.log2(math.e)
FLASH_BLK = 1024
FLASH_DIAG = 512

COL_GATE = 0
COL_A = 3072
COL_BQ = 4608
COL_C = 5120
COL_BKV = 5632
NP = 6144
A_COLS = 1536

LANE = 128
VMEM_LIMIT = 56 * 1024 * 1024

N_KINDS = 4


def _cparams(n_axes):
    return pltpu.CompilerParams(dimension_semantics=("arbitrary",) * n_axes,
                                vmem_limit_bytes=VMEM_LIMIT)


CHEAP_STRIDE = 4


def _proj_kernel(*refs, dil):
    n_x = len(refs) - (4 if dil > CHEAP_STRIDE else 3)
    x_refs, (w_ref, o_ref, xb_ref, *tmp) = refs[:n_x], refs[n_x:]

    @pl.when(pl.program_id(1) == 0)
    def _():
        if dil == 1:
            xb_ref[...] = x_refs[0][...].astype(BF16)
            return
        span = BLOCK * dil
        d1 = min(dil, CHEAP_STRIDE)
        d2 = dil // d1
        for c, x_ref in enumerate(x_refs):
            for s0 in range(0, xb_ref.shape[0], span):
                for r1 in range(d1):
                    if d2 == 1:
                        rows = x_ref[pl.ds(s0 + r1, BLOCK, stride=d1), :]
                        xb_ref[s0 + r1 * BLOCK:s0 + (r1 + 1) * BLOCK, c * LANE:(c + 1) * LANE] = rows.astype(BF16)
                        continue
                    g0 = s0 + r1 * (span // d1)
                    tmp[0][c, g0:g0 + span // d1, :] = x_ref[pl.ds(s0 + r1, span // d1, stride=d1), :]
                    for r2 in range(d2):
                        r = r1 + d1 * r2
                        rows = tmp[0][c, pl.ds(g0 + r2, BLOCK, stride=d2), :]
                        xb_ref[s0 + r * BLOCK:s0 + (r + 1) * BLOCK, c * LANE:(c + 1) * LANE] = rows.astype(BF16)

    o_ref[...] = jnp.dot(xb_ref[...], w_ref[...], preferred_element_type=F32).astype(o_ref.dtype)


def _proj(x2d, w, *, dil, tm, tn, name):
    t, k = x2d.shape
    n = w.shape[1]
    if dil == 1:
        x_specs = [pl.BlockSpec((tm, k), lambda i, j: (i, 0))]
    else:
        x_specs = [pl.BlockSpec((tm, LANE), lambda i, j, c=c: (i, c)) for c in range(k // LANE)]
    scratch = [pltpu.VMEM((tm, k), BF16)]
    if dil > CHEAP_STRIDE:
        scratch.append(pltpu.VMEM((k // LANE, tm, LANE), F32))
    return pl.pallas_call(
        functools.partial(_proj_kernel, dil=dil),
        grid=(t // tm, n // tn),
        in_specs=x_specs + [pl.BlockSpec((k, tn), lambda i, j: (0, j))],
        out_specs=pl.BlockSpec((tm, tn), lambda i, j: (i, j)),
        out_shape=jax.ShapeDtypeStruct((t, n), BF16),
        scratch_shapes=scratch,
        compiler_params=_cparams(2),
        name=name,
    )(*([x2d] * len(x_specs)), w)


def _t5_bucket(dist):
    n = jnp.maximum(dist, 0)
    max_exact = REL_BUCKETS // 2
    scaled = jnp.log(jnp.maximum(n, 1).astype(F32) / max_exact) / math.log(REL_MAX_DIST / max_exact)
    large = max_exact + (scaled * (REL_BUCKETS - max_exact)).astype(jnp.int32)
    return jnp.where(n < max_exact, n, jnp.minimum(large, REL_BUCKETS - 1))


def _bias_codes():
    qi = jnp.arange(BLOCK)[:, None]
    ki = jnp.arange(2 * BLOCK)[None, :]
    step = BLOCK + qi - ki
    has_prev = ki >= BLOCK
    codes = []
    for kind in range(N_KINDS):
        if kind < len(A_GROUPS):
            dil = A_GROUPS[kind][1]
            band = (step >= 0) & (step <= BLOCK)
        else:
            dil = 1
            band = (step >= 0) & (step < B_WINDOW)
        bucket = _t5_bucket(step * dil)
        codes.append(jnp.stack([jnp.where(band & has_prev, bucket, -1),
                                jnp.where(band, bucket, -1)]))
    return jnp.stack(codes).astype(jnp.int32)


def _bias_kernel(rel_ref, code_ref, o_ref):
    kind = pl.program_id(0)
    code = code_ref[0, 0]
    for slot in range(A_HEADS):
        b_head = len(A_GROUPS) * A_HEADS + slot // 2 + 4 * (slot % 2)
        col = jnp.where(kind < len(A_GROUPS), kind * A_HEADS + slot, b_head)
        acc = jnp.full(code.shape, NEG, F32)
        for b in range(REL_BUCKETS):
            acc = jnp.where(code == b, rel_ref[b, col], acc)
        o_ref[0, 0, slot] = acc


def _bias_tables(rel_table):
    codes = _bias_codes()
    return pl.pallas_call(
        _bias_kernel,
        grid=(N_KINDS, 2),
        in_specs=[pl.BlockSpec(memory_space=pltpu.SMEM),
                  pl.BlockSpec((1, 1, BLOCK, 2 * BLOCK), lambda k, v: (k, v, 0, 0))],
        out_specs=pl.BlockSpec((1, 1, A_HEADS, BLOCK, 2 * BLOCK), lambda k, v: (k, v, 0, 0, 0)),
        out_shape=jax.ShapeDtypeStruct((N_KINDS, 2, A_HEADS, BLOCK, 2 * BLOCK), F32),
        compiler_params=_cparams(2),
        name="bias_tables",
    )(rel_table, codes)


BAND_QB = 8
STAT_LANES = LANE // A_HEADS // 2


def _band_kernel(sink_ref, q_ref, kp_ref, kc_ref, vp_ref, vc_ref, bias_ref, o_ref, *lse_refs,
                 shared_kv, with_sink):
    first = pl.program_id(2) == 0
    lane = lax.broadcasted_iota(jnp.int32, (BLOCK, LANE), 1)
    low = lane < HEAD_DIM
    for qb in range(q_ref.shape[0]):
        variant = jnp.where(first, 0, 1) if qb == 0 else 1
        lse_tile = jnp.zeros((BLOCK, LANE), F32)
        for pair in range(4):
            cols = slice(pair * LANE, (pair + 1) * LANE)
            kv_cols = slice(0, LANE) if shared_kv else cols
            qp = q_ref[qb, :, cols] * jnp.asarray(HEAD_DIM ** -0.5, BF16)
            k_prev = kp_ref[:, kv_cols] if qb == 0 else kc_ref[qb - 1, :, kv_cols]
            v_prev = vp_ref[:, kv_cols] if qb == 0 else vc_ref[qb - 1, :, kv_cols]
            kcat = jnp.concatenate([k_prev, kc_ref[qb, :, kv_cols]], axis=0)
            vcat = jnp.concatenate([v_prev, vc_ref[qb, :, kv_cols]], axis=0)
            zero = jnp.zeros_like(qp)
            q2 = jnp.concatenate([jnp.where(low, qp, zero), jnp.where(low, zero, qp)], axis=0)
            s = lax.dot_general(q2, kcat, (((1,), (1,)), ((), ())), preferred_element_type=F32)
            s = s + bias_ref[variant, pair]
            m = jnp.max(s, axis=-1, keepdims=True)
            if with_sink:
                head_a = lax.broadcasted_iota(jnp.int32, (2 * BLOCK, 1), 0) < BLOCK
                sink = jnp.where(head_a, sink_ref[2 * pair], sink_ref[2 * pair + 1])
                m = jnp.maximum(m, sink)
            p = jnp.exp(s - m)
            l = jnp.sum(p, axis=-1, keepdims=True)
            if with_sink:
                l = l + jnp.exp(sink - m)
            o = jnp.dot(p.astype(BF16), vcat, preferred_element_type=F32)
            if lse_refs:
                for sub in range(2):
                    rows = slice(sub * BLOCK, (sub + 1) * BLOCK)
                    field = 2 * (2 * pair + sub)
                    lse_tile = jnp.where(lane // STAT_LANES == field, m[rows],
                                         jnp.where(lane // STAT_LANES == field + 1, l[rows], lse_tile))
            else:
                o = o * (1.0 / l)
            o_ref[qb, :, cols] = jnp.where(low, o[:BLOCK], o[BLOCK:]).astype(o_ref.dtype)
        if lse_refs:
            lse_refs[0][qb] = lse_tile


def _band_attention(p2, bias, sinks, *, kind, batch, seq):
    is_b = kind == len(A_GROUPS)
    dil = 1 if is_b else A_GROUPS[kind][1]
    n_span = seq // (BLOCK * dil)
    view = p2.reshape(batch, n_span, dil, BLOCK, p2.shape[-1])
    if is_b:
        q_blk, k_blk, v_blk, kv_w = COL_BQ // 512, COL_BKV // LANE, COL_BKV // LANE + 1, LANE
    else:
        q_blk = COL_A // 512 if dil == 1 else 0
        k_blk, v_blk, kv_w = q_blk + 1, q_blk + 2, 512

    n_qb = min(BAND_QB, n_span)

    def cur(blk, width):
        return pl.BlockSpec((None, n_qb, None, BLOCK, width), lambda b, r, n: (b, n, r, 0, blk))

    def prev(blk, width):
        return pl.BlockSpec((None, None, None, BLOCK, width),
                            lambda b, r, n: (b, jnp.maximum(n * n_qb - 1, 0), r, 0, blk))

    in_specs = [
        pl.BlockSpec(memory_space=pltpu.SMEM),
        cur(q_blk, 512), prev(k_blk, kv_w), cur(k_blk, kv_w), prev(v_blk, kv_w), cur(v_blk, kv_w),
        pl.BlockSpec((None, 2, A_HEADS // 2, 2 * BLOCK, 2 * BLOCK), lambda b, r, n: (kind, 0, 0, 0, 0)),
    ]
    bias = bias.reshape(N_KINDS, 2, A_HEADS // 2, 2 * BLOCK, 2 * BLOCK)
    out_specs = [pl.BlockSpec((None, n_qb, None, BLOCK, 512), lambda b, r, n: (b, n, r, 0, 0))]
    out_shape = [jax.ShapeDtypeStruct((batch, n_span, dil, BLOCK, 512), BF16)]
    if not is_b:
        out_specs.append(pl.BlockSpec((None, n_qb, None, BLOCK, LANE), lambda b, r, n: (b, n, r, 0, 0)))
        out_shape.append(jax.ShapeDtypeStruct((batch, n_span, dil, BLOCK, LANE), F32))
    outs = pl.pallas_call(
        functools.partial(_band_kernel, shared_kv=is_b, with_sink=is_b),
        grid=(batch, dil, n_span // n_qb),
        in_specs=in_specs,
        out_specs=out_specs,
        out_shape=out_shape,
        compiler_params=_cparams(3),
        name="band_b" if is_b else f"band_a{kind}",
    )(sinks, view, view, view, view, view, bias)
    return outs[0] if is_b else outs


def _mla_prep_kernel(c_ref, cos_ref, sin_ref, gq_ref, gkv_ref, wq1_ref, wq2_ref, wk_ref, wv_ref,
                     sela_ref, selb_ref, q_ref, k_ref, v_ref):
    c = c_ref[...]
    cq = c[:, :C_Q_RANK].astype(F32)
    ckv = c[:, C_Q_RANK:C_Q_RANK + C_KV_RANK].astype(F32)
    nq = cq * lax.rsqrt(jnp.mean(cq * cq, axis=-1, keepdims=True) + RMS_EPS) * gq_ref[...]
    nkv = ckv * lax.rsqrt(jnp.mean(ckv * ckv, axis=-1, keepdims=True) + RMS_EPS) * gkv_ref[...]
    nq = nq.astype(BF16)
    nkv = nkv.astype(BF16)
    cos = cos_ref[...]
    sin = sin_ref[...]
    cos8 = jnp.tile(cos, (1, C_HEADS))
    sin8 = jnp.tile(sin, (1, C_HEADS))
    q = (jnp.dot(nq, wq1_ref[...], preferred_element_type=F32) * cos8
         + jnp.dot(nq, wq2_ref[...], preferred_element_type=F32) * sin8)
    q_ref[...] = (q * ((C_NOPE + C_ROPE) ** -0.5 * LOG2E)).astype(q_ref.dtype)
    kr = (jnp.dot(c, sela_ref[...], preferred_element_type=F32) * cos
          + jnp.dot(c, selb_ref[...], preferred_element_type=F32) * sin)
    k = jnp.dot(nkv, wk_ref[...], preferred_element_type=F32) + jnp.tile(kr, (1, C_HEADS))
    k_ref[...] = k.astype(k_ref.dtype)
    lane = lax.broadcasted_iota(jnp.int32, (1, C_HEADS * LANE), 1)
    ones_col = jnp.where(lane % LANE == C_V, 1.0, 0.0).astype(F32)
    v_ref[...] = (jnp.dot(nkv, wv_ref[...], preferred_element_type=F32) + ones_col).astype(v_ref.dtype)


def _mla_prep(p2, cos_t, sin_t, lw, *, seq, tm=512):
    t = p2.shape[0]
    per_seq = seq // tm
    full = lambda a: pl.BlockSpec(a.shape, lambda i: (0,) * a.ndim)
    consts = (lw["gq"], lw["gkv"], lw["wq1"], lw["wq2"], lw["wk"], lw["wv"], lw["sela"], lw["selb"])
    wide = C_HEADS * LANE
    return pl.pallas_call(
        _mla_prep_kernel,
        grid=(t // tm,),
        in_specs=[pl.BlockSpec((tm, 512), lambda i: (i, COL_C // 512)),
                  pl.BlockSpec((tm, LANE), lambda i: (i % per_seq, 0)),
                  pl.BlockSpec((tm, LANE), lambda i: (i % per_seq, 0))] + [full(a) for a in consts],
        out_specs=[pl.BlockSpec((tm, wide), lambda i: (i, 0))] * 3,
        out_shape=[jax.ShapeDtypeStruct((t, wide), BF16)] * 3,
        compiler_params=_cparams(1),
        name="mla_prep",
    )(p2, cos_t, sin_t, *consts)


def _mla_flash_kernel(q_ref, k_ref, v_ref, o_ref, m_ref, acc_ref, *, blk):
    qi = pl.program_id(1)
    ki = pl.program_id(2)

    @pl.when(ki == 0)
    def _():
        m_ref[...] = jnp.full(m_ref.shape, NEG, F32)
        acc_ref[...] = jnp.zeros(acc_ref.shape, F32)

    def update(h, rows, n_keys, diagonal):
        hc = slice(h * LANE, (h + 1) * LANE)
        s = lax.dot_general(q_ref[rows, hc], k_ref[:n_keys, hc], (((1,), (1,)), ((), ())),
                            preferred_element_type=F32)
        if diagonal:
            row = lax.broadcasted_iota(jnp.int32, s.shape, 0) + rows.start
            col = lax.broadcasted_iota(jnp.int32, s.shape, 1)
            s = jnp.where(row >= col, s, NEG)
        m_prev = m_ref[h, rows]
        m_new = jnp.maximum(m_prev, jnp.max(s, axis=-1, keepdims=True))
        alpha = jnp.exp2(m_prev - m_new)
        p = jnp.exp2(s - jnp.tile(m_new, (1, n_keys // LANE)))
        acc_ref[h, rows] = alpha * acc_ref[h, rows] + jnp.dot(p.astype(BF16), v_ref[:n_keys, hc],
                                                              preferred_element_type=F32)
        m_ref[h, rows] = m_new

    @pl.when(ki < qi)
    def _():
        for h in range(C_HEADS):
            update(h, slice(0, blk), blk, False)

    @pl.when(ki == qi)
    def _():
        for h in range(C_HEADS):
            for r0 in range(0, blk, FLASH_DIAG):
                update(h, slice(r0, r0 + FLASH_DIAG), r0 + FLASH_DIAG, True)
        low = lax.broadcasted_iota(jnp.int32, (blk, LANE), 1) < C_V
        for pair in range(C_HEADS // 2):
            halves = []
            for h in (2 * pair, 2 * pair + 1):
                acc = acc_ref[h]
                halves.append(acc * (1.0 / acc[:, C_V:C_V + 1]))
            odd = pltpu.roll(halves[1], C_V, axis=1)
            o_ref[:, pair * LANE:(pair + 1) * LANE] = jnp.where(low, halves[0], odd).astype(o_ref.dtype)


def _mla_flash(q, k, v, *, batch, seq, blk=FLASH_BLK):
    wide = C_HEADS * LANE
    q3 = q.reshape(batch, seq, wide)
    k3 = k.reshape(batch, seq, wide)
    v3 = v.reshape(batch, seq, wide)
    kv_map = lambda b, i, j: (b, jnp.minimum(j, i), 0)
    out = pl.pallas_call(
        functools.partial(_mla_flash_kernel, blk=blk),
        grid=(batch, seq // blk, seq // blk),
        in_specs=[pl.BlockSpec((None, blk, wide), lambda b, i, j: (b, i, 0)),
                  pl.BlockSpec((None, blk, wide), kv_map),
                  pl.BlockSpec((None, blk, wide), kv_map)],
        out_specs=pl.BlockSpec((None, blk, C_HEADS * C_V), lambda b, i, j: (b, i, 0)),
        out_shape=jax.ShapeDtypeStruct((batch, seq, C_HEADS * C_V), BF16),
        scratch_shapes=[pltpu.VMEM((C_HEADS, blk, LANE), F32),
                        pltpu.VMEM((C_HEADS, blk, LANE), F32)],
        compiler_params=_cparams(3),
        name="mla_flash",
    )(q3, k3, v3)
    return out.reshape(batch * seq, C_HEADS * C_V)


def _layer_norm(y, g, b):
    mu = jnp.mean(y, axis=-1, keepdims=True)
    d = y - mu
    var = jnp.mean(d * d, axis=-1, keepdims=True)
    return d * lax.rsqrt(var + LN_EPS) * g + b


def _token_order(src_ref, tmp_ref, dil):
    if dil == 1:
        return src_ref[...].astype(F32)
    n = src_ref.shape[1]
    n_slab = src_ref.shape[2] // LANE
    for r in range(dil):
        rows = src_ref[r].astype(F32)
        for c in range(n_slab):
            tmp_ref[c, pl.ds(r, n, stride=dil), :] = rows[:, c * LANE:(c + 1) * LANE]
    return jnp.concatenate([tmp_ref[c] for c in range(n_slab)], axis=1)


def _merge_kernel(oa0_ref, oa1_ref, oa2_ref, l0_ref, l1_ref, l2_ref, ob_ref, oc_ref, gate_ref, x_ref,
                  e_ref, wb_ref, wo_ref, bg_ref, g_ref, b_ref, o_ref, ot1_ref, ot2_ref, lt1_ref, lt2_ref,
                  *, alpha):
    dils = [d for _, d in A_GROUPS]
    stats = [_token_order(ref, tmp, d)
             for ref, tmp, d in zip((l0_ref, l1_ref, l2_ref), (None, lt1_ref, lt2_ref), dils)]
    is_max = lax.broadcasted_iota(jnp.int32, stats[0].shape, 1) // STAT_LANES % 2 == 0
    top = jnp.maximum(jnp.maximum(stats[0], stats[1]), stats[2])
    es = [jnp.exp(v - top) for v in stats]
    sums = [pltpu.roll(v, LANE - STAT_LANES, axis=1) for v in stats]
    inv = 1.0 / (es[0] * sums[0] + es[1] * sums[1] + es[2] * sums[2])
    o_a = None
    for e, oa_ref, tmp, d in zip(es, (oa0_ref, oa1_ref, oa2_ref), (None, ot1_ref, ot2_ref), dils):
        w = jnp.where(is_max, e * inv, 0.0)
        hi = w.astype(BF16)
        lo = (w - hi.astype(F32)).astype(BF16)
        wide = (jnp.dot(hi, e_ref[...], preferred_element_type=F32)
                + jnp.dot(lo, e_ref[...], preferred_element_type=F32))
        term = wide * _token_order(oa_ref, tmp, d)
        o_a = term if o_a is None else o_a + term
    branches = (o_a.astype(BF16), ob_ref[...], oc_ref[...])
    merged = None
    for i, br in enumerate(branches):
        gate = jax.nn.sigmoid(gate_ref[:, i * D_MODEL:(i + 1) * D_MODEL].astype(F32)
                              + bg_ref[:, i * D_MODEL:(i + 1) * D_MODEL])
        term = gate * jnp.dot(br, wb_ref[i], preferred_element_type=F32)
        merged = term if merged is None else merged + term
    mix = jnp.dot(merged.astype(BF16), wo_ref[...], preferred_element_type=F32)
    o_ref[...] = _layer_norm(alpha * x_ref[...] + mix, g_ref[...], b_ref[...])


def _merge(oa, lse, ob, oc, p2, x2d, expand, lw, *, alpha, tm=512):
    t = x2d.shape[0]
    row = lambda w: pl.BlockSpec((tm, w), lambda i: (i, 0))
    full = lambda a: pl.BlockSpec(a.shape, lambda i: (0,) * a.ndim)

    def grouped(arrs):
        views, specs = [], []
        for a, (_, dil) in zip(arrs, A_GROUPS):
            w = a.shape[-1]
            if dil == 1:
                views.append(a.reshape(t, w))
                specs.append(row(w))
                continue
            per_span = BLOCK * dil // tm
            views.append(a.reshape(-1, dil, BLOCK, w))
            specs.append(pl.BlockSpec((None, dil, tm // dil, w),
                                      lambda i, per_span=per_span: (i // per_span, 0, i % per_span, 0)))
        return views, specs

    oa_v, oa_s = grouped(oa)
    lse_v, lse_s = grouped(lse)
    consts = (expand, lw["wb"], lw["wo"], lw["bg"], lw["ln1_g"], lw["ln1_b"])
    slabs = lambda w: pltpu.VMEM((w // LANE, tm, LANE), F32)
    return pl.pallas_call(
        functools.partial(_merge_kernel, alpha=alpha),
        grid=(t // tm,),
        in_specs=oa_s + lse_s + [row(512), row(512),
                  pl.BlockSpec((tm, 3 * D_MODEL), lambda i: (i, COL_GATE // (3 * D_MODEL))),
                  row(D_MODEL)] + [full(a) for a in consts],
        out_specs=row(D_MODEL),
        out_shape=jax.ShapeDtypeStruct((t, D_MODEL), F32),
        scratch_shapes=[slabs(512), slabs(512), slabs(LANE), slabs(LANE)],
        compiler_params=_cparams(1),
        name="merge",
    )(*oa_v, *lse_v, ob.reshape(t, 512), oc, p2, x2d, *consts)


HALO = 8
FF_CHUNK = 512


def _ffn_kernel(halo_ref, x_ref, wup_ref, cw_ref, cb_ref, wdn_ref, g_ref, b_ref, o_ref,
                ug_ref, uv_ref, *, alpha, tm, per_seq):
    i = pl.program_id(0)
    x = x_ref[...]
    halo = jnp.where(i % per_seq == 0, jnp.zeros_like(halo_ref[...]), halo_ref[...])
    xh = jnp.concatenate([halo, x], axis=0).astype(BF16)
    chunks = [(c0, min(FF_CHUNK, D_FF - c0)) for c0 in range(0, D_FF, FF_CHUNK)]

    def up(c0, cf):
        return [jnp.dot(xh, wup_ref[:, base:base + cf], preferred_element_type=F32)
                for base in (c0, D_FF + c0)]

    acc = None
    u_next = up(*chunks[0])
    for ci, (c0, cf) in enumerate(chunks):
        u_cur = u_next
        if ci + 1 < len(chunks):
            u_next = up(*chunks[ci + 1])
        parts = []
        for u, base in zip(u_cur, (c0, D_FF + c0)):
            y = cb_ref[:, base:base + cf]
            for tap in range(3):
                y = y + u[HALO - 2 + tap:HALO - 2 + tap + tm] * cw_ref[tap:tap + 1, base:base + cf]
            parts.append(y)
        act = (parts[0] * jax.nn.sigmoid(parts[0]) * parts[1]).astype(BF16)
        term = jnp.dot(act, wdn_ref[c0:c0 + cf, :], preferred_element_type=F32)
        acc = term if acc is None else acc + term
    o_ref[...] = _layer_norm(alpha * x + acc, g_ref[...], b_ref[...])


def _ffn(x2d, lw, *, alpha, seq, tm=512):
    t = x2d.shape[0]
    per_seq = seq // tm
    full = lambda a: pl.BlockSpec(a.shape, lambda i: (0,) * a.ndim, pipeline_mode=pl.Buffered(1))
    consts = (lw["wup"], lw["cw"], lw["cb"], lw["wdn"], lw["ln2_g"], lw["ln2_b"])
    return pl.pallas_call(
        functools.partial(_ffn_kernel, alpha=alpha, tm=tm, per_seq=per_seq),
        grid=(t // tm,),
        in_specs=[pl.BlockSpec((HALO, D_MODEL), lambda i: (jnp.maximum(i * (tm // HALO) - 1, 0), 0)),
                  pl.BlockSpec((tm, D_MODEL), lambda i: (i, 0))] + [full(a) for a in consts],
        out_specs=pl.BlockSpec((tm, D_MODEL), lambda i: (i, 0)),
        out_shape=jax.ShapeDtypeStruct((t, D_MODEL), F32),
        scratch_shapes=[pltpu.VMEM((tm + HALO, FF_CHUNK), F32),
                        pltpu.VMEM((tm + HALO, FF_CHUNK), F32)],
        compiler_params=_cparams(1),
        name="ffn",
    )(x2d, x2d, *consts)


def _layer_weights(l, w_in, b_gate, q_norm_g, kv_norm_g, w_uq, w_ukv, w_branch, w_out,
                   ln1_g, ln1_b, w_ffn_up, conv_w, conv_b, w_ffn_down, ln2_g, ln2_b):
    wi = w_in[l].astype(BF16)
    a_end = 4608
    bq = wi[:, a_end:a_end + 512].reshape(D_MODEL, B_Q_HEADS, HEAD_DIM)
    bq = jnp.stack([bq[:, :4], bq[:, 4:]], axis=2).reshape(D_MODEL, 512)
    bkv = wi[:, 5120:5376]
    cdq = wi[:, 5376:5632]
    ckv = wi[:, 5632:5760]
    kr = wi[:, 5760:5792]
    kr_rot = jnp.concatenate([-kr[:, C_ROPE // 2:], kr[:, :C_ROPE // 2]], axis=1)
    gate = wi[:, 5792:]
    zeros = lambda n: jnp.zeros((D_MODEL, n), wi.dtype)
    wp = jnp.concatenate([gate, wi[:, :A_COLS], bq, cdq, ckv, kr, kr_rot, zeros(64), bkv, zeros(256)],
                         axis=1)
    wa = [wi[:, g * A_COLS:(g + 1) * A_COLS] for g in range(1, len(A_GROUPS))]

    uq = w_uq[l].reshape(C_Q_RANK, C_HEADS, C_NOPE + C_ROPE)
    zq = lambda n: jnp.zeros((C_Q_RANK, C_HEADS, n), uq.dtype)
    half = C_ROPE // 2
    wq1 = jnp.concatenate([uq, zq(LANE - C_NOPE - C_ROPE)], axis=2)
    wq2 = jnp.concatenate([zq(C_NOPE), -uq[:, :, C_NOPE + half:], uq[:, :, C_NOPE:C_NOPE + half],
                           zq(LANE - C_NOPE - C_ROPE)], axis=2)
    ukv = w_ukv[l].reshape(C_KV_RANK, C_HEADS, C_NOPE + C_V)
    wk = jnp.concatenate([ukv[:, :, :C_NOPE], jnp.zeros((C_KV_RANK, C_HEADS, LANE - C_NOPE), ukv.dtype)],
                         axis=2)
    wv = jnp.concatenate([ukv[:, :, C_NOPE:], jnp.zeros((C_KV_RANK, C_HEADS, LANE - C_V), ukv.dtype)],
                         axis=2)

    wb = w_branch[l]
    wb1 = wb[1].reshape(B_Q_HEADS, HEAD_DIM, D_MODEL)
    wb1 = jnp.stack([wb1[:4], wb1[4:]], axis=1).reshape(512, D_MODEL)
    wb = jnp.stack([wb[0], wb1, wb[2]])

    sel_rows = np.zeros((2, 512, LANE), np.float32)
    for j in range(C_ROPE):
        sel_rows[0, C_Q_RANK + C_KV_RANK + j, C_NOPE + j] = 1.0
        sel_rows[1, C_Q_RANK + C_KV_RANK + C_ROPE + j, C_NOPE + j] = 1.0

    return dict(
        wp=wp, wa=wa,
        gq=q_norm_g[l].reshape(1, -1), gkv=kv_norm_g[l].reshape(1, -1),
        wq1=wq1.reshape(C_Q_RANK, -1).astype(BF16), wq2=wq2.reshape(C_Q_RANK, -1).astype(BF16),
        wk=wk.reshape(C_KV_RANK, -1).astype(BF16), wv=wv.reshape(C_KV_RANK, -1).astype(BF16),
        sela=jnp.asarray(sel_rows[0], BF16), selb=jnp.asarray(sel_rows[1], BF16),
        wb=wb.astype(BF16), wo=w_out[l].astype(BF16), bg=b_gate[l].reshape(1, -1),
        ln1_g=ln1_g[l].reshape(1, -1), ln1_b=ln1_b[l].reshape(1, -1),
        wup=w_ffn_up[l].astype(BF16), cw=conv_w[l], cb=conv_b[l].reshape(1, -1),
        wdn=w_ffn_down[l].astype(BF16),
        ln2_g=ln2_g[l].reshape(1, -1), ln2_b=ln2_b[l].reshape(1, -1),
    )


def _rope_tables(seq):
    pos = jnp.arange(seq, dtype=F32)
    inv_freq = ROPE_BASE ** (-jnp.arange(0, C_ROPE, 2, dtype=F32) / C_ROPE)
    ang = pos[:, None] * inv_freq[None, :]
    cos, sin = jnp.cos(ang), jnp.sin(ang)
    pad = jnp.zeros((seq, LANE - C_NOPE - C_ROPE), F32)
    cos_t = jnp.concatenate([jnp.ones((seq, C_NOPE), F32), cos, cos, pad], axis=1)
    sin_t = jnp.concatenate([jnp.zeros((seq, C_NOPE), F32), sin, sin, pad], axis=1)
    return cos_t, sin_t


def _expand_matrix():
    e = np.zeros((LANE, A_HEADS * HEAD_DIM), np.float32)
    for c in range(A_HEADS * HEAD_DIM):
        e[(LANE // A_HEADS) * (c // HEAD_DIM), c] = 1.0
    return jnp.asarray(e, BF16)


def kernel(x, rel_table, w_in, b_gate, sinks, q_norm_g, kv_norm_g, w_uq, w_ukv, w_branch, w_out,
           ln1_g, ln1_b, w_ffn_up, conv_w, conv_b, w_ffn_down, ln2_g, ln2_b):
    batch, seq, d = x.shape
    depth = w_in.shape[0]
    alpha = (2 * depth) ** 0.25
    cos_t, sin_t = _rope_tables(seq)
    expand = _expand_matrix()
    bias = _bias_tables(rel_table)
    x2d = x.reshape(batch * seq, d)
    for l in range(depth):
        lw = _layer_weights(l, w_in, b_gate, q_norm_g, kv_norm_g, w_uq, w_ukv, w_branch, w_out,
                            ln1_g, ln1_b, w_ffn_up, conv_w, conv_b, w_ffn_down, ln2_g, ln2_b)
        p2 = _proj(x2d, lw["wp"], dil=1, tm=1024, tn=A_COLS, name="proj")
        p_dil = []
        for g, (w, (_, dil)) in enumerate(zip(lw["wa"], A_GROUPS[1:])):
            tm = max(1024, BLOCK * dil)
            tn = A_COLS * 1024 // tm
            p_dil.append(_proj(x2d, w, dil=dil, tm=tm, tn=tn, name=f"proj_a{g + 1}"))
        sink_slots = sinks[l].reshape(2, 4).T.reshape(-1)
        oa, lse = [], []
        for kind in range(len(A_GROUPS)):
            src = p2 if kind == 0 else p_dil[kind - 1]
            o_g, lse_g = _band_attention(src, bias, sink_slots, kind=kind, batch=batch, seq=seq)
            oa.append(o_g)
            lse.append(lse_g)
        ob = _band_attention(p2, bias, sink_slots, kind=len(A_GROUPS), batch=batch, seq=seq)
        q, k, v = _mla_prep(p2, cos_t, sin_t, lw, seq=seq)
        oc = _mla_flash(q, k, v, batch=batch, seq=seq)
        x2d = _merge(oa, lse, ob, oc, p2, x2d, expand, lw, alpha=alpha)
        x2d = _ffn(x2d, lw, alpha=alpha, seq=seq)
    return x2d.reshape(batch, seq, d)
```

```python
import functools
import math

import jax
import jax.numpy as jnp
import numpy as np
from jax import lax
from jax.experimental import pallas as pl
from jax.experimental.pallas import tpu as pltpu

F32 = jnp.float32
BF16 = jnp.bfloat16

D_MODEL = 1024
HEAD_DIM = 64
BLOCK = 128
A_GROUPS = ((128, 1), (512, 4), (2048, 16))
A_HEADS = 8
B_Q_HEADS = 8
B_WINDOW = 128
C_HEADS = 8
C_Q_RANK = 256
C_KV_RANK = 128
C_NOPE = 64
C_ROPE = 32
C_V = 64
ROPE_BASE = 10000.0
REL_BUCKETS = 32
REL_MAX_DIST = 2048
D_FF = 2816
LN_EPS = 1e-5
RMS_EPS = 1e-6
NEG = -1e30
LOG2E = math.log2(math.e)
FLASH_BLK = 1024
FLASH_DIAG = 512

COL_GATE = 0
COL_A = 3072
COL_BQ = 4608
COL_C = 5120
COL_BKV = 5632
NP = 6144
A_COLS = 1536

LANE = 128
VMEM_LIMIT = 56 * 1024 * 1024

N_KINDS = 4


def _cparams(n_axes):
    return pltpu.CompilerParams(dimension_semantics=("arbitrary",) * n_axes,
                                vmem_limit_bytes=VMEM_LIMIT)


CHEAP_STRIDE = 4


def _proj_kernel(*refs, dil):
    n_x = len(refs) - (4 if dil > CHEAP_STRIDE else 3)
    x_refs, (w_ref, o_ref, xb_ref, *tmp) = refs[:n_x], refs[n_x:]

    @pl.when(pl.program_id(1) == 0)
    def _():
        if dil == 1:
            xb_ref[...] = x_refs[0][...].astype(BF16)
            return
        span = BLOCK * dil
        d1 = min(dil, CHEAP_STRIDE)
        d2 = dil // d1
        for c, x_ref in enumerate(x_refs):
            for s0 in range(0, xb_ref.shape[0], span):
                for r1 in range(d1):
                    if d2 == 1:
                        rows = x_ref[pl.ds(s0 + r1, BLOCK, stride=d1), :]
                        xb_ref[s0 + r1 * BLOCK:s0 + (r1 + 1) * BLOCK, c * LANE:(c + 1) * LANE] = rows.astype(BF16)
                        continue
                    g0 = s0 + r1 * (span // d1)
                    tmp[0][c, g0:g0 + span // d1, :] = x_ref[pl.ds(s0 + r1, span // d1, stride=d1), :]
                    for r2 in range(d2):
                        r = r1 + d1 * r2
                        rows = tmp[0][c, pl.ds(g0 + r2, BLOCK, stride=d2), :]
                        xb_ref[s0 + r * BLOCK:s0 + (r + 1) * BLOCK, c * LANE:(c + 1) * LANE] = rows.astype(BF16)

    o_ref[...] = jnp.dot(xb_ref[...], w_ref[...], preferred_element_type=F32).astype(o_ref.dtype)


def _proj(x2d, w, *, dil, tm, tn, name):
    t, k = x2d.shape
    n = w.shape[1]
    if dil == 1:
        x_specs = [pl.BlockSpec((tm, k), lambda i, j: (i, 0))]
    else:
        x_specs = [pl.BlockSpec((tm, LANE), lambda i, j, c=c: (i, c)) for c in range(k // LANE)]
    scratch = [pltpu.VMEM((tm, k), BF16)]
    if dil > CHEAP_STRIDE:
        scratch.append(pltpu.VMEM((k // LANE, tm, LANE), F32))
    return pl.pallas_call(
        functools.partial(_proj_kernel, dil=dil),
        grid=(t // tm, n // tn),
        in_specs=x_specs + [pl.BlockSpec((k, tn), lambda i, j: (0, j))],
        out_specs=pl.BlockSpec((tm, tn), lambda i, j: (i, j)),
        out_shape=jax.ShapeDtypeStruct((t, n), BF16),
        scratch_shapes=scratch,
        compiler_params=_cparams(2),
        name=name,
    )(*([x2d] * len(x_specs)), w)


def _t5_bucket(dist):
    n = jnp.maximum(dist, 0)
    max_exact = REL_BUCKETS // 2
    scaled = jnp.log(jnp.maximum(n, 1).astype(F32) / max_exact) / math.log(REL_MAX_DIST / max_exact)
    large = max_exact + (scaled * (REL_BUCKETS - max_exact)).astype(jnp.int32)
    return jnp.where(n < max_exact, n, jnp.minimum(large, REL_BUCKETS - 1))


def _bias_codes():
    qi = jnp.arange(BLOCK)[:, None]
    ki = jnp.arange(2 * BLOCK)[None, :]
    step = BLOCK + qi - ki
    has_prev = ki >= BLOCK
    codes = []
    for kind in range(N_KINDS):
        if kind < len(A_GROUPS):
            dil = A_GROUPS[kind][1]
            band = (step >= 0) & (step <= BLOCK)
        else:
            dil = 1
            band = (step >= 0) & (step < B_WINDOW)
        bucket = _t5_bucket(step * dil)
        codes.append(jnp.stack([jnp.where(band & has_prev, bucket, -1),
                                jnp.where(band, bucket, -1)]))
    return jnp.stack(codes).astype(jnp.int32)


def _bias_kernel(rel_ref, code_ref, o_ref):
    kind = pl.program_id(0)
    code = code_ref[0, 0]
    for slot in range(A_HEADS):
        b_head = len(A_GROUPS) * A_HEADS + slot // 2 + 4 * (slot % 2)
        col = jnp.where(kind < len(A_GROUPS), kind * A_HEADS + slot, b_head)
        acc = jnp.full(code.shape, NEG, F32)
        for b in range(REL_BUCKETS):
            acc = jnp.where(code == b, rel_ref[b, col] * LOG2E, acc)
        o_ref[0, 0, slot] = acc


def _bias_tables(rel_table):
    codes = _bias_codes()
    return pl.pallas_call(
        _bias_kernel,
        grid=(N_KINDS, 2),
        in_specs=[pl.BlockSpec(memory_space=pltpu.SMEM),
                  pl.BlockSpec((1, 1, BLOCK, 2 * BLOCK), lambda k, v: (k, v, 0, 0))],
        out_specs=pl.BlockSpec((1, 1, A_HEADS, BLOCK, 2 * BLOCK), lambda k, v: (k, v, 0, 0, 0)),
        out_shape=jax.ShapeDtypeStruct((N_KINDS, 2, A_HEADS, BLOCK, 2 * BLOCK), F32),
        compiler_params=_cparams(2),
        name="bias_tables",
    )(rel_table, codes)


BAND_QB = 8
STAT_LANES = LANE // A_HEADS // 2


def _band_kernel(sink_ref, q_ref, kp_ref, kc_ref, vp_ref, vc_ref, bias_ref, o_ref, *lse_refs,
                 shared_kv, with_sink):
    first = pl.program_id(2) == 0
    lane = lax.broadcasted_iota(jnp.int32, (BLOCK, LANE), 1)
    low = lane < HEAD_DIM
    for qb in range(q_ref.shape[0]):
        variant = jnp.where(first, 0, 1) if qb == 0 else 1
        lse_tile = jnp.zeros((BLOCK, LANE), F32)
        for pair in range(4):
            cols = slice(pair * LANE, (pair + 1) * LANE)
            kv_cols = slice(0, LANE) if shared_kv else cols
            qp = q_ref[qb, :, cols]
            k_prev = kp_ref[:, kv_cols] if qb == 0 else kc_ref[qb - 1, :, kv_cols]
            v_prev = vp_ref[:, kv_cols] if qb == 0 else vc_ref[qb - 1, :, kv_cols]
            kcat = jnp.concatenate([k_prev, kc_ref[qb, :, kv_cols]], axis=0)
            vcat = jnp.concatenate([v_prev, vc_ref[qb, :, kv_cols]], axis=0)
            zero = jnp.zeros_like(qp)
            q2 = jnp.concatenate([jnp.where(low, qp, zero), jnp.where(low, zero, qp)], axis=0)
            s = lax.dot_general(q2, kcat, (((1,), (1,)), ((), ())), preferred_element_type=F32)
            s = s + bias_ref[variant, pair]
            m = jnp.max(s, axis=-1, keepdims=True)
            if with_sink:
                head_a = lax.broadcasted_iota(jnp.int32, (2 * BLOCK, 1), 0) < BLOCK
                sink = jnp.where(head_a, sink_ref[2 * pair], sink_ref[2 * pair + 1])
                m = jnp.maximum(m, sink)
            p = jnp.exp2(s - m)
            l = jnp.sum(p, axis=-1, keepdims=True)
            if with_sink:
                l = l + jnp.exp2(sink - m)
            o = jnp.dot(p.astype(BF16), vcat, preferred_element_type=F32)
            if lse_refs:
                for sub in range(2):
                    rows = slice(sub * BLOCK, (sub + 1) * BLOCK)
                    field = 2 * (2 * pair + sub)
                    lse_tile = jnp.where(lane // STAT_LANES == field, m[rows],
                                         jnp.where(lane // STAT_LANES == field + 1, l[rows], lse_tile))
            else:
                o = o * (1.0 / l)
            o_ref[qb, :, cols] = jnp.where(low, o[:BLOCK], o[BLOCK:]).astype(o_ref.dtype)
        if lse_refs:
            lse_refs[0][qb] = lse_tile


def _band_attention(p2, bias, sinks, *, kind, batch, seq):
    is_b = kind == len(A_GROUPS)
    dil = 1 if is_b else A_GROUPS[kind][1]
    n_span = seq // (BLOCK * dil)
    view = p2.reshape(batch, n_span, dil, BLOCK, p2.shape[-1])
    if is_b:
        q_blk, k_blk, v_blk, kv_w = COL_BQ // 512, COL_BKV // LANE, COL_BKV // LANE + 1, LANE
    else:
        q_blk = COL_A // 512 if dil == 1 else 0
        k_blk, v_blk, kv_w = q_blk + 1, q_blk + 2, 512

    n_qb = min(BAND_QB, n_span)

    def cur(blk, width):
        return pl.BlockSpec((None, n_qb, None, BLOCK, width), lambda b, r, n: (b, n, r, 0, blk))

    def prev(blk, width):
        return pl.BlockSpec((None, None, None, BLOCK, width),
                            lambda b, r, n: (b, jnp.maximum(n * n_qb - 1, 0), r, 0, blk))

    in_specs = [
        pl.BlockSpec(memory_space=pltpu.SMEM),
        cur(q_blk, 512), prev(k_blk, kv_w), cur(k_blk, kv_w), prev(v_blk, kv_w), cur(v_blk, kv_w),
        pl.BlockSpec((None, 2, A_HEADS // 2, 2 * BLOCK, 2 * BLOCK), lambda b, r, n: (kind, 0, 0, 0, 0)),
    ]
    bias = bias.reshape(N_KINDS, 2, A_HEADS // 2, 2 * BLOCK, 2 * BLOCK)
    out_specs = [pl.BlockSpec((None, n_qb, None, BLOCK, 512), lambda b, r, n: (b, n, r, 0, 0))]
    out_shape = [jax.ShapeDtypeStruct((batch, n_span, dil, BLOCK, 512), BF16)]
    if not is_b:
        out_specs.append(pl.BlockSpec((None, n_qb, None, BLOCK, LANE), lambda b, r, n: (b, n, r, 0, 0)))
        out_shape.append(jax.ShapeDtypeStruct((batch, n_span, dil, BLOCK, LANE), F32))
    outs = pl.pallas_call(
        functools.partial(_band_kernel, shared_kv=is_b, with_sink=is_b),
        grid=(batch, dil, n_span // n_qb),
        in_specs=in_specs,
        out_specs=out_specs,
        out_shape=out_shape,
        compiler_params=_cparams(3),
        name="band_b" if is_b else f"band_a{kind}",
    )(sinks, view, view, view, view, view, bias)
    return outs[0] if is_b else outs


def _mla_prep_kernel(c_ref, cos_ref, sin_ref, gq_ref, gkv_ref, wq1_ref, wq2_ref, wk_ref, wv_ref,
                     sela_ref, selb_ref, q_ref, k_ref, v_ref):
    c = c_ref[...]
    cq = c[:, :C_Q_RANK].astype(F32)
    ckv = c[:, C_Q_RANK:C_Q_RANK + C_KV_RANK].astype(F32)
    nq = cq * lax.rsqrt(jnp.mean(cq * cq, axis=-1, keepdims=True) + RMS_EPS) * gq_ref[...]
    nkv = ckv * lax.rsqrt(jnp.mean(ckv * ckv, axis=-1, keepdims=True) + RMS_EPS) * gkv_ref[...]
    nq = nq.astype(BF16)
    nkv = nkv.astype(BF16)
    cos = cos_ref[...]
    sin = sin_ref[...]
    cos8 = jnp.tile(cos, (1, C_HEADS))
    sin8 = jnp.tile(sin, (1, C_HEADS))
    q = (jnp.dot(nq, wq1_ref[...], preferred_element_type=F32) * cos8
         + jnp.dot(nq, wq2_ref[...], preferred_element_type=F32) * sin8)
    q_ref[...] = q.astype(q_ref.dtype)
    kr = (jnp.dot(c, sela_ref[...], preferred_element_type=F32) * cos
          + jnp.dot(c, selb_ref[...], preferred_element_type=F32) * sin)
    k = jnp.dot(nkv, wk_ref[...], preferred_element_type=F32) + jnp.tile(kr, (1, C_HEADS))
    k_ref[...] = k.astype(k_ref.dtype)
    lane = lax.broadcasted_iota(jnp.int32, (1, C_HEADS * LANE), 1)
    ones_col = jnp.where(lane % LANE == C_V, 1.0, 0.0).astype(F32)
    v_ref[...] = (jnp.dot(nkv, wv_ref[...], preferred_element_type=F32) + ones_col).astype(v_ref.dtype)


def _mla_prep(p2, cos_t, sin_t, lw, *, seq, tm=512):
    t = p2.shape[0]
    per_seq = seq // tm
    full = lambda a: pl.BlockSpec(a.shape, lambda i: (0,) * a.ndim)
    consts = (lw["gq"], lw["gkv"], lw["wq1"], lw["wq2"], lw["wk"], lw["wv"], lw["sela"], lw["selb"])
    wide = C_HEADS * LANE
    return pl.pallas_call(
        _mla_prep_kernel,
        grid=(t // tm,),
        in_specs=[pl.BlockSpec((tm, 512), lambda i: (i, COL_C // 512)),
                  pl.BlockSpec((tm, LANE), lambda i: (i % per_seq, 0)),
                  pl.BlockSpec((tm, LANE), lambda i: (i % per_seq, 0))] + [full(a) for a in consts],
        out_specs=[pl.BlockSpec((tm, wide), lambda i: (i, 0))] * 3,
        out_shape=[jax.ShapeDtypeStruct((t, wide), BF16)] * 3,
        compiler_params=_cparams(1),
        name="mla_prep",
    )(p2, cos_t, sin_t, *consts)


def _mla_flash_kernel(q_ref, k_ref, v_ref, o_ref, m_ref, acc_ref, *, blk):
    qi = pl.program_id(1)
    ki = pl.program_id(2)

    @pl.when(ki == 0)
    def _():
        m_ref[...] = jnp.full(m_ref.shape, NEG, F32)
        acc_ref[...] = jnp.zeros(acc_ref.shape, F32)

    def update(h, rows, n_keys, diagonal):
        hc = slice(h * LANE, (h + 1) * LANE)
        s = lax.dot_general(q_ref[rows, hc], k_ref[:n_keys, hc], (((1,), (1,)), ((), ())),
                            preferred_element_type=F32)
        if diagonal:
            row = lax.broadcasted_iota(jnp.int32, s.shape, 0) + rows.start
            col = lax.broadcasted_iota(jnp.int32, s.shape, 1)
            s = jnp.where(row >= col, s, NEG)
        m_prev = m_ref[h, rows]
        m_new = jnp.maximum(m_prev, jnp.max(s, axis=-1, keepdims=True))
        alpha = jnp.exp2(m_prev - m_new)
        p = jnp.exp2(s - jnp.tile(m_new, (1, n_keys // LANE)))
        acc_ref[h, rows] = alpha * acc_ref[h, rows] + jnp.dot(p.astype(BF16), v_ref[:n_keys, hc],
                                                              preferred_element_type=F32)
        m_ref[h, rows] = m_new

    @pl.when(ki < qi)
    def _():
        for h in range(C_HEADS):
            update(h, slice(0, blk), blk, False)

    @pl.when(ki == qi)
    def _():
        for h in range(C_HEADS):
            for r0 in range(0, blk, FLASH_DIAG):
                update(h, slice(r0, r0 + FLASH_DIAG), r0 + FLASH_DIAG, True)
        low = lax.broadcasted_iota(jnp.int32, (blk, LANE), 1) < C_V
        for pair in range(C_HEADS // 2):
            halves = []
            for h in (2 * pair, 2 * pair + 1):
                acc = acc_ref[h]
                halves.append(acc * (1.0 / acc[:, C_V:C_V + 1]))
            odd = pltpu.roll(halves[1], C_V, axis=1)
            o_ref[:, pair * LANE:(pair + 1) * LANE] = jnp.where(low, halves[0], odd).astype(o_ref.dtype)


def _mla_flash(q, k, v, *, batch, seq, blk=FLASH_BLK):
    wide = C_HEADS * LANE
    q3 = q.reshape(batch, seq, wide)
    k3 = k.reshape(batch, seq, wide)
    v3 = v.reshape(batch, seq, wide)
    kv_map = lambda b, i, j: (b, jnp.minimum(j, i), 0)
    out = pl.pallas_call(
        functools.partial(_mla_flash_kernel, blk=blk),
        grid=(batch, seq // blk, seq // blk),
        in_specs=[pl.BlockSpec((None, blk, wide), lambda b, i, j: (b, i, 0)),
                  pl.BlockSpec((None, blk, wide), kv_map),
                  pl.BlockSpec((None, blk, wide), kv_map)],
        out_specs=pl.BlockSpec((None, blk, C_HEADS * C_V), lambda b, i, j: (b, i, 0)),
        out_shape=jax.ShapeDtypeStruct((batch, seq, C_HEADS * C_V), BF16),
        scratch_shapes=[pltpu.VMEM((C_HEADS, blk, LANE), F32),
                        pltpu.VMEM((C_HEADS, blk, LANE), F32)],
        compiler_params=_cparams(3),
        name="mla_flash",
    )(q3, k3, v3)
    return out.reshape(batch * seq, C_HEADS * C_V)


def _layer_norm(y, g, b):
    mu = jnp.mean(y, axis=-1, keepdims=True)
    d = y - mu
    var = jnp.mean(d * d, axis=-1, keepdims=True)
    return d * lax.rsqrt(var + LN_EPS) * g + b


def _token_order(src_ref, tmp_ref, dil):
    if dil == 1:
        return src_ref[...].astype(F32)
    n = src_ref.shape[1]
    n_slab = src_ref.shape[2] // LANE
    for r in range(dil):
        rows = src_ref[r].astype(F32)
        for c in range(n_slab):
            tmp_ref[c, pl.ds(r, n, stride=dil), :] = rows[:, c * LANE:(c + 1) * LANE]
    return jnp.concatenate([tmp_ref[c] for c in range(n_slab)], axis=1)


def _merge_kernel(oa0_ref, oa1_ref, oa2_ref, l0_ref, l1_ref, l2_ref, ob_ref, oc_ref, gate_ref, x_ref,
                  e_ref, wb_ref, wo_ref, bg_ref, g_ref, b_ref, o_ref, ot1_ref, ot2_ref, lt1_ref, lt2_ref,
                  *, alpha):
    dils = [d for _, d in A_GROUPS]
    stats = [_token_order(ref, tmp, d)
             for ref, tmp, d in zip((l0_ref, l1_ref, l2_ref), (None, lt1_ref, lt2_ref), dils)]
    is_max = lax.broadcasted_iota(jnp.int32, stats[0].shape, 1) // STAT_LANES % 2 == 0
    top = jnp.maximum(jnp.maximum(stats[0], stats[1]), stats[2])
    es = [jnp.exp2(v - top) for v in stats]
    sums = [pltpu.roll(v, LANE - STAT_LANES, axis=1) for v in stats]
    inv = 1.0 / (es[0] * sums[0] + es[1] * sums[1] + es[2] * sums[2])
    o_a = None
    for e, oa_ref, tmp, d in zip(es, (oa0_ref, oa1_ref, oa2_ref), (None, ot1_ref, ot2_ref), dils):
        w = jnp.where(is_max, e * inv, 0.0)
        hi = w.astype(BF16)
        lo = (w - hi.astype(F32)).astype(BF16)
        wide = (jnp.dot(hi, e_ref[...], preferred_element_type=F32)
                + jnp.dot(lo, e_ref[...], preferred_element_type=F32))
        term = wide * _token_order(oa_ref, tmp, d)
        o_a = term if o_a is None else o_a + term
    branches = (o_a.astype(BF16), ob_ref[...], oc_ref[...])
    merged = None
    for i, br in enumerate(branches):
        gate = jax.nn.sigmoid(gate_ref[:, i * D_MODEL:(i + 1) * D_MODEL].astype(F32)
                              + bg_ref[:, i * D_MODEL:(i + 1) * D_MODEL])
        term = gate * jnp.dot(br, wb_ref[i], preferred_element_type=F32)
        merged = term if merged is None else merged + term
    mix = jnp.dot(merged.astype(BF16), wo_ref[...], preferred_element_type=F32)
    o_ref[...] = _layer_norm(alpha * x_ref[...] + mix, g_ref[...], b_ref[...])


def _merge(oa, lse, ob, oc, p2, x2d, expand, lw, *, alpha, tm=512):
    t = x2d.shape[0]
    row = lambda w: pl.BlockSpec((tm, w), lambda i: (i, 0))
    full = lambda a: pl.BlockSpec(a.shape, lambda i: (0,) * a.ndim)

    def grouped(arrs):
        views, specs = [], []
        for a, (_, dil) in zip(arrs, A_GROUPS):
            w = a.shape[-1]
            if dil == 1:
                views.append(a.reshape(t, w))
                specs.append(row(w))
                continue
            per_span = BLOCK * dil // tm
            views.append(a.reshape(-1, dil, BLOCK, w))
            specs.append(pl.BlockSpec((None, dil, tm // dil, w),
                                      lambda i, per_span=per_span: (i // per_span, 0, i % per_span, 0)))
        return views, specs

    oa_v, oa_s = grouped(oa)
    lse_v, lse_s = grouped(lse)
    consts = (expand, lw["wb"], lw["wo"], lw["bg"], lw["ln1_g"], lw["ln1_b"])
    slabs = lambda w: pltpu.VMEM((w // LANE, tm, LANE), F32)
    return pl.pallas_call(
        functools.partial(_merge_kernel, alpha=alpha),
        grid=(t // tm,),
        in_specs=oa_s + lse_s + [row(512), row(512),
                  pl.BlockSpec((tm, 3 * D_MODEL), lambda i: (i, COL_GATE // (3 * D_MODEL))),
                  row(D_MODEL)] + [full(a) for a in consts],
        out_specs=row(D_MODEL),
        out_shape=jax.ShapeDtypeStruct((t, D_MODEL), F32),
        scratch_shapes=[slabs(512), slabs(512), slabs(LANE), slabs(LANE)],
        compiler_params=_cparams(1),
        name="merge",
    )(*oa_v, *lse_v, ob.reshape(t, 512), oc, p2, x2d, *consts)


HALO = 8
FF_CHUNK = 512


def _ffn_kernel(halo_ref, x_ref, wup_ref, cw_ref, cb_ref, wdn_ref, g_ref, b_ref, o_ref,
                ug_ref, uv_ref, *, alpha, tm, per_seq):
    i = pl.program_id(0)
    x = x_ref[...]
    halo = jnp.where(i % per_seq == 0, jnp.zeros_like(halo_ref[...]), halo_ref[...])
    xh = jnp.concatenate([halo, x], axis=0).astype(BF16)
    chunks = [(c0, min(FF_CHUNK, D_FF - c0)) for c0 in range(0, D_FF, FF_CHUNK)]

    def up(c0, cf):
        return [jnp.dot(xh, wup_ref[:, base:base + cf], preferred_element_type=F32)
                for base in (c0, D_FF + c0)]

    acc = None
    u_next = up(*chunks[0])
    for ci, (c0, cf) in enumerate(chunks):
        u_cur = u_next
        if ci + 1 < len(chunks):
            u_next = up(*chunks[ci + 1])
        parts = []
        for u, base in zip(u_cur, (c0, D_FF + c0)):
            y = cb_ref[:, base:base + cf]
            for tap in range(3):
                y = y + u[HALO - 2 + tap:HALO - 2 + tap + tm] * cw_ref[tap:tap + 1, base:base + cf]
            parts.append(y)
        act = (parts[0] * jax.nn.sigmoid(parts[0]) * parts[1]).astype(BF16)
        term = jnp.dot(act, wdn_ref[c0:c0 + cf, :], preferred_element_type=F32)
        acc = term if acc is None else acc + term
    o_ref[...] = _layer_norm(alpha * x + acc, g_ref[...], b_ref[...])


def _ffn(x2d, lw, *, alpha, seq, tm=512):
    t = x2d.shape[0]
    per_seq = seq // tm
    full = lambda a: pl.BlockSpec(a.shape, lambda i: (0,) * a.ndim, pipeline_mode=pl.Buffered(1))
    consts = (lw["wup"], lw["cw"], lw["cb"], lw["wdn"], lw["ln2_g"], lw["ln2_b"])
    return pl.pallas_call(
        functools.partial(_ffn_kernel, alpha=alpha, tm=tm, per_seq=per_seq),
        grid=(t // tm,),
        in_specs=[pl.BlockSpec((HALO, D_MODEL), lambda i: (jnp.maximum(i * (tm // HALO) - 1, 0), 0)),
                  pl.BlockSpec((tm, D_MODEL), lambda i: (i, 0))] + [full(a) for a in consts],
        out_specs=pl.BlockSpec((tm, D_MODEL), lambda i: (i, 0)),
        out_shape=jax.ShapeDtypeStruct((t, D_MODEL), F32),
        scratch_shapes=[pltpu.VMEM((tm + HALO, FF_CHUNK), F32),
                        pltpu.VMEM((tm + HALO, FF_CHUNK), F32)],
        compiler_params=_cparams(1),
        name="ffn",
    )(x2d, x2d, *consts)


def _layer_weights(l, w_in, b_gate, q_norm_g, kv_norm_g, w_uq, w_ukv, w_branch, w_out,
                   ln1_g, ln1_b, w_ffn_up, conv_w, conv_b, w_ffn_down, ln2_g, ln2_b):
    a_end = 4608
    col = np.arange(w_in.shape[-1])
    is_q = ((col < a_end) & (col % A_COLS < A_HEADS * HEAD_DIM)) | ((col >= a_end) & (col < a_end + 512))
    col_scale = jnp.asarray(np.where(is_q, HEAD_DIM ** -0.5 * LOG2E, 1.0), F32)
    wi = (w_in[l] * col_scale).astype(BF16)
    bq = wi[:, a_end:a_end + 512].reshape(D_MODEL, B_Q_HEADS, HEAD_DIM)
    bq = jnp.stack([bq[:, :4], bq[:, 4:]], axis=2).reshape(D_MODEL, 512)
    bkv = wi[:, 5120:5376]
    cdq = wi[:, 5376:5632]
    ckv = wi[:, 5632:5760]
    kr = wi[:, 5760:5792]
    kr_rot = jnp.concatenate([-kr[:, C_ROPE // 2:], kr[:, :C_ROPE // 2]], axis=1)
    gate = wi[:, 5792:]
    zeros = lambda n: jnp.zeros((D_MODEL, n), wi.dtype)
    wp = jnp.concatenate([gate, wi[:, :A_COLS], bq, cdq, ckv, kr, kr_rot, zeros(64), bkv, zeros(256)],
                         axis=1)
    wa = [wi[:, g * A_COLS:(g + 1) * A_COLS] for g in range(1, len(A_GROUPS))]

    uq = (w_uq[l] * ((C_NOPE + C_ROPE) ** -0.5 * LOG2E)).reshape(C_Q_RANK, C_HEADS, C_NOPE + C_ROPE)
    zq = lambda n: jnp.zeros((C_Q_RANK, C_HEADS, n), uq.dtype)
    half = C_ROPE // 2
    wq1 = jnp.concatenate([uq, zq(LANE - C_NOPE - C_ROPE)], axis=2)
    wq2 = jnp.concatenate([zq(C_NOPE), -uq[:, :, C_NOPE + half:], uq[:, :, C_NOPE:C_NOPE + half],
                           zq(LANE - C_NOPE - C_ROPE)], axis=2)
    ukv = w_ukv[l].reshape(C_KV_RANK, C_HEADS, C_NOPE + C_V)
    wk = jnp.concatenate([ukv[:, :, :C_NOPE], jnp.zeros((C_KV_RANK, C_HEADS, LANE - C_NOPE), ukv.dtype)],
                         axis=2)
    wv = jnp.concatenate([ukv[:, :, C_NOPE:], jnp.zeros((C_KV_RANK, C_HEADS, LANE - C_V), ukv.dtype)],
                         axis=2)

    wb = w_branch[l]
    wb1 = wb[1].reshape(B_Q_HEADS, HEAD_DIM, D_MODEL)
    wb1 = jnp.stack([wb1[:4], wb1[4:]], axis=1).reshape(512, D_MODEL)
    wb = jnp.stack([wb[0], wb1, wb[2]])

    sel_rows = np.zeros((2, 512, LANE), np.float32)
    for j in range(C_ROPE):
        sel_rows[0, C_Q_RANK + C_KV_RANK + j, C_NOPE + j] = 1.0
        sel_rows[1, C_Q_RANK + C_KV_RANK + C_ROPE + j, C_NOPE + j] = 1.0

    return dict(
        wp=wp, wa=wa,
        gq=q_norm_g[l].reshape(1, -1), gkv=kv_norm_g[l].reshape(1, -1),
        wq1=wq1.reshape(C_Q_RANK, -1).astype(BF16), wq2=wq2.reshape(C_Q_RANK, -1).astype(BF16),
        wk=wk.reshape(C_KV_RANK, -1).astype(BF16), wv=wv.reshape(C_KV_RANK, -1).astype(BF16),
        sela=jnp.asarray(sel_rows[0], BF16), selb=jnp.asarray(sel_rows[1], BF16),
        wb=wb.astype(BF16), wo=w_out[l].astype(BF16), bg=b_gate[l].reshape(1, -1),
        ln1_g=ln1_g[l].reshape(1, -1), ln1_b=ln1_b[l].reshape(1, -1),
        wup=w_ffn_up[l].astype(BF16), cw=conv_w[l], cb=conv_b[l].reshape(1, -1),
        wdn=w_ffn_down[l].astype(BF16),
        ln2_g=ln2_g[l].reshape(1, -1), ln2_b=ln2_b[l].reshape(1, -1),
    )


def _rope_tables(seq):
    pos = jnp.arange(seq, dtype=F32)
    inv_freq = ROPE_BASE ** (-jnp.arange(0, C_ROPE, 2, dtype=F32) / C_ROPE)
    ang = pos[:, None] * inv_freq[None, :]
    cos, sin = jnp.cos(ang), jnp.sin(ang)
    pad = jnp.zeros((seq, LANE - C_NOPE - C_ROPE), F32)
    cos_t = jnp.concatenate([jnp.ones((seq, C_NOPE), F32), cos, cos, pad], axis=1)
    sin_t = jnp.concatenate([jnp.zeros((seq, C_NOPE), F32), sin, sin, pad], axis=1)
    return cos_t, sin_t


def _expand_matrix():
    e = np.zeros((LANE, A_HEADS * HEAD_DIM), np.float32)
    for c in range(A_HEADS * HEAD_DIM):
        e[(LANE // A_HEADS) * (c // HEAD_DIM), c] = 1.0
    return jnp.asarray(e, BF16)


def kernel(x, rel_table, w_in, b_gate, sinks, q_norm_g, kv_norm_g, w_uq, w_ukv, w_branch, w_out,
           ln1_g, ln1_b, w_ffn_up, conv_w, conv_b, w_ffn_down, ln2_g, ln2_b):
    batch, seq, d = x.shape
    depth = w_in.shape[0]
    alpha = (2 * depth) ** 0.25
    cos_t, sin_t = _rope_tables(seq)
    expand = _expand_matrix()
    bias = _bias_tables(rel_table)
    x2d = x.reshape(batch * seq, d)
    for l in range(depth):
        lw = _layer_weights(l, w_in, b_gate, q_norm_g, kv_norm_g, w_uq, w_ukv, w_branch, w_out,
                            ln1_g, ln1_b, w_ffn_up, conv_w, conv_b, w_ffn_down, ln2_g, ln2_b)
        p2 = _proj(x2d, lw["wp"], dil=1, tm=1024, tn=A_COLS, name="proj")
        p_dil = []
        for g, (w, (_, dil)) in enumerate(zip(lw["wa"], A_GROUPS[1:])):
            tm = max(1024, BLOCK * dil)
            tn = A_COLS * 1024 // tm
            p_dil.append(_proj(x2d, w, dil=dil, tm=tm, tn=tn, name=f"proj_a{g + 1}"))
        sink_slots = sinks[l].reshape(2, 4).T.reshape(-1) * LOG2E
        oa, lse = [], []
        for kind in range(len(A_GROUPS)):
            src = p2 if kind == 0 else p_dil[kind - 1]
            o_g, lse_g = _band_attention(src, bias, sink_slots, kind=kind, batch=batch, seq=seq)
            oa.append(o_g)
            lse.append(lse_g)
        ob = _band_attention(p2, bias, sink_slots, kind=len(A_GROUPS), batch=batch, seq=seq)
        q, k, v = _mla_prep(p2, cos_t, sin_t, lw, seq=seq)
        oc = _mla_flash(q, k, v, batch=batch, seq=seq)
        x2d = _merge(oa, lse, ob, oc, p2, x2d, expand, lw, alpha=alpha)
        x2d = _ffn(x2d, lw, alpha=alpha, seq=seq)
    return x2d.reshape(batch, seq, d)
```

```python
import functools
import math

import jax
import jax.numpy as jnp
import numpy as np
from jax import lax
from jax.experimental import pallas as pl
from jax.experimental.pallas import tpu as pltpu

F32 = jnp.float32
BF16 = jnp.bfloat16

D_MODEL = 1024
HEAD_DIM = 64
BLOCK = 128
A_GROUPS = ((128, 1), (512, 4), (2048, 16))
A_HEADS = 8
B_Q_HEADS = 8
B_WINDOW = 128
C_HEADS = 8
C_Q_RANK = 256
C_KV_RANK = 128
C_NOPE = 64
C_ROPE = 32
C_V = 64
ROPE_BASE = 10000.0
REL_BUCKETS = 32
REL_MAX_DIST = 2048
D_FF = 2816
LN_EPS = 1e-5
RMS_EPS = 1e-6
NEG = -1e30
LOG2E = math.log2(math.e)
FLASH_BLK = 1024
FLASH_DIAG = 512

COL_GATE = 0
COL_A = 3072
COL_BQ = 4608
COL_C = 5120
COL_BKV = 5632
NP = 6144
A_COLS = 1536

LANE = 128
VMEM_LIMIT = 56 * 1024 * 1024

N_KINDS = 4


def _cparams(n_axes):
    return pltpu.CompilerParams(dimension_semantics=("arbitrary",) * n_axes,
                                vmem_limit_bytes=VMEM_LIMIT)


CHEAP_STRIDE = 4


def _proj_kernel(*refs, dil):
    n_x = len(refs) - (4 if dil > CHEAP_STRIDE else 3)
    x_refs, (w_ref, o_ref, xb_ref, *tmp) = refs[:n_x], refs[n_x:]

    @pl.when(pl.program_id(1) == 0)
    def _():
        if dil == 1:
            xb_ref[...] = x_refs[0][...].astype(BF16)
            return
        span = BLOCK * dil
        d1 = min(dil, CHEAP_STRIDE)
        d2 = dil // d1
        for c, x_ref in enumerate(x_refs):
            for s0 in range(0, xb_ref.shape[0], span):
                for r1 in range(d1):
                    if d2 == 1:
                        rows = x_ref[pl.ds(s0 + r1, BLOCK, stride=d1), :]
                        xb_ref[s0 + r1 * BLOCK:s0 + (r1 + 1) * BLOCK, c * LANE:(c + 1) * LANE] = rows.astype(BF16)
                        continue
                    g0 = s0 + r1 * (span // d1)
                    tmp[0][c, g0:g0 + span // d1, :] = x_ref[pl.ds(s0 + r1, span // d1, stride=d1), :]
                    for r2 in range(d2):
                        r = r1 + d1 * r2
                        rows = tmp[0][c, pl.ds(g0 + r2, BLOCK, stride=d2), :]
                        xb_ref[s0 + r * BLOCK:s0 + (r + 1) * BLOCK, c * LANE:(c + 1) * LANE] = rows.astype(BF16)

    w = w_ref[pl.program_id(1)]
    o_ref[...] = jnp.dot(xb_ref[...], w, preferred_element_type=F32).astype(o_ref.dtype)


def _proj(x2d, w, *, dil, tm, tn, name):
    t, k = x2d.shape
    n = w.shape[1]
    if dil == 1:
        x_specs = [pl.BlockSpec((tm, k), lambda i, j: (i, 0))]
    else:
        x_specs = [pl.BlockSpec((tm, LANE), lambda i, j, c=c: (i, c)) for c in range(k // LANE)]
    scratch = [pltpu.VMEM((tm, k), BF16)]
    if dil > CHEAP_STRIDE:
        scratch.append(pltpu.VMEM((k // LANE, tm, LANE), F32))
    w_tiles = w.reshape(k, n // tn, tn).transpose(1, 0, 2)
    return pl.pallas_call(
        functools.partial(_proj_kernel, dil=dil),
        grid=(t // tm, n // tn),
        in_specs=x_specs + [pl.BlockSpec((n // tn, k, tn), lambda i, j: (0, 0, 0),
                                         pipeline_mode=pl.Buffered(1))],
        out_specs=pl.BlockSpec((tm, tn), lambda i, j: (i, j)),
        out_shape=jax.ShapeDtypeStruct((t, n), BF16),
        scratch_shapes=scratch,
        compiler_params=_cparams(2),
        name=name,
    )(*([x2d] * len(x_specs)), w_tiles)


def _t5_bucket(dist):
    n = jnp.maximum(dist, 0)
    max_exact = REL_BUCKETS // 2
    scaled = jnp.log(jnp.maximum(n, 1).astype(F32) / max_exact) / math.log(REL_MAX_DIST / max_exact)
    large = max_exact + (scaled * (REL_BUCKETS - max_exact)).astype(jnp.int32)
    return jnp.where(n < max_exact, n, jnp.minimum(large, REL_BUCKETS - 1))


def _bias_codes():
    qi = jnp.arange(BLOCK)[:, None]
    ki = jnp.arange(2 * BLOCK)[None, :]
    step = BLOCK + qi - ki
    has_prev = ki >= BLOCK
    codes = []
    for kind in range(N_KINDS):
        if kind < len(A_GROUPS):
            dil = A_GROUPS[kind][1]
            band = (step >= 0) & (step <= BLOCK)
        else:
            dil = 1
            band = (step >= 0) & (step < B_WINDOW)
        bucket = _t5_bucket(step * dil)
        codes.append(jnp.stack([jnp.where(band & has_prev, bucket, -1),
                                jnp.where(band, bucket, -1)]))
    return jnp.stack(codes).astype(jnp.int32)


def _bias_kernel(rel_ref, code_ref, o_ref):
    kind = pl.program_id(0)
    code = code_ref[0, 0]
    for slot in range(A_HEADS):
        b_head = len(A_GROUPS) * A_HEADS + slot // 2 + 4 * (slot % 2)
        col = jnp.where(kind < len(A_GROUPS), kind * A_HEADS + slot, b_head)
        acc = jnp.full(code.shape, NEG, F32)
        for b in range(REL_BUCKETS):
            acc = jnp.where(code == b, rel_ref[b, col] * LOG2E, acc)
        o_ref[0, 0, slot] = acc


def _bias_tables(rel_table):
    codes = _bias_codes()
    return pl.pallas_call(
        _bias_kernel,
        grid=(N_KINDS, 2),
        in_specs=[pl.BlockSpec(memory_space=pltpu.SMEM),
                  pl.BlockSpec((1, 1, BLOCK, 2 * BLOCK), lambda k, v: (k, v, 0, 0))],
        out_specs=pl.BlockSpec((1, 1, A_HEADS, BLOCK, 2 * BLOCK), lambda k, v: (k, v, 0, 0, 0)),
        out_shape=jax.ShapeDtypeStruct((N_KINDS, 2, A_HEADS, BLOCK, 2 * BLOCK), F32),
        compiler_params=_cparams(2),
        name="bias_tables",
    )(rel_table, codes)


BAND_QB = 8
STAT_LANES = LANE // A_HEADS // 2


def _band_kernel(sink_ref, q_ref, kp_ref, kc_ref, vp_ref, vc_ref, bias_ref, o_ref, *lse_refs,
                 shared_kv, with_sink):
    first = pl.program_id(2) == 0
    lane = lax.broadcasted_iota(jnp.int32, (BLOCK, LANE), 1)
    low = lane < HEAD_DIM
    for qb in range(q_ref.shape[0]):
        variant = jnp.where(first, 0, 1) if qb == 0 else 1
        lse_tile = jnp.zeros((BLOCK, LANE), F32)
        for pair in range(4):
            cols = slice(pair * LANE, (pair + 1) * LANE)
            kv_cols = slice(0, LANE) if shared_kv else cols
            qp = q_ref[qb, :, cols]
            k_prev = kp_ref[:, kv_cols] if qb == 0 else kc_ref[qb - 1, :, kv_cols]
            v_prev = vp_ref[:, kv_cols] if qb == 0 else vc_ref[qb - 1, :, kv_cols]
            kcat = jnp.concatenate([k_prev, kc_ref[qb, :, kv_cols]], axis=0)
            vcat = jnp.concatenate([v_prev, vc_ref[qb, :, kv_cols]], axis=0)
            zero = jnp.zeros_like(qp)
            q2 = jnp.concatenate([jnp.where(low, qp, zero), jnp.where(low, zero, qp)], axis=0)
            s = lax.dot_general(q2, kcat, (((1,), (1,)), ((), ())), preferred_element_type=F32)
            s = s + bias_ref[variant, pair]
            m = jnp.max(s, axis=-1, keepdims=True)
            if with_sink:
                head_a = lax.broadcasted_iota(jnp.int32, (2 * BLOCK, 1), 0) < BLOCK
                sink = jnp.where(head_a, sink_ref[2 * pair], sink_ref[2 * pair + 1])
                m = jnp.maximum(m, sink)
            p = jnp.exp2(s - m)
            l = jnp.sum(p, axis=-1, keepdims=True)
            if with_sink:
                l = l + jnp.exp2(sink - m)
            o = jnp.dot(p.astype(BF16), vcat, preferred_element_type=F32)
            if lse_refs:
                for sub in range(2):
                    rows = slice(sub * BLOCK, (sub + 1) * BLOCK)
                    field = 2 * (2 * pair + sub)
                    lse_tile = jnp.where(lane // STAT_LANES == field, m[rows],
                                         jnp.where(lane // STAT_LANES == field + 1, l[rows], lse_tile))
            else:
                o = o * (1.0 / l)
            o_ref[qb, :, cols] = jnp.where(low, o[:BLOCK], o[BLOCK:]).astype(o_ref.dtype)
        if lse_refs:
            lse_refs[0][qb] = lse_tile


def _band_attention(p2, bias, sinks, *, kind, batch, seq):
    is_b = kind == len(A_GROUPS)
    dil = 1 if is_b else A_GROUPS[kind][1]
    n_span = seq // (BLOCK * dil)
    view = p2.reshape(batch, n_span, dil, BLOCK, p2.shape[-1])
    if is_b:
        q_blk, k_blk, v_blk, kv_w = COL_BQ // 512, COL_BKV // LANE, COL_BKV // LANE + 1, LANE
    else:
        q_blk = COL_A // 512 if dil == 1 else 0
        k_blk, v_blk, kv_w = q_blk + 1, q_blk + 2, 512

    n_qb = min(BAND_QB, n_span)

    def cur(blk, width):
        return pl.BlockSpec((None, n_qb, None, BLOCK, width), lambda b, r, n: (b, n, r, 0, blk))

    def prev(blk, width):
        return pl.BlockSpec((None, None, None, BLOCK, width),
                            lambda b, r, n: (b, jnp.maximum(n * n_qb - 1, 0), r, 0, blk))

    in_specs = [
        pl.BlockSpec(memory_space=pltpu.SMEM),
        cur(q_blk, 512), prev(k_blk, kv_w), cur(k_blk, kv_w), prev(v_blk, kv_w), cur(v_blk, kv_w),
        pl.BlockSpec((None, 2, A_HEADS // 2, 2 * BLOCK, 2 * BLOCK), lambda b, r, n: (kind, 0, 0, 0, 0)),
    ]
    bias = bias.reshape(N_KINDS, 2, A_HEADS // 2, 2 * BLOCK, 2 * BLOCK)
    out_specs = [pl.BlockSpec((None, n_qb, None, BLOCK, 512), lambda b, r, n: (b, n, r, 0, 0))]
    out_shape = [jax.ShapeDtypeStruct((batch, n_span, dil, BLOCK, 512), BF16)]
    if not is_b:
        out_specs.append(pl.BlockSpec((None, n_qb, None, BLOCK, LANE), lambda b, r, n: (b, n, r, 0, 0)))
        out_shape.append(jax.ShapeDtypeStruct((batch, n_span, dil, BLOCK, LANE), F32))
    outs = pl.pallas_call(
        functools.partial(_band_kernel, shared_kv=is_b, with_sink=is_b),
        grid=(batch, dil, n_span // n_qb),
        in_specs=in_specs,
        out_specs=out_specs,
        out_shape=out_shape,
        compiler_params=_cparams(3),
        name="band_b" if is_b else f"band_a{kind}",
    )(sinks, view, view, view, view, view, bias)
    return outs[0] if is_b else outs


def _mla_prep_kernel(c_ref, cos_ref, sin_ref, gq_ref, gkv_ref, wq1_ref, wq2_ref, wk_ref, wv_ref,
                     sela_ref, selb_ref, q_ref, k_ref, v_ref):
    c = c_ref[...]
    cq = c[:, :C_Q_RANK].astype(F32)
    ckv = c[:, C_Q_RANK:C_Q_RANK + C_KV_RANK].astype(F32)
    nq = cq * lax.rsqrt(jnp.mean(cq * cq, axis=-1, keepdims=True) + RMS_EPS) * gq_ref[...]
    nkv = ckv * lax.rsqrt(jnp.mean(ckv * ckv, axis=-1, keepdims=True) + RMS_EPS) * gkv_ref[...]
    nq = nq.astype(BF16)
    nkv = nkv.astype(BF16)
    cos = cos_ref[...]
    sin = sin_ref[...]
    cos8 = jnp.tile(cos, (1, C_HEADS))
    sin8 = jnp.tile(sin, (1, C_HEADS))
    q = (jnp.dot(nq, wq1_ref[...], preferred_element_type=F32) * cos8
         + jnp.dot(nq, wq2_ref[...], preferred_element_type=F32) * sin8)
    q_ref[...] = q.astype(q_ref.dtype)
    kr = (jnp.dot(c, sela_ref[...], preferred_element_type=F32) * cos
          + jnp.dot(c, selb_ref[...], preferred_element_type=F32) * sin)
    k = jnp.dot(nkv, wk_ref[...], preferred_element_type=F32) + jnp.tile(kr, (1, C_HEADS))
    k_ref[...] = k.astype(k_ref.dtype)
    lane = lax.broadcasted_iota(jnp.int32, (1, C_HEADS * LANE), 1)
    ones_col = jnp.where(lane % LANE == C_V, 1.0, 0.0).astype(F32)
    v_ref[...] = (jnp.dot(nkv, wv_ref[...], preferred_element_type=F32) + ones_col).astype(v_ref.dtype)


def _mla_prep(p2, cos_t, sin_t, lw, *, seq, tm=512):
    t = p2.shape[0]
    per_seq = seq // tm
    full = lambda a: pl.BlockSpec(a.shape, lambda i: (0,) * a.ndim)
    consts = (lw["gq"], lw["gkv"], lw["wq1"], lw["wq2"], lw["wk"], lw["wv"], lw["sela"], lw["selb"])
    wide = C_HEADS * LANE
    return pl.pallas_call(
        _mla_prep_kernel,
        grid=(t // tm,),
        in_specs=[pl.BlockSpec((tm, 512), lambda i: (i, COL_C // 512)),
                  pl.BlockSpec((tm, LANE), lambda i: (i % per_seq, 0)),
                  pl.BlockSpec((tm, LANE), lambda i: (i % per_seq, 0))] + [full(a) for a in consts],
        out_specs=[pl.BlockSpec((tm, wide), lambda i: (i, 0))] * 3,
        out_shape=[jax.ShapeDtypeStruct((t, wide), BF16)] * 3,
        compiler_params=_cparams(1),
        name="mla_prep",
    )(p2, cos_t, sin_t, *consts)


def _mla_flash_kernel(q_ref, k_ref, v_ref, o_ref, m_ref, acc_ref, *, blk):
    qi = pl.program_id(1)
    ki = pl.program_id(2)

    @pl.when(ki == 0)
    def _():
        m_ref[...] = jnp.full(m_ref.shape, NEG, F32)
        acc_ref[...] = jnp.zeros(acc_ref.shape, F32)

    def update(h, rows, n_keys, diagonal):
        hc = slice(h * LANE, (h + 1) * LANE)
        s = lax.dot_general(q_ref[rows, hc], k_ref[:n_keys, hc], (((1,), (1,)), ((), ())),
                            preferred_element_type=F32)
        if diagonal:
            row = lax.broadcasted_iota(jnp.int32, s.shape, 0) + rows.start
            col = lax.broadcasted_iota(jnp.int32, s.shape, 1)
            s = jnp.where(row >= col, s, NEG)
        m_prev = m_ref[h, rows]
        m_new = jnp.maximum(m_prev, jnp.max(s, axis=-1, keepdims=True))
        alpha = jnp.exp2(m_prev - m_new)
        p = jnp.exp2(s - jnp.tile(m_new, (1, n_keys // LANE)))
        acc_ref[h, rows] = alpha * acc_ref[h, rows] + jnp.dot(p.astype(BF16), v_ref[:n_keys, hc],
                                                              preferred_element_type=F32)
        m_ref[h, rows] = m_new

    @pl.when(ki < qi)
    def _():
        for h in range(C_HEADS):
            update(h, slice(0, blk), blk, False)

    @pl.when(ki == qi)
    def _():
        for h in range(C_HEADS):
            for r0 in range(0, blk, FLASH_DIAG):
                update(h, slice(r0, r0 + FLASH_DIAG), r0 + FLASH_DIAG, True)
        low = lax.broadcasted_iota(jnp.int32, (blk, LANE), 1) < C_V
        for pair in range(C_HEADS // 2):
            halves = []
            for h in (2 * pair, 2 * pair + 1):
                acc = acc_ref[h]
                halves.append(acc * (1.0 / acc[:, C_V:C_V + 1]))
            odd = pltpu.roll(halves[1], C_V, axis=1)
            o_ref[:, pair * LANE:(pair + 1) * LANE] = jnp.where(low, halves[0], odd).astype(o_ref.dtype)


def _mla_flash(q, k, v, *, batch, seq, blk=FLASH_BLK):
    wide = C_HEADS * LANE
    q3 = q.reshape(batch, seq, wide)
    k3 = k.reshape(batch, seq, wide)
    v3 = v.reshape(batch, seq, wide)
    kv_map = lambda b, i, j: (b, jnp.minimum(j, i), 0)
    out = pl.pallas_call(
        functools.partial(_mla_flash_kernel, blk=blk),
        grid=(batch, seq // blk, seq // blk),
        in_specs=[pl.BlockSpec((None, blk, wide), lambda b, i, j: (b, i, 0)),
                  pl.BlockSpec((None, blk, wide), kv_map),
                  pl.BlockSpec((None, blk, wide), kv_map)],
        out_specs=pl.BlockSpec((None, blk, C_HEADS * C_V), lambda b, i, j: (b, i, 0)),
        out_shape=jax.ShapeDtypeStruct((batch, seq, C_HEADS * C_V), BF16),
        scratch_shapes=[pltpu.VMEM((C_HEADS, blk, LANE), F32),
                        pltpu.VMEM((C_HEADS, blk, LANE), F32)],
        compiler_params=_cparams(3),
        name="mla_flash",
    )(q3, k3, v3)
    return out.reshape(batch * seq, C_HEADS * C_V)


def _layer_norm(y, g, b):
    mu = jnp.mean(y, axis=-1, keepdims=True)
    d = y - mu
    var = jnp.mean(d * d, axis=-1, keepdims=True)
    return d * lax.rsqrt(var + LN_EPS) * g + b


def _token_order(src_ref, tmp_ref, dil):
    if dil == 1:
        return src_ref[...].astype(F32)
    n = src_ref.shape[1]
    n_slab = src_ref.shape[2] // LANE
    for r in range(dil):
        rows = src_ref[r].astype(F32)
        for c in range(n_slab):
            tmp_ref[c, pl.ds(r, n, stride=dil), :] = rows[:, c * LANE:(c + 1) * LANE]
    return jnp.concatenate([tmp_ref[c] for c in range(n_slab)], axis=1)


def _merge_kernel(oa0_ref, oa1_ref, oa2_ref, l0_ref, l1_ref, l2_ref, ob_ref, oc_ref, gate_ref, x_ref,
                  e_ref, wb_ref, wo_ref, bg_ref, g_ref, b_ref, o_ref, ot1_ref, ot2_ref, lt1_ref, lt2_ref,
                  *, alpha):
    dils = [d for _, d in A_GROUPS]
    stats = [_token_order(ref, tmp, d)
             for ref, tmp, d in zip((l0_ref, l1_ref, l2_ref), (None, lt1_ref, lt2_ref), dils)]
    is_max = lax.broadcasted_iota(jnp.int32, stats[0].shape, 1) // STAT_LANES % 2 == 0
    top = jnp.maximum(jnp.maximum(stats[0], stats[1]), stats[2])
    es = [jnp.exp2(v - top) for v in stats]
    sums = [pltpu.roll(v, LANE - STAT_LANES, axis=1) for v in stats]
    inv = 1.0 / (es[0] * sums[0] + es[1] * sums[1] + es[2] * sums[2])
    o_a = None
    for e, oa_ref, tmp, d in zip(es, (oa0_ref, oa1_ref, oa2_ref), (None, ot1_ref, ot2_ref), dils):
        w = jnp.where(is_max, e * inv, 0.0)
        hi = w.astype(BF16)
        lo = (w - hi.astype(F32)).astype(BF16)
        wide = (jnp.dot(hi, e_ref[...], preferred_element_type=F32)
                + jnp.dot(lo, e_ref[...], preferred_element_type=F32))
        term = wide * _token_order(oa_ref, tmp, d)
        o_a = term if o_a is None else o_a + term
    branches = (o_a.astype(BF16), ob_ref[...], oc_ref[...])
    merged = None
    for i, br in enumerate(branches):
        gate = jax.nn.sigmoid(gate_ref[:, i * D_MODEL:(i + 1) * D_MODEL].astype(F32)
                              + bg_ref[:, i * D_MODEL:(i + 1) * D_MODEL])
        term = gate * jnp.dot(br, wb_ref[i], preferred_element_type=F32)
        merged = term if merged is None else merged + term
    mix = jnp.dot(merged.astype(BF16), wo_ref[...], preferred_element_type=F32)
    o_ref[...] = _layer_norm(alpha * x_ref[...] + mix, g_ref[...], b_ref[...])


def _merge(oa, lse, ob, oc, p2, x2d, expand, lw, *, alpha, tm=512):
    t = x2d.shape[0]
    row = lambda w: pl.BlockSpec((tm, w), lambda i: (i, 0))
    full = lambda a: pl.BlockSpec(a.shape, lambda i: (0,) * a.ndim)

    def grouped(arrs):
        views, specs = [], []
        for a, (_, dil) in zip(arrs, A_GROUPS):
            w = a.shape[-1]
            if dil == 1:
                views.append(a.reshape(t, w))
                specs.append(row(w))
                continue
            per_span = BLOCK * dil // tm
            views.append(a.reshape(-1, dil, BLOCK, w))
            specs.append(pl.BlockSpec((None, dil, tm // dil, w),
                                      lambda i, per_span=per_span: (i // per_span, 0, i % per_span, 0)))
        return views, specs

    oa_v, oa_s = grouped(oa)
    lse_v, lse_s = grouped(lse)
    consts = (expand, lw["wb"], lw["wo"], lw["bg"], lw["ln1_g"], lw["ln1_b"])
    slabs = lambda w: pltpu.VMEM((w // LANE, tm, LANE), F32)
    return pl.pallas_call(
        functools.partial(_merge_kernel, alpha=alpha),
        grid=(t // tm,),
        in_specs=oa_s + lse_s + [row(512), row(512),
                  pl.BlockSpec((tm, 3 * D_MODEL), lambda i: (i, COL_GATE // (3 * D_MODEL))),
                  row(D_MODEL)] + [full(a) for a in consts],
        out_specs=row(D_MODEL),
        out_shape=jax.ShapeDtypeStruct((t, D_MODEL), F32),
        scratch_shapes=[slabs(512), slabs(512), slabs(LANE), slabs(LANE)],
        compiler_params=_cparams(1),
        name="merge",
    )(*oa_v, *lse_v, ob.reshape(t, 512), oc, p2, x2d, *consts)


HALO = 8
FF_CHUNK = 512


def _ffn_kernel(halo_ref, x_ref, wup_ref, cw_ref, cb_ref, wdn_ref, g_ref, b_ref, o_ref,
                ug_ref, uv_ref, *, alpha, tm, per_seq):
    i = pl.program_id(0)
    x = x_ref[...]
    halo = jnp.where(i % per_seq == 0, jnp.zeros_like(halo_ref[...]), halo_ref[...])
    xh = jnp.concatenate([halo, x], axis=0).astype(BF16)
    chunks = [(c0, min(FF_CHUNK, D_FF - c0)) for c0 in range(0, D_FF, FF_CHUNK)]

    def up(c0, cf):
        return [jnp.dot(xh, wup_ref[:, base:base + cf], preferred_element_type=F32)
                for base in (c0, D_FF + c0)]

    acc = None
    u_next = up(*chunks[0])
    for ci, (c0, cf) in enumerate(chunks):
        u_cur = u_next
        if ci + 1 < len(chunks):
            u_next = up(*chunks[ci + 1])
        parts = []
        for u, base in zip(u_cur, (c0, D_FF + c0)):
            y = cb_ref[:, base:base + cf]
            for tap in range(3):
                y = y + u[HALO - 2 + tap:HALO - 2 + tap + tm] * cw_ref[tap:tap + 1, base:base + cf]
            parts.append(y)
        act = (parts[0] * jax.nn.sigmoid(parts[0]) * parts[1]).astype(BF16)
        term = jnp.dot(act, wdn_ref[c0:c0 + cf, :], preferred_element_type=F32)
        acc = term if acc is None else acc + term
    o_ref[...] = _layer_norm(alpha * x + acc, g_ref[...], b_ref[...])


def _ffn(x2d, lw, *, alpha, seq, tm=512):
    t = x2d.shape[0]
    per_seq = seq // tm
    full = lambda a: pl.BlockSpec(a.shape, lambda i: (0,) * a.ndim, pipeline_mode=pl.Buffered(1))
    consts = (lw["wup"], lw["cw"], lw["cb"], lw["wdn"], lw["ln2_g"], lw["ln2_b"])
    return pl.pallas_call(
        functools.partial(_ffn_kernel, alpha=alpha, tm=tm, per_seq=per_seq),
        grid=(t // tm,),
        in_specs=[pl.BlockSpec((HALO, D_MODEL), lambda i: (jnp.maximum(i * (tm // HALO) - 1, 0), 0)),
                  pl.BlockSpec((tm, D_MODEL), lambda i: (i, 0))] + [full(a) for a in consts],
        out_specs=pl.BlockSpec((tm, D_MODEL), lambda i: (i, 0)),
        out_shape=jax.ShapeDtypeStruct((t, D_MODEL), F32),
        scratch_shapes=[pltpu.VMEM((tm + HALO, FF_CHUNK), F32),
                        pltpu.VMEM((tm + HALO, FF_CHUNK), F32)],
        compiler_params=_cparams(1),
        name="ffn",
    )(x2d, x2d, *consts)


def _layer_weights(l, w_in, b_gate, q_norm_g, kv_norm_g, w_uq, w_ukv, w_branch, w_out,
                   ln1_g, ln1_b, w_ffn_up, conv_w, conv_b, w_ffn_down, ln2_g, ln2_b):
    a_end = 4608
    col = np.arange(w_in.shape[-1])
    is_q = ((col < a_end) & (col % A_COLS < A_HEADS * HEAD_DIM)) | ((col >= a_end) & (col < a_end + 512))
    col_scale = jnp.asarray(np.where(is_q, HEAD_DIM ** -0.5 * LOG2E, 1.0), F32)
    wi = (w_in[l] * col_scale).astype(BF16)
    bq = wi[:, a_end:a_end + 512].reshape(D_MODEL, B_Q_HEADS, HEAD_DIM)
    bq = jnp.stack([bq[:, :4], bq[:, 4:]], axis=2).reshape(D_MODEL, 512)
    bkv = wi[:, 5120:5376]
    cdq = wi[:, 5376:5632]
    ckv = wi[:, 5632:5760]
    kr = wi[:, 5760:5792]
    kr_rot = jnp.concatenate([-kr[:, C_ROPE // 2:], kr[:, :C_ROPE // 2]], axis=1)
    gate = wi[:, 5792:]
    zeros = lambda n: jnp.zeros((D_MODEL, n), wi.dtype)
    wp = jnp.concatenate([gate, wi[:, :A_COLS], bq, cdq, ckv, kr, kr_rot, zeros(64), bkv, zeros(256)],
                         axis=1)
    wa = [wi[:, g * A_COLS:(g + 1) * A_COLS] for g in range(1, len(A_GROUPS))]

    uq = (w_uq[l] * ((C_NOPE + C_ROPE) ** -0.5 * LOG2E)).reshape(C_Q_RANK, C_HEADS, C_NOPE + C_ROPE)
    zq = lambda n: jnp.zeros((C_Q_RANK, C_HEADS, n), uq.dtype)
    half = C_ROPE // 2
    wq1 = jnp.concatenate([uq, zq(LANE - C_NOPE - C_ROPE)], axis=2)
    wq2 = jnp.concatenate([zq(C_NOPE), -uq[:, :, C_NOPE + half:], uq[:, :, C_NOPE:C_NOPE + half],
                           zq(LANE - C_NOPE - C_ROPE)], axis=2)
    ukv = w_ukv[l].reshape(C_KV_RANK, C_HEADS, C_NOPE + C_V)
    wk = jnp.concatenate([ukv[:, :, :C_NOPE], jnp.zeros((C_KV_RANK, C_HEADS, LANE - C_NOPE), ukv.dtype)],
                         axis=2)
    wv = jnp.concatenate([ukv[:, :, C_NOPE:], jnp.zeros((C_KV_RANK, C_HEADS, LANE - C_V), ukv.dtype)],
                         axis=2)

    wb = w_branch[l]
    wb1 = wb[1].reshape(B_Q_HEADS, HEAD_DIM, D_MODEL)
    wb1 = jnp.stack([wb1[:4], wb1[4:]], axis=1).reshape(512, D_MODEL)
    wb = jnp.stack([wb[0], wb1, wb[2]])

    sel_rows = np.zeros((2, 512, LANE), np.float32)
    for j in range(C_ROPE):
        sel_rows[0, C_Q_RANK + C_KV_RANK + j, C_NOPE + j] = 1.0
        sel_rows[1, C_Q_RANK + C_KV_RANK + C_ROPE + j, C_NOPE + j] = 1.0

    return dict(
        wp=wp, wa=wa,
        gq=q_norm_g[l].reshape(1, -1), gkv=kv_norm_g[l].reshape(1, -1),
        wq1=wq1.reshape(C_Q_RANK, -1).astype(BF16), wq2=wq2.reshape(C_Q_RANK, -1).astype(BF16),
        wk=wk.reshape(C_KV_RANK, -1).astype(BF16), wv=wv.reshape(C_KV_RANK, -1).astype(BF16),
        sela=jnp.asarray(sel_rows[0], BF16), selb=jnp.asarray(sel_rows[1], BF16),
        wb=wb.astype(BF16), wo=w_out[l].astype(BF16), bg=b_gate[l].reshape(1, -1),
        ln1_g=ln1_g[l].reshape(1, -1), ln1_b=ln1_b[l].reshape(1, -1),
        wup=w_ffn_up[l].astype(BF16), cw=conv_w[l], cb=conv_b[l].reshape(1, -1),
        wdn=w_ffn_down[l].astype(BF16),
        ln2_g=ln2_g[l].reshape(1, -1), ln2_b=ln2_b[l].reshape(1, -1),
    )


def _rope_tables(seq):
    pos = jnp.arange(seq, dtype=F32)
    inv_freq = ROPE_BASE ** (-jnp.arange(0, C_ROPE, 2, dtype=F32) / C_ROPE)
    ang = pos[:, None] * inv_freq[None, :]
    cos, sin = jnp.cos(ang), jnp.sin(ang)
    pad = jnp.zeros((seq, LANE - C_NOPE - C_ROPE), F32)
    cos_t = jnp.concatenate([jnp.ones((seq, C_NOPE), F32), cos, cos, pad], axis=1)
    sin_t = jnp.concatenate([jnp.zeros((seq, C_NOPE), F32), sin, sin, pad], axis=1)
    return cos_t, sin_t


def _expand_matrix():
    e = np.zeros((LANE, A_HEADS * HEAD_DIM), np.float32)
    for c in range(A_HEADS * HEAD_DIM):
        e[(LANE // A_HEADS) * (c // HEAD_DIM), c] = 1.0
    return jnp.asarray(e, BF16)


def kernel(x, rel_table, w_in, b_gate, sinks, q_norm_g, kv_norm_g, w_uq, w_ukv, w_branch, w_out,
           ln1_g, ln1_b, w_ffn_up, conv_w, conv_b, w_ffn_down, ln2_g, ln2_b):
    batch, seq, d = x.shape
    depth = w_in.shape[0]
    alpha = (2 * depth) ** 0.25
    cos_t, sin_t = _rope_tables(seq)
    expand = _expand_matrix()
    bias = _bias_tables(rel_table)
    x2d = x.reshape(batch * seq, d)
    for l in range(depth):
        lw = _layer_weights(l, w_in, b_gate, q_norm_g, kv_norm_g, w_uq, w_ukv, w_branch, w_out,
                            ln1_g, ln1_b, w_ffn_up, conv_w, conv_b, w_ffn_down, ln2_g, ln2_b)
        p2 = _proj(x2d, lw["wp"], dil=1, tm=1024, tn=A_COLS, name="proj")
        p_dil = []
        for g, (w, (_, dil)) in enumerate(zip(lw["wa"], A_GROUPS[1:])):
            tm = max(1024, BLOCK * dil)
            tn = A_COLS * 1024 // tm
            p_dil.append(_proj(x2d, w, dil=dil, tm=tm, tn=tn, name=f"proj_a{g + 1}"))
        sink_slots = sinks[l].reshape(2, 4).T.reshape(-1) * LOG2E
        oa, lse = [], []
        for kind in range(len(A_GROUPS)):
            src = p2 if kind == 0 else p_dil[kind - 1]
            o_g, lse_g = _band_attention(src, bias, sink_slots, kind=kind, batch=batch, seq=seq)
            oa.append(o_g)
            lse.append(lse_g)
        ob = _band_attention(p2, bias, sink_slots, kind=len(A_GROUPS), batch=batch, seq=seq)
        q, k, v = _mla_prep(p2, cos_t, sin_t, lw, seq=seq)
        oc = _mla_flash(q, k, v, batch=batch, seq=seq)
        x2d = _merge(oa, lse, ob, oc, p2, x2d, expand, lw, alpha=alpha)
        x2d = _ffn(x2d, lw, alpha=alpha, seq=seq)
    return x2d.reshape(batch, seq, d)
```

```python
import functools
import math

import jax
import jax.numpy as jnp
import numpy as np
from jax import lax
from jax.experimental import pallas as pl
from jax.experimental.pallas import tpu as pltpu

F32 = jnp.float32
BF16 = jnp.bfloat16

D_MODEL = 1024
HEAD_DIM = 64
BLOCK = 128
A_GROUPS = ((128, 1), (512, 4), (2048, 16))
A_HEADS = 8
B_Q_HEADS = 8
B_WINDOW = 128
C_HEADS = 8
C_Q_RANK = 256
C_KV_RANK = 128
C_NOPE = 64
C_ROPE = 32
C_V = 64
ROPE_BASE = 10000.0
REL_BUCKETS = 32
REL_MAX_DIST = 2048
D_FF = 2816
LN_EPS = 1e-5
RMS_EPS = 1e-6
NEG = -1e30
LOG2E = math.log2(math.e)
FLASH_BLK = 1024
FLASH_DIAG = 512

COL_GATE = 0
COL_A = 3072
COL_BQ = 4608
COL_C = 5120
COL_BKV = 5632
NP = 6144
A_COLS = 1536

LANE = 128
VMEM_LIMIT = 56 * 1024 * 1024

N_KINDS = 4


def _cparams(n_axes):
    return pltpu.CompilerParams(dimension_semantics=("arbitrary",) * n_axes,
                                vmem_limit_bytes=VMEM_LIMIT)


def _layer_block(a, layer, **kwargs):
    zeros = (0,) * (a.ndim - 1)
    return pl.BlockSpec((None,) + a.shape[1:], lambda *_: (layer,) + zeros, **kwargs)


CHEAP_STRIDE = 4


def _proj_kernel(*refs, dil):
    n_x = len(refs) - (4 if dil > CHEAP_STRIDE else 3)
    x_refs, (w_ref, o_ref, xb_ref, *tmp) = refs[:n_x], refs[n_x:]

    @pl.when(pl.program_id(1) == 0)
    def _():
        if dil == 1:
            xb_ref[...] = x_refs[0][...].astype(BF16)
            return
        span = BLOCK * dil
        d1 = min(dil, CHEAP_STRIDE)
        d2 = dil // d1
        for c, x_ref in enumerate(x_refs):
            for s0 in range(0, xb_ref.shape[0], span):
                for r1 in range(d1):
                    if d2 == 1:
                        rows = x_ref[pl.ds(s0 + r1, BLOCK, stride=d1), :]
                        xb_ref[s0 + r1 * BLOCK:s0 + (r1 + 1) * BLOCK, c * LANE:(c + 1) * LANE] = rows.astype(BF16)
                        continue
                    g0 = s0 + r1 * (span // d1)
                    tmp[0][c, g0:g0 + span // d1, :] = x_ref[pl.ds(s0 + r1, span // d1, stride=d1), :]
                    for r2 in range(d2):
                        r = r1 + d1 * r2
                        rows = tmp[0][c, pl.ds(g0 + r2, BLOCK, stride=d2), :]
                        xb_ref[s0 + r * BLOCK:s0 + (r + 1) * BLOCK, c * LANE:(c + 1) * LANE] = rows.astype(BF16)

    w = w_ref[pl.program_id(1)]
    o_ref[...] = jnp.dot(xb_ref[...], w, preferred_element_type=F32).astype(o_ref.dtype)


def _proj(x2d, w_tiles, layer, *, dil, tm, name):
    t, k = x2d.shape
    n_tiles, tn = w_tiles.shape[1], w_tiles.shape[3]
    n = n_tiles * tn
    if dil == 1:
        x_specs = [pl.BlockSpec((tm, k), lambda i, j: (i, 0))]
    else:
        x_specs = [pl.BlockSpec((tm, LANE), lambda i, j, c=c: (i, c)) for c in range(k // LANE)]
    scratch = [pltpu.VMEM((tm, k), BF16)]
    if dil > CHEAP_STRIDE:
        scratch.append(pltpu.VMEM((k // LANE, tm, LANE), F32))
    return pl.pallas_call(
        functools.partial(_proj_kernel, dil=dil),
        grid=(t // tm, n_tiles),
        in_specs=x_specs + [_layer_block(w_tiles, layer, pipeline_mode=pl.Buffered(1))],
        out_specs=pl.BlockSpec((tm, tn), lambda i, j: (i, j)),
        out_shape=jax.ShapeDtypeStruct((t, n), BF16),
        scratch_shapes=scratch,
        compiler_params=_cparams(2),
        name=name,
    )(*([x2d] * len(x_specs)), w_tiles)


def _t5_bucket(dist):
    n = jnp.maximum(dist, 0)
    max_exact = REL_BUCKETS // 2
    scaled = jnp.log(jnp.maximum(n, 1).astype(F32) / max_exact) / math.log(REL_MAX_DIST / max_exact)
    large = max_exact + (scaled * (REL_BUCKETS - max_exact)).astype(jnp.int32)
    return jnp.where(n < max_exact, n, jnp.minimum(large, REL_BUCKETS - 1))


def _bias_codes():
    qi = jnp.arange(BLOCK)[:, None]
    ki = jnp.arange(2 * BLOCK)[None, :]
    step = BLOCK + qi - ki
    has_prev = ki >= BLOCK
    codes = []
    for kind in range(N_KINDS):
        if kind < len(A_GROUPS):
            dil = A_GROUPS[kind][1]
            band = (step >= 0) & (step <= BLOCK)
        else:
            dil = 1
            band = (step >= 0) & (step < B_WINDOW)
        bucket = _t5_bucket(step * dil)
        codes.append(jnp.stack([jnp.where(band & has_prev, bucket, -1),
                                jnp.where(band, bucket, -1)]))
    return jnp.stack(codes).astype(jnp.int32)


def _bias_kernel(rel_ref, code_ref, o_ref):
    kind = pl.program_id(0)
    code = code_ref[0, 0]
    for slot in range(A_HEADS):
        b_head = len(A_GROUPS) * A_HEADS + slot // 2 + 4 * (slot % 2)
        col = jnp.where(kind < len(A_GROUPS), kind * A_HEADS + slot, b_head)
        acc = jnp.full(code.shape, NEG, F32)
        for b in range(REL_BUCKETS):
            acc = jnp.where(code == b, rel_ref[b, col] * LOG2E, acc)
        o_ref[0, 0, slot] = acc


def _bias_tables(rel_table):
    codes = _bias_codes()
    return pl.pallas_call(
        _bias_kernel,
        grid=(N_KINDS, 2),
        in_specs=[pl.BlockSpec(memory_space=pltpu.SMEM),
                  pl.BlockSpec((1, 1, BLOCK, 2 * BLOCK), lambda k, v: (k, v, 0, 0))],
        out_specs=pl.BlockSpec((1, 1, A_HEADS, BLOCK, 2 * BLOCK), lambda k, v: (k, v, 0, 0, 0)),
        out_shape=jax.ShapeDtypeStruct((N_KINDS, 2, A_HEADS, BLOCK, 2 * BLOCK), F32),
        compiler_params=_cparams(2),
        name="bias_tables",
    )(rel_table, codes)


BAND_QB = 8
STAT_LANES = LANE // A_HEADS // 2


def _band_kernel(sink_ref, q_ref, kp_ref, kc_ref, vp_ref, vc_ref, bias_ref, o_ref, *lse_refs,
                 shared_kv, with_sink):
    first = pl.program_id(2) == 0
    lane = lax.broadcasted_iota(jnp.int32, (BLOCK, LANE), 1)
    low = lane < HEAD_DIM
    for qb in range(q_ref.shape[0]):
        variant = jnp.where(first, 0, 1) if qb == 0 else 1
        lse_tile = jnp.zeros((BLOCK, LANE), F32)
        for pair in range(4):
            cols = slice(pair * LANE, (pair + 1) * LANE)
            kv_cols = slice(0, LANE) if shared_kv else cols
            qp = q_ref[qb, :, cols]
            k_prev = kp_ref[:, kv_cols] if qb == 0 else kc_ref[qb - 1, :, kv_cols]
            v_prev = vp_ref[:, kv_cols] if qb == 0 else vc_ref[qb - 1, :, kv_cols]
            kcat = jnp.concatenate([k_prev, kc_ref[qb, :, kv_cols]], axis=0)
            vcat = jnp.concatenate([v_prev, vc_ref[qb, :, kv_cols]], axis=0)
            zero = jnp.zeros_like(qp)
            q2 = jnp.concatenate([jnp.where(low, qp, zero), jnp.where(low, zero, qp)], axis=0)
            s = lax.dot_general(q2, kcat, (((1,), (1,)), ((), ())), preferred_element_type=F32)
            s = s + bias_ref[variant, pair]
            m = jnp.max(s, axis=-1, keepdims=True)
            if with_sink:
                head_a = lax.broadcasted_iota(jnp.int32, (2 * BLOCK, 1), 0) < BLOCK
                sink = jnp.where(head_a, sink_ref[2 * pair], sink_ref[2 * pair + 1])
                m = jnp.maximum(m, sink)
            p = jnp.exp2(s - m)
            l = jnp.sum(p, axis=-1, keepdims=True)
            if with_sink:
                l = l + jnp.exp2(sink - m)
            o = jnp.dot(p.astype(BF16), vcat, preferred_element_type=F32)
            if lse_refs:
                for sub in range(2):
                    rows = slice(sub * BLOCK, (sub + 1) * BLOCK)
                    field = 2 * (2 * pair + sub)
                    lse_tile = jnp.where(lane // STAT_LANES == field, m[rows],
                                         jnp.where(lane // STAT_LANES == field + 1, l[rows], lse_tile))
            else:
                o = o * (1.0 / l)
            o_ref[qb, :, cols] = jnp.where(low, o[:BLOCK], o[BLOCK:]).astype(o_ref.dtype)
        if lse_refs:
            lse_refs[0][qb] = lse_tile


def _band_attention(p2, bias, sinks, *, kind, batch, seq):
    is_b = kind == len(A_GROUPS)
    dil = 1 if is_b else A_GROUPS[kind][1]
    n_span = seq // (BLOCK * dil)
    view = p2.reshape(batch, n_span, dil, BLOCK, p2.shape[-1])
    if is_b:
        q_blk, k_blk, v_blk, kv_w = COL_BQ // 512, COL_BKV // LANE, COL_BKV // LANE + 1, LANE
    else:
        q_blk = COL_A // 512 if dil == 1 else 0
        k_blk, v_blk, kv_w = q_blk + 1, q_blk + 2, 512

    n_qb = min(BAND_QB, n_span)

    def cur(blk, width):
        return pl.BlockSpec((None, n_qb, None, BLOCK, width), lambda b, r, n: (b, n, r, 0, blk))

    def prev(blk, width):
        return pl.BlockSpec((None, None, None, BLOCK, width),
                            lambda b, r, n: (b, jnp.maximum(n * n_qb - 1, 0), r, 0, blk))

    in_specs = [
        pl.BlockSpec(memory_space=pltpu.SMEM),
        cur(q_blk, 512), prev(k_blk, kv_w), cur(k_blk, kv_w), prev(v_blk, kv_w), cur(v_blk, kv_w),
        pl.BlockSpec((None, 2, A_HEADS // 2, 2 * BLOCK, 2 * BLOCK), lambda b, r, n: (kind, 0, 0, 0, 0)),
    ]
    bias = bias.reshape(N_KINDS, 2, A_HEADS // 2, 2 * BLOCK, 2 * BLOCK)
    out_specs = [pl.BlockSpec((None, n_qb, None, BLOCK, 512), lambda b, r, n: (b, n, r, 0, 0))]
    out_shape = [jax.ShapeDtypeStruct((batch, n_span, dil, BLOCK, 512), BF16)]
    if not is_b:
        out_specs.append(pl.BlockSpec((None, n_qb, None, BLOCK, LANE), lambda b, r, n: (b, n, r, 0, 0)))
        out_shape.append(jax.ShapeDtypeStruct((batch, n_span, dil, BLOCK, LANE), F32))
    outs = pl.pallas_call(
        functools.partial(_band_kernel, shared_kv=is_b, with_sink=is_b),
        grid=(batch, dil, n_span // n_qb),
        in_specs=in_specs,
        out_specs=out_specs,
        out_shape=out_shape,
        compiler_params=_cparams(3),
        name="band_b" if is_b else f"band_a{kind}",
    )(sinks, view, view, view, view, view, bias)
    return outs[0] if is_b else outs


def _mla_prep_kernel(c_ref, cos_ref, sin_ref, gq_ref, gkv_ref, wq1_ref, wq2_ref, wk_ref, wv_ref,
                     sela_ref, selb_ref, q_ref, k_ref, v_ref):
    c = c_ref[...]
    cq = c[:, :C_Q_RANK].astype(F32)
    ckv = c[:, C_Q_RANK:C_Q_RANK + C_KV_RANK].astype(F32)
    nq = cq * lax.rsqrt(jnp.mean(cq * cq, axis=-1, keepdims=True) + RMS_EPS) * gq_ref[...]
    nkv = ckv * lax.rsqrt(jnp.mean(ckv * ckv, axis=-1, keepdims=True) + RMS_EPS) * gkv_ref[...]
    nq = nq.astype(BF16)
    nkv = nkv.astype(BF16)
    cos = cos_ref[...]
    sin = sin_ref[...]
    cos8 = jnp.tile(cos, (1, C_HEADS))
    sin8 = jnp.tile(sin, (1, C_HEADS))
    q = (jnp.dot(nq, wq1_ref[...], preferred_element_type=F32) * cos8
         + jnp.dot(nq, wq2_ref[...], preferred_element_type=F32) * sin8)
    q_ref[...] = q.astype(q_ref.dtype)
    kr = (jnp.dot(c, sela_ref[...], preferred_element_type=F32) * cos
          + jnp.dot(c, selb_ref[...], preferred_element_type=F32) * sin)
    k = jnp.dot(nkv, wk_ref[...], preferred_element_type=F32) + jnp.tile(kr, (1, C_HEADS))
    k_ref[...] = k.astype(k_ref.dtype)
    lane = lax.broadcasted_iota(jnp.int32, (1, C_HEADS * LANE), 1)
    ones_col = jnp.where(lane % LANE == C_V, 1.0, 0.0).astype(F32)
    v_ref[...] = (jnp.dot(nkv, wv_ref[...], preferred_element_type=F32) + ones_col).astype(v_ref.dtype)


def _mla_prep(p2, cos_t, sin_t, sel, lw, layer, *, seq, tm=512):
    t = p2.shape[0]
    per_seq = seq // tm
    full = lambda a: pl.BlockSpec(a.shape, lambda i: (0,) * a.ndim)
    stacked = (lw["gq"], lw["gkv"], lw["wq1"], lw["wq2"], lw["wk"], lw["wv"])
    wide = C_HEADS * LANE
    return pl.pallas_call(
        _mla_prep_kernel,
        grid=(t // tm,),
        in_specs=[pl.BlockSpec((tm, 512), lambda i: (i, COL_C // 512)),
                  pl.BlockSpec((tm, LANE), lambda i: (i % per_seq, 0)),
                  pl.BlockSpec((tm, LANE), lambda i: (i % per_seq, 0))]
                 + [_layer_block(a, layer) for a in stacked] + [full(a) for a in sel],
        out_specs=[pl.BlockSpec((tm, wide), lambda i: (i, 0))] * 3,
        out_shape=[jax.ShapeDtypeStruct((t, wide), BF16)] * 3,
        compiler_params=_cparams(1),
        name="mla_prep",
    )(p2, cos_t, sin_t, *stacked, *sel)


def _mla_flash_kernel(q_ref, k_ref, v_ref, o_ref, m_ref, acc_ref, *, blk):
    qi = pl.program_id(1)
    ki = pl.program_id(2)

    @pl.when(ki == 0)
    def _():
        m_ref[...] = jnp.full(m_ref.shape, NEG, F32)
        acc_ref[...] = jnp.zeros(acc_ref.shape, F32)

    def update(h, rows, n_keys, diagonal):
        hc = slice(h * LANE, (h + 1) * LANE)
        s = lax.dot_general(q_ref[rows, hc], k_ref[:n_keys, hc], (((1,), (1,)), ((), ())),
                            preferred_element_type=F32)
        if diagonal:
            row = lax.broadcasted_iota(jnp.int32, s.shape, 0) + rows.start
            col = lax.broadcasted_iota(jnp.int32, s.shape, 1)
            s = jnp.where(row >= col, s, NEG)
        m_prev = m_ref[h, rows]
        m_new = jnp.maximum(m_prev, jnp.max(s, axis=-1, keepdims=True))
        alpha = jnp.exp2(m_prev - m_new)
        p = jnp.exp2(s - jnp.tile(m_new, (1, n_keys // LANE)))
        acc_ref[h, rows] = alpha * acc_ref[h, rows] + jnp.dot(p.astype(BF16), v_ref[:n_keys, hc],
                                                              preferred_element_type=F32)
        m_ref[h, rows] = m_new

    @pl.when(ki < qi)
    def _():
        for h in range(C_HEADS):
            update(h, slice(0, blk), blk, False)

    @pl.when(ki == qi)
    def _():
        for h in range(C_HEADS):
            for r0 in range(0, blk, FLASH_DIAG):
                update(h, slice(r0, r0 + FLASH_DIAG), r0 + FLASH_DIAG, True)
        low = lax.broadcasted_iota(jnp.int32, (blk, LANE), 1) < C_V
        for pair in range(C_HEADS // 2):
            halves = []
            for h in (2 * pair, 2 * pair + 1):
                acc = acc_ref[h]
                halves.append(acc * (1.0 / acc[:, C_V:C_V + 1]))
            odd = pltpu.roll(halves[1], C_V, axis=1)
            o_ref[:, pair * LANE:(pair + 1) * LANE] = jnp.where(low, halves[0], odd).astype(o_ref.dtype)


def _mla_flash(q, k, v, *, batch, seq, blk=FLASH_BLK):
    wide = C_HEADS * LANE
    q3 = q.reshape(batch, seq, wide)
    k3 = k.reshape(batch, seq, wide)
    v3 = v.reshape(batch, seq, wide)
    kv_map = lambda b, i, j: (b, jnp.minimum(j, i), 0)
    out = pl.pallas_call(
        functools.partial(_mla_flash_kernel, blk=blk),
        grid=(batch, seq // blk, seq // blk),
        in_specs=[pl.BlockSpec((None, blk, wide), lambda b, i, j: (b, i, 0)),
                  pl.BlockSpec((None, blk, wide), kv_map),
                  pl.BlockSpec((None, blk, wide), kv_map)],
        out_specs=pl.BlockSpec((None, blk, C_HEADS * C_V), lambda b, i, j: (b, i, 0)),
        out_shape=jax.ShapeDtypeStruct((batch, seq, C_HEADS * C_V), BF16),
        scratch_shapes=[pltpu.VMEM((C_HEADS, blk, LANE), F32),
                        pltpu.VMEM((C_HEADS, blk, LANE), F32)],
        compiler_params=_cparams(3),
        name="mla_flash",
    )(q3, k3, v3)
    return out.reshape(batch * seq, C_HEADS * C_V)


def _layer_norm(y, g, b):
    mu = jnp.mean(y, axis=-1, keepdims=True)
    d = y - mu
    var = jnp.mean(d * d, axis=-1, keepdims=True)
    return d * lax.rsqrt(var + LN_EPS) * g + b


def _token_order(src_ref, tmp_ref, dil):
    if dil == 1:
        return src_ref[...].astype(F32)
    n = src_ref.shape[1]
    n_slab = src_ref.shape[2] // LANE
    for r in range(dil):
        rows = src_ref[r].astype(F32)
        for c in range(n_slab):
            tmp_ref[c, pl.ds(r, n, stride=dil), :] = rows[:, c * LANE:(c + 1) * LANE]
    return jnp.concatenate([tmp_ref[c] for c in range(n_slab)], axis=1)


def _merge_kernel(oa0_ref, oa1_ref, oa2_ref, l0_ref, l1_ref, l2_ref, ob_ref, oc_ref, gate_ref, x_ref,
                  e_ref, wb_ref, wo_ref, bg_ref, g_ref, b_ref, o_ref, ot1_ref, ot2_ref, lt1_ref, lt2_ref,
                  *, alpha):
    dils = [d for _, d in A_GROUPS]
    stats = [_token_order(ref, tmp, d)
             for ref, tmp, d in zip((l0_ref, l1_ref, l2_ref), (None, lt1_ref, lt2_ref), dils)]
    is_max = lax.broadcasted_iota(jnp.int32, stats[0].shape, 1) // STAT_LANES % 2 == 0
    top = jnp.maximum(jnp.maximum(stats[0], stats[1]), stats[2])
    es = [jnp.exp2(v - top) for v in stats]
    sums = [pltpu.roll(v, LANE - STAT_LANES, axis=1) for v in stats]
    inv = 1.0 / (es[0] * sums[0] + es[1] * sums[1] + es[2] * sums[2])
    o_a = None
    for e, oa_ref, tmp, d in zip(es, (oa0_ref, oa1_ref, oa2_ref), (None, ot1_ref, ot2_ref), dils):
        w = jnp.where(is_max, e * inv, 0.0)
        hi = w.astype(BF16)
        lo = (w - hi.astype(F32)).astype(BF16)
        wide = (jnp.dot(hi, e_ref[...], preferred_element_type=F32)
                + jnp.dot(lo, e_ref[...], preferred_element_type=F32))
        term = wide * _token_order(oa_ref, tmp, d)
        o_a = term if o_a is None else o_a + term
    branches = (o_a.astype(BF16), ob_ref[...], oc_ref[...])
    merged = None
    for i, br in enumerate(branches):
        gate = jax.nn.sigmoid(gate_ref[:, i * D_MODEL:(i + 1) * D_MODEL].astype(F32)
                              + bg_ref[:, i * D_MODEL:(i + 1) * D_MODEL])
        term = gate * jnp.dot(br, wb_ref[i], preferred_element_type=F32)
        merged = term if merged is None else merged + term
    mix = jnp.dot(merged.astype(BF16), wo_ref[...], preferred_element_type=F32)
    o_ref[...] = _layer_norm(alpha * x_ref[...] + mix, g_ref[...], b_ref[...])


def _merge(oa, lse, ob, oc, p2, x2d, expand, lw, layer, *, alpha, tm=512):
    t = x2d.shape[0]
    row = lambda w: pl.BlockSpec((tm, w), lambda i: (i, 0))
    full = lambda a: pl.BlockSpec(a.shape, lambda i: (0,) * a.ndim)

    def grouped(arrs):
        views, specs = [], []
        for a, (_, dil) in zip(arrs, A_GROUPS):
            w = a.shape[-1]
            if dil == 1:
                views.append(a.reshape(t, w))
                specs.append(row(w))
                continue
            per_span = BLOCK * dil // tm
            views.append(a.reshape(-1, dil, BLOCK, w))
            specs.append(pl.BlockSpec((None, dil, tm // dil, w),
                                      lambda i, per_span=per_span: (i // per_span, 0, i % per_span, 0)))
        return views, specs

    oa_v, oa_s = grouped(oa)
    lse_v, lse_s = grouped(lse)
    stacked = (lw["wb"], lw["wo"], lw["bg"], lw["ln1_g"], lw["ln1_b"])
    slabs = lambda w: pltpu.VMEM((w // LANE, tm, LANE), F32)
    return pl.pallas_call(
        functools.partial(_merge_kernel, alpha=alpha),
        grid=(t // tm,),
        in_specs=oa_s + lse_s + [row(512), row(512),
                  pl.BlockSpec((tm, 3 * D_MODEL), lambda i: (i, COL_GATE // (3 * D_MODEL))),
                  row(D_MODEL), full(expand)] + [_layer_block(a, layer) for a in stacked],
        out_specs=row(D_MODEL),
        out_shape=jax.ShapeDtypeStruct((t, D_MODEL), F32),
        scratch_shapes=[slabs(512), slabs(512), slabs(LANE), slabs(LANE)],
        compiler_params=_cparams(1),
        name="merge",
    )(*oa_v, *lse_v, ob.reshape(t, 512), oc, p2, x2d, expand, *stacked)


HALO = 8
FF_CHUNK = 512


def _ffn_kernel(halo_ref, x_ref, wup_ref, cw_ref, cb_ref, wdn_ref, g_ref, b_ref, o_ref,
                ug_ref, uv_ref, *, alpha, tm, per_seq):
    i = pl.program_id(0)
    x = x_ref[...]
    halo = jnp.where(i % per_seq == 0, jnp.zeros_like(halo_ref[...]), halo_ref[...])
    xh = jnp.concatenate([halo, x], axis=0).astype(BF16)
    chunks = [(c0, min(FF_CHUNK, D_FF - c0)) for c0 in range(0, D_FF, FF_CHUNK)]

    def up(c0, cf):
        return [jnp.dot(xh, wup_ref[:, base:base + cf], preferred_element_type=F32)
                for base in (c0, D_FF + c0)]

    acc = None
    u_next = up(*chunks[0])
    for ci, (c0, cf) in enumerate(chunks):
        u_cur = u_next
        if ci + 1 < len(chunks):
            u_next = up(*chunks[ci + 1])
        parts = []
        for u, base in zip(u_cur, (c0, D_FF + c0)):
            y = cb_ref[:, base:base + cf]
            for tap in range(3):
                y = y + u[HALO - 2 + tap:HALO - 2 + tap + tm] * cw_ref[tap:tap + 1, base:base + cf]
            parts.append(y)
        act = (parts[0] * jax.nn.sigmoid(parts[0]) * parts[1]).astype(BF16)
        term = jnp.dot(act, wdn_ref[c0:c0 + cf, :], preferred_element_type=F32)
        acc = term if acc is None else acc + term
    o_ref[...] = _layer_norm(alpha * x + acc, g_ref[...], b_ref[...])


def _ffn(x2d, lw, layer, *, alpha, seq, tm=512):
    t = x2d.shape[0]
    per_seq = seq // tm
    consts = (lw["wup"], lw["cw"], lw["cb"], lw["wdn"], lw["ln2_g"], lw["ln2_b"])
    return pl.pallas_call(
        functools.partial(_ffn_kernel, alpha=alpha, tm=tm, per_seq=per_seq),
        grid=(t // tm,),
        in_specs=[pl.BlockSpec((HALO, D_MODEL), lambda i: (jnp.maximum(i * (tm // HALO) - 1, 0), 0)),
                  pl.BlockSpec((tm, D_MODEL), lambda i: (i, 0))]
                 + [_layer_block(a, layer, pipeline_mode=pl.Buffered(1)) for a in consts],
        out_specs=pl.BlockSpec((tm, D_MODEL), lambda i: (i, 0)),
        out_shape=jax.ShapeDtypeStruct((t, D_MODEL), F32),
        scratch_shapes=[pltpu.VMEM((tm + HALO, FF_CHUNK), F32),
                        pltpu.VMEM((tm + HALO, FF_CHUNK), F32)],
        compiler_params=_cparams(1),
        name="ffn",
    )(x2d, x2d, *consts)


def _prepare_weights(w_in, b_gate, q_norm_g, kv_norm_g, w_uq, w_ukv, w_branch, w_out,
                     ln1_g, ln1_b, w_ffn_up, conv_w, conv_b, w_ffn_down, ln2_g, ln2_b):
    depth = w_in.shape[0]
    a_end = 4608
    col = np.arange(w_in.shape[-1])
    is_q = ((col < a_end) & (col % A_COLS < A_HEADS * HEAD_DIM)) | ((col >= a_end) & (col < a_end + 512))
    col_scale = jnp.asarray(np.where(is_q, HEAD_DIM ** -0.5 * LOG2E, 1.0), F32)
    wi = (w_in * col_scale).astype(BF16)
    bq = wi[:, :, a_end:a_end + 512].reshape(depth, D_MODEL, B_Q_HEADS, HEAD_DIM)
    bq = jnp.stack([bq[:, :, :4], bq[:, :, 4:]], axis=3).reshape(depth, D_MODEL, 512)
    bkv = wi[:, :, 5120:5376]
    cdq = wi[:, :, 5376:5632]
    ckv = wi[:, :, 5632:5760]
    kr = wi[:, :, 5760:5792]
    kr_rot = jnp.concatenate([-kr[:, :, C_ROPE // 2:], kr[:, :, :C_ROPE // 2]], axis=2)
    gate = wi[:, :, 5792:]
    zeros = lambda n: jnp.zeros((depth, D_MODEL, n), wi.dtype)
    rest = jnp.concatenate([bq, cdq, ckv, kr, kr_rot, zeros(64), bkv, zeros(256)], axis=2)
    wp = jnp.stack([gate[:, :, :A_COLS], gate[:, :, A_COLS:], wi[:, :, :A_COLS], rest], axis=1)
    wa = []
    for g, (_, dil) in enumerate(A_GROUPS[1:], start=1):
        tn = A_COLS * 1024 // max(1024, BLOCK * dil)
        w = wi[:, :, g * A_COLS:(g + 1) * A_COLS]
        wa.append(w.reshape(depth, D_MODEL, A_COLS // tn, tn).transpose(0, 2, 1, 3))

    uq = (w_uq * ((C_NOPE + C_ROPE) ** -0.5 * LOG2E)).reshape(depth, C_Q_RANK, C_HEADS, C_NOPE + C_ROPE)
    zq = lambda n: jnp.zeros((depth, C_Q_RANK, C_HEADS, n), uq.dtype)
    half = C_ROPE // 2
    wq1 = jnp.concatenate([uq, zq(LANE - C_NOPE - C_ROPE)], axis=3)
    wq2 = jnp.concatenate([zq(C_NOPE), -uq[..., C_NOPE + half:], uq[..., C_NOPE:C_NOPE + half],
                           zq(LANE - C_NOPE - C_ROPE)], axis=3)
    ukv = w_ukv.reshape(depth, C_KV_RANK, C_HEADS, C_NOPE + C_V)
    zkv = lambda n: jnp.zeros((depth, C_KV_RANK, C_HEADS, n), ukv.dtype)
    wk = jnp.concatenate([ukv[..., :C_NOPE], zkv(LANE - C_NOPE)], axis=3)
    wv = jnp.concatenate([ukv[..., C_NOPE:], zkv(LANE - C_V)], axis=3)

    wb1 = w_branch[:, 1].reshape(depth, B_Q_HEADS, HEAD_DIM, D_MODEL)
    wb1 = jnp.stack([wb1[:, :4], wb1[:, 4:]], axis=2).reshape(depth, 512, D_MODEL)
    wb = jnp.stack([w_branch[:, 0], wb1, w_branch[:, 2]], axis=1)

    row = lambda a: a.reshape(depth, 1, -1)
    return dict(
        wp=wp, wa=wa,
        gq=row(q_norm_g), gkv=row(kv_norm_g),
        wq1=wq1.reshape(depth, C_Q_RANK, -1).astype(BF16), wq2=wq2.reshape(depth, C_Q_RANK, -1).astype(BF16),
        wk=wk.reshape(depth, C_KV_RANK, -1).astype(BF16), wv=wv.reshape(depth, C_KV_RANK, -1).astype(BF16),
        wb=wb.astype(BF16), wo=w_out.astype(BF16), bg=row(b_gate),
        ln1_g=row(ln1_g), ln1_b=row(ln1_b),
        wup=w_ffn_up.astype(BF16), cw=conv_w, cb=row(conv_b), wdn=w_ffn_down.astype(BF16),
        ln2_g=row(ln2_g), ln2_b=row(ln2_b),
    )


def _rope_selectors():
    sel = np.zeros((2, 512, LANE), np.float32)
    for j in range(C_ROPE):
        sel[0, C_Q_RANK + C_KV_RANK + j, C_NOPE + j] = 1.0
        sel[1, C_Q_RANK + C_KV_RANK + C_ROPE + j, C_NOPE + j] = 1.0
    return jnp.asarray(sel[0], BF16), jnp.asarray(sel[1], BF16)


def _rope_tables(seq):
    pos = jnp.arange(seq, dtype=F32)
    inv_freq = ROPE_BASE ** (-jnp.arange(0, C_ROPE, 2, dtype=F32) / C_ROPE)
    ang = pos[:, None] * inv_freq[None, :]
    cos, sin = jnp.cos(ang), jnp.sin(ang)
    pad = jnp.zeros((seq, LANE - C_NOPE - C_ROPE), F32)
    cos_t = jnp.concatenate([jnp.ones((seq, C_NOPE), F32), cos, cos, pad], axis=1)
    sin_t = jnp.concatenate([jnp.zeros((seq, C_NOPE), F32), sin, sin, pad], axis=1)
    return cos_t, sin_t


def _expand_matrix():
    e = np.zeros((LANE, A_HEADS * HEAD_DIM), np.float32)
    for c in range(A_HEADS * HEAD_DIM):
        e[(LANE // A_HEADS) * (c // HEAD_DIM), c] = 1.0
    return jnp.asarray(e, BF16)


def kernel(x, rel_table, w_in, b_gate, sinks, q_norm_g, kv_norm_g, w_uq, w_ukv, w_branch, w_out,
           ln1_g, ln1_b, w_ffn_up, conv_w, conv_b, w_ffn_down, ln2_g, ln2_b):
    batch, seq, d = x.shape
    depth = w_in.shape[0]
    alpha = (2 * depth) ** 0.25
    cos_t, sin_t = _rope_tables(seq)
    sel = _rope_selectors()
    expand = _expand_matrix()
    bias = _bias_tables(rel_table)
    lw = _prepare_weights(w_in, b_gate, q_norm_g, kv_norm_g, w_uq, w_ukv, w_branch, w_out,
                          ln1_g, ln1_b, w_ffn_up, conv_w, conv_b, w_ffn_down, ln2_g, ln2_b)
    x2d = x.reshape(batch * seq, d)
    for l in range(depth):
        p2 = _proj(x2d, lw["wp"], l, dil=1, tm=1024, name="proj")
        p_dil = [_proj(x2d, w, l, dil=dil, tm=max(1024, BLOCK * dil), name=f"proj_a{g + 1}")
                 for g, (w, (_, dil)) in enumerate(zip(lw["wa"], A_GROUPS[1:]))]
        sink_slots = sinks[l].reshape(2, 4).T.reshape(-1) * LOG2E
        oa, lse = [], []
        for kind in range(len(A_GROUPS)):
            src = p2 if kind == 0 else p_dil[kind - 1]
            o_g, lse_g = _band_attention(src, bias, sink_slots, kind=kind, batch=batch, seq=seq)
            oa.append(o_g)
            lse.append(lse_g)
        ob = _band_attention(p2, bias, sink_slots, kind=len(A_GROUPS), batch=batch, seq=seq)
        q, k, v = _mla_prep(p2, cos_t, sin_t, sel, lw, l, seq=seq)
        oc = _mla_flash(q, k, v, batch=batch, seq=seq)
        x2d = _merge(oa, lse, ob, oc, p2, x2d, expand, lw, l, alpha=alpha)
        x2d = _ffn(x2d, lw, l, alpha=alpha, seq=seq)
    return x2d.reshape(batch, seq, d)
```

```python
import functools
import math

import jax
import jax.numpy as jnp
import numpy as np
from jax import lax
from jax.experimental import pallas as pl
from jax.experimental.pallas import tpu as pltpu

F32 = jnp.float32
BF16 = jnp.bfloat16

D_MODEL = 1024
HEAD_DIM = 64
BLOCK = 128
A_GROUPS = ((128, 1), (512, 4), (2048, 16))
A_HEADS = 8
B_Q_HEADS = 8
B_WINDOW = 128
C_HEADS = 8
C_Q_RANK = 256
C_KV_RANK = 128
C_NOPE = 64
C_ROPE = 32
C_V = 64
ROPE_BASE = 10000.0
REL_BUCKETS = 32
REL_MAX_DIST = 2048
D_FF = 2816
LN_EPS = 1e-5
RMS_EPS = 1e-6
NEG = -1e30
LOG2E = math.log2(math.e)
FLASH_TQ = 2048
FLASH_TK = 1024
FLASH_HEADS = 4
FLASH_DIAG = 512

COL_GATE = 0
COL_A = 3072
COL_BQ = 4608
COL_C = 5120
COL_BKV = 5632
NP = 6144
A_COLS = 1536

LANE = 128
VMEM_LIMIT = 56 * 1024 * 1024

N_KINDS = 4


def _cparams(n_axes):
    return pltpu.CompilerParams(dimension_semantics=("arbitrary",) * n_axes,
                                vmem_limit_bytes=VMEM_LIMIT)


def _layer_block(a, layer, **kwargs):
    zeros = (0,) * (a.ndim - 1)
    return pl.BlockSpec((None,) + a.shape[1:], lambda *_: (layer,) + zeros, **kwargs)


CHEAP_STRIDE = 4


def _proj_kernel(*refs, dil):
    n_x = len(refs) - (4 if dil > CHEAP_STRIDE else 3)
    x_refs, (w_ref, o_ref, xb_ref, *tmp) = refs[:n_x], refs[n_x:]

    @pl.when(pl.program_id(1) == 0)
    def _():
        if dil == 1:
            xb_ref[...] = x_refs[0][...].astype(BF16)
            return
        span = BLOCK * dil
        d1 = min(dil, CHEAP_STRIDE)
        d2 = dil // d1
        for c, x_ref in enumerate(x_refs):
            for s0 in range(0, xb_ref.shape[0], span):
                for r1 in range(d1):
                    if d2 == 1:
                        rows = x_ref[pl.ds(s0 + r1, BLOCK, stride=d1), :]
                        xb_ref[s0 + r1 * BLOCK:s0 + (r1 + 1) * BLOCK, c * LANE:(c + 1) * LANE] = rows.astype(BF16)
                        continue
                    g0 = s0 + r1 * (span // d1)
                    tmp[0][c, g0:g0 + span // d1, :] = x_ref[pl.ds(s0 + r1, span // d1, stride=d1), :]
                    for r2 in range(d2):
                        r = r1 + d1 * r2
                        rows = tmp[0][c, pl.ds(g0 + r2, BLOCK, stride=d2), :]
                        xb_ref[s0 + r * BLOCK:s0 + (r + 1) * BLOCK, c * LANE:(c + 1) * LANE] = rows.astype(BF16)

    w = w_ref[pl.program_id(1)]
    o_ref[...] = jnp.dot(xb_ref[...], w, preferred_element_type=F32).astype(o_ref.dtype)


def _proj(x2d, w_tiles, layer, *, dil, tm, name):
    t, k = x2d.shape
    n_tiles, tn = w_tiles.shape[1], w_tiles.shape[3]
    n = n_tiles * tn
    if dil == 1:
        x_specs = [pl.BlockSpec((tm, k), lambda i, j: (i, 0))]
    else:
        x_specs = [pl.BlockSpec((tm, LANE), lambda i, j, c=c: (i, c)) for c in range(k // LANE)]
    scratch = [pltpu.VMEM((tm, k), BF16)]
    if dil > CHEAP_STRIDE:
        scratch.append(pltpu.VMEM((k // LANE, tm, LANE), F32))
    return pl.pallas_call(
        functools.partial(_proj_kernel, dil=dil),
        grid=(t // tm, n_tiles),
        in_specs=x_specs + [_layer_block(w_tiles, layer, pipeline_mode=pl.Buffered(1))],
        out_specs=pl.BlockSpec((tm, tn), lambda i, j: (i, j)),
        out_shape=jax.ShapeDtypeStruct((t, n), BF16),
        scratch_shapes=scratch,
        compiler_params=_cparams(2),
        name=name,
    )(*([x2d] * len(x_specs)), w_tiles)


def _t5_bucket(dist):
    n = jnp.maximum(dist, 0)
    max_exact = REL_BUCKETS // 2
    scaled = jnp.log(jnp.maximum(n, 1).astype(F32) / max_exact) / math.log(REL_MAX_DIST / max_exact)
    large = max_exact + (scaled * (REL_BUCKETS - max_exact)).astype(jnp.int32)
    return jnp.where(n < max_exact, n, jnp.minimum(large, REL_BUCKETS - 1))


def _bias_codes():
    qi = jnp.arange(BLOCK)[:, None]
    ki = jnp.arange(2 * BLOCK)[None, :]
    step = BLOCK + qi - ki
    has_prev = ki >= BLOCK
    codes = []
    for kind in range(N_KINDS):
        if kind < len(A_GROUPS):
            dil = A_GROUPS[kind][1]
            band = (step >= 0) & (step <= BLOCK)
        else:
            dil = 1
            band = (step >= 0) & (step < B_WINDOW)
        bucket = _t5_bucket(step * dil)
        codes.append(jnp.stack([jnp.where(band & has_prev, bucket, -1),
                                jnp.where(band, bucket, -1)]))
    return jnp.stack(codes).astype(jnp.int32)


def _bias_kernel(rel_ref, code_ref, o_ref):
    kind = pl.program_id(0)
    code = code_ref[0, 0]
    for slot in range(A_HEADS):
        b_head = len(A_GROUPS) * A_HEADS + slot // 2 + 4 * (slot % 2)
        col = jnp.where(kind < len(A_GROUPS), kind * A_HEADS + slot, b_head)
        acc = jnp.full(code.shape, NEG, F32)
        for b in range(REL_BUCKETS):
            acc = jnp.where(code == b, rel_ref[b, col] * LOG2E, acc)
        o_ref[0, 0, slot] = acc


def _bias_tables(rel_table):
    codes = _bias_codes()
    return pl.pallas_call(
        _bias_kernel,
        grid=(N_KINDS, 2),
        in_specs=[pl.BlockSpec(memory_space=pltpu.SMEM),
                  pl.BlockSpec((1, 1, BLOCK, 2 * BLOCK), lambda k, v: (k, v, 0, 0))],
        out_specs=pl.BlockSpec((1, 1, A_HEADS, BLOCK, 2 * BLOCK), lambda k, v: (k, v, 0, 0, 0)),
        out_shape=jax.ShapeDtypeStruct((N_KINDS, 2, A_HEADS, BLOCK, 2 * BLOCK), F32),
        compiler_params=_cparams(2),
        name="bias_tables",
    )(rel_table, codes)


BAND_QB = 8
STAT_LANES = LANE // A_HEADS // 2


def _band_kernel(sink_ref, q_ref, kp_ref, kc_ref, vp_ref, vc_ref, bias_ref, o_ref, *lse_refs,
                 shared_kv, with_sink):
    first = pl.program_id(2) == 0
    lane = lax.broadcasted_iota(jnp.int32, (BLOCK, LANE), 1)
    low = lane < HEAD_DIM
    for qb in range(q_ref.shape[0]):
        variant = jnp.where(first, 0, 1) if qb == 0 else 1
        lse_tile = jnp.zeros((BLOCK, LANE), F32)
        for pair in range(4):
            cols = slice(pair * LANE, (pair + 1) * LANE)
            kv_cols = slice(0, LANE) if shared_kv else cols
            qp = q_ref[qb, :, cols]
            k_prev = kp_ref[:, kv_cols] if qb == 0 else kc_ref[qb - 1, :, kv_cols]
            v_prev = vp_ref[:, kv_cols] if qb == 0 else vc_ref[qb - 1, :, kv_cols]
            kcat = jnp.concatenate([k_prev, kc_ref[qb, :, kv_cols]], axis=0)
            vcat = jnp.concatenate([v_prev, vc_ref[qb, :, kv_cols]], axis=0)
            zero = jnp.zeros_like(qp)
            q2 = jnp.concatenate([jnp.where(low, qp, zero), jnp.where(low, zero, qp)], axis=0)
            s = lax.dot_general(q2, kcat, (((1,), (1,)), ((), ())), preferred_element_type=F32)
            s = s + bias_ref[variant, pair]
            m = jnp.max(s, axis=-1, keepdims=True)
            if with_sink:
                head_a = lax.broadcasted_iota(jnp.int32, (2 * BLOCK, 1), 0) < BLOCK
                sink = jnp.where(head_a, sink_ref[2 * pair], sink_ref[2 * pair + 1])
                m = jnp.maximum(m, sink)
            p = jnp.exp2(s - m)
            l = jnp.sum(p, axis=-1, keepdims=True)
            if with_sink:
                l = l + jnp.exp2(sink - m)
            o = jnp.dot(p.astype(BF16), vcat, preferred_element_type=F32)
            if lse_refs:
                for sub in range(2):
                    rows = slice(sub * BLOCK, (sub + 1) * BLOCK)
                    field = 2 * (2 * pair + sub)
                    lse_tile = jnp.where(lane // STAT_LANES == field, m[rows],
                                         jnp.where(lane // STAT_LANES == field + 1, l[rows], lse_tile))
            else:
                o = o * (1.0 / l)
            o_ref[qb, :, cols] = jnp.where(low, o[:BLOCK], o[BLOCK:]).astype(o_ref.dtype)
        if lse_refs:
            lse_refs[0][qb] = lse_tile


def _band_attention(p2, bias, sinks, *, kind, batch, seq):
    is_b = kind == len(A_GROUPS)
    dil = 1 if is_b else A_GROUPS[kind][1]
    n_span = seq // (BLOCK * dil)
    view = p2.reshape(batch, n_span, dil, BLOCK, p2.shape[-1])
    if is_b:
        q_blk, k_blk, v_blk, kv_w = COL_BQ // 512, COL_BKV // LANE, COL_BKV // LANE + 1, LANE
    else:
        q_blk = COL_A // 512 if dil == 1 else 0
        k_blk, v_blk, kv_w = q_blk + 1, q_blk + 2, 512

    n_qb = min(BAND_QB, n_span)

    def cur(blk, width):
        return pl.BlockSpec((None, n_qb, None, BLOCK, width), lambda b, r, n: (b, n, r, 0, blk))

    def prev(blk, width):
        return pl.BlockSpec((None, None, None, BLOCK, width),
                            lambda b, r, n: (b, jnp.maximum(n * n_qb - 1, 0), r, 0, blk))

    in_specs = [
        pl.BlockSpec(memory_space=pltpu.SMEM),
        cur(q_blk, 512), prev(k_blk, kv_w), cur(k_blk, kv_w), prev(v_blk, kv_w), cur(v_blk, kv_w),
        pl.BlockSpec((None, 2, A_HEADS // 2, 2 * BLOCK, 2 * BLOCK), lambda b, r, n: (kind, 0, 0, 0, 0)),
    ]
    bias = bias.reshape(N_KINDS, 2, A_HEADS // 2, 2 * BLOCK, 2 * BLOCK)
    out_specs = [pl.BlockSpec((None, n_qb, None, BLOCK, 512), lambda b, r, n: (b, n, r, 0, 0))]
    out_shape = [jax.ShapeDtypeStruct((batch, n_span, dil, BLOCK, 512), BF16)]
    if not is_b:
        out_specs.append(pl.BlockSpec((None, n_qb, None, BLOCK, LANE), lambda b, r, n: (b, n, r, 0, 0)))
        out_shape.append(jax.ShapeDtypeStruct((batch, n_span, dil, BLOCK, LANE), F32))
    outs = pl.pallas_call(
        functools.partial(_band_kernel, shared_kv=is_b, with_sink=is_b),
        grid=(batch, dil, n_span // n_qb),
        in_specs=in_specs,
        out_specs=out_specs,
        out_shape=out_shape,
        compiler_params=_cparams(3),
        name="band_b" if is_b else f"band_a{kind}",
    )(sinks, view, view, view, view, view, bias)
    return outs[0] if is_b else outs


def _mla_prep_kernel(c_ref, cos_ref, sin_ref, gq_ref, gkv_ref, wq1_ref, wq2_ref, wk_ref, wv_ref,
                     sela_ref, selb_ref, q_ref, k_ref, v_ref):
    c = c_ref[...]
    cq = c[:, :C_Q_RANK].astype(F32)
    ckv = c[:, C_Q_RANK:C_Q_RANK + C_KV_RANK].astype(F32)
    nq = cq * lax.rsqrt(jnp.mean(cq * cq, axis=-1, keepdims=True) + RMS_EPS) * gq_ref[...]
    nkv = ckv * lax.rsqrt(jnp.mean(ckv * ckv, axis=-1, keepdims=True) + RMS_EPS) * gkv_ref[...]
    nq = nq.astype(BF16)
    nkv = nkv.astype(BF16)
    cos = cos_ref[...]
    sin = sin_ref[...]
    cos8 = jnp.tile(cos, (1, C_HEADS))
    sin8 = jnp.tile(sin, (1, C_HEADS))
    q = (jnp.dot(nq, wq1_ref[...], preferred_element_type=F32) * cos8
         + jnp.dot(nq, wq2_ref[...], preferred_element_type=F32) * sin8)
    q_ref[...] = q.astype(q_ref.dtype)
    kr = (jnp.dot(c, sela_ref[...], preferred_element_type=F32) * cos
          + jnp.dot(c, selb_ref[...], preferred_element_type=F32) * sin)
    k = jnp.dot(nkv, wk_ref[...], preferred_element_type=F32) + jnp.tile(kr, (1, C_HEADS))
    k_ref[...] = k.astype(k_ref.dtype)
    lane = lax.broadcasted_iota(jnp.int32, (1, C_HEADS * LANE), 1)
    ones_col = jnp.where(lane % LANE == C_V, 1.0, 0.0).astype(F32)
    v_ref[...] = (jnp.dot(nkv, wv_ref[...], preferred_element_type=F32) + ones_col).astype(v_ref.dtype)


def _mla_prep(p2, cos_t, sin_t, sel, lw, layer, *, seq, tm=512):
    t = p2.shape[0]
    per_seq = seq // tm
    full = lambda a: pl.BlockSpec(a.shape, lambda i: (0,) * a.ndim)
    stacked = (lw["gq"], lw["gkv"], lw["wq1"], lw["wq2"], lw["wk"], lw["wv"])
    wide = C_HEADS * LANE
    return pl.pallas_call(
        _mla_prep_kernel,
        grid=(t // tm,),
        in_specs=[pl.BlockSpec((tm, 512), lambda i: (i, COL_C // 512)),
                  pl.BlockSpec((tm, LANE), lambda i: (i % per_seq, 0)),
                  pl.BlockSpec((tm, LANE), lambda i: (i % per_seq, 0))]
                 + [_layer_block(a, layer) for a in stacked] + [full(a) for a in sel],
        out_specs=[pl.BlockSpec((tm, wide), lambda i: (i, 0))] * 3,
        out_shape=[jax.ShapeDtypeStruct((t, wide), BF16)] * 3,
        compiler_params=_cparams(1),
        name="mla_prep",
    )(p2, cos_t, sin_t, *stacked, *sel)


def _mla_flash_kernel(q_ref, k_ref, v_ref, o_ref, m_ref, acc_ref, *, tq, tk, heads):
    qi = pl.program_id(2)
    ki = pl.program_id(3)
    ratio = tq // tk

    @pl.when(ki == 0)
    def _():
        m_ref[...] = jnp.full(m_ref.shape, NEG, F32)
        acc_ref[...] = jnp.zeros(acc_ref.shape, F32)

    def update(h, rows, n_keys, shift):
        hc = slice(h * LANE, (h + 1) * LANE)
        s = lax.dot_general(q_ref[rows, hc], k_ref[:n_keys, hc], (((1,), (1,)), ((), ())),
                            preferred_element_type=F32)
        if shift is not None:
            row = lax.broadcasted_iota(jnp.int32, s.shape, 0) + shift
            col = lax.broadcasted_iota(jnp.int32, s.shape, 1)
            s = jnp.where(row >= col, s, NEG)
        m_prev = m_ref[h, rows]
        m_new = jnp.maximum(m_prev, jnp.max(s, axis=-1, keepdims=True))
        alpha = jnp.exp2(m_prev - m_new)
        p = jnp.exp2(s - jnp.tile(m_new, (1, n_keys // LANE)))
        acc_ref[h, rows] = alpha * acc_ref[h, rows] + jnp.dot(p.astype(BF16), v_ref[:n_keys, hc],
                                                              preferred_element_type=F32)
        m_ref[h, rows] = m_new

    @pl.when(ki < qi * ratio)
    def _():
        for h in range(heads):
            update(h, slice(0, tq), tk, None)

    for d in range(ratio):
        @pl.when(ki == qi * ratio + d)
        def _(d=d):
            for h in range(heads):
                if (d + 1) * tk < tq:
                    update(h, slice((d + 1) * tk, tq), tk, None)
                for r0 in range(0, tk, FLASH_DIAG):
                    update(h, slice(d * tk + r0, d * tk + r0 + FLASH_DIAG), r0 + FLASH_DIAG, r0)

    @pl.when(ki == (qi + 1) * ratio - 1)
    def _():
        low = lax.broadcasted_iota(jnp.int32, (tq, LANE), 1) < C_V
        for pair in range(heads // 2):
            halves = []
            for h in (2 * pair, 2 * pair + 1):
                acc = acc_ref[h]
                halves.append(acc * (1.0 / acc[:, C_V:C_V + 1]))
            odd = pltpu.roll(halves[1], C_V, axis=1)
            o_ref[:, pair * LANE:(pair + 1) * LANE] = jnp.where(low, halves[0], odd).astype(o_ref.dtype)


def _mla_flash(q, k, v, *, batch, seq, tq=FLASH_TQ, tk=FLASH_TK, heads=FLASH_HEADS):
    wide = C_HEADS * LANE
    ratio = tq // tk
    q3 = q.reshape(batch, seq, wide)
    k3 = k.reshape(batch, seq, wide)
    v3 = v.reshape(batch, seq, wide)
    kv_map = lambda b, g, i, j: (b, jnp.minimum(j, (i + 1) * ratio - 1), g)
    out = pl.pallas_call(
        functools.partial(_mla_flash_kernel, tq=tq, tk=tk, heads=heads),
        grid=(batch, C_HEADS // heads, seq // tq, seq // tk),
        in_specs=[pl.BlockSpec((None, tq, heads * LANE), lambda b, g, i, j: (b, i, g)),
                  pl.BlockSpec((None, tk, heads * LANE), kv_map),
                  pl.BlockSpec((None, tk, heads * LANE), kv_map)],
        out_specs=pl.BlockSpec((None, tq, heads * C_V), lambda b, g, i, j: (b, i, g)),
        out_shape=jax.ShapeDtypeStruct((batch, seq, C_HEADS * C_V), BF16),
        scratch_shapes=[pltpu.VMEM((heads, tq, LANE), F32),
                        pltpu.VMEM((heads, tq, LANE), F32)],
        compiler_params=_cparams(4),
        name="mla_flash",
    )(q3, k3, v3)
    return out.reshape(batch * seq, C_HEADS * C_V)


def _layer_norm(y, g, b):
    mu = jnp.mean(y, axis=-1, keepdims=True)
    d = y - mu
    var = jnp.mean(d * d, axis=-1, keepdims=True)
    return d * lax.rsqrt(var + LN_EPS) * g + b


def _token_order(src_ref, tmp_ref, dil):
    if dil == 1:
        return src_ref[...].astype(F32)
    n = src_ref.shape[1]
    n_slab = src_ref.shape[2] // LANE
    for r in range(dil):
        rows = src_ref[r].astype(F32)
        for c in range(n_slab):
            tmp_ref[c, pl.ds(r, n, stride=dil), :] = rows[:, c * LANE:(c + 1) * LANE]
    return jnp.concatenate([tmp_ref[c] for c in range(n_slab)], axis=1)


def _merge_kernel(oa0_ref, oa1_ref, oa2_ref, l0_ref, l1_ref, l2_ref, ob_ref, oc_ref, gate_ref, x_ref,
                  e_ref, wb_ref, wo_ref, bg_ref, g_ref, b_ref, o_ref, ot1_ref, ot2_ref, lt1_ref, lt2_ref,
                  *, alpha):
    dils = [d for _, d in A_GROUPS]
    stats = [_token_order(ref, tmp, d)
             for ref, tmp, d in zip((l0_ref, l1_ref, l2_ref), (None, lt1_ref, lt2_ref), dils)]
    is_max = lax.broadcasted_iota(jnp.int32, stats[0].shape, 1) // STAT_LANES % 2 == 0
    top = jnp.maximum(jnp.maximum(stats[0], stats[1]), stats[2])
    es = [jnp.exp2(v - top) for v in stats]
    sums = [pltpu.roll(v, LANE - STAT_LANES, axis=1) for v in stats]
    inv = 1.0 / (es[0] * sums[0] + es[1] * sums[1] + es[2] * sums[2])
    o_a = None
    for e, oa_ref, tmp, d in zip(es, (oa0_ref, oa1_ref, oa2_ref), (None, ot1_ref, ot2_ref), dils):
        w = jnp.where(is_max, e * inv, 0.0)
        hi = w.astype(BF16)
        lo = (w - hi.astype(F32)).astype(BF16)
        wide = (jnp.dot(hi, e_ref[...], preferred_element_type=F32)
                + jnp.dot(lo, e_ref[...], preferred_element_type=F32))
        term = wide * _token_order(oa_ref, tmp, d)
        o_a = term if o_a is None else o_a + term
    branches = (o_a.astype(BF16), ob_ref[...], oc_ref[...])
    merged = None
    for i, br in enumerate(branches):
        gate = jax.nn.sigmoid(gate_ref[:, i * D_MODEL:(i + 1) * D_MODEL].astype(F32)
                              + bg_ref[:, i * D_MODEL:(i + 1) * D_MODEL])
        term = gate * jnp.dot(br, wb_ref[i], preferred_element_type=F32)
        merged = term if merged is None else merged + term
    mix = jnp.dot(merged.astype(BF16), wo_ref[...], preferred_element_type=F32)
    o_ref[...] = _layer_norm(alpha * x_ref[...] + mix, g_ref[...], b_ref[...])


def _merge(oa, lse, ob, oc, p2, x2d, expand, lw, layer, *, alpha, tm=512):
    t = x2d.shape[0]
    row = lambda w: pl.BlockSpec((tm, w), lambda i: (i, 0))
    full = lambda a: pl.BlockSpec(a.shape, lambda i: (0,) * a.ndim)

    def grouped(arrs):
        views, specs = [], []
        for a, (_, dil) in zip(arrs, A_GROUPS):
            w = a.shape[-1]
            if dil == 1:
                views.append(a.reshape(t, w))
                specs.append(row(w))
                continue
            per_span = BLOCK * dil // tm
            views.append(a.reshape(-1, dil, BLOCK, w))
            specs.append(pl.BlockSpec((None, dil, tm // dil, w),
                                      lambda i, per_span=per_span: (i // per_span, 0, i % per_span, 0)))
        return views, specs

    oa_v, oa_s = grouped(oa)
    lse_v, lse_s = grouped(lse)
    stacked = (lw["wb"], lw["wo"], lw["bg"], lw["ln1_g"], lw["ln1_b"])
    slabs = lambda w: pltpu.VMEM((w // LANE, tm, LANE), F32)
    return pl.pallas_call(
        functools.partial(_merge_kernel, alpha=alpha),
        grid=(t // tm,),
        in_specs=oa_s + lse_s + [row(512), row(512),
                  pl.BlockSpec((tm, 3 * D_MODEL), lambda i: (i, COL_GATE // (3 * D_MODEL))),
                  row(D_MODEL), full(expand)] + [_layer_block(a, layer) for a in stacked],
        out_specs=row(D_MODEL),
        out_shape=jax.ShapeDtypeStruct((t, D_MODEL), F32),
        scratch_shapes=[slabs(512), slabs(512), slabs(LANE), slabs(LANE)],
        compiler_params=_cparams(1),
        name="merge",
    )(*oa_v, *lse_v, ob.reshape(t, 512), oc, p2, x2d, expand, *stacked)


HALO = 8
FF_CHUNK = 512


def _ffn_kernel(halo_ref, x_ref, wup_ref, cw_ref, cb_ref, wdn_ref, g_ref, b_ref, o_ref,
                ug_ref, uv_ref, *, alpha, tm, per_seq):
    i = pl.program_id(0)
    x = x_ref[...]
    halo = jnp.where(i % per_seq == 0, jnp.zeros_like(halo_ref[...]), halo_ref[...])
    xh = jnp.concatenate([halo, x], axis=0).astype(BF16)
    chunks = [(c0, min(FF_CHUNK, D_FF - c0)) for c0 in range(0, D_FF, FF_CHUNK)]

    def up(c0, cf):
        return [jnp.dot(xh, wup_ref[:, base:base + cf], preferred_element_type=F32)
                for base in (c0, D_FF + c0)]

    acc = None
    u_next = up(*chunks[0])
    for ci, (c0, cf) in enumerate(chunks):
        u_cur = u_next
        if ci + 1 < len(chunks):
            u_next = up(*chunks[ci + 1])
        parts = []
        for u, base in zip(u_cur, (c0, D_FF + c0)):
            y = cb_ref[:, base:base + cf]
            for tap in range(3):
                y = y + u[HALO - 2 + tap:HALO - 2 + tap + tm] * cw_ref[tap:tap + 1, base:base + cf]
            parts.append(y)
        act = (parts[0] * jax.nn.sigmoid(parts[0]) * parts[1]).astype(BF16)
        term = jnp.dot(act, wdn_ref[c0:c0 + cf, :], preferred_element_type=F32)
        acc = term if acc is None else acc + term
    o_ref[...] = _layer_norm(alpha * x + acc, g_ref[...], b_ref[...])


def _ffn(x2d, lw, layer, *, alpha, seq, tm=512):
    t = x2d.shape[0]
    per_seq = seq // tm
    consts = (lw["wup"], lw["cw"], lw["cb"], lw["wdn"], lw["ln2_g"], lw["ln2_b"])
    return pl.pallas_call(
        functools.partial(_ffn_kernel, alpha=alpha, tm=tm, per_seq=per_seq),
        grid=(t // tm,),
        in_specs=[pl.BlockSpec((HALO, D_MODEL), lambda i: (jnp.maximum(i * (tm // HALO) - 1, 0), 0)),
                  pl.BlockSpec((tm, D_MODEL), lambda i: (i, 0))]
                 + [_layer_block(a, layer, pipeline_mode=pl.Buffered(1)) for a in consts],
        out_specs=pl.BlockSpec((tm, D_MODEL), lambda i: (i, 0)),
        out_shape=jax.ShapeDtypeStruct((t, D_MODEL), F32),
        scratch_shapes=[pltpu.VMEM((tm + HALO, FF_CHUNK), F32),
                        pltpu.VMEM((tm + HALO, FF_CHUNK), F32)],
        compiler_params=_cparams(1),
        name="ffn",
    )(x2d, x2d, *consts)


def _prepare_weights(w_in, b_gate, q_norm_g, kv_norm_g, w_uq, w_ukv, w_branch, w_out,
                     ln1_g, ln1_b, w_ffn_up, conv_w, conv_b, w_ffn_down, ln2_g, ln2_b):
    depth = w_in.shape[0]
    a_end = 4608
    col = np.arange(w_in.shape[-1])
    is_q = ((col < a_end) & (col % A_COLS < A_HEADS * HEAD_DIM)) | ((col >= a_end) & (col < a_end + 512))
    col_scale = jnp.asarray(np.where(is_q, HEAD_DIM ** -0.5 * LOG2E, 1.0), F32)
    wi = (w_in * col_scale).astype(BF16)
    bq = wi[:, :, a_end:a_end + 512].reshape(depth, D_MODEL, B_Q_HEADS, HEAD_DIM)
    bq = jnp.stack([bq[:, :, :4], bq[:, :, 4:]], axis=3).reshape(depth, D_MODEL, 512)
    bkv = wi[:, :, 5120:5376]
    cdq = wi[:, :, 5376:5632]
    ckv = wi[:, :, 5632:5760]
    kr = wi[:, :, 5760:5792]
    kr_rot = jnp.concatenate([-kr[:, :, C_ROPE // 2:], kr[:, :, :C_ROPE // 2]], axis=2)
    gate = wi[:, :, 5792:]
    zeros = lambda n: jnp.zeros((depth, D_MODEL, n), wi.dtype)
    rest = jnp.concatenate([bq, cdq, ckv, kr, kr_rot, zeros(64), bkv, zeros(256)], axis=2)
    wp = jnp.stack([gate[:, :, :A_COLS], gate[:, :, A_COLS:], wi[:, :, :A_COLS], rest], axis=1)
    wa = []
    for g, (_, dil) in enumerate(A_GROUPS[1:], start=1):
        tn = A_COLS * 1024 // max(1024, BLOCK * dil)
        w = wi[:, :, g * A_COLS:(g + 1) * A_COLS]
        wa.append(w.reshape(depth, D_MODEL, A_COLS // tn, tn).transpose(0, 2, 1, 3))

    uq = (w_uq * ((C_NOPE + C_ROPE) ** -0.5 * LOG2E)).reshape(depth, C_Q_RANK, C_HEADS, C_NOPE + C_ROPE)
    zq = lambda n: jnp.zeros((depth, C_Q_RANK, C_HEADS, n), uq.dtype)
    half = C_ROPE // 2
    wq1 = jnp.concatenate([uq, zq(LANE - C_NOPE - C_ROPE)], axis=3)
    wq2 = jnp.concatenate([zq(C_NOPE), -uq[..., C_NOPE + half:], uq[..., C_NOPE:C_NOPE + half],
                           zq(LANE - C_NOPE - C_ROPE)], axis=3)
    ukv = w_ukv.reshape(depth, C_KV_RANK, C_HEADS, C_NOPE + C_V)
    zkv = lambda n: jnp.zeros((depth, C_KV_RANK, C_HEADS, n), ukv.dtype)
    wk = jnp.concatenate([ukv[..., :C_NOPE], zkv(LANE - C_NOPE)], axis=3)
    wv = jnp.concatenate([ukv[..., C_NOPE:], zkv(LANE - C_V)], axis=3)

    wb1 = w_branch[:, 1].reshape(depth, B_Q_HEADS, HEAD_DIM, D_MODEL)
    wb1 = jnp.stack([wb1[:, :4], wb1[:, 4:]], axis=2).reshape(depth, 512, D_MODEL)
    wb = jnp.stack([w_branch[:, 0], wb1, w_branch[:, 2]], axis=1)

    row = lambda a: a.reshape(depth, 1, -1)
    return dict(
        wp=wp, wa=wa,
        gq=row(q_norm_g), gkv=row(kv_norm_g),
        wq1=wq1.reshape(depth, C_Q_RANK, -1).astype(BF16), wq2=wq2.reshape(depth, C_Q_RANK, -1).astype(BF16),
        wk=wk.reshape(depth, C_KV_RANK, -1).astype(BF16), wv=wv.reshape(depth, C_KV_RANK, -1).astype(BF16),
        wb=wb.astype(BF16), wo=w_out.astype(BF16), bg=row(b_gate),
        ln1_g=row(ln1_g), ln1_b=row(ln1_b),
        wup=w_ffn_up.astype(BF16), cw=conv_w, cb=row(conv_b), wdn=w_ffn_down.astype(BF16),
        ln2_g=row(ln2_g), ln2_b=row(ln2_b),
    )


def _rope_selectors():
    sel = np.zeros((2, 512, LANE), np.float32)
    for j in range(C_ROPE):
        sel[0, C_Q_RANK + C_KV_RANK + j, C_NOPE + j] = 1.0
        sel[1, C_Q_RANK + C_KV_RANK + C_ROPE + j, C_NOPE + j] = 1.0
    return jnp.asarray(sel[0], BF16), jnp.asarray(sel[1], BF16)


def _rope_tables(seq):
    pos = jnp.arange(seq, dtype=F32)
    inv_freq = ROPE_BASE ** (-jnp.arange(0, C_ROPE, 2, dtype=F32) / C_ROPE)
    ang = pos[:, None] * inv_freq[None, :]
    cos, sin = jnp.cos(ang), jnp.sin(ang)
    pad = jnp.zeros((seq, LANE - C_NOPE - C_ROPE), F32)
    cos_t = jnp.concatenate([jnp.ones((seq, C_NOPE), F32), cos, cos, pad], axis=1)
    sin_t = jnp.concatenate([jnp.zeros((seq, C_NOPE), F32), sin, sin, pad], axis=1)
    return cos_t, sin_t


def _expand_matrix():
    e = np.zeros((LANE, A_HEADS * HEAD_DIM), np.float32)
    for c in range(A_HEADS * HEAD_DIM):
        e[(LANE // A_HEADS) * (c // HEAD_DIM), c] = 1.0
    return jnp.asarray(e, BF16)


def kernel(x, rel_table, w_in, b_gate, sinks, q_norm_g, kv_norm_g, w_uq, w_ukv, w_branch, w_out,
           ln1_g, ln1_b, w_ffn_up, conv_w, conv_b, w_ffn_down, ln2_g, ln2_b):
    batch, seq, d = x.shape
    depth = w_in.shape[0]
    alpha = (2 * depth) ** 0.25
    cos_t, sin_t = _rope_tables(seq)
    sel = _rope_selectors()
    expand = _expand_matrix()
    bias = _bias_tables(rel_table)
    lw = _prepare_weights(w_in, b_gate, q_norm_g, kv_norm_g, w_uq, w_ukv, w_branch, w_out,
                          ln1_g, ln1_b, w_ffn_up, conv_w, conv_b, w_ffn_down, ln2_g, ln2_b)
    x2d = x.reshape(batch * seq, d)
    for l in range(depth):
        p2 = _proj(x2d, lw["wp"], l, dil=1, tm=1024, name="proj")
        p_dil = [_proj(x2d, w, l, dil=dil, tm=max(1024, BLOCK * dil), name=f"proj_a{g + 1}")
                 for g, (w, (_, dil)) in enumerate(zip(lw["wa"], A_GROUPS[1:]))]
        sink_slots = sinks[l].reshape(2, 4).T.reshape(-1) * LOG2E
        oa, lse = [], []
        for kind in range(len(A_GROUPS)):
            src = p2 if kind == 0 else p_dil[kind - 1]
            o_g, lse_g = _band_attention(src, bias, sink_slots, kind=kind, batch=batch, seq=seq)
            oa.append(o_g)
            lse.append(lse_g)
        ob = _band_attention(p2, bias, sink_slots, kind=len(A_GROUPS), batch=batch, seq=seq)
        q, k, v = _mla_prep(p2, cos_t, sin_t, sel, lw, l, seq=seq)
        oc = _mla_flash(q, k, v, batch=batch, seq=seq)
        x2d = _merge(oa, lse, ob, oc, p2, x2d, expand, lw, l, alpha=alpha)
        x2d = _ffn(x2d, lw, l, alpha=alpha, seq=seq)
    return x2d.reshape(batch, seq, d)
```

```python
import functools
import math

import jax
import jax.numpy as jnp
import numpy as np
from jax import lax
from jax.experimental import pallas as pl
from jax.experimental.pallas import tpu as pltpu

F32 = jnp.float32
BF16 = jnp.bfloat16

D_MODEL = 1024
HEAD_DIM = 64
BLOCK = 128
A_GROUPS = ((128, 1), (512, 4), (2048, 16))
A_HEADS = 8
B_Q_HEADS = 8
B_WINDOW = 128
C_HEADS = 8
C_Q_RANK = 256
C_KV_RANK = 128
C_NOPE = 64
C_ROPE = 32
C_V = 64
ROPE_BASE = 10000.0
REL_BUCKETS = 32
REL_MAX_DIST = 2048
D_FF = 2816
LN_EPS = 1e-5
RMS_EPS = 1e-6
NEG = -1e30
LOG2E = math.log2(math.e)
FLASH_TQ = 1024
FLASH_TK = 1024
FLASH_HEADS = 8
FLASH_DIAG = 512

COL_GATE = 0
COL_A = 3072
COL_BQ = 4608
COL_C = 5120
COL_BKV = 5632
NP = 6144
A_COLS = 1536

LANE = 128
SUBLANES = 8
VMEM_LIMIT = 56 * 1024 * 1024

N_KINDS = 4


def _cparams(n_axes):
    return pltpu.CompilerParams(dimension_semantics=("arbitrary",) * n_axes,
                                vmem_limit_bytes=VMEM_LIMIT)


def _layer_block(a, layer, **kwargs):
    zeros = (0,) * (a.ndim - 1)
    return pl.BlockSpec((None,) + a.shape[1:], lambda *_: (layer,) + zeros, **kwargs)


CHEAP_STRIDE = 4


def _proj_kernel(*refs, dil):
    n_x = len(refs) - (4 if dil > CHEAP_STRIDE else 3)
    x_refs, (w_ref, o_ref, xb_ref, *tmp) = refs[:n_x], refs[n_x:]

    @pl.when(pl.program_id(1) == 0)
    def _():
        if dil == 1:
            xb_ref[...] = x_refs[0][...].astype(BF16)
            return
        span = BLOCK * dil
        d1 = min(dil, CHEAP_STRIDE)
        d2 = dil // d1
        for c, x_ref in enumerate(x_refs):
            for s0 in range(0, xb_ref.shape[0], span):
                for r1 in range(d1):
                    if d2 == 1:
                        rows = x_ref[pl.ds(s0 + r1, BLOCK, stride=d1), :]
                        xb_ref[s0 + r1 * BLOCK:s0 + (r1 + 1) * BLOCK, c * LANE:(c + 1) * LANE] = rows.astype(BF16)
                        continue
                    g0 = s0 + r1 * (span // d1)
                    tmp[0][c, g0:g0 + span // d1, :] = x_ref[pl.ds(s0 + r1, span // d1, stride=d1), :]
                    for r2 in range(d2):
                        r = r1 + d1 * r2
                        rows = tmp[0][c, pl.ds(g0 + r2, BLOCK, stride=d2), :]
                        xb_ref[s0 + r * BLOCK:s0 + (r + 1) * BLOCK, c * LANE:(c + 1) * LANE] = rows.astype(BF16)

    w = w_ref[pl.program_id(1)]
    o_ref[...] = jnp.dot(xb_ref[...], w, preferred_element_type=F32).astype(o_ref.dtype)


def _proj(x2d, w_tiles, layer, *, dil, tm, name):
    t, k = x2d.shape
    n_tiles, tn = w_tiles.shape[1], w_tiles.shape[3]
    n = n_tiles * tn
    if dil == 1:
        x_specs = [pl.BlockSpec((tm, k), lambda i, j: (i, 0))]
    else:
        x_specs = [pl.BlockSpec((tm, LANE), lambda i, j, c=c: (i, c)) for c in range(k // LANE)]
    scratch = [pltpu.VMEM((tm, k), BF16)]
    if dil > CHEAP_STRIDE:
        scratch.append(pltpu.VMEM((k // LANE, tm, LANE), F32))
    return pl.pallas_call(
        functools.partial(_proj_kernel, dil=dil),
        grid=(t // tm, n_tiles),
        in_specs=x_specs + [_layer_block(w_tiles, layer, pipeline_mode=pl.Buffered(1))],
        out_specs=pl.BlockSpec((tm, tn), lambda i, j: (i, j)),
        out_shape=jax.ShapeDtypeStruct((t, n), BF16),
        scratch_shapes=scratch,
        compiler_params=_cparams(2),
        name=name,
    )(*([x2d] * len(x_specs)), w_tiles)


def _t5_bucket(dist):
    n = jnp.maximum(dist, 0)
    max_exact = REL_BUCKETS // 2
    scaled = jnp.log(jnp.maximum(n, 1).astype(F32) / max_exact) / math.log(REL_MAX_DIST / max_exact)
    large = max_exact + (scaled * (REL_BUCKETS - max_exact)).astype(jnp.int32)
    return jnp.where(n < max_exact, n, jnp.minimum(large, REL_BUCKETS - 1))


def _bias_codes():
    qi = jnp.arange(BLOCK)[:, None]
    ki = jnp.arange(2 * BLOCK)[None, :]
    step = BLOCK + qi - ki
    has_prev = ki >= BLOCK
    codes = []
    for kind in range(N_KINDS):
        if kind < len(A_GROUPS):
            dil = A_GROUPS[kind][1]
            band = (step >= 0) & (step <= BLOCK)
        else:
            dil = 1
            band = (step >= 0) & (step < B_WINDOW)
        bucket = _t5_bucket(step * dil)
        codes.append(jnp.stack([jnp.where(band & has_prev, bucket, -1),
                                jnp.where(band, bucket, -1)]))
    return jnp.stack(codes).astype(jnp.int32)


def _bias_kernel(rel_ref, code_ref, o_ref):
    kind = pl.program_id(0)
    code = code_ref[0, 0]
    for slot in range(A_HEADS):
        b_head = len(A_GROUPS) * A_HEADS + slot // 2 + 4 * (slot % 2)
        col = jnp.where(kind < len(A_GROUPS), kind * A_HEADS + slot, b_head)
        acc = jnp.full(code.shape, NEG, F32)
        for b in range(REL_BUCKETS):
            acc = jnp.where(code == b, rel_ref[b, col] * LOG2E, acc)
        o_ref[0, 0, slot] = acc


def _bias_tables(rel_table):
    codes = _bias_codes()
    return pl.pallas_call(
        _bias_kernel,
        grid=(N_KINDS, 2),
        in_specs=[pl.BlockSpec(memory_space=pltpu.SMEM),
                  pl.BlockSpec((1, 1, BLOCK, 2 * BLOCK), lambda k, v: (k, v, 0, 0))],
        out_specs=pl.BlockSpec((1, 1, A_HEADS, BLOCK, 2 * BLOCK), lambda k, v: (k, v, 0, 0, 0)),
        out_shape=jax.ShapeDtypeStruct((N_KINDS, 2, A_HEADS, BLOCK, 2 * BLOCK), F32),
        compiler_params=_cparams(2),
        name="bias_tables",
    )(rel_table, codes)


BAND_QB = 8
STAT_LANES = LANE // A_HEADS // 2


def _band_kernel(sink_ref, q_ref, kp_ref, kc_ref, vp_ref, vc_ref, bias_ref, o_ref, *lse_refs,
                 shared_kv, with_sink):
    first = pl.program_id(2) == 0
    lane = lax.broadcasted_iota(jnp.int32, (BLOCK, LANE), 1)
    low = lane < HEAD_DIM
    for qb in range(q_ref.shape[0]):
        variant = jnp.where(first, 0, 1) if qb == 0 else 1
        lse_tile = jnp.zeros((BLOCK, LANE), F32)
        for pair in range(4):
            cols = slice(pair * LANE, (pair + 1) * LANE)
            kv_cols = slice(0, LANE) if shared_kv else cols
            qp = q_ref[qb, :, cols]
            k_prev = kp_ref[:, kv_cols] if qb == 0 else kc_ref[qb - 1, :, kv_cols]
            v_prev = vp_ref[:, kv_cols] if qb == 0 else vc_ref[qb - 1, :, kv_cols]
            kcat = jnp.concatenate([k_prev, kc_ref[qb, :, kv_cols]], axis=0)
            vcat = jnp.concatenate([v_prev, vc_ref[qb, :, kv_cols]], axis=0)
            zero = jnp.zeros_like(qp)
            q2 = jnp.concatenate([jnp.where(low, qp, zero), jnp.where(low, zero, qp)], axis=0)
            s = lax.dot_general(q2, kcat, (((1,), (1,)), ((), ())), preferred_element_type=F32)
            s = s + bias_ref[variant, pair]
            m = jnp.max(s, axis=-1, keepdims=True)
            if with_sink:
                head_a = lax.broadcasted_iota(jnp.int32, (2 * BLOCK, 1), 0) < BLOCK
                sink = jnp.where(head_a, sink_ref[2 * pair], sink_ref[2 * pair + 1])
                m = jnp.maximum(m, sink)
            p = jnp.exp2(s - m)
            l = jnp.sum(p, axis=-1, keepdims=True)
            if with_sink:
                l = l + jnp.exp2(sink - m)
            o = jnp.dot(p.astype(BF16), vcat, preferred_element_type=F32)
            if lse_refs:
                for sub in range(2):
                    rows = slice(sub * BLOCK, (sub + 1) * BLOCK)
                    field = 2 * (2 * pair + sub)
                    lse_tile = jnp.where(lane // STAT_LANES == field, m[rows],
                                         jnp.where(lane // STAT_LANES == field + 1, l[rows], lse_tile))
            else:
                o = o * (1.0 / l)
            o_ref[qb, :, cols] = jnp.where(low, o[:BLOCK], o[BLOCK:]).astype(o_ref.dtype)
        if lse_refs:
            lse_refs[0][qb] = lse_tile


def _band_attention(p2, bias, sinks, *, kind, batch, seq):
    is_b = kind == len(A_GROUPS)
    dil = 1 if is_b else A_GROUPS[kind][1]
    n_span = seq // (BLOCK * dil)
    view = p2.reshape(batch, n_span, dil, BLOCK, p2.shape[-1])
    if is_b:
        q_blk, k_blk, v_blk, kv_w = COL_BQ // 512, COL_BKV // LANE, COL_BKV // LANE + 1, LANE
    else:
        q_blk = COL_A // 512 if dil == 1 else 0
        k_blk, v_blk, kv_w = q_blk + 1, q_blk + 2, 512

    n_qb = min(BAND_QB, n_span)

    def cur(blk, width):
        return pl.BlockSpec((None, n_qb, None, BLOCK, width), lambda b, r, n: (b, n, r, 0, blk))

    def prev(blk, width):
        return pl.BlockSpec((None, None, None, BLOCK, width),
                            lambda b, r, n: (b, jnp.maximum(n * n_qb - 1, 0), r, 0, blk))

    in_specs = [
        pl.BlockSpec(memory_space=pltpu.SMEM),
        cur(q_blk, 512), prev(k_blk, kv_w), cur(k_blk, kv_w), prev(v_blk, kv_w), cur(v_blk, kv_w),
        pl.BlockSpec((None, 2, A_HEADS // 2, 2 * BLOCK, 2 * BLOCK), lambda b, r, n: (kind, 0, 0, 0, 0)),
    ]
    bias = bias.reshape(N_KINDS, 2, A_HEADS // 2, 2 * BLOCK, 2 * BLOCK)
    out_specs = [pl.BlockSpec((None, n_qb, None, BLOCK, 512), lambda b, r, n: (b, n, r, 0, 0))]
    out_shape = [jax.ShapeDtypeStruct((batch, n_span, dil, BLOCK, 512), BF16)]
    if not is_b:
        out_specs.append(pl.BlockSpec((None, n_qb, None, BLOCK, LANE), lambda b, r, n: (b, n, r, 0, 0)))
        out_shape.append(jax.ShapeDtypeStruct((batch, n_span, dil, BLOCK, LANE), F32))
    outs = pl.pallas_call(
        functools.partial(_band_kernel, shared_kv=is_b, with_sink=is_b),
        grid=(batch, dil, n_span // n_qb),
        in_specs=in_specs,
        out_specs=out_specs,
        out_shape=out_shape,
        compiler_params=_cparams(3),
        name="band_b" if is_b else f"band_a{kind}",
    )(sinks, view, view, view, view, view, bias)
    return outs[0] if is_b else outs


def _mla_prep_kernel(c_ref, cos_ref, sin_ref, gq_ref, gkv_ref, wq1_ref, wq2_ref, wk_ref, wv_ref,
                     sela_ref, selb_ref, q_ref, k_ref, v_ref):
    c = c_ref[...]
    cq = c[:, :C_Q_RANK].astype(F32)
    ckv = c[:, C_Q_RANK:C_Q_RANK + C_KV_RANK].astype(F32)
    nq = cq * lax.rsqrt(jnp.mean(cq * cq, axis=-1, keepdims=True) + RMS_EPS) * gq_ref[...]
    nkv = ckv * lax.rsqrt(jnp.mean(ckv * ckv, axis=-1, keepdims=True) + RMS_EPS) * gkv_ref[...]
    nq = nq.astype(BF16)
    nkv = nkv.astype(BF16)
    cos = cos_ref[...]
    sin = sin_ref[...]
    cos8 = jnp.tile(cos, (1, C_HEADS))
    sin8 = jnp.tile(sin, (1, C_HEADS))
    q = (jnp.dot(nq, wq1_ref[...], preferred_element_type=F32) * cos8
         + jnp.dot(nq, wq2_ref[...], preferred_element_type=F32) * sin8)
    q_ref[...] = q.astype(q_ref.dtype)
    kr = (jnp.dot(c, sela_ref[...], preferred_element_type=F32) * cos
          + jnp.dot(c, selb_ref[...], preferred_element_type=F32) * sin)
    k = jnp.dot(nkv, wk_ref[...], preferred_element_type=F32) + jnp.tile(kr, (1, C_HEADS))
    k_ref[...] = k.astype(k_ref.dtype)
    lane = lax.broadcasted_iota(jnp.int32, (1, C_HEADS * LANE), 1)
    ones_col = jnp.where(lane % LANE == C_V, 1.0, 0.0).astype(F32)
    v_ref[...] = (jnp.dot(nkv, wv_ref[...], preferred_element_type=F32) + ones_col).astype(v_ref.dtype)


def _mla_prep(p2, cos_t, sin_t, sel, lw, layer, *, seq, tm=512):
    t = p2.shape[0]
    per_seq = seq // tm
    full = lambda a: pl.BlockSpec(a.shape, lambda i: (0,) * a.ndim)
    stacked = (lw["gq"], lw["gkv"], lw["wq1"], lw["wq2"], lw["wk"], lw["wv"])
    wide = C_HEADS * LANE
    return pl.pallas_call(
        _mla_prep_kernel,
        grid=(t // tm,),
        in_specs=[pl.BlockSpec((tm, 512), lambda i: (i, COL_C // 512)),
                  pl.BlockSpec((tm, LANE), lambda i: (i % per_seq, 0)),
                  pl.BlockSpec((tm, LANE), lambda i: (i % per_seq, 0))]
                 + [_layer_block(a, layer) for a in stacked] + [full(a) for a in sel],
        out_specs=[pl.BlockSpec((tm, wide), lambda i: (i, 0))] * 3,
        out_shape=[jax.ShapeDtypeStruct((t, wide), BF16)] * 3,
        compiler_params=_cparams(1),
        name="mla_prep",
    )(p2, cos_t, sin_t, *stacked, *sel)


def _mla_flash_kernel(q_ref, k_ref, v_ref, o_ref, m_ref, acc_ref, *, tq, tk, heads):
    qi = pl.program_id(2)
    ki = pl.program_id(3)
    ratio = tq // tk

    @pl.when(ki == 0)
    def _():
        m_ref[...] = jnp.full(m_ref.shape, NEG, F32)
        acc_ref[...] = jnp.zeros(acc_ref.shape, F32)

    def update(h, rows, n_keys, shift):
        hc = slice(h * LANE, (h + 1) * LANE)
        s = lax.dot_general(q_ref[rows, hc], k_ref[:n_keys, hc], (((1,), (1,)), ((), ())),
                            preferred_element_type=F32)
        if shift is not None:
            row = lax.broadcasted_iota(jnp.int32, s.shape, 0) + shift
            col = lax.broadcasted_iota(jnp.int32, s.shape, 1)
            s = jnp.where(row >= col, s, NEG)
        m_prev = m_ref[h, rows]
        m_new = jnp.maximum(m_prev, jnp.max(s, axis=-1, keepdims=True))
        alpha = jnp.exp2(m_prev - m_new)
        p = jnp.exp2(s - jnp.tile(m_new, (1, n_keys // LANE)))
        acc_ref[h, rows] = alpha * acc_ref[h, rows] + jnp.dot(p.astype(BF16), v_ref[:n_keys, hc],
                                                              preferred_element_type=F32)
        m_ref[h, rows] = m_new

    @pl.when(ki < qi * ratio)
    def _():
        for h in range(heads):
            update(h, slice(0, tq), tk, None)

    for d in range(ratio):
        @pl.when(ki == qi * ratio + d)
        def _(d=d):
            for h in range(heads):
                if (d + 1) * tk < tq:
                    update(h, slice((d + 1) * tk, tq), tk, None)
                for r0 in range(0, tk, FLASH_DIAG):
                    update(h, slice(d * tk + r0, d * tk + r0 + FLASH_DIAG), r0 + FLASH_DIAG, r0)

    @pl.when(ki == (qi + 1) * ratio - 1)
    def _():
        low = lax.broadcasted_iota(jnp.int32, (tq, LANE), 1) < C_V
        for pair in range(heads // 2):
            halves = []
            for h in (2 * pair, 2 * pair + 1):
                acc = acc_ref[h]
                halves.append(acc * (1.0 / acc[:, C_V:C_V + 1]))
            odd = pltpu.roll(halves[1], C_V, axis=1)
            o_ref[:, pair * LANE:(pair + 1) * LANE] = jnp.where(low, halves[0], odd).astype(o_ref.dtype)


def _mla_flash(q, k, v, *, batch, seq, tq=FLASH_TQ, tk=FLASH_TK, heads=FLASH_HEADS):
    wide = C_HEADS * LANE
    ratio = tq // tk
    q3 = q.reshape(batch, seq, wide)
    k3 = k.reshape(batch, seq, wide)
    v3 = v.reshape(batch, seq, wide)
    kv_map = lambda b, g, i, j: (b, jnp.minimum(j, (i + 1) * ratio - 1), g)
    out = pl.pallas_call(
        functools.partial(_mla_flash_kernel, tq=tq, tk=tk, heads=heads),
        grid=(batch, C_HEADS // heads, seq // tq, seq // tk),
        in_specs=[pl.BlockSpec((None, tq, heads * LANE), lambda b, g, i, j: (b, i, g)),
                  pl.BlockSpec((None, tk, heads * LANE), kv_map),
                  pl.BlockSpec((None, tk, heads * LANE), kv_map)],
        out_specs=pl.BlockSpec((None, tq, heads * C_V), lambda b, g, i, j: (b, i, g)),
        out_shape=jax.ShapeDtypeStruct((batch, seq, C_HEADS * C_V), BF16),
        scratch_shapes=[pltpu.VMEM((heads, tq, LANE), F32),
                        pltpu.VMEM((heads, tq, LANE), F32)],
        compiler_params=_cparams(4),
        name="mla_flash",
    )(q3, k3, v3)
    return out.reshape(batch * seq, C_HEADS * C_V)


def _layer_norm(y, g, b):
    mu = jnp.mean(y, axis=-1, keepdims=True)
    d = y - mu
    var = jnp.mean(d * d, axis=-1, keepdims=True)
    return d * lax.rsqrt(var + LN_EPS) * g + b


def _token_order(src_ref, tmp_ref, dil):
    if dil == 1:
        return src_ref[...].astype(F32)
    n = src_ref.shape[1]
    n_slab = src_ref.shape[2] // LANE
    for r in range(dil):
        rows = src_ref[r].astype(F32)
        for c in range(n_slab):
            tmp_ref[c, pl.ds(r, n, stride=dil), :] = rows[:, c * LANE:(c + 1) * LANE]
    return jnp.concatenate([tmp_ref[c] for c in range(n_slab)], axis=1)


def _merge_kernel(oa0_ref, oa1_ref, oa2_ref, l0_ref, l1_ref, l2_ref, ob_ref, oc_ref, gate_ref, x_ref,
                  e_ref, wb_ref, wo_ref, bg_ref, g_ref, b_ref, o_ref, ot1_ref, ot2_ref, lt1_ref, lt2_ref,
                  *, alpha):
    dils = [d for _, d in A_GROUPS]
    stats = [_token_order(ref, tmp, d)
             for ref, tmp, d in zip((l0_ref, l1_ref, l2_ref), (None, lt1_ref, lt2_ref), dils)]
    is_max = lax.broadcasted_iota(jnp.int32, stats[0].shape, 1) // STAT_LANES % 2 == 0
    top = jnp.maximum(jnp.maximum(stats[0], stats[1]), stats[2])
    es = [jnp.exp2(v - top) for v in stats]
    sums = [pltpu.roll(v, LANE - STAT_LANES, axis=1) for v in stats]
    inv = 1.0 / (es[0] * sums[0] + es[1] * sums[1] + es[2] * sums[2])
    o_a = None
    for e, oa_ref, tmp, d in zip(es, (oa0_ref, oa1_ref, oa2_ref), (None, ot1_ref, ot2_ref), dils):
        w = jnp.where(is_max, e * inv, 0.0)
        hi = w.astype(BF16)
        lo = (w - hi.astype(F32)).astype(BF16)
        wide = (jnp.dot(hi, e_ref[...], preferred_element_type=F32)
                + jnp.dot(lo, e_ref[...], preferred_element_type=F32))
        term = wide * _token_order(oa_ref, tmp, d)
        o_a = term if o_a is None else o_a + term
    branches = (o_a.astype(BF16), ob_ref[...], oc_ref[...])
    merged = None
    for i, br in enumerate(branches):
        gate = jax.nn.sigmoid(gate_ref[:, i * D_MODEL:(i + 1) * D_MODEL].astype(F32)
                              + bg_ref[:, i * D_MODEL:(i + 1) * D_MODEL])
        term = gate * jnp.dot(br, wb_ref[i], preferred_element_type=F32)
        merged = term if merged is None else merged + term
    mix = jnp.dot(merged.astype(BF16), wo_ref[...], preferred_element_type=F32)
    o_ref[...] = _layer_norm(alpha * x_ref[...] + mix, g_ref[...], b_ref[...])


def _merge(oa, lse, ob, oc, p2, x2d, expand, lw, layer, *, alpha, tm=512):
    t = x2d.shape[0]
    row = lambda w: pl.BlockSpec((tm, w), lambda i: (i, 0))
    full = lambda a: pl.BlockSpec(a.shape, lambda i: (0,) * a.ndim)

    def grouped(arrs):
        views, specs = [], []
        for a, (_, dil) in zip(arrs, A_GROUPS):
            w = a.shape[-1]
            if dil == 1:
                views.append(a.reshape(t, w))
                specs.append(row(w))
                continue
            per_span = BLOCK * dil // tm
            views.append(a.reshape(-1, dil, BLOCK, w))
            specs.append(pl.BlockSpec((None, dil, tm // dil, w),
                                      lambda i, per_span=per_span: (i // per_span, 0, i % per_span, 0)))
        return views, specs

    oa_v, oa_s = grouped(oa)
    lse_v, lse_s = grouped(lse)
    stacked = (lw["wb"], lw["wo"], lw["bg"], lw["ln1_g"], lw["ln1_b"])
    slabs = lambda w: pltpu.VMEM((w // LANE, tm, LANE), F32)
    return pl.pallas_call(
        functools.partial(_merge_kernel, alpha=alpha),
        grid=(t // tm,),
        in_specs=oa_s + lse_s + [row(512), row(512),
                  pl.BlockSpec((tm, 3 * D_MODEL), lambda i: (i, COL_GATE // (3 * D_MODEL))),
                  row(D_MODEL), full(expand)] + [_layer_block(a, layer) for a in stacked],
        out_specs=row(D_MODEL),
        out_shape=jax.ShapeDtypeStruct((t, D_MODEL), F32),
        scratch_shapes=[slabs(512), slabs(512), slabs(LANE), slabs(LANE)],
        compiler_params=_cparams(1),
        name="merge",
    )(*oa_v, *lse_v, ob.reshape(t, 512), oc, p2, x2d, expand, *stacked)


HALO = 8
FF_CHUNK = 512


def _ffn_kernel(halo_ref, *refs, alpha, tm, per_seq):
    n_slab = D_MODEL // LANE
    x_refs = refs[:n_slab]
    wup_ref, cw_ref, cb_ref, wdn_ref, g_ref, b_ref, o_ref, xs_ref, os_ref = refs[n_slab:]
    rows = tm + HALO
    groups = rows // SUBLANES
    i = pl.program_id(0)
    halo = jnp.where(i % per_seq == 0, jnp.zeros_like(halo_ref[...]), halo_ref[...])
    for c, x_ref in enumerate(x_refs):
        xs_ref[c, :HALO, :] = halo[:, c * LANE:(c + 1) * LANE]
        xs_ref[c, HALO:, :] = x_ref[...]
    xp = jnp.concatenate(
        [jnp.concatenate([xs_ref[c, pl.ds(j, SUBLANES, stride=groups), :] for j in range(groups)], axis=0)
         for c in range(n_slab)], axis=1)
    xh = xp.astype(BF16)
    chunks = [(c0, min(FF_CHUNK, D_FF - c0)) for c0 in range(0, D_FF, FF_CHUNK)]

    def up(c0, cf):
        return [jnp.dot(xh, wup_ref[:, base:base + cf], preferred_element_type=F32)
                for base in (c0, D_FF + c0)]

    def conv(u, base, cf):
        u3 = u.reshape(groups, SUBLANES, cf)
        wrap = pltpu.roll(u3[groups - 2:], 1, axis=1)
        prev1 = jnp.concatenate([wrap[1:], u3[:groups - 1]], axis=0)
        prev2 = jnp.concatenate([wrap, u3[:groups - 2]], axis=0)
        tap = lambda k: cw_ref[k:k + 1, base:base + cf]
        return cb_ref[:, base:base + cf] + prev2 * tap(0) + prev1 * tap(1) + u3 * tap(2)

    acc = None
    u_next = up(*chunks[0])
    for ci, (c0, cf) in enumerate(chunks):
        u_cur = u_next
        if ci + 1 < len(chunks):
            u_next = up(*chunks[ci + 1])
        gate, val = (conv(u, base, cf) for u, base in zip(u_cur, (c0, D_FF + c0)))
        act = (gate * jax.nn.sigmoid(gate) * val).reshape(rows, cf).astype(BF16)
        term = jnp.dot(act, wdn_ref[c0:c0 + cf, :], preferred_element_type=F32)
        acc = term if acc is None else acc + term
    out = _layer_norm(alpha * xp + acc, g_ref[...], b_ref[...])
    for c in range(n_slab):
        for j in range(groups):
            os_ref[c, pl.ds(j, SUBLANES, stride=groups), :] = out[j * SUBLANES:(j + 1) * SUBLANES,
                                                                  c * LANE:(c + 1) * LANE]
        o_ref[:, c * LANE:(c + 1) * LANE] = os_ref[c, HALO:, :]


def _ffn(x2d, lw, layer, *, alpha, seq, tm=512):
    t = x2d.shape[0]
    per_seq = seq // tm
    n_slab = D_MODEL // LANE
    consts = (lw["wup"], lw["cw"], lw["cb"], lw["wdn"], lw["ln2_g"], lw["ln2_b"])
    return pl.pallas_call(
        functools.partial(_ffn_kernel, alpha=alpha, tm=tm, per_seq=per_seq),
        grid=(t // tm,),
        in_specs=[pl.BlockSpec((HALO, D_MODEL), lambda i: (jnp.maximum(i * (tm // HALO) - 1, 0), 0))]
                 + [pl.BlockSpec((tm, LANE), lambda i, c=c: (i, c)) for c in range(n_slab)]
                 + [_layer_block(a, layer, pipeline_mode=pl.Buffered(1)) for a in consts],
        out_specs=pl.BlockSpec((tm, D_MODEL), lambda i: (i, 0)),
        out_shape=jax.ShapeDtypeStruct((t, D_MODEL), F32),
        scratch_shapes=[pltpu.VMEM((n_slab, tm + HALO, LANE), F32),
                        pltpu.VMEM((n_slab, tm + HALO, LANE), F32)],
        compiler_params=_cparams(1),
        name="ffn",
    )(x2d, *([x2d] * n_slab), *consts)


def _prepare_weights(w_in, b_gate, q_norm_g, kv_norm_g, w_uq, w_ukv, w_branch, w_out,
                     ln1_g, ln1_b, w_ffn_up, conv_w, conv_b, w_ffn_down, ln2_g, ln2_b):
    depth = w_in.shape[0]
    a_end = 4608
    col = np.arange(w_in.shape[-1])
    is_q = ((col < a_end) & (col % A_COLS < A_HEADS * HEAD_DIM)) | ((col >= a_end) & (col < a_end + 512))
    col_scale = jnp.asarray(np.where(is_q, HEAD_DIM ** -0.5 * LOG2E, 1.0), F32)
    wi = (w_in * col_scale).astype(BF16)
    bq = wi[:, :, a_end:a_end + 512].reshape(depth, D_MODEL, B_Q_HEADS, HEAD_DIM)
    bq = jnp.stack([bq[:, :, :4], bq[:, :, 4:]], axis=3).reshape(depth, D_MODEL, 512)
    bkv = wi[:, :, 5120:5376]
    cdq = wi[:, :, 5376:5632]
    ckv = wi[:, :, 5632:5760]
    kr = wi[:, :, 5760:5792]
    kr_rot = jnp.concatenate([-kr[:, :, C_ROPE // 2:], kr[:, :, :C_ROPE // 2]], axis=2)
    gate = wi[:, :, 5792:]
    zeros = lambda n: jnp.zeros((depth, D_MODEL, n), wi.dtype)
    rest = jnp.concatenate([bq, cdq, ckv, kr, kr_rot, zeros(64), bkv, zeros(256)], axis=2)
    wp = jnp.stack([gate[:, :, :A_COLS], gate[:, :, A_COLS:], wi[:, :, :A_COLS], rest], axis=1)
    wa = []
    for g, (_, dil) in enumerate(A_GROUPS[1:], start=1):
        tn = A_COLS * 1024 // max(1024, BLOCK * dil)
        w = wi[:, :, g * A_COLS:(g + 1) * A_COLS]
        wa.append(w.reshape(depth, D_MODEL, A_COLS // tn, tn).transpose(0, 2, 1, 3))

    uq = (w_uq * ((C_NOPE + C_ROPE) ** -0.5 * LOG2E)).reshape(depth, C_Q_RANK, C_HEADS, C_NOPE + C_ROPE)
    zq = lambda n: jnp.zeros((depth, C_Q_RANK, C_HEADS, n), uq.dtype)
    half = C_ROPE // 2
    wq1 = jnp.concatenate([uq, zq(LANE - C_NOPE - C_ROPE)], axis=3)
    wq2 = jnp.concatenate([zq(C_NOPE), -uq[..., C_NOPE + half:], uq[..., C_NOPE:C_NOPE + half],
                           zq(LANE - C_NOPE - C_ROPE)], axis=3)
    ukv = w_ukv.reshape(depth, C_KV_RANK, C_HEADS, C_NOPE + C_V)
    zkv = lambda n: jnp.zeros((depth, C_KV_RANK, C_HEADS, n), ukv.dtype)
    wk = jnp.concatenate([ukv[..., :C_NOPE], zkv(LANE - C_NOPE)], axis=3)
    wv = jnp.concatenate([ukv[..., C_NOPE:], zkv(LANE - C_V)], axis=3)

    wb1 = w_branch[:, 1].reshape(depth, B_Q_HEADS, HEAD_DIM, D_MODEL)
    wb1 = jnp.stack([wb1[:, :4], wb1[:, 4:]], axis=2).reshape(depth, 512, D_MODEL)
    wb = jnp.stack([w_branch[:, 0], wb1, w_branch[:, 2]], axis=1)

    row = lambda a: a.reshape(depth, 1, -1)
    return dict(
        wp=wp, wa=wa,
        gq=row(q_norm_g), gkv=row(kv_norm_g),
        wq1=wq1.reshape(depth, C_Q_RANK, -1).astype(BF16), wq2=wq2.reshape(depth, C_Q_RANK, -1).astype(BF16),
        wk=wk.reshape(depth, C_KV_RANK, -1).astype(BF16), wv=wv.reshape(depth, C_KV_RANK, -1).astype(BF16),
        wb=wb.astype(BF16), wo=w_out.astype(BF16), bg=row(b_gate),
        ln1_g=row(ln1_g), ln1_b=row(ln1_b),
        wup=w_ffn_up.astype(BF16), cw=conv_w, cb=row(conv_b), wdn=w_ffn_down.astype(BF16),
        ln2_g=row(ln2_g), ln2_b=row(ln2_b),
    )


def _rope_selectors():
    sel = np.zeros((2, 512, LANE), np.float32)
    for j in range(C_ROPE):
        sel[0, C_Q_RANK + C_KV_RANK + j, C_NOPE + j] = 1.0
        sel[1, C_Q_RANK + C_KV_RANK + C_ROPE + j, C_NOPE + j] = 1.0
    return jnp.asarray(sel[0], BF16), jnp.asarray(sel[1], BF16)


def _rope_tables(seq):
    pos = jnp.arange(seq, dtype=F32)
    inv_freq = ROPE_BASE ** (-jnp.arange(0, C_ROPE, 2, dtype=F32) / C_ROPE)
    ang = pos[:, None] * inv_freq[None, :]
    cos, sin = jnp.cos(ang), jnp.sin(ang)
    pad = jnp.zeros((seq, LANE - C_NOPE - C_ROPE), F32)
    cos_t = jnp.concatenate([jnp.ones((seq, C_NOPE), F32), cos, cos, pad], axis=1)
    sin_t = jnp.concatenate([jnp.zeros((seq, C_NOPE), F32), sin, sin, pad], axis=1)
    return cos_t, sin_t


def _expand_matrix():
    e = np.zeros((LANE, A_HEADS * HEAD_DIM), np.float32)
    for c in range(A_HEADS * HEAD_DIM):
        e[(LANE // A_HEADS) * (c // HEAD_DIM), c] = 1.0
    return jnp.asarray(e, BF16)


def kernel(x, rel_table, w_in, b_gate, sinks, q_norm_g, kv_norm_g, w_uq, w_ukv, w_branch, w_out,
           ln1_g, ln1_b, w_ffn_up, conv_w, conv_b, w_ffn_down, ln2_g, ln2_b):
    batch, seq, d = x.shape
    depth = w_in.shape[0]
    alpha = (2 * depth) ** 0.25
    cos_t, sin_t = _rope_tables(seq)
    sel = _rope_selectors()
    expand = _expand_matrix()
    bias = _bias_tables(rel_table)
    lw = _prepare_weights(w_in, b_gate, q_norm_g, kv_norm_g, w_uq, w_ukv, w_branch, w_out,
                          ln1_g, ln1_b, w_ffn_up, conv_w, conv_b, w_ffn_down, ln2_g, ln2_b)
    x2d = x.reshape(batch * seq, d)
    for l in range(depth):
        p2 = _proj(x2d, lw["wp"], l, dil=1, tm=1024, name="proj")
        p_dil = [_proj(x2d, w, l, dil=dil, tm=max(1024, BLOCK * dil), name=f"proj_a{g + 1}")
                 for g, (w, (_, dil)) in enumerate(zip(lw["wa"], A_GROUPS[1:]))]
        sink_slots = sinks[l].reshape(2, 4).T.reshape(-1) * LOG2E
        oa, lse = [], []
        for kind in range(len(A_GROUPS)):
            src = p2 if kind == 0 else p_dil[kind - 1]
            o_g, lse_g = _band_attention(src, bias, sink_slots, kind=kind, batch=batch, seq=seq)
            oa.append(o_g)
            lse.append(lse_g)
        ob = _band_attention(p2, bias, sink_slots, kind=len(A_GROUPS), batch=batch, seq=seq)
        q, k, v = _mla_prep(p2, cos_t, sin_t, sel, lw, l, seq=seq)
        oc = _mla_flash(q, k, v, batch=batch, seq=seq)
        x2d = _merge(oa, lse, ob, oc, p2, x2d, expand, lw, l, alpha=alpha)
        x2d = _ffn(x2d, lw, l, alpha=alpha, seq=seq)
    return x2d.reshape(batch, seq, d)
```

```python
import functools
import math

import jax
import jax.numpy as jnp
import numpy as np
from jax import lax
from jax.experimental import pallas as pl
from jax.experimental.pallas import tpu as pltpu

F32 = jnp.float32
BF16 = jnp.bfloat16

D_MODEL = 1024
HEAD_DIM = 64
BLOCK = 128
A_GROUPS = ((128, 1), (512, 4), (2048, 16))
A_HEADS = 8
B_Q_HEADS = 8
B_WINDOW = 128
C_HEADS = 8
C_Q_RANK = 256
C_KV_RANK = 128
C_NOPE = 64
C_ROPE = 32
C_V = 64
ROPE_BASE = 10000.0
REL_BUCKETS = 32
REL_MAX_DIST = 2048
D_FF = 2816
LN_EPS = 1e-5
RMS_EPS = 1e-6
NEG = -1e30
LOG2E = math.log2(math.e)
FLASH_TQ = 1024
FLASH_TK = 1024
FLASH_HEADS = 8
FLASH_DIAG = 512

COL_GATE = 0
COL_A = 3072
COL_BQ = 4608
COL_C = 5120
COL_BKV = 5632
NP = 6144
A_COLS = 1536

LANE = 128
SUBLANES = 8
VMEM_LIMIT = 56 * 1024 * 1024

N_KINDS = 4


def _cparams(n_axes):
    return pltpu.CompilerParams(dimension_semantics=("arbitrary",) * n_axes,
                                vmem_limit_bytes=VMEM_LIMIT)


def _layer_block(a, layer, **kwargs):
    zeros = (0,) * (a.ndim - 1)
    return pl.BlockSpec((None,) + a.shape[1:], lambda *_: (layer,) + zeros, **kwargs)


CHEAP_STRIDE = 4


def _proj_kernel(*refs, dil):
    n_x = len(refs) - (4 if dil > CHEAP_STRIDE else 3)
    x_refs, (w_ref, o_ref, xb_ref, *tmp) = refs[:n_x], refs[n_x:]

    @pl.when(pl.program_id(1) == 0)
    def _():
        if dil == 1:
            xb_ref[...] = x_refs[0][...].astype(BF16)
            return
        span = BLOCK * dil
        d1 = min(dil, CHEAP_STRIDE)
        d2 = dil // d1
        for c, x_ref in enumerate(x_refs):
            for s0 in range(0, xb_ref.shape[0], span):
                for r1 in range(d1):
                    if d2 == 1:
                        rows = x_ref[pl.ds(s0 + r1, BLOCK, stride=d1), :]
                        xb_ref[s0 + r1 * BLOCK:s0 + (r1 + 1) * BLOCK, c * LANE:(c + 1) * LANE] = rows.astype(BF16)
                        continue
                    g0 = s0 + r1 * (span // d1)
                    tmp[0][c, g0:g0 + span // d1, :] = x_ref[pl.ds(s0 + r1, span // d1, stride=d1), :]
                    for r2 in range(d2):
                        r = r1 + d1 * r2
                        rows = tmp[0][c, pl.ds(g0 + r2, BLOCK, stride=d2), :]
                        xb_ref[s0 + r * BLOCK:s0 + (r + 1) * BLOCK, c * LANE:(c + 1) * LANE] = rows.astype(BF16)

    w = w_ref[pl.program_id(1)]
    o_ref[...] = jnp.dot(xb_ref[...], w, preferred_element_type=F32).astype(o_ref.dtype)


def _proj(x2d, w_tiles, layer, *, dil, tm, name):
    t, k = x2d.shape
    n_tiles, tn = w_tiles.shape[1], w_tiles.shape[3]
    n = n_tiles * tn
    if dil == 1:
        x_specs = [pl.BlockSpec((tm, k), lambda i, j: (i, 0))]
    else:
        x_specs = [pl.BlockSpec((tm, LANE), lambda i, j, c=c: (i, c)) for c in range(k // LANE)]
    scratch = [pltpu.VMEM((tm, k), BF16)]
    if dil > CHEAP_STRIDE:
        scratch.append(pltpu.VMEM((k // LANE, tm, LANE), F32))
    return pl.pallas_call(
        functools.partial(_proj_kernel, dil=dil),
        grid=(t // tm, n_tiles),
        in_specs=x_specs + [_layer_block(w_tiles, layer, pipeline_mode=pl.Buffered(1))],
        out_specs=pl.BlockSpec((tm, tn), lambda i, j: (i, j)),
        out_shape=jax.ShapeDtypeStruct((t, n), BF16),
        scratch_shapes=scratch,
        compiler_params=_cparams(2),
        name=name,
    )(*([x2d] * len(x_specs)), w_tiles)


def _t5_bucket(dist):
    n = jnp.maximum(dist, 0)
    max_exact = REL_BUCKETS // 2
    scaled = jnp.log(jnp.maximum(n, 1).astype(F32) / max_exact) / math.log(REL_MAX_DIST / max_exact)
    large = max_exact + (scaled * (REL_BUCKETS - max_exact)).astype(jnp.int32)
    return jnp.where(n < max_exact, n, jnp.minimum(large, REL_BUCKETS - 1))


def _bias_codes():
    qi = jnp.arange(BLOCK)[:, None]
    ki = jnp.arange(2 * BLOCK)[None, :]
    step = BLOCK + qi - ki
    has_prev = ki >= BLOCK
    codes = []
    for kind in range(N_KINDS):
        if kind < len(A_GROUPS):
            dil = A_GROUPS[kind][1]
            band = (step >= 0) & (step <= BLOCK)
        else:
            dil = 1
            band = (step >= 0) & (step < B_WINDOW)
        bucket = _t5_bucket(step * dil)
        codes.append(jnp.stack([jnp.where(band & has_prev, bucket, -1),
                                jnp.where(band, bucket, -1)]))
    return jnp.stack(codes).astype(jnp.int32)


def _bias_kernel(rel_ref, code_ref, o_ref):
    kind = pl.program_id(0)
    code = code_ref[0, 0]
    for slot in range(A_HEADS):
        b_head = len(A_GROUPS) * A_HEADS + slot // 2 + 4 * (slot % 2)
        col = jnp.where(kind < len(A_GROUPS), kind * A_HEADS + slot, b_head)
        acc = jnp.full(code.shape, NEG, F32)
        for b in range(REL_BUCKETS):
            acc = jnp.where(code == b, rel_ref[b, col] * LOG2E, acc)
        o_ref[0, 0, slot] = acc


def _bias_tables(rel_table):
    codes = _bias_codes()
    return pl.pallas_call(
        _bias_kernel,
        grid=(N_KINDS, 2),
        in_specs=[pl.BlockSpec(memory_space=pltpu.SMEM),
                  pl.BlockSpec((1, 1, BLOCK, 2 * BLOCK), lambda k, v: (k, v, 0, 0))],
        out_specs=pl.BlockSpec((1, 1, A_HEADS, BLOCK, 2 * BLOCK), lambda k, v: (k, v, 0, 0, 0)),
        out_shape=jax.ShapeDtypeStruct((N_KINDS, 2, A_HEADS, BLOCK, 2 * BLOCK), F32),
        compiler_params=_cparams(2),
        name="bias_tables",
    )(rel_table, codes)


BAND_QB = 8
STAT_LANES = LANE // A_HEADS // 2


def _band_kernel(sink_ref, q_ref, kp_ref, kc_ref, vp_ref, vc_ref, bias_ref, o_ref, *lse_refs,
                 shared_kv, with_sink):
    first = pl.program_id(2) == 0
    lane = lax.broadcasted_iota(jnp.int32, (BLOCK, LANE), 1)
    low = lane < HEAD_DIM
    n_qb, n_res = q_ref.shape[:2]
    for res, qb in ((res, qb) for res in range(n_res) for qb in range(n_qb)):
        variant = jnp.where(first, 0, 1) if qb == 0 else 1
        lse_tile = jnp.zeros((BLOCK, LANE), F32)
        for pair in range(4):
            cols = slice(pair * LANE, (pair + 1) * LANE)
            kv_cols = slice(0, LANE) if shared_kv else cols
            qp = q_ref[qb, res, :, cols]
            k_prev = kp_ref[res, :, kv_cols] if qb == 0 else kc_ref[qb - 1, res, :, kv_cols]
            v_prev = vp_ref[res, :, kv_cols] if qb == 0 else vc_ref[qb - 1, res, :, kv_cols]
            kcat = jnp.concatenate([k_prev, kc_ref[qb, res, :, kv_cols]], axis=0)
            vcat = jnp.concatenate([v_prev, vc_ref[qb, res, :, kv_cols]], axis=0)
            zero = jnp.zeros_like(qp)
            q2 = jnp.concatenate([jnp.where(low, qp, zero), jnp.where(low, zero, qp)], axis=0)
            s = lax.dot_general(q2, kcat, (((1,), (1,)), ((), ())), preferred_element_type=F32)
            s = s + bias_ref[variant, pair]
            m = jnp.max(s, axis=-1, keepdims=True)
            if with_sink:
                head_a = lax.broadcasted_iota(jnp.int32, (2 * BLOCK, 1), 0) < BLOCK
                sink = jnp.where(head_a, sink_ref[2 * pair], sink_ref[2 * pair + 1])
                m = jnp.maximum(m, sink)
            p = jnp.exp2(s - m)
            l = jnp.sum(p, axis=-1, keepdims=True)
            if with_sink:
                l = l + jnp.exp2(sink - m)
            o = jnp.dot(p.astype(BF16), vcat, preferred_element_type=F32)
            if lse_refs:
                for sub in range(2):
                    rows = slice(sub * BLOCK, (sub + 1) * BLOCK)
                    field = 2 * (2 * pair + sub)
                    lse_tile = jnp.where(lane // STAT_LANES == field, m[rows],
                                         jnp.where(lane // STAT_LANES == field + 1, l[rows], lse_tile))
            else:
                o = o * (1.0 / l)
            o_ref[qb, res, :, cols] = jnp.where(low, o[:BLOCK], o[BLOCK:]).astype(o_ref.dtype)
        if lse_refs:
            lse_refs[0][qb, res] = lse_tile


def _band_attention(p2, bias, sinks, *, kind, batch, seq):
    is_b = kind == len(A_GROUPS)
    dil = 1 if is_b else A_GROUPS[kind][1]
    n_span = seq // (BLOCK * dil)
    view = p2.reshape(batch, n_span, dil, BLOCK, p2.shape[-1])
    if is_b:
        q_blk, k_blk, v_blk, kv_w = COL_BQ // 512, COL_BKV // LANE, COL_BKV // LANE + 1, LANE
    else:
        q_blk = COL_A // 512 if dil == 1 else 0
        k_blk, v_blk, kv_w = q_blk + 1, q_blk + 2, 512

    n_qb = min(BAND_QB, n_span)
    n_res = BAND_QB // n_qb

    def cur(blk, width):
        return pl.BlockSpec((None, n_qb, n_res, BLOCK, width), lambda b, r, n: (b, n, r, 0, blk))

    def prev(blk, width):
        return pl.BlockSpec((None, None, n_res, BLOCK, width),
                            lambda b, r, n: (b, jnp.maximum(n * n_qb - 1, 0), r, 0, blk))

    in_specs = [
        pl.BlockSpec(memory_space=pltpu.SMEM),
        cur(q_blk, 512), prev(k_blk, kv_w), cur(k_blk, kv_w), prev(v_blk, kv_w), cur(v_blk, kv_w),
        pl.BlockSpec((None, 2, A_HEADS // 2, 2 * BLOCK, 2 * BLOCK), lambda b, r, n: (kind, 0, 0, 0, 0)),
    ]
    bias = bias.reshape(N_KINDS, 2, A_HEADS // 2, 2 * BLOCK, 2 * BLOCK)
    out_specs = [pl.BlockSpec((None, n_qb, n_res, BLOCK, 512), lambda b, r, n: (b, n, r, 0, 0))]
    out_shape = [jax.ShapeDtypeStruct((batch, n_span, dil, BLOCK, 512), BF16)]
    if not is_b:
        out_specs.append(pl.BlockSpec((None, n_qb, n_res, BLOCK, LANE), lambda b, r, n: (b, n, r, 0, 0)))
        out_shape.append(jax.ShapeDtypeStruct((batch, n_span, dil, BLOCK, LANE), F32))
    outs = pl.pallas_call(
        functools.partial(_band_kernel, shared_kv=is_b, with_sink=is_b),
        grid=(batch, dil // n_res, n_span // n_qb),
        in_specs=in_specs,
        out_specs=out_specs,
        out_shape=out_shape,
        compiler_params=_cparams(3),
        name="band_b" if is_b else f"band_a{kind}",
    )(sinks, view, view, view, view, view, bias)
    return outs[0] if is_b else outs


def _mla_prep_kernel(c_ref, cos_ref, sin_ref, gq_ref, gkv_ref, wq1_ref, wq2_ref, wk_ref, wv_ref,
                     sel_ref, q_ref, k_ref, v_ref):
    c = c_ref[...]
    cq = c[:, :C_Q_RANK].astype(F32)
    ckv = c[:, C_Q_RANK:C_Q_RANK + C_KV_RANK].astype(F32)
    nq = cq * lax.rsqrt(jnp.mean(cq * cq, axis=-1, keepdims=True) + RMS_EPS) * gq_ref[...]
    nkv = ckv * lax.rsqrt(jnp.mean(ckv * ckv, axis=-1, keepdims=True) + RMS_EPS) * gkv_ref[...]
    nq = nq.astype(BF16)
    nkv = nkv.astype(BF16)
    cos = cos_ref[...]
    sin = sin_ref[...]
    cos8 = jnp.tile(cos, (1, C_HEADS))
    sin8 = jnp.tile(sin, (1, C_HEADS))
    q = (jnp.dot(nq, wq1_ref[...], preferred_element_type=F32) * cos8
         + jnp.dot(nq, wq2_ref[...], preferred_element_type=F32) * sin8)
    q_ref[...] = q.astype(q_ref.dtype)
    picked = jnp.dot(c, sel_ref[...], preferred_element_type=F32)
    kr = picked[:, :LANE] * cos + picked[:, LANE:] * sin
    k = jnp.dot(nkv, wk_ref[...], preferred_element_type=F32) + jnp.tile(kr, (1, C_HEADS))
    k_ref[...] = k.astype(k_ref.dtype)
    lane = lax.broadcasted_iota(jnp.int32, (1, C_HEADS * LANE), 1)
    ones_col = jnp.where(lane % LANE == C_V, 1.0, 0.0).astype(F32)
    v_ref[...] = (jnp.dot(nkv, wv_ref[...], preferred_element_type=F32) + ones_col).astype(v_ref.dtype)


def _mla_prep(p2, cos_t, sin_t, sel, lw, layer, *, seq, tm=512):
    t = p2.shape[0]
    per_seq = seq // tm
    full = lambda a: pl.BlockSpec(a.shape, lambda i: (0,) * a.ndim)
    stacked = (lw["gq"], lw["gkv"], lw["wq1"], lw["wq2"], lw["wk"], lw["wv"])
    wide = C_HEADS * LANE
    return pl.pallas_call(
        _mla_prep_kernel,
        grid=(t // tm,),
        in_specs=[pl.BlockSpec((tm, 512), lambda i: (i, COL_C // 512)),
                  pl.BlockSpec((tm, LANE), lambda i: (i % per_seq, 0)),
                  pl.BlockSpec((tm, LANE), lambda i: (i % per_seq, 0))]
                 + [_layer_block(a, layer) for a in stacked] + [full(sel)],
        out_specs=[pl.BlockSpec((tm, wide), lambda i: (i, 0))] * 3,
        out_shape=[jax.ShapeDtypeStruct((t, wide), BF16)] * 3,
        compiler_params=_cparams(1),
        name="mla_prep",
    )(p2, cos_t, sin_t, *stacked, sel)


def _mla_flash_kernel(q_ref, k_ref, v_ref, o_ref, m_ref, acc_ref, *, tq, tk, heads):
    qi = pl.program_id(2)
    ki = pl.program_id(3)
    ratio = tq // tk

    @pl.when(ki == 0)
    def _():
        m_ref[...] = jnp.full(m_ref.shape, NEG, F32)
        acc_ref[...] = jnp.zeros(acc_ref.shape, F32)

    def update(h, rows, n_keys, shift):
        hc = slice(h * LANE, (h + 1) * LANE)
        s = lax.dot_general(q_ref[rows, hc], k_ref[:n_keys, hc], (((1,), (1,)), ((), ())),
                            preferred_element_type=F32)
        if shift is not None:
            row = lax.broadcasted_iota(jnp.int32, s.shape, 0) + shift
            col = lax.broadcasted_iota(jnp.int32, s.shape, 1)
            s = jnp.where(row >= col, s, NEG)
        m_prev = m_ref[h, rows]
        m_new = jnp.maximum(m_prev, jnp.max(s, axis=-1, keepdims=True))
        alpha = jnp.exp2(m_prev - m_new)
        p = jnp.exp2(s - jnp.tile(m_new, (1, n_keys // LANE)))
        acc_ref[h, rows] = alpha * acc_ref[h, rows] + jnp.dot(p.astype(BF16), v_ref[:n_keys, hc],
                                                              preferred_element_type=F32)
        m_ref[h, rows] = m_new

    @pl.when(ki < qi * ratio)
    def _():
        for h in range(heads):
            update(h, slice(0, tq), tk, None)

    for d in range(ratio):
        @pl.when(ki == qi * ratio + d)
        def _(d=d):
            for h in range(heads):
                if (d + 1) * tk < tq:
                    update(h, slice((d + 1) * tk, tq), tk, None)
                for r0 in range(0, tk, FLASH_DIAG):
                    update(h, slice(d * tk + r0, d * tk + r0 + FLASH_DIAG), r0 + FLASH_DIAG, r0)

    @pl.when(ki == (qi + 1) * ratio - 1)
    def _():
        low = lax.broadcasted_iota(jnp.int32, (tq, LANE), 1) < C_V
        for pair in range(heads // 2):
            halves = []
            for h in (2 * pair, 2 * pair + 1):
                acc = acc_ref[h]
                halves.append(acc * (1.0 / acc[:, C_V:C_V + 1]))
            odd = pltpu.roll(halves[1], C_V, axis=1)
            o_ref[:, pair * LANE:(pair + 1) * LANE] = jnp.where(low, halves[0], odd).astype(o_ref.dtype)


def _mla_flash(q, k, v, *, batch, seq, tq=FLASH_TQ, tk=FLASH_TK, heads=FLASH_HEADS):
    wide = C_HEADS * LANE
    ratio = tq // tk
    q3 = q.reshape(batch, seq, wide)
    k3 = k.reshape(batch, seq, wide)
    v3 = v.reshape(batch, seq, wide)
    kv_map = lambda b, g, i, j: (b, jnp.minimum(j, (i + 1) * ratio - 1), g)
    out = pl.pallas_call(
        functools.partial(_mla_flash_kernel, tq=tq, tk=tk, heads=heads),
        grid=(batch, C_HEADS // heads, seq // tq, seq // tk),
        in_specs=[pl.BlockSpec((None, tq, heads * LANE), lambda b, g, i, j: (b, i, g)),
                  pl.BlockSpec((None, tk, heads * LANE), kv_map),
                  pl.BlockSpec((None, tk, heads * LANE), kv_map)],
        out_specs=pl.BlockSpec((None, tq, heads * C_V), lambda b, g, i, j: (b, i, g)),
        out_shape=jax.ShapeDtypeStruct((batch, seq, C_HEADS * C_V), BF16),
        scratch_shapes=[pltpu.VMEM((heads, tq, LANE), F32),
                        pltpu.VMEM((heads, tq, LANE), F32)],
        compiler_params=_cparams(4),
        name="mla_flash",
    )(q3, k3, v3)
    return out.reshape(batch * seq, C_HEADS * C_V)


def _layer_norm(y, g, b):
    mu = jnp.mean(y, axis=-1, keepdims=True)
    d = y - mu
    var = jnp.mean(d * d, axis=-1, keepdims=True)
    return d * lax.rsqrt(var + LN_EPS) * g + b


def _token_order(src_ref, tmp_ref, dil):
    if dil == 1:
        return src_ref[...].astype(F32)
    n = src_ref.shape[1]
    n_slab = src_ref.shape[2] // LANE
    for r in range(dil):
        rows = src_ref[r].astype(F32)
        for c in range(n_slab):
            tmp_ref[c, pl.ds(r, n, stride=dil), :] = rows[:, c * LANE:(c + 1) * LANE]
    return jnp.concatenate([tmp_ref[c] for c in range(n_slab)], axis=1)


def _merge_kernel(oa0_ref, oa1_ref, oa2_ref, l0_ref, l1_ref, l2_ref, ob_ref, oc_ref, gate_ref, x_ref,
                  e_ref, wb_ref, wo_ref, bg_ref, g_ref, b_ref, o_ref, ot1_ref, ot2_ref, lt1_ref, lt2_ref,
                  *, alpha):
    dils = [d for _, d in A_GROUPS]
    stats = [_token_order(ref, tmp, d)
             for ref, tmp, d in zip((l0_ref, l1_ref, l2_ref), (None, lt1_ref, lt2_ref), dils)]
    is_max = lax.broadcasted_iota(jnp.int32, stats[0].shape, 1) // STAT_LANES % 2 == 0
    top = jnp.maximum(jnp.maximum(stats[0], stats[1]), stats[2])
    es = [jnp.exp2(v - top) for v in stats]
    sums = [pltpu.roll(v, LANE - STAT_LANES, axis=1) for v in stats]
    inv = 1.0 / (es[0] * sums[0] + es[1] * sums[1] + es[2] * sums[2])
    o_a = None
    for e, oa_ref, tmp, d in zip(es, (oa0_ref, oa1_ref, oa2_ref), (None, ot1_ref, ot2_ref), dils):
        w = jnp.where(is_max, e * inv, 0.0)
        hi = w.astype(BF16)
        lo = (w - hi.astype(F32)).astype(BF16)
        wide = jnp.dot(jnp.concatenate([hi, lo], axis=1), e_ref[...], preferred_element_type=F32)
        term = wide * _token_order(oa_ref, tmp, d)
        o_a = term if o_a is None else o_a + term
    branches = (o_a.astype(BF16), ob_ref[...], oc_ref[...])
    merged = None
    for i, br in enumerate(branches):
        gate = jax.nn.sigmoid(gate_ref[:, i * D_MODEL:(i + 1) * D_MODEL].astype(F32)
                              + bg_ref[:, i * D_MODEL:(i + 1) * D_MODEL])
        term = gate * jnp.dot(br, wb_ref[i], preferred_element_type=F32)
        merged = term if merged is None else merged + term
    mix = jnp.dot(merged.astype(BF16), wo_ref[...], preferred_element_type=F32)
    o_ref[...] = _layer_norm(alpha * x_ref[...] + mix, g_ref[...], b_ref[...])


def _merge(oa, lse, ob, oc, p2, x2d, expand, lw, layer, *, alpha, tm=512):
    t = x2d.shape[0]
    row = lambda w: pl.BlockSpec((tm, w), lambda i: (i, 0))
    full = lambda a: pl.BlockSpec(a.shape, lambda i: (0,) * a.ndim)

    def grouped(arrs):
        views, specs = [], []
        for a, (_, dil) in zip(arrs, A_GROUPS):
            w = a.shape[-1]
            if dil == 1:
                views.append(a.reshape(t, w))
                specs.append(row(w))
                continue
            per_span = BLOCK * dil // tm
            views.append(a.reshape(-1, dil, BLOCK, w))
            specs.append(pl.BlockSpec((None, dil, tm // dil, w),
                                      lambda i, per_span=per_span: (i // per_span, 0, i % per_span, 0)))
        return views, specs

    oa_v, oa_s = grouped(oa)
    lse_v, lse_s = grouped(lse)
    stacked = (lw["wb"], lw["wo"], lw["bg"], lw["ln1_g"], lw["ln1_b"])
    slabs = lambda w: pltpu.VMEM((w // LANE, tm, LANE), F32)
    return pl.pallas_call(
        functools.partial(_merge_kernel, alpha=alpha),
        grid=(t // tm,),
        in_specs=oa_s + lse_s + [row(512), row(512),
                  pl.BlockSpec((tm, 3 * D_MODEL), lambda i: (i, COL_GATE // (3 * D_MODEL))),
                  row(D_MODEL), full(expand)] + [_layer_block(a, layer) for a in stacked],
        out_specs=row(D_MODEL),
        out_shape=jax.ShapeDtypeStruct((t, D_MODEL), F32),
        scratch_shapes=[slabs(512), slabs(512), slabs(LANE), slabs(LANE)],
        compiler_params=_cparams(1),
        name="merge",
    )(*oa_v, *lse_v, ob.reshape(t, 512), oc, p2, x2d, expand, *stacked)


HALO = 8
FF_CHUNK = 1024


def _ffn_kernel(halo_ref, *refs, alpha, tm, per_seq):
    n_slab = D_MODEL // LANE
    x_refs = refs[:n_slab]
    wup_ref, cw_ref, cb_ref, wdn_ref, g_ref, b_ref, o_ref, xs_ref, os_ref = refs[n_slab:]
    rows = tm + HALO
    groups = rows // SUBLANES
    i = pl.program_id(0)
    halo = jnp.where(i % per_seq == 0, jnp.zeros_like(halo_ref[...]), halo_ref[...])
    for c, x_ref in enumerate(x_refs):
        xs_ref[c, :HALO, :] = halo[:, c * LANE:(c + 1) * LANE]
        xs_ref[c, HALO:, :] = x_ref[...]
    xp = jnp.concatenate(
        [jnp.concatenate([xs_ref[c, pl.ds(j, SUBLANES, stride=groups), :] for j in range(groups)], axis=0)
         for c in range(n_slab)], axis=1)
    xh = xp.astype(BF16)
    chunks = [(c0, min(FF_CHUNK, D_FF - c0)) for c0 in range(0, D_FF, FF_CHUNK)]

    def up(c0, cf):
        return [jnp.dot(xh, wup_ref[:, base:base + cf], preferred_element_type=F32)
                for base in (c0, D_FF + c0)]

    def conv(u, base, cf):
        u3 = u.reshape(groups, SUBLANES, cf)
        wrap = pltpu.roll(u3[groups - 2:], 1, axis=1)
        prev1 = jnp.concatenate([wrap[1:], u3[:groups - 1]], axis=0)
        prev2 = jnp.concatenate([wrap, u3[:groups - 2]], axis=0)
        tap = lambda k: cw_ref[k:k + 1, base:base + cf]
        return cb_ref[:, base:base + cf] + prev2 * tap(0) + prev1 * tap(1) + u3 * tap(2)

    acc = None
    u_next = up(*chunks[0])
    for ci, (c0, cf) in enumerate(chunks):
        u_cur = u_next
        if ci + 1 < len(chunks):
            u_next = up(*chunks[ci + 1])
        gate, val = (conv(u, base, cf) for u, base in zip(u_cur, (c0, D_FF + c0)))
        act = (gate * jax.nn.sigmoid(gate) * val).reshape(rows, cf).astype(BF16)
        term = jnp.dot(act, wdn_ref[c0:c0 + cf, :], preferred_element_type=F32)
        acc = term if acc is None else acc + term
    out = _layer_norm(alpha * xp + acc, g_ref[...], b_ref[...])
    for c in range(n_slab):
        for j in range(groups):
            os_ref[c, pl.ds(j, SUBLANES, stride=groups), :] = out[j * SUBLANES:(j + 1) * SUBLANES,
                                                                  c * LANE:(c + 1) * LANE]
        o_ref[:, c * LANE:(c + 1) * LANE] = os_ref[c, HALO:, :]


def _ffn(x2d, lw, layer, *, alpha, seq, tm=512):
    t = x2d.shape[0]
    per_seq = seq // tm
    n_slab = D_MODEL // LANE
    consts = (lw["wup"], lw["cw"], lw["cb"], lw["wdn"], lw["ln2_g"], lw["ln2_b"])
    return pl.pallas_call(
        functools.partial(_ffn_kernel, alpha=alpha, tm=tm, per_seq=per_seq),
        grid=(t // tm,),
        in_specs=[pl.BlockSpec((HALO, D_MODEL), lambda i: (jnp.maximum(i * (tm // HALO) - 1, 0), 0))]
                 + [pl.BlockSpec((tm, LANE), lambda i, c=c: (i, c)) for c in range(n_slab)]
                 + [_layer_block(a, layer, pipeline_mode=pl.Buffered(1)) for a in consts],
        out_specs=pl.BlockSpec((tm, D_MODEL), lambda i: (i, 0)),
        out_shape=jax.ShapeDtypeStruct((t, D_MODEL), F32),
        scratch_shapes=[pltpu.VMEM((n_slab, tm + HALO, LANE), F32),
                        pltpu.VMEM((n_slab, tm + HALO, LANE), F32)],
        compiler_params=_cparams(1),
        name="ffn",
    )(x2d, *([x2d] * n_slab), *consts)


def _prepare_weights(w_in, b_gate, q_norm_g, kv_norm_g, w_uq, w_ukv, w_branch, w_out,
                     ln1_g, ln1_b, w_ffn_up, conv_w, conv_b, w_ffn_down, ln2_g, ln2_b):
    depth = w_in.shape[0]
    a_end = 4608
    col = np.arange(w_in.shape[-1])
    is_q = ((col < a_end) & (col % A_COLS < A_HEADS * HEAD_DIM)) | ((col >= a_end) & (col < a_end + 512))
    col_scale = jnp.asarray(np.where(is_q, HEAD_DIM ** -0.5 * LOG2E, 1.0), F32)
    wi = (w_in * col_scale).astype(BF16)
    bq = wi[:, :, a_end:a_end + 512].reshape(depth, D_MODEL, B_Q_HEADS, HEAD_DIM)
    bq = jnp.stack([bq[:, :, :4], bq[:, :, 4:]], axis=3).reshape(depth, D_MODEL, 512)
    bkv = wi[:, :, 5120:5376]
    cdq = wi[:, :, 5376:5632]
    ckv = wi[:, :, 5632:5760]
    kr = wi[:, :, 5760:5792]
    kr_rot = jnp.concatenate([-kr[:, :, C_ROPE // 2:], kr[:, :, :C_ROPE // 2]], axis=2)
    gate = wi[:, :, 5792:]
    zeros = lambda n: jnp.zeros((depth, D_MODEL, n), wi.dtype)
    rest = jnp.concatenate([bq, cdq, ckv, kr, kr_rot, zeros(64), bkv, zeros(256)], axis=2)
    wp = jnp.stack([gate[:, :, :A_COLS], gate[:, :, A_COLS:], wi[:, :, :A_COLS], rest], axis=1)
    wa = []
    for g, (_, dil) in enumerate(A_GROUPS[1:], start=1):
        tn = A_COLS * 1024 // max(1024, BLOCK * dil)
        w = wi[:, :, g * A_COLS:(g + 1) * A_COLS]
        wa.append(w.reshape(depth, D_MODEL, A_COLS // tn, tn).transpose(0, 2, 1, 3))

    uq = (w_uq * ((C_NOPE + C_ROPE) ** -0.5 * LOG2E)).reshape(depth, C_Q_RANK, C_HEADS, C_NOPE + C_ROPE)
    zq = lambda n: jnp.zeros((depth, C_Q_RANK, C_HEADS, n), uq.dtype)
    half = C_ROPE // 2
    wq1 = jnp.concatenate([uq, zq(LANE - C_NOPE - C_ROPE)], axis=3)
    wq2 = jnp.concatenate([zq(C_NOPE), -uq[..., C_NOPE + half:], uq[..., C_NOPE:C_NOPE + half],
                           zq(LANE - C_NOPE - C_ROPE)], axis=3)
    ukv = w_ukv.reshape(depth, C_KV_RANK, C_HEADS, C_NOPE + C_V)
    zkv = lambda n: jnp.zeros((depth, C_KV_RANK, C_HEADS, n), ukv.dtype)
    wk = jnp.concatenate([ukv[..., :C_NOPE], zkv(LANE - C_NOPE)], axis=3)
    wv = jnp.concatenate([ukv[..., C_NOPE:], zkv(LANE - C_V)], axis=3)

    wb1 = w_branch[:, 1].reshape(depth, B_Q_HEADS, HEAD_DIM, D_MODEL)
    wb1 = jnp.stack([wb1[:, :4], wb1[:, 4:]], axis=2).reshape(depth, 512, D_MODEL)
    wb = jnp.stack([w_branch[:, 0], wb1, w_branch[:, 2]], axis=1)

    row = lambda a: a.reshape(depth, 1, -1)
    return dict(
        wp=wp, wa=wa,
        gq=row(q_norm_g), gkv=row(kv_norm_g),
        wq1=wq1.reshape(depth, C_Q_RANK, -1).astype(BF16), wq2=wq2.reshape(depth, C_Q_RANK, -1).astype(BF16),
        wk=wk.reshape(depth, C_KV_RANK, -1).astype(BF16), wv=wv.reshape(depth, C_KV_RANK, -1).astype(BF16),
        wb=wb.astype(BF16), wo=w_out.astype(BF16), bg=row(b_gate),
        ln1_g=row(ln1_g), ln1_b=row(ln1_b),
        wup=w_ffn_up.astype(BF16), cw=conv_w, cb=row(conv_b), wdn=w_ffn_down.astype(BF16),
        ln2_g=row(ln2_g), ln2_b=row(ln2_b),
    )


def _rope_selectors():
    sel = np.zeros((512, 2 * LANE), np.float32)
    for j in range(C_ROPE):
        sel[C_Q_RANK + C_KV_RANK + j, C_NOPE + j] = 1.0
        sel[C_Q_RANK + C_KV_RANK + C_ROPE + j, LANE + C_NOPE + j] = 1.0
    return jnp.asarray(sel, BF16)


def _rope_tables(seq):
    pos = jnp.arange(seq, dtype=F32)
    inv_freq = ROPE_BASE ** (-jnp.arange(0, C_ROPE, 2, dtype=F32) / C_ROPE)
    ang = pos[:, None] * inv_freq[None, :]
    cos, sin = jnp.cos(ang), jnp.sin(ang)
    pad = jnp.zeros((seq, LANE - C_NOPE - C_ROPE), F32)
    cos_t = jnp.concatenate([jnp.ones((seq, C_NOPE), F32), cos, cos, pad], axis=1)
    sin_t = jnp.concatenate([jnp.zeros((seq, C_NOPE), F32), sin, sin, pad], axis=1)
    return cos_t, sin_t


def _expand_matrix():
    e = np.zeros((LANE, A_HEADS * HEAD_DIM), np.float32)
    for c in range(A_HEADS * HEAD_DIM):
        e[(LANE // A_HEADS) * (c // HEAD_DIM), c] = 1.0
    return jnp.asarray(np.concatenate([e, e], axis=0), BF16)


def kernel(x, rel_table, w_in, b_gate, sinks, q_norm_g, kv_norm_g, w_uq, w_ukv, w_branch, w_out,
           ln1_g, ln1_b, w_ffn_up, conv_w, conv_b, w_ffn_down, ln2_g, ln2_b):
    batch, seq, d = x.shape
    depth = w_in.shape[0]
    alpha = (2 * depth) ** 0.25
    cos_t, sin_t = _rope_tables(seq)
    sel = _rope_selectors()
    expand = _expand_matrix()
    bias = _bias_tables(rel_table)
    lw = _prepare_weights(w_in, b_gate, q_norm_g, kv_norm_g, w_uq, w_ukv, w_branch, w_out,
                          ln1_g, ln1_b, w_ffn_up, conv_w, conv_b, w_ffn_down, ln2_g, ln2_b)
    x2d = x.reshape(batch * seq, d)
    for l in range(depth):
        p2 = _proj(x2d, lw["wp"], l, dil=1, tm=1024, name="proj")
        p_dil = [_proj(x2d, w, l, dil=dil, tm=max(1024, BLOCK * dil), name=f"proj_a{g + 1}")
                 for g, (w, (_, dil)) in enumerate(zip(lw["wa"], A_GROUPS[1:]))]
        sink_slots = sinks[l].reshape(2, 4).T.reshape(-1) * LOG2E
        oa, lse = [], []
        for kind in range(len(A_GROUPS)):
            src = p2 if kind == 0 else p_dil[kind - 1]
            o_g, lse_g = _band_attention(src, bias, sink_slots, kind=kind, batch=batch, seq=seq)
            oa.append(o_g)
            lse.append(lse_g)
        ob = _band_attention(p2, bias, sink_slots, kind=len(A_GROUPS), batch=batch, seq=seq)
        q, k, v = _mla_prep(p2, cos_t, sin_t, sel, lw, l, seq=seq)
        oc = _mla_flash(q, k, v, batch=batch, seq=seq)
        x2d = _merge(oa, lse, ob, oc, p2, x2d, expand, lw, l, alpha=alpha)
        x2d = _ffn(x2d, lw, l, alpha=alpha, seq=seq)
    return x2d.reshape(batch, seq, d)
```

```python
import functools
import math

import jax
import jax.numpy as jnp
import numpy as np
from jax import lax
from jax.experimental import pallas as pl
from jax.experimental.pallas import tpu as pltpu

F32 = jnp.float32
BF16 = jnp.bfloat16

D_MODEL = 1024
HEAD_DIM = 64
BLOCK = 128
A_GROUPS = ((128, 1), (512, 4), (2048, 16))
A_HEADS = 8
B_Q_HEADS = 8
B_WINDOW = 128
C_HEADS = 8
C_Q_RANK = 256
C_KV_RANK = 128
C_NOPE = 64
C_ROPE = 32
C_V = 64
ROPE_BASE = 10000.0
REL_BUCKETS = 32
REL_MAX_DIST = 2048
D_FF = 2816
LN_EPS = 1e-5
RMS_EPS = 1e-6
NEG = -1e30
LOG2E = math.log2(math.e)
FLASH_TQ = 1024
FLASH_TK = 1024
FLASH_HEADS = 8
FLASH_DIAG = 512

COL_GATE = 0
COL_A = 3072
COL_BQ = 4608
COL_C = 5120
COL_BKV = 5632
NP = 6144
A_COLS = 1536

LANE = 128
SUBLANES = 8
VMEM_LIMIT = 56 * 1024 * 1024

N_KINDS = 4


def _cparams(n_axes):
    return pltpu.CompilerParams(dimension_semantics=("arbitrary",) * n_axes,
                                vmem_limit_bytes=VMEM_LIMIT)


def _layer_block(a, layer, **kwargs):
    zeros = (0,) * (a.ndim - 1)
    return pl.BlockSpec((None,) + a.shape[1:], lambda *_: (layer,) + zeros, **kwargs)


CHEAP_STRIDE = 4


def _proj_kernel(*refs, dil):
    n_x = len(refs) - (4 if dil > CHEAP_STRIDE else 3)
    x_refs, (w_ref, o_ref, xb_ref, *tmp) = refs[:n_x], refs[n_x:]

    @pl.when(pl.program_id(1) == 0)
    def _():
        if dil == 1:
            xb_ref[...] = x_refs[0][...].astype(BF16)
            return
        span = BLOCK * dil
        d1 = min(dil, CHEAP_STRIDE)
        d2 = dil // d1
        for c, x_ref in enumerate(x_refs):
            for s0 in range(0, xb_ref.shape[0], span):
                for r1 in range(d1):
                    if d2 == 1:
                        rows = x_ref[pl.ds(s0 + r1, BLOCK, stride=d1), :]
                        xb_ref[s0 + r1 * BLOCK:s0 + (r1 + 1) * BLOCK, c * LANE:(c + 1) * LANE] = rows.astype(BF16)
                        continue
                    g0 = s0 + r1 * (span // d1)
                    tmp[0][c, g0:g0 + span // d1, :] = x_ref[pl.ds(s0 + r1, span // d1, stride=d1), :]
                    for r2 in range(d2):
                        r = r1 + d1 * r2
                        rows = tmp[0][c, pl.ds(g0 + r2, BLOCK, stride=d2), :]
                        xb_ref[s0 + r * BLOCK:s0 + (r + 1) * BLOCK, c * LANE:(c + 1) * LANE] = rows.astype(BF16)

    w = w_ref[pl.program_id(1)]
    o_ref[...] = jnp.dot(xb_ref[...], w, preferred_element_type=F32).astype(o_ref.dtype)


def _proj(x2d, w_tiles, layer, *, dil, tm, name):
    t, k = x2d.shape
    n_tiles, tn = w_tiles.shape[1], w_tiles.shape[3]
    n = n_tiles * tn
    if dil == 1:
        x_specs = [pl.BlockSpec((tm, k), lambda i, j: (i, 0))]
    else:
        x_specs = [pl.BlockSpec((tm, LANE), lambda i, j, c=c: (i, c)) for c in range(k // LANE)]
    scratch = [pltpu.VMEM((tm, k), BF16)]
    if dil > CHEAP_STRIDE:
        scratch.append(pltpu.VMEM((k // LANE, tm, LANE), F32))
    return pl.pallas_call(
        functools.partial(_proj_kernel, dil=dil),
        grid=(t // tm, n_tiles),
        in_specs=x_specs + [_layer_block(w_tiles, layer, pipeline_mode=pl.Buffered(1))],
        out_specs=pl.BlockSpec((tm, tn), lambda i, j: (i, j)),
        out_shape=jax.ShapeDtypeStruct((t, n), BF16),
        scratch_shapes=scratch,
        compiler_params=_cparams(2),
        name=name,
    )(*([x2d] * len(x_specs)), w_tiles)


def _t5_bucket(dist):
    n = jnp.maximum(dist, 0)
    max_exact = REL_BUCKETS // 2
    scaled = jnp.log(jnp.maximum(n, 1).astype(F32) / max_exact) / math.log(REL_MAX_DIST / max_exact)
    large = max_exact + (scaled * (REL_BUCKETS - max_exact)).astype(jnp.int32)
    return jnp.where(n < max_exact, n, jnp.minimum(large, REL_BUCKETS - 1))


def _bias_codes():
    qi = jnp.arange(BLOCK)[:, None]
    ki = jnp.arange(2 * BLOCK)[None, :]
    step = BLOCK + qi - ki
    has_prev = ki >= BLOCK
    codes = []
    for kind in range(N_KINDS):
        if kind < len(A_GROUPS):
            dil = A_GROUPS[kind][1]
            band = (step >= 0) & (step <= BLOCK)
        else:
            dil = 1
            band = (step >= 0) & (step < B_WINDOW)
        bucket = _t5_bucket(step * dil)
        codes.append(jnp.stack([jnp.where(band & has_prev, bucket, -1),
                                jnp.where(band, bucket, -1)]))
    return jnp.stack(codes).astype(jnp.int32)


def _bias_kernel(rel_ref, code_ref, o_ref):
    kind = pl.program_id(0)
    code = code_ref[0, 0]
    for slot in range(A_HEADS):
        b_head = len(A_GROUPS) * A_HEADS + slot // 2 + 4 * (slot % 2)
        col = jnp.where(kind < len(A_GROUPS), kind * A_HEADS + slot, b_head)
        acc = jnp.full(code.shape, NEG, F32)
        for b in range(REL_BUCKETS):
            acc = jnp.where(code == b, rel_ref[b, col] * LOG2E, acc)
        o_ref[0, 0, slot] = acc


def _bias_tables(rel_table):
    codes = _bias_codes()
    return pl.pallas_call(
        _bias_kernel,
        grid=(N_KINDS, 2),
        in_specs=[pl.BlockSpec(memory_space=pltpu.SMEM),
                  pl.BlockSpec((1, 1, BLOCK, 2 * BLOCK), lambda k, v: (k, v, 0, 0))],
        out_specs=pl.BlockSpec((1, 1, A_HEADS, BLOCK, 2 * BLOCK), lambda k, v: (k, v, 0, 0, 0)),
        out_shape=jax.ShapeDtypeStruct((N_KINDS, 2, A_HEADS, BLOCK, 2 * BLOCK), F32),
        compiler_params=_cparams(2),
        name="bias_tables",
    )(rel_table, codes)


BAND_QB = 16
STAT_LANES = LANE // A_HEADS // 2


def _band_kernel(sink_ref, q_ref, kp_ref, kc_ref, vp_ref, vc_ref, bias_ref, o_ref, *lse_refs,
                 shared_kv, with_sink):
    first = pl.program_id(2) == 0
    lane = lax.broadcasted_iota(jnp.int32, (BLOCK, LANE), 1)
    low = lane < HEAD_DIM
    n_qb, n_res = q_ref.shape[:2]
    for res, qb in ((res, qb) for res in range(n_res) for qb in range(n_qb)):
        variant = jnp.where(first, 0, 1) if qb == 0 else 1
        lse_tile = jnp.zeros((BLOCK, LANE), F32)
        for pair in range(4):
            cols = slice(pair * LANE, (pair + 1) * LANE)
            kv_cols = slice(0, LANE) if shared_kv else cols
            qp = q_ref[qb, res, :, cols]
            k_prev = kp_ref[res, :, kv_cols] if qb == 0 else kc_ref[qb - 1, res, :, kv_cols]
            v_prev = vp_ref[res, :, kv_cols] if qb == 0 else vc_ref[qb - 1, res, :, kv_cols]
            kcat = jnp.concatenate([k_prev, kc_ref[qb, res, :, kv_cols]], axis=0)
            vcat = jnp.concatenate([v_prev, vc_ref[qb, res, :, kv_cols]], axis=0)
            zero = jnp.zeros_like(qp)
            q2 = jnp.concatenate([jnp.where(low, qp, zero), jnp.where(low, zero, qp)], axis=0)
            s = lax.dot_general(q2, kcat, (((1,), (1,)), ((), ())), preferred_element_type=F32)
            s = s + bias_ref[variant, pair]
            m = jnp.max(s, axis=-1, keepdims=True)
            if with_sink:
                head_a = lax.broadcasted_iota(jnp.int32, (2 * BLOCK, 1), 0) < BLOCK
                sink = jnp.where(head_a, sink_ref[2 * pair], sink_ref[2 * pair + 1])
                m = jnp.maximum(m, sink)
            p = jnp.exp2(s - m)
            l = jnp.sum(p, axis=-1, keepdims=True)
            if with_sink:
                l = l + jnp.exp2(sink - m)
            o = jnp.dot(p.astype(BF16), vcat, preferred_element_type=F32)
            if lse_refs:
                for sub in range(2):
                    rows = slice(sub * BLOCK, (sub + 1) * BLOCK)
                    field = 2 * (2 * pair + sub)
                    lse_tile = jnp.where(lane // STAT_LANES == field, m[rows],
                                         jnp.where(lane // STAT_LANES == field + 1, l[rows], lse_tile))
            else:
                o = o * (1.0 / l)
            o_ref[qb, res, :, cols] = jnp.where(low, o[:BLOCK], o[BLOCK:]).astype(o_ref.dtype)
        if lse_refs:
            lse_refs[0][qb, res] = lse_tile


def _band_attention(p2, bias, sinks, *, kind, batch, seq):
    is_b = kind == len(A_GROUPS)
    dil = 1 if is_b else A_GROUPS[kind][1]
    n_span = seq // (BLOCK * dil)
    view = p2.reshape(batch, n_span, dil, BLOCK, p2.shape[-1])
    if is_b:
        q_blk, k_blk, v_blk, kv_w = COL_BQ // 512, COL_BKV // LANE, COL_BKV // LANE + 1, LANE
    else:
        q_blk = COL_A // 512 if dil == 1 else 0
        k_blk, v_blk, kv_w = q_blk + 1, q_blk + 2, 512

    n_qb = min(BAND_QB, n_span)
    n_res = BAND_QB // n_qb

    def cur(blk, width):
        return pl.BlockSpec((None, n_qb, n_res, BLOCK, width), lambda b, r, n: (b, n, r, 0, blk))

    def prev(blk, width):
        return pl.BlockSpec((None, None, n_res, BLOCK, width),
                            lambda b, r, n: (b, jnp.maximum(n * n_qb - 1, 0), r, 0, blk))

    in_specs = [
        pl.BlockSpec(memory_space=pltpu.SMEM),
        cur(q_blk, 512), prev(k_blk, kv_w), cur(k_blk, kv_w), prev(v_blk, kv_w), cur(v_blk, kv_w),
        pl.BlockSpec((None, 2, A_HEADS // 2, 2 * BLOCK, 2 * BLOCK), lambda b, r, n: (kind, 0, 0, 0, 0)),
    ]
    bias = bias.reshape(N_KINDS, 2, A_HEADS // 2, 2 * BLOCK, 2 * BLOCK)
    out_specs = [pl.BlockSpec((None, n_qb, n_res, BLOCK, 512), lambda b, r, n: (b, n, r, 0, 0))]
    out_shape = [jax.ShapeDtypeStruct((batch, n_span, dil, BLOCK, 512), BF16)]
    if not is_b:
        out_specs.append(pl.BlockSpec((None, n_qb, n_res, BLOCK, LANE), lambda b, r, n: (b, n, r, 0, 0)))
        out_shape.append(jax.ShapeDtypeStruct((batch, n_span, dil, BLOCK, LANE), F32))
    outs = pl.pallas_call(
        functools.partial(_band_kernel, shared_kv=is_b, with_sink=is_b),
        grid=(batch, dil // n_res, n_span // n_qb),
        in_specs=in_specs,
        out_specs=out_specs,
        out_shape=out_shape,
        compiler_params=_cparams(3),
        name="band_b" if is_b else f"band_a{kind}",
    )(sinks, view, view, view, view, view, bias)
    return outs[0] if is_b else outs


def _mla_prep_kernel(c_ref, cos_ref, sin_ref, gq_ref, gkv_ref, wq1_ref, wq2_ref, wk_ref, wv_ref,
                     sel_ref, q_ref, k_ref, v_ref):
    c = c_ref[...]
    cq = c[:, :C_Q_RANK].astype(F32)
    ckv = c[:, C_Q_RANK:C_Q_RANK + C_KV_RANK].astype(F32)
    nq = cq * lax.rsqrt(jnp.mean(cq * cq, axis=-1, keepdims=True) + RMS_EPS) * gq_ref[...]
    nkv = ckv * lax.rsqrt(jnp.mean(ckv * ckv, axis=-1, keepdims=True) + RMS_EPS) * gkv_ref[...]
    nq = nq.astype(BF16)
    nkv = nkv.astype(BF16)
    cos = cos_ref[...]
    sin = sin_ref[...]
    cos8 = jnp.tile(cos, (1, C_HEADS))
    sin8 = jnp.tile(sin, (1, C_HEADS))
    q = (jnp.dot(nq, wq1_ref[...], preferred_element_type=F32) * cos8
         + jnp.dot(nq, wq2_ref[...], preferred_element_type=F32) * sin8)
    q_ref[...] = q.astype(q_ref.dtype)
    picked = jnp.dot(c, sel_ref[...], preferred_element_type=F32)
    kr = picked[:, :LANE] * cos + picked[:, LANE:] * sin
    k = jnp.dot(nkv, wk_ref[...], preferred_element_type=F32) + jnp.tile(kr, (1, C_HEADS))
    k_ref[...] = k.astype(k_ref.dtype)
    lane = lax.broadcasted_iota(jnp.int32, (1, C_HEADS * LANE), 1)
    ones_col = jnp.where(lane % LANE == C_V, 1.0, 0.0).astype(F32)
    v_ref[...] = (jnp.dot(nkv, wv_ref[...], preferred_element_type=F32) + ones_col).astype(v_ref.dtype)


def _mla_prep(p2, cos_t, sin_t, sel, lw, layer, *, seq, tm=1024):
    t = p2.shape[0]
    per_seq = seq // tm
    full = lambda a: pl.BlockSpec(a.shape, lambda i: (0,) * a.ndim)
    stacked = (lw["gq"], lw["gkv"], lw["wq1"], lw["wq2"], lw["wk"], lw["wv"])
    wide = C_HEADS * LANE
    return pl.pallas_call(
        _mla_prep_kernel,
        grid=(t // tm,),
        in_specs=[pl.BlockSpec((tm, 512), lambda i: (i, COL_C // 512)),
                  pl.BlockSpec((tm, LANE), lambda i: (i % per_seq, 0)),
                  pl.BlockSpec((tm, LANE), lambda i: (i % per_seq, 0))]
                 + [_layer_block(a, layer) for a in stacked] + [full(sel)],
        out_specs=[pl.BlockSpec((tm, wide), lambda i: (i, 0))] * 3,
        out_shape=[jax.ShapeDtypeStruct((t, wide), BF16)] * 3,
        compiler_params=_cparams(1),
        name="mla_prep",
    )(p2, cos_t, sin_t, *stacked, sel)


def _mla_flash_kernel(q_ref, k_ref, v_ref, o_ref, m_ref, acc_ref, *, tq, tk, heads):
    qi = pl.program_id(2)
    ki = pl.program_id(3)
    ratio = tq // tk

    @pl.when(ki == 0)
    def _():
        m_ref[...] = jnp.full(m_ref.shape, NEG, F32)
        acc_ref[...] = jnp.zeros(acc_ref.shape, F32)

    def update(h, rows, n_keys, shift):
        hc = slice(h * LANE, (h + 1) * LANE)
        s = lax.dot_general(q_ref[rows, hc], k_ref[:n_keys, hc], (((1,), (1,)), ((), ())),
                            preferred_element_type=F32)
        if shift is not None:
            row = lax.broadcasted_iota(jnp.int32, s.shape, 0) + shift
            col = lax.broadcasted_iota(jnp.int32, s.shape, 1)
            s = jnp.where(row >= col, s, NEG)
        m_prev = m_ref[h, rows]
        m_new = jnp.maximum(m_prev, jnp.max(s, axis=-1, keepdims=True))
        alpha = jnp.exp2(m_prev - m_new)
        p = jnp.exp2(s - jnp.tile(m_new, (1, n_keys // LANE)))
        acc_ref[h, rows] = alpha * acc_ref[h, rows] + jnp.dot(p.astype(BF16), v_ref[:n_keys, hc],
                                                              preferred_element_type=F32)
        m_ref[h, rows] = m_new

    @pl.when(ki < qi * ratio)
    def _():
        for h in range(heads):
            update(h, slice(0, tq), tk, None)

    for d in range(ratio):
        @pl.when(ki == qi * ratio + d)
        def _(d=d):
            for h in range(heads):
                if (d + 1) * tk < tq:
                    update(h, slice((d + 1) * tk, tq), tk, None)
                for r0 in range(0, tk, FLASH_DIAG):
                    update(h, slice(d * tk + r0, d * tk + r0 + FLASH_DIAG), r0 + FLASH_DIAG, r0)

    @pl.when(ki == (qi + 1) * ratio - 1)
    def _():
        low = lax.broadcasted_iota(jnp.int32, (tq, LANE), 1) < C_V
        for pair in range(heads // 2):
            halves = []
            for h in (2 * pair, 2 * pair + 1):
                acc = acc_ref[h]
                halves.append(acc * (1.0 / acc[:, C_V:C_V + 1]))
            odd = pltpu.roll(halves[1], C_V, axis=1)
            o_ref[:, pair * LANE:(pair + 1) * LANE] = jnp.where(low, halves[0], odd).astype(o_ref.dtype)


def _mla_flash(q, k, v, *, batch, seq, tq=FLASH_TQ, tk=FLASH_TK, heads=FLASH_HEADS):
    wide = C_HEADS * LANE
    ratio = tq // tk
    q3 = q.reshape(batch, seq, wide)
    k3 = k.reshape(batch, seq, wide)
    v3 = v.reshape(batch, seq, wide)
    kv_map = lambda b, g, i, j: (b, jnp.minimum(j, (i + 1) * ratio - 1), g)
    out = pl.pallas_call(
        functools.partial(_mla_flash_kernel, tq=tq, tk=tk, heads=heads),
        grid=(batch, C_HEADS // heads, seq // tq, seq // tk),
        in_specs=[pl.BlockSpec((None, tq, heads * LANE), lambda b, g, i, j: (b, i, g)),
                  pl.BlockSpec((None, tk, heads * LANE), kv_map),
                  pl.BlockSpec((None, tk, heads * LANE), kv_map)],
        out_specs=pl.BlockSpec((None, tq, heads * C_V), lambda b, g, i, j: (b, i, g)),
        out_shape=jax.ShapeDtypeStruct((batch, seq, C_HEADS * C_V), BF16),
        scratch_shapes=[pltpu.VMEM((heads, tq, LANE), F32),
                        pltpu.VMEM((heads, tq, LANE), F32)],
        compiler_params=_cparams(4),
        name="mla_flash",
    )(q3, k3, v3)
    return out.reshape(batch * seq, C_HEADS * C_V)


def _layer_norm(y, g, b):
    mu = jnp.mean(y, axis=-1, keepdims=True)
    d = y - mu
    var = jnp.mean(d * d, axis=-1, keepdims=True)
    return d * lax.rsqrt(var + LN_EPS) * g + b


def _token_order(src_ref, tmp_ref, dil):
    if dil == 1:
        return src_ref[...].astype(F32)
    n = src_ref.shape[1]
    n_slab = src_ref.shape[2] // LANE
    for r in range(dil):
        rows = src_ref[r].astype(F32)
        for c in range(n_slab):
            tmp_ref[c, pl.ds(r, n, stride=dil), :] = rows[:, c * LANE:(c + 1) * LANE]
    return jnp.concatenate([tmp_ref[c] for c in range(n_slab)], axis=1)


def _merge_kernel(oa0_ref, oa1_ref, oa2_ref, l0_ref, l1_ref, l2_ref, ob_ref, oc_ref, gate_ref, x_ref,
                  e_ref, wb_ref, wo_ref, bg_ref, g_ref, b_ref, o_ref, ot1_ref, ot2_ref, lt1_ref, lt2_ref,
                  *, alpha):
    dils = [d for _, d in A_GROUPS]
    stats = [_token_order(ref, tmp, d)
             for ref, tmp, d in zip((l0_ref, l1_ref, l2_ref), (None, lt1_ref, lt2_ref), dils)]
    is_max = lax.broadcasted_iota(jnp.int32, stats[0].shape, 1) // STAT_LANES % 2 == 0
    top = jnp.maximum(jnp.maximum(stats[0], stats[1]), stats[2])
    es = [jnp.exp2(v - top) for v in stats]
    sums = [pltpu.roll(v, LANE - STAT_LANES, axis=1) for v in stats]
    inv = 1.0 / (es[0] * sums[0] + es[1] * sums[1] + es[2] * sums[2])
    o_a = None
    for e, oa_ref, tmp, d in zip(es, (oa0_ref, oa1_ref, oa2_ref), (None, ot1_ref, ot2_ref), dils):
        w = jnp.where(is_max, e * inv, 0.0)
        hi = w.astype(BF16)
        lo = (w - hi.astype(F32)).astype(BF16)
        wide = jnp.dot(jnp.concatenate([hi, lo], axis=1), e_ref[...], preferred_element_type=F32)
        term = wide * _token_order(oa_ref, tmp, d)
        o_a = term if o_a is None else o_a + term
    branches = (o_a.astype(BF16), ob_ref[...], oc_ref[...])
    merged = None
    for i, br in enumerate(branches):
        gate = jax.nn.sigmoid(gate_ref[:, i * D_MODEL:(i + 1) * D_MODEL].astype(F32)
                              + bg_ref[:, i * D_MODEL:(i + 1) * D_MODEL])
        term = gate * jnp.dot(br, wb_ref[i], preferred_element_type=F32)
        merged = term if merged is None else merged + term
    mix = jnp.dot(merged.astype(BF16), wo_ref[...], preferred_element_type=F32)
    o_ref[...] = _layer_norm(alpha * x_ref[...] + mix, g_ref[...], b_ref[...])


def _merge(oa, lse, ob, oc, p2, x2d, expand, lw, layer, *, alpha, tm=512):
    t = x2d.shape[0]
    row = lambda w: pl.BlockSpec((tm, w), lambda i: (i, 0))
    full = lambda a: pl.BlockSpec(a.shape, lambda i: (0,) * a.ndim)

    def grouped(arrs):
        views, specs = [], []
        for a, (_, dil) in zip(arrs, A_GROUPS):
            w = a.shape[-1]
            if dil == 1:
                views.append(a.reshape(t, w))
                specs.append(row(w))
                continue
            per_span = BLOCK * dil // tm
            views.append(a.reshape(-1, dil, BLOCK, w))
            specs.append(pl.BlockSpec((None, dil, tm // dil, w),
                                      lambda i, per_span=per_span: (i // per_span, 0, i % per_span, 0)))
        return views, specs

    oa_v, oa_s = grouped(oa)
    lse_v, lse_s = grouped(lse)
    stacked = (lw["wb"], lw["wo"], lw["bg"], lw["ln1_g"], lw["ln1_b"])
    slabs = lambda w: pltpu.VMEM((w // LANE, tm, LANE), F32)
    return pl.pallas_call(
        functools.partial(_merge_kernel, alpha=alpha),
        grid=(t // tm,),
        in_specs=oa_s + lse_s + [row(512), row(512),
                  pl.BlockSpec((tm, 3 * D_MODEL), lambda i: (i, COL_GATE // (3 * D_MODEL))),
                  row(D_MODEL), full(expand)] + [_layer_block(a, layer) for a in stacked],
        out_specs=row(D_MODEL),
        out_shape=jax.ShapeDtypeStruct((t, D_MODEL), F32),
        scratch_shapes=[slabs(512), slabs(512), slabs(LANE), slabs(LANE)],
        compiler_params=_cparams(1),
        name="merge",
    )(*oa_v, *lse_v, ob.reshape(t, 512), oc, p2, x2d, expand, *stacked)


HALO = 8
FF_CHUNK = 1024


def _ffn_kernel(halo_ref, *refs, alpha, tm, per_seq):
    n_slab = D_MODEL // LANE
    x_refs = refs[:n_slab]
    wup_ref, cw_ref, cb_ref, wdn_ref, g_ref, b_ref, o_ref, xs_ref, os_ref = refs[n_slab:]
    rows = tm + HALO
    groups = rows // SUBLANES
    i = pl.program_id(0)
    halo = jnp.where(i % per_seq == 0, jnp.zeros_like(halo_ref[...]), halo_ref[...])
    for c, x_ref in enumerate(x_refs):
        xs_ref[c, :HALO, :] = halo[:, c * LANE:(c + 1) * LANE]
        xs_ref[c, HALO:, :] = x_ref[...]
    xp = jnp.concatenate(
        [jnp.concatenate([xs_ref[c, pl.ds(j, SUBLANES, stride=groups), :] for j in range(groups)], axis=0)
         for c in range(n_slab)], axis=1)
    xh = xp.astype(BF16)
    chunks = [(c0, min(FF_CHUNK, D_FF - c0)) for c0 in range(0, D_FF, FF_CHUNK)]

    def up(c0, cf):
        return [jnp.dot(xh, wup_ref[:, base:base + cf], preferred_element_type=F32)
                for base in (c0, D_FF + c0)]

    def conv(u, base, cf):
        u3 = u.reshape(groups, SUBLANES, cf)
        wrap = pltpu.roll(u3[groups - 2:], 1, axis=1)
        prev1 = jnp.concatenate([wrap[1:], u3[:groups - 1]], axis=0)
        prev2 = jnp.concatenate([wrap, u3[:groups - 2]], axis=0)
        tap = lambda k: cw_ref[k:k + 1, base:base + cf]
        return cb_ref[:, base:base + cf] + prev2 * tap(0) + prev1 * tap(1) + u3 * tap(2)

    acc = None
    u_next = up(*chunks[0])
    for ci, (c0, cf) in enumerate(chunks):
        u_cur = u_next
        if ci + 1 < len(chunks):
            u_next = up(*chunks[ci + 1])
        gate, val = (conv(u, base, cf) for u, base in zip(u_cur, (c0, D_FF + c0)))
        act = (gate * jax.nn.sigmoid(gate) * val).reshape(rows, cf).astype(BF16)
        term = jnp.dot(act, wdn_ref[c0:c0 + cf, :], preferred_element_type=F32)
        acc = term if acc is None else acc + term
    out = _layer_norm(alpha * xp + acc, g_ref[...], b_ref[...])
    for c in range(n_slab):
        for j in range(groups):
            os_ref[c, pl.ds(j, SUBLANES, stride=groups), :] = out[j * SUBLANES:(j + 1) * SUBLANES,
                                                                  c * LANE:(c + 1) * LANE]
        o_ref[:, c * LANE:(c + 1) * LANE] = os_ref[c, HALO:, :]


def _ffn(x2d, lw, layer, *, alpha, seq, tm=512):
    t = x2d.shape[0]
    per_seq = seq // tm
    n_slab = D_MODEL // LANE
    consts = (lw["wup"], lw["cw"], lw["cb"], lw["wdn"], lw["ln2_g"], lw["ln2_b"])
    return pl.pallas_call(
        functools.partial(_ffn_kernel, alpha=alpha, tm=tm, per_seq=per_seq),
        grid=(t // tm,),
        in_specs=[pl.BlockSpec((HALO, D_MODEL), lambda i: (jnp.maximum(i * (tm // HALO) - 1, 0), 0))]
                 + [pl.BlockSpec((tm, LANE), lambda i, c=c: (i, c)) for c in range(n_slab)]
                 + [_layer_block(a, layer, pipeline_mode=pl.Buffered(1)) for a in consts],
        out_specs=pl.BlockSpec((tm, D_MODEL), lambda i: (i, 0)),
        out_shape=jax.ShapeDtypeStruct((t, D_MODEL), F32),
        scratch_shapes=[pltpu.VMEM((n_slab, tm + HALO, LANE), F32),
                        pltpu.VMEM((n_slab, tm + HALO, LANE), F32)],
        compiler_params=_cparams(1),
        name="ffn",
    )(x2d, *([x2d] * n_slab), *consts)


def _prepare_weights(w_in, b_gate, q_norm_g, kv_norm_g, w_uq, w_ukv, w_branch, w_out,
                     ln1_g, ln1_b, w_ffn_up, conv_w, conv_b, w_ffn_down, ln2_g, ln2_b):
    depth = w_in.shape[0]
    a_end = 4608
    col = np.arange(w_in.shape[-1])
    is_q = ((col < a_end) & (col % A_COLS < A_HEADS * HEAD_DIM)) | ((col >= a_end) & (col < a_end + 512))
    col_scale = jnp.asarray(np.where(is_q, HEAD_DIM ** -0.5 * LOG2E, 1.0), F32)
    wi = (w_in * col_scale).astype(BF16)
    bq = wi[:, :, a_end:a_end + 512].reshape(depth, D_MODEL, B_Q_HEADS, HEAD_DIM)
    bq = jnp.stack([bq[:, :, :4], bq[:, :, 4:]], axis=3).reshape(depth, D_MODEL, 512)
    bkv = wi[:, :, 5120:5376]
    cdq = wi[:, :, 5376:5632]
    ckv = wi[:, :, 5632:5760]
    kr = wi[:, :, 5760:5792]
    kr_rot = jnp.concatenate([-kr[:, :, C_ROPE // 2:], kr[:, :, :C_ROPE // 2]], axis=2)
    gate = wi[:, :, 5792:]
    zeros = lambda n: jnp.zeros((depth, D_MODEL, n), wi.dtype)
    rest = jnp.concatenate([bq, cdq, ckv, kr, kr_rot, zeros(64), bkv, zeros(256)], axis=2)
    wp = jnp.stack([gate[:, :, :A_COLS], gate[:, :, A_COLS:], wi[:, :, :A_COLS], rest], axis=1)
    wa = []
    for g, (_, dil) in enumerate(A_GROUPS[1:], start=1):
        tn = A_COLS * 1024 // max(1024, BLOCK * dil)
        w = wi[:, :, g * A_COLS:(g + 1) * A_COLS]
        wa.append(w.reshape(depth, D_MODEL, A_COLS // tn, tn).transpose(0, 2, 1, 3))

    uq = (w_uq * ((C_NOPE + C_ROPE) ** -0.5 * LOG2E)).reshape(depth, C_Q_RANK, C_HEADS, C_NOPE + C_ROPE)
    zq = lambda n: jnp.zeros((depth, C_Q_RANK, C_HEADS, n), uq.dtype)
    half = C_ROPE // 2
    wq1 = jnp.concatenate([uq, zq(LANE - C_NOPE - C_ROPE)], axis=3)
    wq2 = jnp.concatenate([zq(C_NOPE), -uq[..., C_NOPE + half:], uq[..., C_NOPE:C_NOPE + half],
                           zq(LANE - C_NOPE - C_ROPE)], axis=3)
    ukv = w_ukv.reshape(depth, C_KV_RANK, C_HEADS, C_NOPE + C_V)
    zkv = lambda n: jnp.zeros((depth, C_KV_RANK, C_HEADS, n), ukv.dtype)
    wk = jnp.concatenate([ukv[..., :C_NOPE], zkv(LANE - C_NOPE)], axis=3)
    wv = jnp.concatenate([ukv[..., C_NOPE:], zkv(LANE - C_V)], axis=3)

    wb1 = w_branch[:, 1].reshape(depth, B_Q_HEADS, HEAD_DIM, D_MODEL)
    wb1 = jnp.stack([wb1[:, :4], wb1[:, 4:]], axis=2).reshape(depth, 512, D_MODEL)
    wb = jnp.stack([w_branch[:, 0], wb1, w_branch[:, 2]], axis=1)

    row = lambda a: a.reshape(depth, 1, -1)
    return dict(
        wp=wp, wa=wa,
        gq=row(q_norm_g), gkv=row(kv_norm_g),
        wq1=wq1.reshape(depth, C_Q_RANK, -1).astype(BF16), wq2=wq2.reshape(depth, C_Q_RANK, -1).astype(BF16),
        wk=wk.reshape(depth, C_KV_RANK, -1).astype(BF16), wv=wv.reshape(depth, C_KV_RANK, -1).astype(BF16),
        wb=wb.astype(BF16), wo=w_out.astype(BF16), bg=row(b_gate),
        ln1_g=row(ln1_g), ln1_b=row(ln1_b),
        wup=w_ffn_up.astype(BF16), cw=conv_w, cb=row(conv_b), wdn=w_ffn_down.astype(BF16),
        ln2_g=row(ln2_g), ln2_b=row(ln2_b),
    )


def _rope_selectors():
    sel = np.zeros((512, 2 * LANE), np.float32)
    for j in range(C_ROPE):
        sel[C_Q_RANK + C_KV_RANK + j, C_NOPE + j] = 1.0
        sel[C_Q_RANK + C_KV_RANK + C_ROPE + j, LANE + C_NOPE + j] = 1.0
    return jnp.asarray(sel, BF16)


def _rope_tables(seq):
    pos = jnp.arange(seq, dtype=F32)
    inv_freq = ROPE_BASE ** (-jnp.arange(0, C_ROPE, 2, dtype=F32) / C_ROPE)
    ang = pos[:, None] * inv_freq[None, :]
    cos, sin = jnp.cos(ang), jnp.sin(ang)
    pad = jnp.zeros((seq, LANE - C_NOPE - C_ROPE), F32)
    cos_t = jnp.concatenate([jnp.ones((seq, C_NOPE), F32), cos, cos, pad], axis=1)
    sin_t = jnp.concatenate([jnp.zeros((seq, C_NOPE), F32), sin, sin, pad], axis=1)
    return cos_t, sin_t


def _expand_matrix():
    e = np.zeros((LANE, A_HEADS * HEAD_DIM), np.float32)
    for c in range(A_HEADS * HEAD_DIM):
        e[(LANE // A_HEADS) * (c // HEAD_DIM), c] = 1.0
    return jnp.asarray(np.concatenate([e, e], axis=0), BF16)


def kernel(x, rel_table, w_in, b_gate, sinks, q_norm_g, kv_norm_g, w_uq, w_ukv, w_branch, w_out,
           ln1_g, ln1_b, w_ffn_up, conv_w, conv_b, w_ffn_down, ln2_g, ln2_b):
    batch, seq, d = x.shape
    depth = w_in.shape[0]
    alpha = (2 * depth) ** 0.25
    cos_t, sin_t = _rope_tables(seq)
    sel = _rope_selectors()
    expand = _expand_matrix()
    bias = _bias_tables(rel_table)
    lw = _prepare_weights(w_in, b_gate, q_norm_g, kv_norm_g, w_uq, w_ukv, w_branch, w_out,
                          ln1_g, ln1_b, w_ffn_up, conv_w, conv_b, w_ffn_down, ln2_g, ln2_b)
    x2d = x.reshape(batch * seq, d)
    for l in range(depth):
        p2 = _proj(x2d, lw["wp"], l, dil=1, tm=1024, name="proj")
        p_dil = [_proj(x2d, w, l, dil=dil, tm=max(1024, BLOCK * dil), name=f"proj_a{g + 1}")
                 for g, (w, (_, dil)) in enumerate(zip(lw["wa"], A_GROUPS[1:]))]
        sink_slots = sinks[l].reshape(2, 4).T.reshape(-1) * LOG2E
        oa, lse = [], []
        for kind in range(len(A_GROUPS)):
            src = p2 if kind == 0 else p_dil[kind - 1]
            o_g, lse_g = _band_attention(src, bias, sink_slots, kind=kind, batch=batch, seq=seq)
            oa.append(o_g)
            lse.append(lse_g)
        ob = _band_attention(p2, bias, sink_slots, kind=len(A_GROUPS), batch=batch, seq=seq)
        q, k, v = _mla_prep(p2, cos_t, sin_t, sel, lw, l, seq=seq)
        oc = _mla_flash(q, k, v, batch=batch, seq=seq)
        x2d = _merge(oa, lse, ob, oc, p2, x2d, expand, lw, l, alpha=alpha)
        x2d = _ffn(x2d, lw, l, alpha=alpha, seq=seq)
    return x2d.reshape(batch, seq, d)
```

```python
import functools
import math

import jax
import jax.numpy as jnp
import numpy as np
from jax import lax
from jax.experimental import pallas as pl
from jax.experimental.pallas import tpu as pltpu

F32 = jnp.float32
BF16 = jnp.bfloat16

D_MODEL = 1024
HEAD_DIM = 64
BLOCK = 128
A_GROUPS = ((128, 1), (512, 4), (2048, 16))
A_HEADS = 8
B_Q_HEADS = 8
B_WINDOW = 128
C_HEADS = 8
C_Q_RANK = 256
C_KV_RANK = 128
C_NOPE = 64
C_ROPE = 32
C_V = 64
ROPE_BASE = 10000.0
REL_BUCKETS = 32
REL_MAX_DIST = 2048
D_FF = 2816
LN_EPS = 1e-5
RMS_EPS = 1e-6
NEG = -1e30
LOG2E = math.log2(math.e)
FLASH_TQ = 1024
FLASH_TK = 1024
FLASH_HEADS = 8
FLASH_DIAG = 512

COL_GATE = 0
COL_A = 3072
COL_BQ = 4608
COL_C = 5120
COL_BKV = 5632
NP = 6144
A_COLS = 1536

LANE = 128
SUBLANES = 8
VMEM_LIMIT = 56 * 1024 * 1024

N_KINDS = 4


def _cparams(n_axes):
    return pltpu.CompilerParams(dimension_semantics=("arbitrary",) * n_axes,
                                vmem_limit_bytes=VMEM_LIMIT)


def _layer_block(a, layer, **kwargs):
    zeros = (0,) * (a.ndim - 1)
    return pl.BlockSpec((None,) + a.shape[1:], lambda *_: (layer,) + zeros, **kwargs)


CHEAP_STRIDE = 4


def _proj_kernel(*refs, dil):
    n_x = len(refs) - (4 if dil > CHEAP_STRIDE else 3)
    x_refs, (w_ref, o_ref, xb_ref, *tmp) = refs[:n_x], refs[n_x:]

    @pl.when(pl.program_id(1) == 0)
    def _():
        if dil == 1:
            xb_ref[...] = x_refs[0][...].astype(BF16)
            return
        span = BLOCK * dil
        d1 = min(dil, CHEAP_STRIDE)
        d2 = dil // d1
        for c, x_ref in enumerate(x_refs):
            for s0 in range(0, xb_ref.shape[0], span):
                for r1 in range(d1):
                    if d2 == 1:
                        rows = x_ref[pl.ds(s0 + r1, BLOCK, stride=d1), :]
                        xb_ref[s0 + r1 * BLOCK:s0 + (r1 + 1) * BLOCK, c * LANE:(c + 1) * LANE] = rows.astype(BF16)
                        continue
                    g0 = s0 + r1 * (span // d1)
                    tmp[0][c, g0:g0 + span // d1, :] = x_ref[pl.ds(s0 + r1, span // d1, stride=d1), :]
                    for r2 in range(d2):
                        r = r1 + d1 * r2
                        rows = tmp[0][c, pl.ds(g0 + r2, BLOCK, stride=d2), :]
                        xb_ref[s0 + r * BLOCK:s0 + (r + 1) * BLOCK, c * LANE:(c + 1) * LANE] = rows.astype(BF16)

    w = w_ref[pl.program_id(1)]
    o_ref[...] = jnp.dot(xb_ref[...], w, preferred_element_type=F32).astype(o_ref.dtype)


def _proj(x2d, w_tiles, layer, *, dil, tm, name):
    t, k = x2d.shape
    n_tiles, tn = w_tiles.shape[1], w_tiles.shape[3]
    n = n_tiles * tn
    if dil == 1:
        x_specs = [pl.BlockSpec((tm, k), lambda i, j: (i, 0))]
    else:
        x_specs = [pl.BlockSpec((tm, LANE), lambda i, j, c=c: (i, c)) for c in range(k // LANE)]
    scratch = [pltpu.VMEM((tm, k), BF16)]
    if dil > CHEAP_STRIDE:
        scratch.append(pltpu.VMEM((k // LANE, tm, LANE), F32))
    return pl.pallas_call(
        functools.partial(_proj_kernel, dil=dil),
        grid=(t // tm, n_tiles),
        in_specs=x_specs + [_layer_block(w_tiles, layer, pipeline_mode=pl.Buffered(1))],
        out_specs=pl.BlockSpec((tm, tn), lambda i, j: (i, j)),
        out_shape=jax.ShapeDtypeStruct((t, n), BF16),
        scratch_shapes=scratch,
        compiler_params=_cparams(2),
        name=name,
    )(*([x2d] * len(x_specs)), w_tiles)


def _t5_bucket(dist):
    n = jnp.maximum(dist, 0)
    max_exact = REL_BUCKETS // 2
    scaled = jnp.log(jnp.maximum(n, 1).astype(F32) / max_exact) / math.log(REL_MAX_DIST / max_exact)
    large = max_exact + (scaled * (REL_BUCKETS - max_exact)).astype(jnp.int32)
    return jnp.where(n < max_exact, n, jnp.minimum(large, REL_BUCKETS - 1))


def _bias_codes():
    qi = jnp.arange(BLOCK)[:, None]
    ki = jnp.arange(2 * BLOCK)[None, :]
    step = BLOCK + qi - ki
    has_prev = ki >= BLOCK
    codes = []
    for kind in range(N_KINDS):
        if kind < len(A_GROUPS):
            dil = A_GROUPS[kind][1]
            band = (step >= 0) & (step <= BLOCK)
        else:
            dil = 1
            band = (step >= 0) & (step < B_WINDOW)
        bucket = _t5_bucket(step * dil)
        codes.append(jnp.stack([jnp.where(band & has_prev, bucket, -1),
                                jnp.where(band, bucket, -1)]))
    return jnp.stack(codes).astype(jnp.int32)


def _bias_kernel(rel_ref, code_ref, o_ref):
    kind = pl.program_id(0)
    code = code_ref[0, 0]
    for slot in range(A_HEADS):
        b_head = len(A_GROUPS) * A_HEADS + slot // 2 + 4 * (slot % 2)
        col = jnp.where(kind < len(A_GROUPS), kind * A_HEADS + slot, b_head)
        acc = jnp.full(code.shape, NEG, F32)
        for b in range(REL_BUCKETS):
            acc = jnp.where(code == b, rel_ref[b, col] * LOG2E, acc)
        o_ref[0, 0, slot] = acc


def _bias_tables(rel_table):
    codes = _bias_codes()
    return pl.pallas_call(
        _bias_kernel,
        grid=(N_KINDS, 2),
        in_specs=[pl.BlockSpec(memory_space=pltpu.SMEM),
                  pl.BlockSpec((1, 1, BLOCK, 2 * BLOCK), lambda k, v: (k, v, 0, 0))],
        out_specs=pl.BlockSpec((1, 1, A_HEADS, BLOCK, 2 * BLOCK), lambda k, v: (k, v, 0, 0, 0)),
        out_shape=jax.ShapeDtypeStruct((N_KINDS, 2, A_HEADS, BLOCK, 2 * BLOCK), F32),
        compiler_params=_cparams(2),
        name="bias_tables",
    )(rel_table, codes)


BAND_QB = 16
STAT_LANES = LANE // A_HEADS // 2


def _band_kernel(sink_ref, q_ref, kp_ref, kc_ref, vp_ref, vc_ref, bias_ref, o_ref, *lse_refs,
                 shared_kv, with_sink):
    first = pl.program_id(2) == 0
    lane = lax.broadcasted_iota(jnp.int32, (BLOCK, LANE), 1)
    low = lane < HEAD_DIM
    n_qb, n_res = q_ref.shape[:2]
    for res, qb in ((res, qb) for res in range(n_res) for qb in range(n_qb)):
        variant = jnp.where(first, 0, 1) if qb == 0 else 1
        lse_tile = jnp.zeros((BLOCK, LANE), F32)
        for pair in range(4):
            cols = slice(pair * LANE, (pair + 1) * LANE)
            kv_cols = slice(0, LANE) if shared_kv else cols
            qp = q_ref[qb, res, :, cols]
            k_prev = kp_ref[res, :, kv_cols] if qb == 0 else kc_ref[qb - 1, res, :, kv_cols]
            v_prev = vp_ref[res, :, kv_cols] if qb == 0 else vc_ref[qb - 1, res, :, kv_cols]
            kcat = jnp.concatenate([k_prev, kc_ref[qb, res, :, kv_cols]], axis=0)
            vcat = jnp.concatenate([v_prev, vc_ref[qb, res, :, kv_cols]], axis=0)
            zero = jnp.zeros_like(qp)
            q2 = jnp.concatenate([jnp.where(low, qp, zero), jnp.where(low, zero, qp)], axis=0)
            s = lax.dot_general(q2, kcat, (((1,), (1,)), ((), ())), preferred_element_type=F32)
            s = s + bias_ref[variant, pair]
            m = jnp.max(s, axis=-1, keepdims=True)
            if with_sink:
                head_a = lax.broadcasted_iota(jnp.int32, (2 * BLOCK, 1), 0) < BLOCK
                sink = jnp.where(head_a, sink_ref[2 * pair], sink_ref[2 * pair + 1])
                m = jnp.maximum(m, sink)
            p = jnp.exp2(s - m)
            l = jnp.sum(p, axis=-1, keepdims=True)
            if with_sink:
                l = l + jnp.exp2(sink - m)
            o = jnp.dot(p.astype(BF16), vcat, preferred_element_type=F32)
            if lse_refs:
                for sub in range(2):
                    rows = slice(sub * BLOCK, (sub + 1) * BLOCK)
                    field = 2 * (2 * pair + sub)
                    lse_tile = jnp.where(lane // STAT_LANES == field, m[rows],
                                         jnp.where(lane // STAT_LANES == field + 1, l[rows], lse_tile))
            else:
                o = o * (1.0 / l)
            o_ref[qb, res, :, cols] = jnp.where(low, o[:BLOCK], o[BLOCK:]).astype(o_ref.dtype)
        if lse_refs:
            lse_refs[0][qb, res] = lse_tile


def _band_attention(p2, bias, sinks, *, kind, batch, seq):
    is_b = kind == len(A_GROUPS)
    dil = 1 if is_b else A_GROUPS[kind][1]
    n_span = seq // (BLOCK * dil)
    view = p2.reshape(batch, n_span, dil, BLOCK, p2.shape[-1])
    if is_b:
        q_blk, k_blk, v_blk, kv_w = COL_BQ // 512, COL_BKV // LANE, COL_BKV // LANE + 1, LANE
    else:
        q_blk = COL_A // 512 if dil == 1 else 0
        k_blk, v_blk, kv_w = q_blk + 1, q_blk + 2, 512

    n_qb = min(BAND_QB, n_span)
    n_res = BAND_QB // n_qb

    def cur(blk, width):
        return pl.BlockSpec((None, n_qb, n_res, BLOCK, width), lambda b, r, n: (b, n, r, 0, blk))

    def prev(blk, width):
        return pl.BlockSpec((None, None, n_res, BLOCK, width),
                            lambda b, r, n: (b, jnp.maximum(n * n_qb - 1, 0), r, 0, blk))

    in_specs = [
        pl.BlockSpec(memory_space=pltpu.SMEM),
        cur(q_blk, 512), prev(k_blk, kv_w), cur(k_blk, kv_w), prev(v_blk, kv_w), cur(v_blk, kv_w),
        pl.BlockSpec((None, 2, A_HEADS // 2, 2 * BLOCK, 2 * BLOCK), lambda b, r, n: (kind, 0, 0, 0, 0)),
    ]
    bias = bias.reshape(N_KINDS, 2, A_HEADS // 2, 2 * BLOCK, 2 * BLOCK)
    out_specs = [pl.BlockSpec((None, n_qb, n_res, BLOCK, 512), lambda b, r, n: (b, n, r, 0, 0))]
    out_shape = [jax.ShapeDtypeStruct((batch, n_span, dil, BLOCK, 512), BF16)]
    if not is_b:
        out_specs.append(pl.BlockSpec((None, n_qb, n_res, BLOCK, LANE), lambda b, r, n: (b, n, r, 0, 0)))
        out_shape.append(jax.ShapeDtypeStruct((batch, n_span, dil, BLOCK, LANE), F32))
    outs = pl.pallas_call(
        functools.partial(_band_kernel, shared_kv=is_b, with_sink=is_b),
        grid=(batch, dil // n_res, n_span // n_qb),
        in_specs=in_specs,
        out_specs=out_specs,
        out_shape=out_shape,
        compiler_params=_cparams(3),
        name="band_b" if is_b else f"band_a{kind}",
    )(sinks, view, view, view, view, view, bias)
    return outs[0] if is_b else outs


def _mla_prep_kernel(c_ref, cos_ref, sin_ref, gq_ref, gkv_ref, wq1_ref, wq2_ref, wk_ref, wv_ref,
                     sel_ref, q_ref, k_ref, v_ref):
    c = c_ref[...]
    cq = c[:, :C_Q_RANK].astype(F32)
    ckv = c[:, C_Q_RANK:C_Q_RANK + C_KV_RANK].astype(F32)
    nq = cq * lax.rsqrt(jnp.mean(cq * cq, axis=-1, keepdims=True) + RMS_EPS) * gq_ref[...]
    nkv = ckv * lax.rsqrt(jnp.mean(ckv * ckv, axis=-1, keepdims=True) + RMS_EPS) * gkv_ref[...]
    nq = nq.astype(BF16)
    nkv = nkv.astype(BF16)
    cos = cos_ref[...]
    sin = sin_ref[...]
    cos8 = jnp.tile(cos, (1, C_HEADS))
    sin8 = jnp.tile(sin, (1, C_HEADS))
    q = (jnp.dot(nq, wq1_ref[...], preferred_element_type=F32) * cos8
         + jnp.dot(nq, wq2_ref[...], preferred_element_type=F32) * sin8)
    q_ref[...] = q.astype(q_ref.dtype)
    picked = jnp.dot(c, sel_ref[...], preferred_element_type=F32)
    kr = picked[:, :LANE] * cos + picked[:, LANE:] * sin
    k = jnp.dot(nkv, wk_ref[...], preferred_element_type=F32) + jnp.tile(kr, (1, C_HEADS))
    k_ref[...] = k.astype(k_ref.dtype)
    lane = lax.broadcasted_iota(jnp.int32, (1, C_HEADS * LANE), 1)
    ones_col = jnp.where(lane % LANE == C_V, 1.0, 0.0).astype(F32)
    v_ref[...] = (jnp.dot(nkv, wv_ref[...], preferred_element_type=F32) + ones_col).astype(v_ref.dtype)


def _mla_prep(p2, cos_t, sin_t, sel, lw, layer, *, seq, tm=1024):
    t = p2.shape[0]
    per_seq = seq // tm
    full = lambda a: pl.BlockSpec(a.shape, lambda i: (0,) * a.ndim)
    stacked = (lw["gq"], lw["gkv"], lw["wq1"], lw["wq2"], lw["wk"], lw["wv"])
    wide = C_HEADS * LANE
    return pl.pallas_call(
        _mla_prep_kernel,
        grid=(t // tm,),
        in_specs=[pl.BlockSpec((tm, 512), lambda i: (i, COL_C // 512)),
                  pl.BlockSpec((tm, LANE), lambda i: (i % per_seq, 0)),
                  pl.BlockSpec((tm, LANE), lambda i: (i % per_seq, 0))]
                 + [_layer_block(a, layer) for a in stacked] + [full(sel)],
        out_specs=[pl.BlockSpec((tm, wide), lambda i: (i, 0))] * 3,
        out_shape=[jax.ShapeDtypeStruct((t, wide), BF16)] * 3,
        compiler_params=_cparams(1),
        name="mla_prep",
    )(p2, cos_t, sin_t, *stacked, sel)


def _mla_flash_kernel(qi_ref, ki_ref, q_ref, k_ref, v_ref, o_ref, m_ref, acc_ref, *, tq, tk, heads):
    step = pl.program_id(2)
    qi = qi_ref[step]
    ki = ki_ref[step]
    ratio = tq // tk

    @pl.when(ki == 0)
    def _():
        m_ref[...] = jnp.full(m_ref.shape, NEG, F32)
        acc_ref[...] = jnp.zeros(acc_ref.shape, F32)

    def update(h, rows, n_keys, shift):
        hc = slice(h * LANE, (h + 1) * LANE)
        s = lax.dot_general(q_ref[rows, hc], k_ref[:n_keys, hc], (((1,), (1,)), ((), ())),
                            preferred_element_type=F32)
        if shift is not None:
            row = lax.broadcasted_iota(jnp.int32, s.shape, 0) + shift
            col = lax.broadcasted_iota(jnp.int32, s.shape, 1)
            s = jnp.where(row >= col, s, NEG)
        m_prev = m_ref[h, rows]
        m_new = jnp.maximum(m_prev, jnp.max(s, axis=-1, keepdims=True))
        alpha = jnp.exp2(m_prev - m_new)
        p = jnp.exp2(s - jnp.tile(m_new, (1, n_keys // LANE)))
        acc_ref[h, rows] = alpha * acc_ref[h, rows] + jnp.dot(p.astype(BF16), v_ref[:n_keys, hc],
                                                              preferred_element_type=F32)
        m_ref[h, rows] = m_new

    @pl.when(ki < qi * ratio)
    def _():
        for h in range(heads):
            update(h, slice(0, tq), tk, None)

    for d in range(ratio):
        @pl.when(ki == qi * ratio + d)
        def _(d=d):
            for h in range(heads):
                if (d + 1) * tk < tq:
                    update(h, slice((d + 1) * tk, tq), tk, None)
                for r0 in range(0, tk, FLASH_DIAG):
                    update(h, slice(d * tk + r0, d * tk + r0 + FLASH_DIAG), r0 + FLASH_DIAG, r0)

    @pl.when(ki == (qi + 1) * ratio - 1)
    def _():
        low = lax.broadcasted_iota(jnp.int32, (tq, LANE), 1) < C_V
        for pair in range(heads // 2):
            halves = []
            for h in (2 * pair, 2 * pair + 1):
                acc = acc_ref[h]
                halves.append(acc * (1.0 / acc[:, C_V:C_V + 1]))
            odd = pltpu.roll(halves[1], C_V, axis=1)
            o_ref[:, pair * LANE:(pair + 1) * LANE] = jnp.where(low, halves[0], odd).astype(o_ref.dtype)


def _mla_flash(q, k, v, *, batch, seq, tq=FLASH_TQ, tk=FLASH_TK, heads=FLASH_HEADS):
    wide = C_HEADS * LANE
    ratio = tq // tk
    q3 = q.reshape(batch, seq, wide)
    k3 = k.reshape(batch, seq, wide)
    v3 = v.reshape(batch, seq, wide)
    pairs = [(i, j) for i in range(seq // tq) for j in range((i + 1) * ratio)]
    qi_of = jnp.asarray([i for i, _ in pairs], jnp.int32)
    ki_of = jnp.asarray([j for _, j in pairs], jnp.int32)
    q_map = lambda b, g, t, qi, ki: (b, qi[t], g)
    kv_map = lambda b, g, t, qi, ki: (b, ki[t], g)
    out = pl.pallas_call(
        functools.partial(_mla_flash_kernel, tq=tq, tk=tk, heads=heads),
        grid_spec=pltpu.PrefetchScalarGridSpec(
            num_scalar_prefetch=2,
            grid=(batch, C_HEADS // heads, len(pairs)),
            in_specs=[pl.BlockSpec((None, tq, heads * LANE), q_map),
                      pl.BlockSpec((None, tk, heads * LANE), kv_map),
                      pl.BlockSpec((None, tk, heads * LANE), kv_map)],
            out_specs=pl.BlockSpec((None, tq, heads * C_V), q_map),
            scratch_shapes=[pltpu.VMEM((heads, tq, LANE), F32),
                            pltpu.VMEM((heads, tq, LANE), F32)]),
        out_shape=jax.ShapeDtypeStruct((batch, seq, C_HEADS * C_V), BF16),
        compiler_params=_cparams(3),
        name="mla_flash",
    )(qi_of, ki_of, q3, k3, v3)
    return out.reshape(batch * seq, C_HEADS * C_V)


def _layer_norm(y, g, b):
    mu = jnp.mean(y, axis=-1, keepdims=True)
    d = y - mu
    var = jnp.mean(d * d, axis=-1, keepdims=True)
    return d * lax.rsqrt(var + LN_EPS) * g + b


def _token_order(src_ref, tmp_ref, dil):
    if dil == 1:
        return src_ref[...].astype(F32)
    n = src_ref.shape[1]
    n_slab = src_ref.shape[2] // LANE
    for r in range(dil):
        rows = src_ref[r].astype(F32)
        for c in range(n_slab):
            tmp_ref[c, pl.ds(r, n, stride=dil), :] = rows[:, c * LANE:(c + 1) * LANE]
    return jnp.concatenate([tmp_ref[c] for c in range(n_slab)], axis=1)


def _merge_kernel(oa0_ref, oa1_ref, oa2_ref, l0_ref, l1_ref, l2_ref, ob_ref, oc_ref, gate_ref, x_ref,
                  e_ref, wb_ref, wo_ref, bg_ref, g_ref, b_ref, o_ref, ot1_ref, ot2_ref, lt1_ref, lt2_ref,
                  *, alpha):
    dils = [d for _, d in A_GROUPS]
    stats = [_token_order(ref, tmp, d)
             for ref, tmp, d in zip((l0_ref, l1_ref, l2_ref), (None, lt1_ref, lt2_ref), dils)]
    is_max = lax.broadcasted_iota(jnp.int32, stats[0].shape, 1) // STAT_LANES % 2 == 0
    top = jnp.maximum(jnp.maximum(stats[0], stats[1]), stats[2])
    es = [jnp.exp2(v - top) for v in stats]
    sums = [pltpu.roll(v, LANE - STAT_LANES, axis=1) for v in stats]
    inv = 1.0 / (es[0] * sums[0] + es[1] * sums[1] + es[2] * sums[2])
    o_a = None
    for e, oa_ref, tmp, d in zip(es, (oa0_ref, oa1_ref, oa2_ref), (None, ot1_ref, ot2_ref), dils):
        w = jnp.where(is_max, e * inv, 0.0)
        hi = w.astype(BF16)
        lo = (w - hi.astype(F32)).astype(BF16)
        wide = jnp.dot(jnp.concatenate([hi, lo], axis=1), e_ref[...], preferred_element_type=F32)
        term = wide * _token_order(oa_ref, tmp, d)
        o_a = term if o_a is None else o_a + term
    branches = (o_a.astype(BF16), ob_ref[...], oc_ref[...])
    merged = None
    for i, br in enumerate(branches):
        gate = jax.nn.sigmoid(gate_ref[:, i * D_MODEL:(i + 1) * D_MODEL].astype(F32)
                              + bg_ref[:, i * D_MODEL:(i + 1) * D_MODEL])
        term = gate * jnp.dot(br, wb_ref[i], preferred_element_type=F32)
        merged = term if merged is None else merged + term
    mix = jnp.dot(merged.astype(BF16), wo_ref[...], preferred_element_type=F32)
    o_ref[...] = _layer_norm(alpha * x_ref[...] + mix, g_ref[...], b_ref[...])


def _merge(oa, lse, ob, oc, p2, x2d, expand, lw, layer, *, alpha, tm=512):
    t = x2d.shape[0]
    row = lambda w: pl.BlockSpec((tm, w), lambda i: (i, 0))
    full = lambda a: pl.BlockSpec(a.shape, lambda i: (0,) * a.ndim)

    def grouped(arrs):
        views, specs = [], []
        for a, (_, dil) in zip(arrs, A_GROUPS):
            w = a.shape[-1]
            if dil == 1:
                views.append(a.reshape(t, w))
                specs.append(row(w))
                continue
            per_span = BLOCK * dil // tm
            views.append(a.reshape(-1, dil, BLOCK, w))
            specs.append(pl.BlockSpec((None, dil, tm // dil, w),
                                      lambda i, per_span=per_span: (i // per_span, 0, i % per_span, 0)))
        return views, specs

    oa_v, oa_s = grouped(oa)
    lse_v, lse_s = grouped(lse)
    stacked = (lw["wb"], lw["wo"], lw["bg"], lw["ln1_g"], lw["ln1_b"])
    slabs = lambda w: pltpu.VMEM((w // LANE, tm, LANE), F32)
    return pl.pallas_call(
        functools.partial(_merge_kernel, alpha=alpha),
        grid=(t // tm,),
        in_specs=oa_s + lse_s + [row(512), row(512),
                  pl.BlockSpec((tm, 3 * D_MODEL), lambda i: (i, COL_GATE // (3 * D_MODEL))),
                  row(D_MODEL), full(expand)] + [_layer_block(a, layer) for a in stacked],
        out_specs=row(D_MODEL),
        out_shape=jax.ShapeDtypeStruct((t, D_MODEL), F32),
        scratch_shapes=[slabs(512), slabs(512), slabs(LANE), slabs(LANE)],
        compiler_params=_cparams(1),
        name="merge",
    )(*oa_v, *lse_v, ob.reshape(t, 512), oc, p2, x2d, expand, *stacked)


HALO = 8
FF_CHUNK = 1024


def _ffn_kernel(halo_ref, *refs, alpha, tm, per_seq):
    n_slab = D_MODEL // LANE
    x_refs = refs[:n_slab]
    wup_ref, cw_ref, cb_ref, wdn_ref, g_ref, b_ref, o_ref, xs_ref, os_ref = refs[n_slab:]
    rows = tm + HALO
    groups = rows // SUBLANES
    i = pl.program_id(0)
    halo = jnp.where(i % per_seq == 0, jnp.zeros_like(halo_ref[...]), halo_ref[...])
    for c, x_ref in enumerate(x_refs):
        xs_ref[c, :HALO, :] = halo[:, c * LANE:(c + 1) * LANE]
        xs_ref[c, HALO:, :] = x_ref[...]
    xp = jnp.concatenate(
        [jnp.concatenate([xs_ref[c, pl.ds(j, SUBLANES, stride=groups), :] for j in range(groups)], axis=0)
         for c in range(n_slab)], axis=1)
    xh = xp.astype(BF16)
    chunks = [(c0, min(FF_CHUNK, D_FF - c0)) for c0 in range(0, D_FF, FF_CHUNK)]

    def up(c0, cf):
        return [jnp.dot(xh, wup_ref[:, base:base + cf], preferred_element_type=F32)
                for base in (c0, D_FF + c0)]

    def conv(u, base, cf):
        u3 = u.reshape(groups, SUBLANES, cf)
        wrap = pltpu.roll(u3[groups - 2:], 1, axis=1)
        prev1 = jnp.concatenate([wrap[1:], u3[:groups - 1]], axis=0)
        prev2 = jnp.concatenate([wrap, u3[:groups - 2]], axis=0)
        tap = lambda k: cw_ref[k:k + 1, base:base + cf]
        return cb_ref[:, base:base + cf] + prev2 * tap(0) + prev1 * tap(1) + u3 * tap(2)

    acc = None
    u_next = up(*chunks[0])
    for ci, (c0, cf) in enumerate(chunks):
        u_cur = u_next
        if ci + 1 < len(chunks):
            u_next = up(*chunks[ci + 1])
        gate, val = (conv(u, base, cf) for u, base in zip(u_cur, (c0, D_FF + c0)))
        act = (gate * jax.nn.sigmoid(gate) * val).reshape(rows, cf).astype(BF16)
        term = jnp.dot(act, wdn_ref[c0:c0 + cf, :], preferred_element_type=F32)
        acc = term if acc is None else acc + term
    out = _layer_norm(alpha * xp + acc, g_ref[...], b_ref[...])
    for c in range(n_slab):
        for j in range(groups):
            os_ref[c, pl.ds(j, SUBLANES, stride=groups), :] = out[j * SUBLANES:(j + 1) * SUBLANES,
                                                                  c * LANE:(c + 1) * LANE]
        o_ref[:, c * LANE:(c + 1) * LANE] = os_ref[c, HALO:, :]


def _ffn(x2d, lw, layer, *, alpha, seq, tm=512):
    t = x2d.shape[0]
    per_seq = seq // tm
    n_slab = D_MODEL // LANE
    consts = (lw["wup"], lw["cw"], lw["cb"], lw["wdn"], lw["ln2_g"], lw["ln2_b"])
    return pl.pallas_call(
        functools.partial(_ffn_kernel, alpha=alpha, tm=tm, per_seq=per_seq),
        grid=(t // tm,),
        in_specs=[pl.BlockSpec((HALO, D_MODEL), lambda i: (jnp.maximum(i * (tm // HALO) - 1, 0), 0))]
                 + [pl.BlockSpec((tm, LANE), lambda i, c=c: (i, c)) for c in range(n_slab)]
                 + [_layer_block(a, layer, pipeline_mode=pl.Buffered(1)) for a in consts],
        out_specs=pl.BlockSpec((tm, D_MODEL), lambda i: (i, 0)),
        out_shape=jax.ShapeDtypeStruct((t, D_MODEL), F32),
        scratch_shapes=[pltpu.VMEM((n_slab, tm + HALO, LANE), F32),
                        pltpu.VMEM((n_slab, tm + HALO, LANE), F32)],
        compiler_params=_cparams(1),
        name="ffn",
    )(x2d, *([x2d] * n_slab), *consts)


def _prepare_weights(w_in, b_gate, q_norm_g, kv_norm_g, w_uq, w_ukv, w_branch, w_out,
                     ln1_g, ln1_b, w_ffn_up, conv_w, conv_b, w_ffn_down, ln2_g, ln2_b):
    depth = w_in.shape[0]
    a_end = 4608
    col = np.arange(w_in.shape[-1])
    is_q = ((col < a_end) & (col % A_COLS < A_HEADS * HEAD_DIM)) | ((col >= a_end) & (col < a_end + 512))
    col_scale = jnp.asarray(np.where(is_q, HEAD_DIM ** -0.5 * LOG2E, 1.0), F32)
    wi = (w_in * col_scale).astype(BF16)
    bq = wi[:, :, a_end:a_end + 512].reshape(depth, D_MODEL, B_Q_HEADS, HEAD_DIM)
    bq = jnp.stack([bq[:, :, :4], bq[:, :, 4:]], axis=3).reshape(depth, D_MODEL, 512)
    bkv = wi[:, :, 5120:5376]
    cdq = wi[:, :, 5376:5632]
    ckv = wi[:, :, 5632:5760]
    kr = wi[:, :, 5760:5792]
    kr_rot = jnp.concatenate([-kr[:, :, C_ROPE // 2:], kr[:, :, :C_ROPE // 2]], axis=2)
    gate = wi[:, :, 5792:]
    zeros = lambda n: jnp.zeros((depth, D_MODEL, n), wi.dtype)
    rest = jnp.concatenate([bq, cdq, ckv, kr, kr_rot, zeros(64), bkv, zeros(256)], axis=2)
    wp = jnp.stack([gate[:, :, :A_COLS], gate[:, :, A_COLS:], wi[:, :, :A_COLS], rest], axis=1)
    wa = []
    for g, (_, dil) in enumerate(A_GROUPS[1:], start=1):
        tn = A_COLS * 1024 // max(1024, BLOCK * dil)
        w = wi[:, :, g * A_COLS:(g + 1) * A_COLS]
        wa.append(w.reshape(depth, D_MODEL, A_COLS // tn, tn).transpose(0, 2, 1, 3))

    uq = (w_uq * ((C_NOPE + C_ROPE) ** -0.5 * LOG2E)).reshape(depth, C_Q_RANK, C_HEADS, C_NOPE + C_ROPE)
    zq = lambda n: jnp.zeros((depth, C_Q_RANK, C_HEADS, n), uq.dtype)
    half = C_ROPE // 2
    wq1 = jnp.concatenate([uq, zq(LANE - C_NOPE - C_ROPE)], axis=3)
    wq2 = jnp.concatenate([zq(C_NOPE), -uq[..., C_NOPE + half:], uq[..., C_NOPE:C_NOPE + half],
                           zq(LANE - C_NOPE - C_ROPE)], axis=3)
    ukv = w_ukv.reshape(depth, C_KV_RANK, C_HEADS, C_NOPE + C_V)
    zkv = lambda n: jnp.zeros((depth, C_KV_RANK, C_HEADS, n), ukv.dtype)
    wk = jnp.concatenate([ukv[..., :C_NOPE], zkv(LANE - C_NOPE)], axis=3)
    wv = jnp.concatenate([ukv[..., C_NOPE:], zkv(LANE - C_V)], axis=3)

    wb1 = w_branch[:, 1].reshape(depth, B_Q_HEADS, HEAD_DIM, D_MODEL)
    wb1 = jnp.stack([wb1[:, :4], wb1[:, 4:]], axis=2).reshape(depth, 512, D_MODEL)
    wb = jnp.stack([w_branch[:, 0], wb1, w_branch[:, 2]], axis=1)

    row = lambda a: a.reshape(depth, 1, -1)
    return dict(
        wp=wp, wa=wa,
        gq=row(q_norm_g), gkv=row(kv_norm_g),
        wq1=wq1.reshape(depth, C_Q_RANK, -1).astype(BF16), wq2=wq2.reshape(depth, C_Q_RANK, -1).astype(BF16),
        wk=wk.reshape(depth, C_KV_RANK, -1).astype(BF16), wv=wv.reshape(depth, C_KV_RANK, -1).astype(BF16),
        wb=wb.astype(BF16), wo=w_out.astype(BF16), bg=row(b_gate),
        ln1_g=row(ln1_g), ln1_b=row(ln1_b),
        wup=w_ffn_up.astype(BF16), cw=conv_w, cb=row(conv_b), wdn=w_ffn_down.astype(BF16),
        ln2_g=row(ln2_g), ln2_b=row(ln2_b),
    )


def _rope_selectors():
    sel = np.zeros((512, 2 * LANE), np.float32)
    for j in range(C_ROPE):
        sel[C_Q_RANK + C_KV_RANK + j, C_NOPE + j] = 1.0
        sel[C_Q_RANK + C_KV_RANK + C_ROPE + j, LANE + C_NOPE + j] = 1.0
    return jnp.asarray(sel, BF16)


def _rope_tables(seq):
    pos = jnp.arange(seq, dtype=F32)
    inv_freq = ROPE_BASE ** (-jnp.arange(0, C_ROPE, 2, dtype=F32) / C_ROPE)
    ang = pos[:, None] * inv_freq[None, :]
    cos, sin = jnp.cos(ang), jnp.sin(ang)
    pad = jnp.zeros((seq, LANE - C_NOPE - C_ROPE), F32)
    cos_t = jnp.concatenate([jnp.ones((seq, C_NOPE), F32), cos, cos, pad], axis=1)
    sin_t = jnp.concatenate([jnp.zeros((seq, C_NOPE), F32), sin, sin, pad], axis=1)
    return cos_t, sin_t


def _expand_matrix():
    e = np.zeros((LANE, A_HEADS * HEAD_DIM), np.float32)
    for c in range(A_HEADS * HEAD_DIM):
        e[(LANE // A_HEADS) * (c // HEAD_DIM), c] = 1.0
    return jnp.asarray(np.concatenate([e, e], axis=0), BF16)


def kernel(x, rel_table, w_in, b_gate, sinks, q_norm_g, kv_norm_g, w_uq, w_ukv, w_branch, w_out,
           ln1_g, ln1_b, w_ffn_up, conv_w, conv_b, w_ffn_down, ln2_g, ln2_b):
    batch, seq, d = x.shape
    depth = w_in.shape[0]
    alpha = (2 * depth) ** 0.25
    cos_t, sin_t = _rope_tables(seq)
    sel = _rope_selectors()
    expand = _expand_matrix()
    bias = _bias_tables(rel_table)
    lw = _prepare_weights(w_in, b_gate, q_norm_g, kv_norm_g, w_uq, w_ukv, w_branch, w_out,
                          ln1_g, ln1_b, w_ffn_up, conv_w, conv_b, w_ffn_down, ln2_g, ln2_b)
    x2d = x.reshape(batch * seq, d)
    for l in range(depth):
        p2 = _proj(x2d, lw["wp"], l, dil=1, tm=1024, name="proj")
        p_dil = [_proj(x2d, w, l, dil=dil, tm=max(1024, BLOCK * dil), name=f"proj_a{g + 1}")
                 for g, (w, (_, dil)) in enumerate(zip(lw["wa"], A_GROUPS[1:]))]
        sink_slots = sinks[l].reshape(2, 4).T.reshape(-1) * LOG2E
        oa, lse = [], []
        for kind in range(len(A_GROUPS)):
            src = p2 if kind == 0 else p_dil[kind - 1]
            o_g, lse_g = _band_attention(src, bias, sink_slots, kind=kind, batch=batch, seq=seq)
            oa.append(o_g)
            lse.append(lse_g)
        ob = _band_attention(p2, bias, sink_slots, kind=len(A_GROUPS), batch=batch, seq=seq)
        q, k, v = _mla_prep(p2, cos_t, sin_t, sel, lw, l, seq=seq)
        oc = _mla_flash(q, k, v, batch=batch, seq=seq)
        x2d = _merge(oa, lse, ob, oc, p2, x2d, expand, lw, l, alpha=alpha)
        x2d = _ffn(x2d, lw, l, alpha=alpha, seq=seq)
    return x2d.reshape(batch, seq, d)
```

```python
import functools
import math

import jax
import jax.numpy as jnp
import numpy as np
from jax import lax
from jax.experimental import pallas as pl
from jax.experimental.pallas import tpu as pltpu

F32 = jnp.float32
BF16 = jnp.bfloat16

D_MODEL = 1024
HEAD_DIM = 64
BLOCK = 128
A_GROUPS = ((128, 1), (512, 4), (2048, 16))
A_HEADS = 8
B_Q_HEADS = 8
B_WINDOW = 128
C_HEADS = 8
C_Q_RANK = 256
C_KV_RANK = 128
C_NOPE = 64
C_ROPE = 32
C_V = 64
ROPE_BASE = 10000.0
REL_BUCKETS = 32
REL_MAX_DIST = 2048
D_FF = 2816
LN_EPS = 1e-5
RMS_EPS = 1e-6
NEG = -1e30
LOG2E = math.log2(math.e)
FLASH_TQ = 1024
FLASH_TK = 1024
FLASH_HEADS = 8
FLASH_DIAG = 512

COL_GATE = 0
COL_A = 3072
COL_BQ = 4608
COL_C = 5120
COL_BKV = 5632
NP = 6144
A_COLS = 1536

LANE = 128
SUBLANES = 8
VMEM_LIMIT = 56 * 1024 * 1024

N_KINDS = 4


def _cparams(n_axes):
    return pltpu.CompilerParams(dimension_semantics=("arbitrary",) * n_axes,
                                vmem_limit_bytes=VMEM_LIMIT)


def _layer_block(a, layer, **kwargs):
    zeros = (0,) * (a.ndim - 1)
    return pl.BlockSpec((None,) + a.shape[1:], lambda *_: (layer,) + zeros, **kwargs)


CHEAP_STRIDE = 4


def _proj_kernel(*refs, dil):
    n_x = len(refs) - (4 if dil > CHEAP_STRIDE else 3)
    x_refs, (w_ref, o_ref, xb_ref, *tmp) = refs[:n_x], refs[n_x:]

    @pl.when(pl.program_id(1) == 0)
    def _():
        if dil == 1:
            xb_ref[...] = x_refs[0][...].astype(BF16)
            return
        span = BLOCK * dil
        d1 = min(dil, CHEAP_STRIDE)
        d2 = dil // d1
        for c, x_ref in enumerate(x_refs):
            for s0 in range(0, xb_ref.shape[0], span):
                for r1 in range(d1):
                    if d2 == 1:
                        rows = x_ref[pl.ds(s0 + r1, BLOCK, stride=d1), :]
                        xb_ref[s0 + r1 * BLOCK:s0 + (r1 + 1) * BLOCK, c * LANE:(c + 1) * LANE] = rows.astype(BF16)
                        continue
                    g0 = s0 + r1 * (span // d1)
                    tmp[0][c, g0:g0 + span // d1, :] = x_ref[pl.ds(s0 + r1, span // d1, stride=d1), :]
                    for r2 in range(d2):
                        r = r1 + d1 * r2
                        rows = tmp[0][c, pl.ds(g0 + r2, BLOCK, stride=d2), :]
                        xb_ref[s0 + r * BLOCK:s0 + (r + 1) * BLOCK, c * LANE:(c + 1) * LANE] = rows.astype(BF16)

    w = w_ref[pl.program_id(1)]
    o_ref[...] = jnp.dot(xb_ref[...], w, preferred_element_type=F32).astype(o_ref.dtype)


def _proj(x2d, w_tiles, layer, *, dil, tm, name):
    t, k = x2d.shape
    n_tiles, tn = w_tiles.shape[1], w_tiles.shape[3]
    n = n_tiles * tn
    if dil == 1:
        x_specs = [pl.BlockSpec((tm, k), lambda i, j: (i, 0))]
    else:
        x_specs = [pl.BlockSpec((tm, LANE), lambda i, j, c=c: (i, c)) for c in range(k // LANE)]
    scratch = [pltpu.VMEM((tm, k), BF16)]
    if dil > CHEAP_STRIDE:
        scratch.append(pltpu.VMEM((k // LANE, tm, LANE), F32))
    return pl.pallas_call(
        functools.partial(_proj_kernel, dil=dil),
        grid=(t // tm, n_tiles),
        in_specs=x_specs + [_layer_block(w_tiles, layer, pipeline_mode=pl.Buffered(1))],
        out_specs=pl.BlockSpec((tm, tn), lambda i, j: (i, j)),
        out_shape=jax.ShapeDtypeStruct((t, n), BF16),
        scratch_shapes=scratch,
        compiler_params=_cparams(2),
        name=name,
    )(*([x2d] * len(x_specs)), w_tiles)


def _t5_bucket(dist):
    n = jnp.maximum(dist, 0)
    max_exact = REL_BUCKETS // 2
    scaled = jnp.log(jnp.maximum(n, 1).astype(F32) / max_exact) / math.log(REL_MAX_DIST / max_exact)
    large = max_exact + (scaled * (REL_BUCKETS - max_exact)).astype(jnp.int32)
    return jnp.where(n < max_exact, n, jnp.minimum(large, REL_BUCKETS - 1))


def _bias_codes():
    qi = jnp.arange(BLOCK)[:, None]
    ki = jnp.arange(2 * BLOCK)[None, :]
    step = BLOCK + qi - ki
    has_prev = ki >= BLOCK
    codes = []
    for kind in range(N_KINDS):
        if kind < len(A_GROUPS):
            dil = A_GROUPS[kind][1]
            band = (step >= 0) & (step <= BLOCK)
        else:
            dil = 1
            band = (step >= 0) & (step < B_WINDOW)
        bucket = _t5_bucket(step * dil)
        codes.append(jnp.stack([jnp.where(band & has_prev, bucket, -1),
                                jnp.where(band, bucket, -1)]))
    return jnp.stack(codes).astype(jnp.int32)


def _bias_kernel(rel_ref, code_ref, o_ref):
    kind = pl.program_id(0)
    code = code_ref[0, 0]
    for slot in range(A_HEADS):
        b_head = len(A_GROUPS) * A_HEADS + slot // 2 + 4 * (slot % 2)
        col = jnp.where(kind < len(A_GROUPS), kind * A_HEADS + slot, b_head)
        acc = jnp.full(code.shape, NEG, F32)
        for b in range(REL_BUCKETS):
            acc = jnp.where(code == b, rel_ref[b, col] * LOG2E, acc)
        o_ref[0, 0, slot] = acc


def _bias_tables(rel_table):
    codes = _bias_codes()
    return pl.pallas_call(
        _bias_kernel,
        grid=(N_KINDS, 2),
        in_specs=[pl.BlockSpec(memory_space=pltpu.SMEM),
                  pl.BlockSpec((1, 1, BLOCK, 2 * BLOCK), lambda k, v: (k, v, 0, 0))],
        out_specs=pl.BlockSpec((1, 1, A_HEADS, BLOCK, 2 * BLOCK), lambda k, v: (k, v, 0, 0, 0)),
        out_shape=jax.ShapeDtypeStruct((N_KINDS, 2, A_HEADS, BLOCK, 2 * BLOCK), F32),
        compiler_params=_cparams(2),
        name="bias_tables",
    )(rel_table, codes)


BAND_QB = 32
STAT_LANES = LANE // A_HEADS // 2


def _band_kernel(sink_ref, q_ref, kp_ref, kc_ref, vp_ref, vc_ref, bias_ref, o_ref, *lse_refs,
                 shared_kv, with_sink):
    first = pl.program_id(2) == 0
    lane = lax.broadcasted_iota(jnp.int32, (BLOCK, LANE), 1)
    low = lane < HEAD_DIM
    n_qb, n_res = q_ref.shape[:2]
    for res, qb in ((res, qb) for res in range(n_res) for qb in range(n_qb)):
        variant = jnp.where(first, 0, 1) if qb == 0 else 1
        lse_tile = jnp.zeros((BLOCK, LANE), F32)
        for pair in range(4):
            cols = slice(pair * LANE, (pair + 1) * LANE)
            kv_cols = slice(0, LANE) if shared_kv else cols
            qp = q_ref[qb, res, :, cols]
            k_prev = kp_ref[res, :, kv_cols] if qb == 0 else kc_ref[qb - 1, res, :, kv_cols]
            v_prev = vp_ref[res, :, kv_cols] if qb == 0 else vc_ref[qb - 1, res, :, kv_cols]
            kcat = jnp.concatenate([k_prev, kc_ref[qb, res, :, kv_cols]], axis=0)
            vcat = jnp.concatenate([v_prev, vc_ref[qb, res, :, kv_cols]], axis=0)
            zero = jnp.zeros_like(qp)
            q2 = jnp.concatenate([jnp.where(low, qp, zero), jnp.where(low, zero, qp)], axis=0)
            s = lax.dot_general(q2, kcat, (((1,), (1,)), ((), ())), preferred_element_type=F32)
            s = s + bias_ref[variant, pair]
            m = jnp.max(s, axis=-1, keepdims=True)
            if with_sink:
                head_a = lax.broadcasted_iota(jnp.int32, (2 * BLOCK, 1), 0) < BLOCK
                sink = jnp.where(head_a, sink_ref[2 * pair], sink_ref[2 * pair + 1])
                m = jnp.maximum(m, sink)
            p = jnp.exp2(s - m)
            l = jnp.sum(p, axis=-1, keepdims=True)
            if with_sink:
                l = l + jnp.exp2(sink - m)
            o = jnp.dot(p.astype(BF16), vcat, preferred_element_type=F32)
            if lse_refs:
                for sub in range(2):
                    rows = slice(sub * BLOCK, (sub + 1) * BLOCK)
                    field = 2 * (2 * pair + sub)
                    lse_tile = jnp.where(lane // STAT_LANES == field, m[rows],
                                         jnp.where(lane // STAT_LANES == field + 1, l[rows], lse_tile))
            else:
                o = o * (1.0 / l)
            o_ref[qb, res, :, cols] = jnp.where(low, o[:BLOCK], o[BLOCK:]).astype(o_ref.dtype)
        if lse_refs:
            lse_refs[0][qb, res] = lse_tile


def _band_attention(p2, bias, sinks, *, kind, batch, seq):
    is_b = kind == len(A_GROUPS)
    dil = 1 if is_b else A_GROUPS[kind][1]
    n_span = seq // (BLOCK * dil)
    view = p2.reshape(batch, n_span, dil, BLOCK, p2.shape[-1])
    if is_b:
        q_blk, k_blk, v_blk, kv_w = COL_BQ // 512, COL_BKV // LANE, COL_BKV // LANE + 1, LANE
    else:
        q_blk = COL_A // 512 if dil == 1 else 0
        k_blk, v_blk, kv_w = q_blk + 1, q_blk + 2, 512

    n_qb = min(BAND_QB, n_span)
    n_res = BAND_QB // n_qb

    def cur(blk, width):
        return pl.BlockSpec((None, n_qb, n_res, BLOCK, width), lambda b, r, n: (b, n, r, 0, blk))

    def prev(blk, width):
        return pl.BlockSpec((None, None, n_res, BLOCK, width),
                            lambda b, r, n: (b, jnp.maximum(n * n_qb - 1, 0), r, 0, blk))

    in_specs = [
        pl.BlockSpec(memory_space=pltpu.SMEM),
        cur(q_blk, 512), prev(k_blk, kv_w), cur(k_blk, kv_w), prev(v_blk, kv_w), cur(v_blk, kv_w),
        pl.BlockSpec((None, 2, A_HEADS // 2, 2 * BLOCK, 2 * BLOCK), lambda b, r, n: (kind, 0, 0, 0, 0)),
    ]
    bias = bias.reshape(N_KINDS, 2, A_HEADS // 2, 2 * BLOCK, 2 * BLOCK)
    out_specs = [pl.BlockSpec((None, n_qb, n_res, BLOCK, 512), lambda b, r, n: (b, n, r, 0, 0))]
    out_shape = [jax.ShapeDtypeStruct((batch, n_span, dil, BLOCK, 512), BF16)]
    if not is_b:
        out_specs.append(pl.BlockSpec((None, n_qb, n_res, BLOCK, LANE), lambda b, r, n: (b, n, r, 0, 0)))
        out_shape.append(jax.ShapeDtypeStruct((batch, n_span, dil, BLOCK, LANE), F32))
    outs = pl.pallas_call(
        functools.partial(_band_kernel, shared_kv=is_b, with_sink=is_b),
        grid=(batch, dil // n_res, n_span // n_qb),
        in_specs=in_specs,
        out_specs=out_specs,
        out_shape=out_shape,
        compiler_params=_cparams(3),
        name="band_b" if is_b else f"band_a{kind}",
    )(sinks, view, view, view, view, view, bias)
    return outs[0] if is_b else outs


def _mla_prep_kernel(c_ref, cos_ref, sin_ref, gq_ref, gkv_ref, wq1_ref, wq2_ref, wk_ref, wv_ref,
                     sel_ref, q_ref, k_ref, v_ref):
    c = c_ref[...]
    cq = c[:, :C_Q_RANK].astype(F32)
    ckv = c[:, C_Q_RANK:C_Q_RANK + C_KV_RANK].astype(F32)
    nq = cq * lax.rsqrt(jnp.mean(cq * cq, axis=-1, keepdims=True) + RMS_EPS) * gq_ref[...]
    nkv = ckv * lax.rsqrt(jnp.mean(ckv * ckv, axis=-1, keepdims=True) + RMS_EPS) * gkv_ref[...]
    nq = nq.astype(BF16)
    nkv = nkv.astype(BF16)
    cos = cos_ref[...]
    sin = sin_ref[...]
    cos8 = jnp.tile(cos, (1, C_HEADS))
    sin8 = jnp.tile(sin, (1, C_HEADS))
    q = (jnp.dot(nq, wq1_ref[...], preferred_element_type=F32) * cos8
         + jnp.dot(nq, wq2_ref[...], preferred_element_type=F32) * sin8)
    q_ref[...] = q.astype(q_ref.dtype)
    picked = jnp.dot(c, sel_ref[...], preferred_element_type=F32)
    kr = picked[:, :LANE] * cos + picked[:, LANE:] * sin
    k = jnp.dot(nkv, wk_ref[...], preferred_element_type=F32) + jnp.tile(kr, (1, C_HEADS))
    k_ref[...] = k.astype(k_ref.dtype)
    lane = lax.broadcasted_iota(jnp.int32, (1, C_HEADS * LANE), 1)
    ones_col = jnp.where(lane % LANE == C_V, 1.0, 0.0).astype(F32)
    v_ref[...] = (jnp.dot(nkv, wv_ref[...], preferred_element_type=F32) + ones_col).astype(v_ref.dtype)


def _mla_prep(p2, cos_t, sin_t, sel, lw, layer, *, seq, tm=1024):
    t = p2.shape[0]
    per_seq = seq // tm
    full = lambda a: pl.BlockSpec(a.shape, lambda i: (0,) * a.ndim)
    stacked = (lw["gq"], lw["gkv"], lw["wq1"], lw["wq2"], lw["wk"], lw["wv"])
    wide = C_HEADS * LANE
    return pl.pallas_call(
        _mla_prep_kernel,
        grid=(t // tm,),
        in_specs=[pl.BlockSpec((tm, 512), lambda i: (i, COL_C // 512)),
                  pl.BlockSpec((tm, LANE), lambda i: (i % per_seq, 0)),
                  pl.BlockSpec((tm, LANE), lambda i: (i % per_seq, 0))]
                 + [_layer_block(a, layer) for a in stacked] + [full(sel)],
        out_specs=[pl.BlockSpec((tm, wide), lambda i: (i, 0))] * 3,
        out_shape=[jax.ShapeDtypeStruct((t, wide), BF16)] * 3,
        compiler_params=_cparams(1),
        name="mla_prep",
    )(p2, cos_t, sin_t, *stacked, sel)


def _mla_flash_kernel(qi_ref, ki_ref, q_ref, k_ref, v_ref, o_ref, m_ref, acc_ref, *, tq, tk, heads):
    step = pl.program_id(2)
    qi = qi_ref[step]
    ki = ki_ref[step]
    ratio = tq // tk

    @pl.when(ki == 0)
    def _():
        m_ref[...] = jnp.full(m_ref.shape, NEG, F32)
        acc_ref[...] = jnp.zeros(acc_ref.shape, F32)

    def update(h, rows, n_keys, shift):
        hc = slice(h * LANE, (h + 1) * LANE)
        s = lax.dot_general(q_ref[rows, hc], k_ref[:n_keys, hc], (((1,), (1,)), ((), ())),
                            preferred_element_type=F32)
        if shift is not None:
            row = lax.broadcasted_iota(jnp.int32, s.shape, 0) + shift
            col = lax.broadcasted_iota(jnp.int32, s.shape, 1)
            s = jnp.where(row >= col, s, NEG)
        m_prev = m_ref[h, rows]
        m_new = jnp.maximum(m_prev, jnp.max(s, axis=-1, keepdims=True))
        alpha = jnp.exp2(m_prev - m_new)
        p = jnp.exp2(s - jnp.tile(m_new, (1, n_keys // LANE)))
        acc_ref[h, rows] = alpha * acc_ref[h, rows] + jnp.dot(p.astype(BF16), v_ref[:n_keys, hc],
                                                              preferred_element_type=F32)
        m_ref[h, rows] = m_new

    @pl.when(ki < qi * ratio)
    def _():
        for h in range(heads):
            update(h, slice(0, tq), tk, None)

    for d in range(ratio):
        @pl.when(ki == qi * ratio + d)
        def _(d=d):
            for h in range(heads):
                if (d + 1) * tk < tq:
                    update(h, slice((d + 1) * tk, tq), tk, None)
                for r0 in range(0, tk, FLASH_DIAG):
                    update(h, slice(d * tk + r0, d * tk + r0 + FLASH_DIAG), r0 + FLASH_DIAG, r0)

    @pl.when(ki == (qi + 1) * ratio - 1)
    def _():
        low = lax.broadcasted_iota(jnp.int32, (tq, LANE), 1) < C_V
        for pair in range(heads // 2):
            halves = []
            for h in (2 * pair, 2 * pair + 1):
                acc = acc_ref[h]
                halves.append(acc * (1.0 / acc[:, C_V:C_V + 1]))
            odd = pltpu.roll(halves[1], C_V, axis=1)
            o_ref[:, pair * LANE:(pair + 1) * LANE] = jnp.where(low, halves[0], odd).astype(o_ref.dtype)


def _mla_flash(q, k, v, *, batch, seq, tq=FLASH_TQ, tk=FLASH_TK, heads=FLASH_HEADS):
    wide = C_HEADS * LANE
    ratio = tq // tk
    q3 = q.reshape(batch, seq, wide)
    k3 = k.reshape(batch, seq, wide)
    v3 = v.reshape(batch, seq, wide)
    pairs = [(i, j) for i in range(seq // tq) for j in range((i + 1) * ratio)]
    qi_of = jnp.asarray([i for i, _ in pairs], jnp.int32)
    ki_of = jnp.asarray([j for _, j in pairs], jnp.int32)
    q_map = lambda b, g, t, qi, ki: (b, qi[t], g)
    kv_map = lambda b, g, t, qi, ki: (b, ki[t], g)
    out = pl.pallas_call(
        functools.partial(_mla_flash_kernel, tq=tq, tk=tk, heads=heads),
        grid_spec=pltpu.PrefetchScalarGridSpec(
            num_scalar_prefetch=2,
            grid=(batch, C_HEADS // heads, len(pairs)),
            in_specs=[pl.BlockSpec((None, tq, heads * LANE), q_map),
                      pl.BlockSpec((None, tk, heads * LANE), kv_map),
                      pl.BlockSpec((None, tk, heads * LANE), kv_map)],
            out_specs=pl.BlockSpec((None, tq, heads * C_V), q_map),
            scratch_shapes=[pltpu.VMEM((heads, tq, LANE), F32),
                            pltpu.VMEM((heads, tq, LANE), F32)]),
        out_shape=jax.ShapeDtypeStruct((batch, seq, C_HEADS * C_V), BF16),
        compiler_params=_cparams(3),
        name="mla_flash",
    )(qi_of, ki_of, q3, k3, v3)
    return out.reshape(batch * seq, C_HEADS * C_V)


def _layer_norm(y, g, b):
    mu = jnp.mean(y, axis=-1, keepdims=True)
    d = y - mu
    var = jnp.mean(d * d, axis=-1, keepdims=True)
    return d * lax.rsqrt(var + LN_EPS) * g + b


def _token_order(src_ref, tmp_ref, dil):
    if dil == 1:
        return src_ref[...].astype(F32)
    n = src_ref.shape[1]
    n_slab = src_ref.shape[2] // LANE
    for r in range(dil):
        rows = src_ref[r].astype(F32)
        for c in range(n_slab):
            tmp_ref[c, pl.ds(r, n, stride=dil), :] = rows[:, c * LANE:(c + 1) * LANE]
    return jnp.concatenate([tmp_ref[c] for c in range(n_slab)], axis=1)


def _merge_kernel(oa0_ref, oa1_ref, oa2_ref, l0_ref, l1_ref, l2_ref, ob_ref, oc_ref, gate_ref, x_ref,
                  e_ref, wb_ref, wo_ref, bg_ref, g_ref, b_ref, o_ref, ot1_ref, ot2_ref, lt1_ref, lt2_ref,
                  *, alpha):
    dils = [d for _, d in A_GROUPS]
    stats = [_token_order(ref, tmp, d)
             for ref, tmp, d in zip((l0_ref, l1_ref, l2_ref), (None, lt1_ref, lt2_ref), dils)]
    is_max = lax.broadcasted_iota(jnp.int32, stats[0].shape, 1) // STAT_LANES % 2 == 0
    top = jnp.maximum(jnp.maximum(stats[0], stats[1]), stats[2])
    es = [jnp.exp2(v - top) for v in stats]
    sums = [pltpu.roll(v, LANE - STAT_LANES, axis=1) for v in stats]
    inv = 1.0 / (es[0] * sums[0] + es[1] * sums[1] + es[2] * sums[2])
    o_a = None
    for e, oa_ref, tmp, d in zip(es, (oa0_ref, oa1_ref, oa2_ref), (None, ot1_ref, ot2_ref), dils):
        w = jnp.where(is_max, e * inv, 0.0)
        hi = w.astype(BF16)
        lo = (w - hi.astype(F32)).astype(BF16)
        wide = jnp.dot(jnp.concatenate([hi, lo], axis=1), e_ref[...], preferred_element_type=F32)
        term = wide * _token_order(oa_ref, tmp, d)
        o_a = term if o_a is None else o_a + term
    branches = (o_a.astype(BF16), ob_ref[...], oc_ref[...])
    merged = None
    for i, br in enumerate(branches):
        gate = jax.nn.sigmoid(gate_ref[:, i * D_MODEL:(i + 1) * D_MODEL].astype(F32)
                              + bg_ref[:, i * D_MODEL:(i + 1) * D_MODEL])
        term = gate * jnp.dot(br, wb_ref[i], preferred_element_type=F32)
        merged = term if merged is None else merged + term
    mix = jnp.dot(merged.astype(BF16), wo_ref[...], preferred_element_type=F32)
    o_ref[...] = _layer_norm(alpha * x_ref[...] + mix, g_ref[...], b_ref[...])


def _merge(oa, lse, ob, oc, p2, x2d, expand, lw, layer, *, alpha, tm=512):
    t = x2d.shape[0]
    row = lambda w: pl.BlockSpec((tm, w), lambda i: (i, 0))
    full = lambda a: pl.BlockSpec(a.shape, lambda i: (0,) * a.ndim)

    def grouped(arrs):
        views, specs = [], []
        for a, (_, dil) in zip(arrs, A_GROUPS):
            w = a.shape[-1]
            if dil == 1:
                views.append(a.reshape(t, w))
                specs.append(row(w))
                continue
            per_span = BLOCK * dil // tm
            views.append(a.reshape(-1, dil, BLOCK, w))
            specs.append(pl.BlockSpec((None, dil, tm // dil, w),
                                      lambda i, per_span=per_span: (i // per_span, 0, i % per_span, 0)))
        return views, specs

    oa_v, oa_s = grouped(oa)
    lse_v, lse_s = grouped(lse)
    stacked = (lw["wb"], lw["wo"], lw["bg"], lw["ln1_g"], lw["ln1_b"])
    slabs = lambda w: pltpu.VMEM((w // LANE, tm, LANE), F32)
    return pl.pallas_call(
        functools.partial(_merge_kernel, alpha=alpha),
        grid=(t // tm,),
        in_specs=oa_s + lse_s + [row(512), row(512),
                  pl.BlockSpec((tm, 3 * D_MODEL), lambda i: (i, COL_GATE // (3 * D_MODEL))),
                  row(D_MODEL), full(expand)] + [_layer_block(a, layer) for a in stacked],
        out_specs=row(D_MODEL),
        out_shape=jax.ShapeDtypeStruct((t, D_MODEL), F32),
        scratch_shapes=[slabs(512), slabs(512), slabs(LANE), slabs(LANE)],
        compiler_params=_cparams(1),
        name="merge",
    )(*oa_v, *lse_v, ob.reshape(t, 512), oc, p2, x2d, expand, *stacked)


HALO = 8
FF_CHUNK = 1024


def _ffn_kernel(halo_ref, *refs, alpha, tm, per_seq):
    n_slab = D_MODEL // LANE
    x_refs = refs[:n_slab]
    wup_ref, cw_ref, cb_ref, wdn_ref, g_ref, b_ref, o_ref, xs_ref, os_ref = refs[n_slab:]
    rows = tm + HALO
    groups = rows // SUBLANES
    i = pl.program_id(0)
    halo = jnp.where(i % per_seq == 0, jnp.zeros_like(halo_ref[...]), halo_ref[...])
    for c, x_ref in enumerate(x_refs):
        xs_ref[c, :HALO, :] = halo[:, c * LANE:(c + 1) * LANE]
        xs_ref[c, HALO:, :] = x_ref[...]
    xp = jnp.concatenate(
        [jnp.concatenate([xs_ref[c, pl.ds(j, SUBLANES, stride=groups), :] for j in range(groups)], axis=0)
         for c in range(n_slab)], axis=1)
    xh = xp.astype(BF16)
    chunks = [(c0, min(FF_CHUNK, D_FF - c0)) for c0 in range(0, D_FF, FF_CHUNK)]

    def up(c0, cf):
        return [jnp.dot(xh, wup_ref[:, base:base + cf], preferred_element_type=F32)
                for base in (c0, D_FF + c0)]

    def conv(u, base, cf):
        u3 = u.reshape(groups, SUBLANES, cf)
        wrap = pltpu.roll(u3[groups - 2:], 1, axis=1)
        prev1 = jnp.concatenate([wrap[1:], u3[:groups - 1]], axis=0)
        prev2 = jnp.concatenate([wrap, u3[:groups - 2]], axis=0)
        tap = lambda k: cw_ref[k:k + 1, base:base + cf]
        return cb_ref[:, base:base + cf] + prev2 * tap(0) + prev1 * tap(1) + u3 * tap(2)

    acc = None
    u_next = up(*chunks[0])
    for ci, (c0, cf) in enumerate(chunks):
        u_cur = u_next
        if ci + 1 < len(chunks):
            u_next = up(*chunks[ci + 1])
        gate, val = (conv(u, base, cf) for u, base in zip(u_cur, (c0, D_FF + c0)))
        act = (gate * jax.nn.sigmoid(gate) * val).reshape(rows, cf).astype(BF16)
        term = jnp.dot(act, wdn_ref[c0:c0 + cf, :], preferred_element_type=F32)
        acc = term if acc is None else acc + term
    out = _layer_norm(alpha * xp + acc, g_ref[...], b_ref[...])
    for c in range(n_slab):
        for j in range(groups):
            os_ref[c, pl.ds(j, SUBLANES, stride=groups), :] = out[j * SUBLANES:(j + 1) * SUBLANES,
                                                                  c * LANE:(c + 1) * LANE]
        o_ref[:, c * LANE:(c + 1) * LANE] = os_ref[c, HALO:, :]


def _ffn(x2d, lw, layer, *, alpha, seq, tm=512):
    t = x2d.shape[0]
    per_seq = seq // tm
    n_slab = D_MODEL // LANE
    consts = (lw["wup"], lw["cw"], lw["cb"], lw["wdn"], lw["ln2_g"], lw["ln2_b"])
    return pl.pallas_call(
        functools.partial(_ffn_kernel, alpha=alpha, tm=tm, per_seq=per_seq),
        grid=(t // tm,),
        in_specs=[pl.BlockSpec((HALO, D_MODEL), lambda i: (jnp.maximum(i * (tm // HALO) - 1, 0), 0))]
                 + [pl.BlockSpec((tm, LANE), lambda i, c=c: (i, c)) for c in range(n_slab)]
                 + [_layer_block(a, layer, pipeline_mode=pl.Buffered(1)) for a in consts],
        out_specs=pl.BlockSpec((tm, D_MODEL), lambda i: (i, 0)),
        out_shape=jax.ShapeDtypeStruct((t, D_MODEL), F32),
        scratch_shapes=[pltpu.VMEM((n_slab, tm + HALO, LANE), F32),
                        pltpu.VMEM((n_slab, tm + HALO, LANE), F32)],
        compiler_params=_cparams(1),
        name="ffn",
    )(x2d, *([x2d] * n_slab), *consts)


def _prepare_weights(w_in, b_gate, q_norm_g, kv_norm_g, w_uq, w_ukv, w_branch, w_out,
                     ln1_g, ln1_b, w_ffn_up, conv_w, conv_b, w_ffn_down, ln2_g, ln2_b):
    depth = w_in.shape[0]
    a_end = 4608
    col = np.arange(w_in.shape[-1])
    is_q = ((col < a_end) & (col % A_COLS < A_HEADS * HEAD_DIM)) | ((col >= a_end) & (col < a_end + 512))
    col_scale = jnp.asarray(np.where(is_q, HEAD_DIM ** -0.5 * LOG2E, 1.0), F32)
    wi = (w_in * col_scale).astype(BF16)
    bq = wi[:, :, a_end:a_end + 512].reshape(depth, D_MODEL, B_Q_HEADS, HEAD_DIM)
    bq = jnp.stack([bq[:, :, :4], bq[:, :, 4:]], axis=3).reshape(depth, D_MODEL, 512)
    bkv = wi[:, :, 5120:5376]
    cdq = wi[:, :, 5376:5632]
    ckv = wi[:, :, 5632:5760]
    kr = wi[:, :, 5760:5792]
    kr_rot = jnp.concatenate([-kr[:, :, C_ROPE // 2:], kr[:, :, :C_ROPE // 2]], axis=2)
    gate = wi[:, :, 5792:]
    zeros = lambda n: jnp.zeros((depth, D_MODEL, n), wi.dtype)
    rest = jnp.concatenate([wi[:, :, :A_COLS], bq, cdq, ckv, kr, kr_rot, zeros(64), bkv, zeros(256)], axis=2)
    wp = jnp.stack([gate, rest], axis=1)
    wa = []
    for g, (_, dil) in enumerate(A_GROUPS[1:], start=1):
        tn = A_COLS * 1024 // max(1024, BLOCK * dil)
        w = wi[:, :, g * A_COLS:(g + 1) * A_COLS]
        wa.append(w.reshape(depth, D_MODEL, A_COLS // tn, tn).transpose(0, 2, 1, 3))

    uq = (w_uq * ((C_NOPE + C_ROPE) ** -0.5 * LOG2E)).reshape(depth, C_Q_RANK, C_HEADS, C_NOPE + C_ROPE)
    zq = lambda n: jnp.zeros((depth, C_Q_RANK, C_HEADS, n), uq.dtype)
    half = C_ROPE // 2
    wq1 = jnp.concatenate([uq, zq(LANE - C_NOPE - C_ROPE)], axis=3)
    wq2 = jnp.concatenate([zq(C_NOPE), -uq[..., C_NOPE + half:], uq[..., C_NOPE:C_NOPE + half],
                           zq(LANE - C_NOPE - C_ROPE)], axis=3)
    ukv = w_ukv.reshape(depth, C_KV_RANK, C_HEADS, C_NOPE + C_V)
    zkv = lambda n: jnp.zeros((depth, C_KV_RANK, C_HEADS, n), ukv.dtype)
    wk = jnp.concatenate([ukv[..., :C_NOPE], zkv(LANE - C_NOPE)], axis=3)
    wv = jnp.concatenate([ukv[..., C_NOPE:], zkv(LANE - C_V)], axis=3)

    wb1 = w_branch[:, 1].reshape(depth, B_Q_HEADS, HEAD_DIM, D_MODEL)
    wb1 = jnp.stack([wb1[:, :4], wb1[:, 4:]], axis=2).reshape(depth, 512, D_MODEL)
    wb = jnp.stack([w_branch[:, 0], wb1, w_branch[:, 2]], axis=1)

    row = lambda a: a.reshape(depth, 1, -1)
    return dict(
        wp=wp, wa=wa,
        gq=row(q_norm_g), gkv=row(kv_norm_g),
        wq1=wq1.reshape(depth, C_Q_RANK, -1).astype(BF16), wq2=wq2.reshape(depth, C_Q_RANK, -1).astype(BF16),
        wk=wk.reshape(depth, C_KV_RANK, -1).astype(BF16), wv=wv.reshape(depth, C_KV_RANK, -1).astype(BF16),
        wb=wb.astype(BF16), wo=w_out.astype(BF16), bg=row(b_gate),
        ln1_g=row(ln1_g), ln1_b=row(ln1_b),
        wup=w_ffn_up.astype(BF16), cw=conv_w, cb=row(conv_b), wdn=w_ffn_down.astype(BF16),
        ln2_g=row(ln2_g), ln2_b=row(ln2_b),
    )


def _rope_selectors():
    sel = np.zeros((512, 2 * LANE), np.float32)
    for j in range(C_ROPE):
        sel[C_Q_RANK + C_KV_RANK + j, C_NOPE + j] = 1.0
        sel[C_Q_RANK + C_KV_RANK + C_ROPE + j, LANE + C_NOPE + j] = 1.0
    return jnp.asarray(sel, BF16)


def _rope_tables(seq):
    pos = jnp.arange(seq, dtype=F32)
    inv_freq = ROPE_BASE ** (-jnp.arange(0, C_ROPE, 2, dtype=F32) / C_ROPE)
    ang = pos[:, None] * inv_freq[None, :]
    cos, sin = jnp.cos(ang), jnp.sin(ang)
    pad = jnp.zeros((seq, LANE - C_NOPE - C_ROPE), F32)
    cos_t = jnp.concatenate([jnp.ones((seq, C_NOPE), F32), cos, cos, pad], axis=1)
    sin_t = jnp.concatenate([jnp.zeros((seq, C_NOPE), F32), sin, sin, pad], axis=1)
    return cos_t, sin_t


def _expand_matrix():
    e = np.zeros((LANE, A_HEADS * HEAD_DIM), np.float32)
    for c in range(A_HEADS * HEAD_DIM):
        e[(LANE // A_HEADS) * (c // HEAD_DIM), c] = 1.0
    return jnp.asarray(np.concatenate([e, e], axis=0), BF16)


def kernel(x, rel_table, w_in, b_gate, sinks, q_norm_g, kv_norm_g, w_uq, w_ukv, w_branch, w_out,
           ln1_g, ln1_b, w_ffn_up, conv_w, conv_b, w_ffn_down, ln2_g, ln2_b):
    batch, seq, d = x.shape
    depth = w_in.shape[0]
    alpha = (2 * depth) ** 0.25
    cos_t, sin_t = _rope_tables(seq)
    sel = _rope_selectors()
    expand = _expand_matrix()
    bias = _bias_tables(rel_table)
    lw = _prepare_weights(w_in, b_gate, q_norm_g, kv_norm_g, w_uq, w_ukv, w_branch, w_out,
                          ln1_g, ln1_b, w_ffn_up, conv_w, conv_b, w_ffn_down, ln2_g, ln2_b)
    x2d = x.reshape(batch * seq, d)
    for l in range(depth):
        p2 = _proj(x2d, lw["wp"], l, dil=1, tm=1024, name="proj")
        p_dil = [_proj(x2d, w, l, dil=dil, tm=max(1024, BLOCK * dil), name=f"proj_a{g + 1}")
                 for g, (w, (_, dil)) in enumerate(zip(lw["wa"], A_GROUPS[1:]))]
        sink_slots = sinks[l].reshape(2, 4).T.reshape(-1) * LOG2E
        oa, lse = [], []
        for kind in range(len(A_GROUPS)):
            src = p2 if kind == 0 else p_dil[kind - 1]
            o_g, lse_g = _band_attention(src, bias, sink_slots, kind=kind, batch=batch, seq=seq)
            oa.append(o_g)
            lse.append(lse_g)
        ob = _band_attention(p2, bias, sink_slots, kind=len(A_GROUPS), batch=batch, seq=seq)
        q, k, v = _mla_prep(p2, cos_t, sin_t, sel, lw, l, seq=seq)
        oc = _mla_flash(q, k, v, batch=batch, seq=seq)
        x2d = _merge(oa, lse, ob, oc, p2, x2d, expand, lw, l, alpha=alpha)
        x2d = _ffn(x2d, lw, l, alpha=alpha, seq=seq)
    return x2d.reshape(batch, seq, d)
```

```python
import functools
import math

import jax
import jax.numpy as jnp
import numpy as np
from jax import lax
from jax.experimental import pallas as pl
from jax.experimental.pallas import tpu as pltpu

F32 = jnp.float32
BF16 = jnp.bfloat16

D_MODEL = 1024
HEAD_DIM = 64
BLOCK = 128
A_GROUPS = ((128, 1), (512, 4), (2048, 16))
A_HEADS = 8
B_Q_HEADS = 8
B_KV_HEADS = 2
B_WINDOW = 128
C_HEADS = 8
C_Q_RANK = 256
C_KV_RANK = 128
C_NOPE = 64
C_ROPE = 32
C_V = 64
ROPE_BASE = 10000.0
REL_BUCKETS = 32
REL_MAX_DIST = 2048
D_FF = 2816
LN_EPS = 1e-5
RMS_EPS = 1e-6
NEG = -1e30
LOG2E = math.log2(math.e)
FLASH_TQ = 1024
FLASH_TK = 1024
FLASH_HEADS = 8
FLASH_DIAG = 512

COL_GATE = 0
COL_A = 3072
COL_BQ = 4608
COL_C = 5120
COL_BKV = 5632
A_COLS = 1536

LANE = 128
SUBLANES = 8
VMEM_LIMIT = 56 * 1024 * 1024

N_KINDS = 4


def _cparams(n_axes):
    return pltpu.CompilerParams(dimension_semantics=("arbitrary",) * n_axes,
                                vmem_limit_bytes=VMEM_LIMIT)


def _layer_block(a, layer, **kwargs):
    zeros = (0,) * (a.ndim - 1)
    return pl.BlockSpec((None,) + a.shape[1:], lambda *_: (layer,) + zeros, **kwargs)


CHEAP_STRIDE = 4


def _proj_kernel(*refs, dil):
    n_x = len(refs) - (4 if dil > CHEAP_STRIDE else 3)
    x_refs, (w_ref, o_ref, xb_ref, *tmp) = refs[:n_x], refs[n_x:]

    @pl.when(pl.program_id(1) == 0)
    def _():
        if dil == 1:
            xb_ref[...] = x_refs[0][...].astype(BF16)
            return
        span = BLOCK * dil
        d1 = min(dil, CHEAP_STRIDE)
        d2 = dil // d1
        for c, x_ref in enumerate(x_refs):
            for s0 in range(0, xb_ref.shape[0], span):
                for r1 in range(d1):
                    if d2 == 1:
                        rows = x_ref[pl.ds(s0 + r1, BLOCK, stride=d1), :]
                        xb_ref[s0 + r1 * BLOCK:s0 + (r1 + 1) * BLOCK, c * LANE:(c + 1) * LANE] = rows.astype(BF16)
                        continue
                    g0 = s0 + r1 * (span // d1)
                    tmp[0][c, g0:g0 + span // d1, :] = x_ref[pl.ds(s0 + r1, span // d1, stride=d1), :]
                    for r2 in range(d2):
                        r = r1 + d1 * r2
                        rows = tmp[0][c, pl.ds(g0 + r2, BLOCK, stride=d2), :]
                        xb_ref[s0 + r * BLOCK:s0 + (r + 1) * BLOCK, c * LANE:(c + 1) * LANE] = rows.astype(BF16)

    w = w_ref[pl.program_id(1)]
    o_ref[...] = jnp.dot(xb_ref[...], w, preferred_element_type=F32).astype(o_ref.dtype)


def _proj(x2d, w_tiles, layer, *, dil, tm, name):
    t, k = x2d.shape
    n_tiles, tn = w_tiles.shape[1], w_tiles.shape[3]
    n = n_tiles * tn
    if dil == 1:
        x_specs = [pl.BlockSpec((tm, k), lambda i, j: (i, 0))]
    else:
        x_specs = [pl.BlockSpec((tm, LANE), lambda i, j, c=c: (i, c)) for c in range(k // LANE)]
    scratch = [pltpu.VMEM((tm, k), BF16)]
    if dil > CHEAP_STRIDE:
        scratch.append(pltpu.VMEM((k // LANE, tm, LANE), F32))
    return pl.pallas_call(
        functools.partial(_proj_kernel, dil=dil),
        grid=(t // tm, n_tiles),
        in_specs=x_specs + [_layer_block(w_tiles, layer, pipeline_mode=pl.Buffered(1))],
        out_specs=pl.BlockSpec((tm, tn), lambda i, j: (i, j)),
        out_shape=jax.ShapeDtypeStruct((t, n), BF16),
        scratch_shapes=scratch,
        compiler_params=_cparams(2),
        name=name,
    )(*([x2d] * len(x_specs)), w_tiles)


def _t5_bucket(dist):
    n = jnp.maximum(dist, 0)
    max_exact = REL_BUCKETS // 2
    scaled = jnp.log(jnp.maximum(n, 1).astype(F32) / max_exact) / math.log(REL_MAX_DIST / max_exact)
    large = max_exact + (scaled * (REL_BUCKETS - max_exact)).astype(jnp.int32)
    return jnp.where(n < max_exact, n, jnp.minimum(large, REL_BUCKETS - 1))


def _bias_codes():
    qi = jnp.arange(BLOCK)[:, None]
    ki = jnp.arange(2 * BLOCK)[None, :]
    step = BLOCK + qi - ki
    has_prev = ki >= BLOCK
    codes = []
    for kind in range(N_KINDS):
        if kind < len(A_GROUPS):
            dil = A_GROUPS[kind][1]
            band = (step >= 0) & (step <= BLOCK)
        else:
            dil = 1
            band = (step >= 0) & (step < B_WINDOW)
        bucket = _t5_bucket(step * dil)
        codes.append(jnp.stack([jnp.where(band & has_prev, bucket, -1),
                                jnp.where(band, bucket, -1)]))
    return jnp.stack(codes).astype(jnp.int32)


def _bias_kernel(rel_ref, code_ref, o_ref):
    kind = pl.program_id(0)
    code = code_ref[0, 0]
    for slot in range(A_HEADS):
        b_head = len(A_GROUPS) * A_HEADS + slot // 2 + 4 * (slot % 2)
        col = jnp.where(kind < len(A_GROUPS), kind * A_HEADS + slot, b_head)
        acc = jnp.full(code.shape, NEG, F32)
        for b in range(REL_BUCKETS):
            acc = jnp.where(code == b, rel_ref[b, col] * LOG2E, acc)
        o_ref[0, 0, slot] = acc


def _bias_tables(rel_table):
    codes = _bias_codes()
    return pl.pallas_call(
        _bias_kernel,
        grid=(N_KINDS, 2),
        in_specs=[pl.BlockSpec(memory_space=pltpu.SMEM),
                  pl.BlockSpec((1, 1, BLOCK, 2 * BLOCK), lambda k, v: (k, v, 0, 0))],
        out_specs=pl.BlockSpec((1, 1, A_HEADS, BLOCK, 2 * BLOCK), lambda k, v: (k, v, 0, 0, 0)),
        out_shape=jax.ShapeDtypeStruct((N_KINDS, 2, A_HEADS, BLOCK, 2 * BLOCK), F32),
        compiler_params=_cparams(2),
        name="bias_tables",
    )(rel_table, codes)


BAND_QB = 32
STAT_LANES = LANE // A_HEADS // 2


def _band_kernel(sink_ref, q_ref, kp_ref, kc_ref, vp_ref, vc_ref, bias_ref, o_ref, *lse_refs,
                 shared_kv, with_sink):
    first = pl.program_id(2) == 0
    lane = lax.broadcasted_iota(jnp.int32, (BLOCK, LANE), 1)
    low = lane < HEAD_DIM
    n_qb, n_res = q_ref.shape[:2]
    for res, qb in ((res, qb) for res in range(n_res) for qb in range(n_qb)):
        variant = jnp.where(first, 0, 1) if qb == 0 else 1
        lse_tile = jnp.zeros((BLOCK, LANE), F32)
        for pair in range(4):
            cols = slice(pair * LANE, (pair + 1) * LANE)
            kv_cols = slice(0, LANE) if shared_kv else cols
            qp = q_ref[qb, res, :, cols]
            k_prev = kp_ref[res, :, kv_cols] if qb == 0 else kc_ref[qb - 1, res, :, kv_cols]
            v_prev = vp_ref[res, :, kv_cols] if qb == 0 else vc_ref[qb - 1, res, :, kv_cols]
            kcat = jnp.concatenate([k_prev, kc_ref[qb, res, :, kv_cols]], axis=0)
            vcat = jnp.concatenate([v_prev, vc_ref[qb, res, :, kv_cols]], axis=0)
            zero = jnp.zeros_like(qp)
            q2 = jnp.concatenate([jnp.where(low, qp, zero), jnp.where(low, zero, qp)], axis=0)
            s = lax.dot_general(q2, kcat, (((1,), (1,)), ((), ())), preferred_element_type=F32)
            s = s + bias_ref[variant, pair]
            m = jnp.max(s, axis=-1, keepdims=True)
            if with_sink:
                head_a = lax.broadcasted_iota(jnp.int32, (2 * BLOCK, 1), 0) < BLOCK
                sink = jnp.where(head_a, sink_ref[2 * pair], sink_ref[2 * pair + 1])
                m = jnp.maximum(m, sink)
            p = jnp.exp2(s - m)
            l = jnp.sum(p, axis=-1, keepdims=True)
            if with_sink:
                l = l + jnp.exp2(sink - m)
            o = jnp.dot(p.astype(BF16), vcat, preferred_element_type=F32)
            if lse_refs:
                for sub in range(2):
                    rows = slice(sub * BLOCK, (sub + 1) * BLOCK)
                    field = 2 * (2 * pair + sub)
                    lse_tile = jnp.where(lane // STAT_LANES == field, m[rows],
                                         jnp.where(lane // STAT_LANES == field + 1, l[rows], lse_tile))
            else:
                o = o * (1.0 / l)
            o_ref[qb, res, :, cols] = jnp.where(low, o[:BLOCK], o[BLOCK:]).astype(o_ref.dtype)
        if lse_refs:
            lse_refs[0][qb, res] = lse_tile


def _band_attention(p2, bias, sinks, *, kind, batch, seq):
    is_b = kind == len(A_GROUPS)
    dil = 1 if is_b else A_GROUPS[kind][1]
    n_span = seq // (BLOCK * dil)
    view = p2.reshape(batch, n_span, dil, BLOCK, p2.shape[-1])
    if is_b:
        q_blk, k_blk, v_blk, kv_w = COL_BQ // 512, COL_BKV // LANE, COL_BKV // LANE + 1, LANE
    else:
        q_blk = COL_A // 512 if dil == 1 else 0
        k_blk, v_blk, kv_w = q_blk + 1, q_blk + 2, 512

    n_qb = min(BAND_QB, n_span)
    n_res = BAND_QB // n_qb

    def cur(blk, width):
        return pl.BlockSpec((None, n_qb, n_res, BLOCK, width), lambda b, r, n: (b, n, r, 0, blk))

    def prev(blk, width):
        return pl.BlockSpec((None, None, n_res, BLOCK, width),
                            lambda b, r, n: (b, jnp.maximum(n * n_qb - 1, 0), r, 0, blk))

    in_specs = [
        pl.BlockSpec(memory_space=pltpu.SMEM),
        cur(q_blk, 512), prev(k_blk, kv_w), cur(k_blk, kv_w), prev(v_blk, kv_w), cur(v_blk, kv_w),
        pl.BlockSpec((None, 2, A_HEADS // 2, 2 * BLOCK, 2 * BLOCK), lambda b, r, n: (kind, 0, 0, 0, 0)),
    ]
    bias = bias.reshape(N_KINDS, 2, A_HEADS // 2, 2 * BLOCK, 2 * BLOCK)
    out_specs = [pl.BlockSpec((None, n_qb, n_res, BLOCK, 512), lambda b, r, n: (b, n, r, 0, 0))]
    out_shape = [jax.ShapeDtypeStruct((batch, n_span, dil, BLOCK, 512), BF16)]
    if not is_b:
        out_specs.append(pl.BlockSpec((None, n_qb, n_res, BLOCK, LANE), lambda b, r, n: (b, n, r, 0, 0)))
        out_shape.append(jax.ShapeDtypeStruct((batch, n_span, dil, BLOCK, LANE), F32))
    outs = pl.pallas_call(
        functools.partial(_band_kernel, shared_kv=is_b, with_sink=is_b),
        grid=(batch, dil // n_res, n_span // n_qb),
        in_specs=in_specs,
        out_specs=out_specs,
        out_shape=out_shape,
        compiler_params=_cparams(3),
        name="band_b" if is_b else f"band_a{kind}",
    )(sinks, view, view, view, view, view, bias)
    return outs[0] if is_b else outs


def _mla_prep_kernel(c_ref, cos_ref, sin_ref, gq_ref, gkv_ref, wq1_ref, wq2_ref, wk_ref, wv_ref,
                     sel_ref, q_ref, k_ref, v_ref):
    c = c_ref[...]
    cq = c[:, :C_Q_RANK].astype(F32)
    ckv = c[:, C_Q_RANK:C_Q_RANK + C_KV_RANK].astype(F32)
    nq = cq * lax.rsqrt(jnp.mean(cq * cq, axis=-1, keepdims=True) + RMS_EPS) * gq_ref[...]
    nkv = ckv * lax.rsqrt(jnp.mean(ckv * ckv, axis=-1, keepdims=True) + RMS_EPS) * gkv_ref[...]
    nq = nq.astype(BF16)
    nkv = nkv.astype(BF16)
    cos = cos_ref[...]
    sin = sin_ref[...]
    cos8 = jnp.tile(cos, (1, C_HEADS))
    sin8 = jnp.tile(sin, (1, C_HEADS))
    q = (jnp.dot(nq, wq1_ref[...], preferred_element_type=F32) * cos8
         + jnp.dot(nq, wq2_ref[...], preferred_element_type=F32) * sin8)
    q_ref[...] = q.astype(q_ref.dtype)
    picked = jnp.dot(c, sel_ref[...], preferred_element_type=F32)
    kr = picked[:, :LANE] * cos + picked[:, LANE:] * sin
    k = jnp.dot(nkv, wk_ref[...], preferred_element_type=F32) + jnp.tile(kr, (1, C_HEADS))
    k_ref[...] = k.astype(k_ref.dtype)
    lane = lax.broadcasted_iota(jnp.int32, (1, C_HEADS * LANE), 1)
    ones_col = jnp.where(lane % LANE == C_V, 1.0, 0.0).astype(F32)
    v_ref[...] = (jnp.dot(nkv, wv_ref[...], preferred_element_type=F32) + ones_col).astype(v_ref.dtype)


def _mla_prep(p2, cos_t, sin_t, sel, lw, layer, *, seq, tm=1024):
    t = p2.shape[0]
    per_seq = seq // tm
    full = lambda a: pl.BlockSpec(a.shape, lambda i: (0,) * a.ndim)
    stacked = (lw["gq"], lw["gkv"], lw["wq1"], lw["wq2"], lw["wk"], lw["wv"])
    wide = C_HEADS * LANE
    return pl.pallas_call(
        _mla_prep_kernel,
        grid=(t // tm,),
        in_specs=[pl.BlockSpec((tm, 512), lambda i: (i, COL_C // 512)),
                  pl.BlockSpec((tm, LANE), lambda i: (i % per_seq, 0)),
                  pl.BlockSpec((tm, LANE), lambda i: (i % per_seq, 0))]
                 + [_layer_block(a, layer) for a in stacked] + [full(sel)],
        out_specs=[pl.BlockSpec((tm, wide), lambda i: (i, 0))] * 3,
        out_shape=[jax.ShapeDtypeStruct((t, wide), BF16)] * 3,
        compiler_params=_cparams(1),
        name="mla_prep",
    )(p2, cos_t, sin_t, *stacked, sel)


def _mla_flash_kernel(qi_ref, ki_ref, q_ref, k_ref, v_ref, o_ref, m_ref, acc_ref, *, tq, tk, heads):
    step = pl.program_id(2)
    qi = qi_ref[step]
    ki = ki_ref[step]
    ratio = tq // tk

    @pl.when(ki == 0)
    def _():
        m_ref[...] = jnp.full(m_ref.shape, NEG, F32)
        acc_ref[...] = jnp.zeros(acc_ref.shape, F32)

    def update(h, rows, n_keys, shift):
        hc = slice(h * LANE, (h + 1) * LANE)
        s = lax.dot_general(q_ref[rows, hc], k_ref[:n_keys, hc], (((1,), (1,)), ((), ())),
                            preferred_element_type=F32)
        if shift is not None:
            row = lax.broadcasted_iota(jnp.int32, s.shape, 0) + shift
            col = lax.broadcasted_iota(jnp.int32, s.shape, 1)
            s = jnp.where(row >= col, s, NEG)
        m_prev = m_ref[h, rows]
        m_new = jnp.maximum(m_prev, jnp.max(s, axis=-1, keepdims=True))
        alpha = jnp.exp2(m_prev - m_new)
        p = jnp.exp2(s - jnp.tile(m_new, (1, n_keys // LANE)))
        acc_ref[h, rows] = alpha * acc_ref[h, rows] + jnp.dot(p.astype(BF16), v_ref[:n_keys, hc],
                                                              preferred_element_type=F32)
        m_ref[h, rows] = m_new

    @pl.when(ki < qi * ratio)
    def _():
        for h in range(heads):
            update(h, slice(0, tq), tk, None)

    for d in range(ratio):
        @pl.when(ki == qi * ratio + d)
        def _(d=d):
            for h in range(heads):
                if (d + 1) * tk < tq:
                    update(h, slice((d + 1) * tk, tq), tk, None)
                for r0 in range(0, tk, FLASH_DIAG):
                    update(h, slice(d * tk + r0, d * tk + r0 + FLASH_DIAG), r0 + FLASH_DIAG, r0)

    @pl.when(ki == (qi + 1) * ratio - 1)
    def _():
        low = lax.broadcasted_iota(jnp.int32, (tq, LANE), 1) < C_V
        for pair in range(heads // 2):
            halves = []
            for h in (2 * pair, 2 * pair + 1):
                acc = acc_ref[h]
                halves.append(acc * (1.0 / acc[:, C_V:C_V + 1]))
            odd = pltpu.roll(halves[1], C_V, axis=1)
            o_ref[:, pair * LANE:(pair + 1) * LANE] = jnp.where(low, halves[0], odd).astype(o_ref.dtype)


def _mla_flash(q, k, v, *, batch, seq, tq=FLASH_TQ, tk=FLASH_TK, heads=FLASH_HEADS):
    wide = C_HEADS * LANE
    ratio = tq // tk
    q3 = q.reshape(batch, seq, wide)
    k3 = k.reshape(batch, seq, wide)
    v3 = v.reshape(batch, seq, wide)
    pairs = [(i, j) for i in range(seq // tq) for j in range((i + 1) * ratio)]
    qi_of = jnp.asarray([i for i, _ in pairs], jnp.int32)
    ki_of = jnp.asarray([j for _, j in pairs], jnp.int32)
    q_map = lambda b, g, t, qi, ki: (b, qi[t], g)
    kv_map = lambda b, g, t, qi, ki: (b, ki[t], g)
    out = pl.pallas_call(
        functools.partial(_mla_flash_kernel, tq=tq, tk=tk, heads=heads),
        grid_spec=pltpu.PrefetchScalarGridSpec(
            num_scalar_prefetch=2,
            grid=(batch, C_HEADS // heads, len(pairs)),
            in_specs=[pl.BlockSpec((None, tq, heads * LANE), q_map),
                      pl.BlockSpec((None, tk, heads * LANE), kv_map),
                      pl.BlockSpec((None, tk, heads * LANE), kv_map)],
            out_specs=pl.BlockSpec((None, tq, heads * C_V), q_map),
            scratch_shapes=[pltpu.VMEM((heads, tq, LANE), F32),
                            pltpu.VMEM((heads, tq, LANE), F32)]),
        out_shape=jax.ShapeDtypeStruct((batch, seq, C_HEADS * C_V), BF16),
        compiler_params=_cparams(3),
        name="mla_flash",
    )(qi_of, ki_of, q3, k3, v3)
    return out.reshape(batch * seq, C_HEADS * C_V)


def _layer_norm(y, g, b):
    mu = jnp.mean(y, axis=-1, keepdims=True)
    d = y - mu
    var = jnp.mean(d * d, axis=-1, keepdims=True)
    return d * lax.rsqrt(var + LN_EPS) * g + b


def _token_order(src_ref, tmp_ref, dil):
    if dil == 1:
        return src_ref[...].astype(F32)
    n = src_ref.shape[1]
    n_slab = src_ref.shape[2] // LANE
    for r in range(dil):
        rows = src_ref[r].astype(F32)
        for c in range(n_slab):
            tmp_ref[c, pl.ds(r, n, stride=dil), :] = rows[:, c * LANE:(c + 1) * LANE]
    return jnp.concatenate([tmp_ref[c] for c in range(n_slab)], axis=1)


def _merge_kernel(oa0_ref, oa1_ref, oa2_ref, l0_ref, l1_ref, l2_ref, ob_ref, oc_ref, gate_ref, x_ref,
                  e_ref, wb_ref, wo_ref, bg_ref, g_ref, b_ref, o_ref, ot1_ref, ot2_ref, lt1_ref, lt2_ref,
                  *, alpha):
    dils = [d for _, d in A_GROUPS]
    stats = [_token_order(ref, tmp, d)
             for ref, tmp, d in zip((l0_ref, l1_ref, l2_ref), (None, lt1_ref, lt2_ref), dils)]
    is_max = lax.broadcasted_iota(jnp.int32, stats[0].shape, 1) // STAT_LANES % 2 == 0
    top = jnp.maximum(jnp.maximum(stats[0], stats[1]), stats[2])
    es = [jnp.exp2(v - top) for v in stats]
    sums = [pltpu.roll(v, LANE - STAT_LANES, axis=1) for v in stats]
    inv = 1.0 / (es[0] * sums[0] + es[1] * sums[1] + es[2] * sums[2])
    o_a = None
    for e, oa_ref, tmp, d in zip(es, (oa0_ref, oa1_ref, oa2_ref), (None, ot1_ref, ot2_ref), dils):
        w = jnp.where(is_max, e * inv, 0.0)
        hi = w.astype(BF16)
        lo = (w - hi.astype(F32)).astype(BF16)
        wide = jnp.dot(jnp.concatenate([hi, lo], axis=1), e_ref[...], preferred_element_type=F32)
        term = wide * _token_order(oa_ref, tmp, d)
        o_a = term if o_a is None else o_a + term
    branches = (o_a.astype(BF16), ob_ref[...], oc_ref[...])
    merged = None
    for i, br in enumerate(branches):
        gate = jax.nn.sigmoid(gate_ref[:, i * D_MODEL:(i + 1) * D_MODEL].astype(F32)
                              + bg_ref[:, i * D_MODEL:(i + 1) * D_MODEL])
        term = gate * jnp.dot(br, wb_ref[i], preferred_element_type=F32)
        merged = term if merged is None else merged + term
    mix = jnp.dot(merged.astype(BF16), wo_ref[...], preferred_element_type=F32)
    o_ref[...] = _layer_norm(alpha * x_ref[...] + mix, g_ref[...], b_ref[...])


def _merge(oa, lse, ob, oc, p2, x2d, expand, lw, layer, *, alpha, tm=512):
    t = x2d.shape[0]
    row = lambda w: pl.BlockSpec((tm, w), lambda i: (i, 0))
    full = lambda a: pl.BlockSpec(a.shape, lambda i: (0,) * a.ndim)

    def grouped(arrs):
        views, specs = [], []
        for a, (_, dil) in zip(arrs, A_GROUPS):
            w = a.shape[-1]
            if dil == 1:
                views.append(a.reshape(t, w))
                specs.append(row(w))
                continue
            per_span = BLOCK * dil // tm
            views.append(a.reshape(-1, dil, BLOCK, w))
            specs.append(pl.BlockSpec((None, dil, tm // dil, w),
                                      lambda i, per_span=per_span: (i // per_span, 0, i % per_span, 0)))
        return views, specs

    oa_v, oa_s = grouped(oa)
    lse_v, lse_s = grouped(lse)
    stacked = (lw["wb"], lw["wo"], lw["bg"], lw["ln1_g"], lw["ln1_b"])
    slabs = lambda w: pltpu.VMEM((w // LANE, tm, LANE), F32)
    return pl.pallas_call(
        functools.partial(_merge_kernel, alpha=alpha),
        grid=(t // tm,),
        in_specs=oa_s + lse_s + [row(512), row(512),
                  pl.BlockSpec((tm, 3 * D_MODEL), lambda i: (i, COL_GATE // (3 * D_MODEL))),
                  row(D_MODEL), full(expand)] + [_layer_block(a, layer) for a in stacked],
        out_specs=row(D_MODEL),
        out_shape=jax.ShapeDtypeStruct((t, D_MODEL), F32),
        scratch_shapes=[slabs(512), slabs(512), slabs(LANE), slabs(LANE)],
        compiler_params=_cparams(1),
        name="merge",
    )(*oa_v, *lse_v, ob.reshape(t, 512), oc, p2, x2d, expand, *stacked)


HALO = 8
FF_CHUNK = 1024


def _ffn_kernel(halo_ref, *refs, alpha, tm, per_seq):
    n_slab = D_MODEL // LANE
    x_refs = refs[:n_slab]
    wup_ref, cw_ref, cb_ref, wdn_ref, g_ref, b_ref, o_ref, xs_ref, os_ref = refs[n_slab:]
    rows = tm + HALO
    groups = rows // SUBLANES
    i = pl.program_id(0)
    halo = jnp.where(i % per_seq == 0, jnp.zeros_like(halo_ref[...]), halo_ref[...])
    for c, x_ref in enumerate(x_refs):
        xs_ref[c, :HALO, :] = halo[:, c * LANE:(c + 1) * LANE]
        xs_ref[c, HALO:, :] = x_ref[...]
    xp = jnp.concatenate(
        [jnp.concatenate([xs_ref[c, pl.ds(j, SUBLANES, stride=groups), :] for j in range(groups)], axis=0)
         for c in range(n_slab)], axis=1)
    xh = xp.astype(BF16)
    chunks = [(c0, min(FF_CHUNK, D_FF - c0)) for c0 in range(0, D_FF, FF_CHUNK)]

    def up(c0, cf):
        return [jnp.dot(xh, wup_ref[:, base:base + cf], preferred_element_type=F32)
                for base in (c0, D_FF + c0)]

    def conv(u, base, cf):
        u3 = u.reshape(groups, SUBLANES, cf)
        wrap = pltpu.roll(u3[groups - 2:], 1, axis=1)
        prev1 = jnp.concatenate([wrap[1:], u3[:groups - 1]], axis=0)
        prev2 = jnp.concatenate([wrap, u3[:groups - 2]], axis=0)
        tap = lambda k: cw_ref[k:k + 1, base:base + cf]
        return cb_ref[:, base:base + cf] + prev2 * tap(0) + prev1 * tap(1) + u3 * tap(2)

    acc = None
    u_next = up(*chunks[0])
    for ci, (c0, cf) in enumerate(chunks):
        u_cur = u_next
        if ci + 1 < len(chunks):
            u_next = up(*chunks[ci + 1])
        gate, val = (conv(u, base, cf) for u, base in zip(u_cur, (c0, D_FF + c0)))
        act = (gate * jax.nn.sigmoid(gate) * val).reshape(rows, cf).astype(BF16)
        term = jnp.dot(act, wdn_ref[c0:c0 + cf, :], preferred_element_type=F32)
        acc = term if acc is None else acc + term
    out = _layer_norm(alpha * xp + acc, g_ref[...], b_ref[...])
    for c in range(n_slab):
        for j in range(groups):
            os_ref[c, pl.ds(j, SUBLANES, stride=groups), :] = out[j * SUBLANES:(j + 1) * SUBLANES,
                                                                  c * LANE:(c + 1) * LANE]
        o_ref[:, c * LANE:(c + 1) * LANE] = os_ref[c, HALO:, :]


def _ffn(x2d, lw, layer, *, alpha, seq, tm=512):
    t = x2d.shape[0]
    per_seq = seq // tm
    n_slab = D_MODEL // LANE
    consts = (lw["wup"], lw["cw"], lw["cb"], lw["wdn"], lw["ln2_g"], lw["ln2_b"])
    return pl.pallas_call(
        functools.partial(_ffn_kernel, alpha=alpha, tm=tm, per_seq=per_seq),
        grid=(t // tm,),
        in_specs=[pl.BlockSpec((HALO, D_MODEL), lambda i: (jnp.maximum(i * (tm // HALO) - 1, 0), 0))]
                 + [pl.BlockSpec((tm, LANE), lambda i, c=c: (i, c)) for c in range(n_slab)]
                 + [_layer_block(a, layer, pipeline_mode=pl.Buffered(1)) for a in consts],
        out_specs=pl.BlockSpec((tm, D_MODEL), lambda i: (i, 0)),
        out_shape=jax.ShapeDtypeStruct((t, D_MODEL), F32),
        scratch_shapes=[pltpu.VMEM((n_slab, tm + HALO, LANE), F32),
                        pltpu.VMEM((n_slab, tm + HALO, LANE), F32)],
        compiler_params=_cparams(1),
        name="ffn",
    )(x2d, *([x2d] * n_slab), *consts)


def _prepare_weights(w_in, b_gate, q_norm_g, kv_norm_g, w_uq, w_ukv, w_branch, w_out,
                     ln1_g, ln1_b, w_ffn_up, conv_w, conv_b, w_ffn_down, ln2_g, ln2_b):
    depth = w_in.shape[0]
    heads_w = A_HEADS * HEAD_DIM
    a_end = len(A_GROUPS) * A_COLS
    bq_end = a_end + B_Q_HEADS * HEAD_DIM
    bkv_end = bq_end + 2 * B_KV_HEADS * HEAD_DIM
    cdq_end = bkv_end + C_Q_RANK
    ckv_end = cdq_end + C_KV_RANK
    kr_end = ckv_end + C_ROPE
    col = np.arange(w_in.shape[-1])
    is_q = ((col < a_end) & (col % A_COLS < heads_w)) | ((col >= a_end) & (col < bq_end))
    col_scale = jnp.asarray(np.where(is_q, HEAD_DIM ** -0.5 * LOG2E, 1.0), F32)
    wi = (w_in * col_scale).astype(BF16)
    bq = wi[:, :, a_end:bq_end].reshape(depth, D_MODEL, B_Q_HEADS, HEAD_DIM)
    bq = jnp.stack([bq[:, :, :4], bq[:, :, 4:]], axis=3).reshape(depth, D_MODEL, heads_w)
    bkv = wi[:, :, bq_end:bkv_end]
    cdq = wi[:, :, bkv_end:cdq_end]
    ckv = wi[:, :, cdq_end:ckv_end]
    kr = wi[:, :, ckv_end:kr_end]
    kr_rot = jnp.concatenate([-kr[:, :, C_ROPE // 2:], kr[:, :, :C_ROPE // 2]], axis=2)
    gate = wi[:, :, kr_end:]
    zeros = lambda n: jnp.zeros((depth, D_MODEL, n), wi.dtype)
    rest = jnp.concatenate([wi[:, :, :A_COLS], bq, cdq, ckv, kr, kr_rot, zeros(64), bkv, zeros(256)], axis=2)
    wp = jnp.stack([gate, rest], axis=1)
    wa = []
    for g, (_, dil) in enumerate(A_GROUPS[1:], start=1):
        tn = A_COLS * 1024 // max(1024, BLOCK * dil)
        w = wi[:, :, g * A_COLS:(g + 1) * A_COLS]
        wa.append(w.reshape(depth, D_MODEL, A_COLS // tn, tn).transpose(0, 2, 1, 3))

    uq = (w_uq * ((C_NOPE + C_ROPE) ** -0.5 * LOG2E)).reshape(depth, C_Q_RANK, C_HEADS, C_NOPE + C_ROPE)
    zq = lambda n: jnp.zeros((depth, C_Q_RANK, C_HEADS, n), uq.dtype)
    half = C_ROPE // 2
    wq1 = jnp.concatenate([uq, zq(LANE - C_NOPE - C_ROPE)], axis=3)
    wq2 = jnp.concatenate([zq(C_NOPE), -uq[..., C_NOPE + half:], uq[..., C_NOPE:C_NOPE + half],
                           zq(LANE - C_NOPE - C_ROPE)], axis=3)
    ukv = w_ukv.reshape(depth, C_KV_RANK, C_HEADS, C_NOPE + C_V)
    zkv = lambda n: jnp.zeros((depth, C_KV_RANK, C_HEADS, n), ukv.dtype)
    wk = jnp.concatenate([ukv[..., :C_NOPE], zkv(LANE - C_NOPE)], axis=3)
    wv = jnp.concatenate([ukv[..., C_NOPE:], zkv(LANE - C_V)], axis=3)

    wb1 = w_branch[:, 1].reshape(depth, B_Q_HEADS, HEAD_DIM, D_MODEL)
    wb1 = jnp.stack([wb1[:, :4], wb1[:, 4:]], axis=2).reshape(depth, 512, D_MODEL)
    wb = jnp.stack([w_branch[:, 0], wb1, w_branch[:, 2]], axis=1)

    row = lambda a: a.reshape(depth, 1, -1)
    return dict(
        wp=wp, wa=wa,
        gq=row(q_norm_g), gkv=row(kv_norm_g),
        wq1=wq1.reshape(depth, C_Q_RANK, -1).astype(BF16), wq2=wq2.reshape(depth, C_Q_RANK, -1).astype(BF16),
        wk=wk.reshape(depth, C_KV_RANK, -1).astype(BF16), wv=wv.reshape(depth, C_KV_RANK, -1).astype(BF16),
        wb=wb.astype(BF16), wo=w_out.astype(BF16), bg=row(b_gate),
        ln1_g=row(ln1_g), ln1_b=row(ln1_b),
        wup=w_ffn_up.astype(BF16), cw=conv_w, cb=row(conv_b), wdn=w_ffn_down.astype(BF16),
        ln2_g=row(ln2_g), ln2_b=row(ln2_b),
    )


def _rope_selectors():
    sel = np.zeros((512, 2 * LANE), np.float32)
    for j in range(C_ROPE):
        sel[C_Q_RANK + C_KV_RANK + j, C_NOPE + j] = 1.0
        sel[C_Q_RANK + C_KV_RANK + C_ROPE + j, LANE + C_NOPE + j] = 1.0
    return jnp.asarray(sel, BF16)


def _rope_tables(seq):
    pos = jnp.arange(seq, dtype=F32)
    inv_freq = ROPE_BASE ** (-jnp.arange(0, C_ROPE, 2, dtype=F32) / C_ROPE)
    ang = pos[:, None] * inv_freq[None, :]
    cos, sin = jnp.cos(ang), jnp.sin(ang)
    pad = jnp.zeros((seq, LANE - C_NOPE - C_ROPE), F32)
    cos_t = jnp.concatenate([jnp.ones((seq, C_NOPE), F32), cos, cos, pad], axis=1)
    sin_t = jnp.concatenate([jnp.zeros((seq, C_NOPE), F32), sin, sin, pad], axis=1)
    return cos_t, sin_t


def _expand_matrix():
    e = np.zeros((LANE, A_HEADS * HEAD_DIM), np.float32)
    for c in range(A_HEADS * HEAD_DIM):
        e[(LANE // A_HEADS) * (c // HEAD_DIM), c] = 1.0
    return jnp.asarray(np.concatenate([e, e], axis=0), BF16)


def kernel(x, rel_table, w_in, b_gate, sinks, q_norm_g, kv_norm_g, w_uq, w_ukv, w_branch, w_out,
           ln1_g, ln1_b, w_ffn_up, conv_w, conv_b, w_ffn_down, ln2_g, ln2_b):
    batch, seq, d = x.shape
    depth = w_in.shape[0]
    alpha = (2 * depth) ** 0.25
    cos_t, sin_t = _rope_tables(seq)
    sel = _rope_selectors()
    expand = _expand_matrix()
    bias = _bias_tables(rel_table)
    lw = _prepare_weights(w_in, b_gate, q_norm_g, kv_norm_g, w_uq, w_ukv, w_branch, w_out,
                          ln1_g, ln1_b, w_ffn_up, conv_w, conv_b, w_ffn_down, ln2_g, ln2_b)
    x2d = x.reshape(batch * seq, d)
    for l in range(depth):
        p2 = _proj(x2d, lw["wp"], l, dil=1, tm=1024, name="proj")
        p_dil = [_proj(x2d, w, l, dil=dil, tm=max(1024, BLOCK * dil), name=f"proj_a{g + 1}")
                 for g, (w, (_, dil)) in enumerate(zip(lw["wa"], A_GROUPS[1:]))]
        sink_slots = sinks[l].reshape(2, 4).T.reshape(-1) * LOG2E
        oa, lse = [], []
        for kind in range(len(A_GROUPS)):
            src = p2 if kind == 0 else p_dil[kind - 1]
            o_g, lse_g = _band_attention(src, bias, sink_slots, kind=kind, batch=batch, seq=seq)
            oa.append(o_g)
            lse.append(lse_g)
        ob = _band_attention(p2, bias, sink_slots, kind=len(A_GROUPS), batch=batch, seq=seq)
        q, k, v = _mla_prep(p2, cos_t, sin_t, sel, lw, l, seq=seq)
        oc = _mla_flash(q, k, v, batch=batch, seq=seq)
        x2d = _merge(oa, lse, ob, oc, p2, x2d, expand, lw, l, alpha=alpha)
        x2d = _ffn(x2d, lw, l, alpha=alpha, seq=seq)
    return x2d.reshape(batch, seq, d)
```

```python
import functools
import math

import jax
import jax.numpy as jnp
import numpy as np
from jax import lax
from jax.experimental import pallas as pl
from jax.experimental.pallas import tpu as pltpu

F32 = jnp.float32
BF16 = jnp.bfloat16

D_MODEL = 1024
HEAD_DIM = 64
BLOCK = 128
A_GROUPS = ((128, 1), (512, 4), (2048, 16))
A_HEADS = 8
B_Q_HEADS = 8
B_KV_HEADS = 2
B_WINDOW = 128
C_HEADS = 8
C_Q_RANK = 256
C_KV_RANK = 128
C_NOPE = 64
C_ROPE = 32
C_V = 64
ROPE_BASE = 10000.0
REL_BUCKETS = 32
REL_MAX_DIST = 2048
D_FF = 2816
LN_EPS = 1e-5
RMS_EPS = 1e-6
NEG = -1e30
LOG2E = math.log2(math.e)
FLASH_TQ = 1024
FLASH_TK = 1024
FLASH_HEADS = 8
FLASH_DIAG = 512

COL_GATE = 0
COL_A = 3072
COL_BQ = 4608
COL_C = 5120
COL_BKV = 5632
A_COLS = 1536

LANE = 128
SUBLANES = 8
VMEM_LIMIT = 56 * 1024 * 1024

N_KINDS = 4


def _cparams(n_axes):
    return pltpu.CompilerParams(dimension_semantics=("arbitrary",) * n_axes,
                                vmem_limit_bytes=VMEM_LIMIT)


def _layer_block(a, layer, **kwargs):
    zeros = (0,) * (a.ndim - 1)
    return pl.BlockSpec((None,) + a.shape[1:], lambda *_: (layer,) + zeros, **kwargs)


CHEAP_STRIDE = 4


def _proj_kernel(*refs, dil):
    n_x = len(refs) - (4 if dil > CHEAP_STRIDE else 3)
    x_refs, (w_ref, o_ref, xb_ref, *tmp) = refs[:n_x], refs[n_x:]

    @pl.when(pl.program_id(1) == 0)
    def _():
        if dil == 1:
            xb_ref[...] = x_refs[0][...].astype(BF16)
            return
        span = BLOCK * dil
        d1 = min(dil, CHEAP_STRIDE)
        d2 = dil // d1
        for c, x_ref in enumerate(x_refs):
            for s0 in range(0, xb_ref.shape[0], span):
                for r1 in range(d1):
                    if d2 == 1:
                        rows = x_ref[pl.ds(s0 + r1, BLOCK, stride=d1), :]
                        xb_ref[s0 + r1 * BLOCK:s0 + (r1 + 1) * BLOCK, c * LANE:(c + 1) * LANE] = rows.astype(BF16)
                        continue
                    g0 = s0 + r1 * (span // d1)
                    tmp[0][c, g0:g0 + span // d1, :] = x_ref[pl.ds(s0 + r1, span // d1, stride=d1), :]
                    for r2 in range(d2):
                        r = r1 + d1 * r2
                        rows = tmp[0][c, pl.ds(g0 + r2, BLOCK, stride=d2), :]
                        xb_ref[s0 + r * BLOCK:s0 + (r + 1) * BLOCK, c * LANE:(c + 1) * LANE] = rows.astype(BF16)

    w = w_ref[pl.program_id(1)]
    o_ref[...] = jnp.dot(xb_ref[...], w, preferred_element_type=F32).astype(o_ref.dtype)


def _proj(x2d, w_tiles, layer, *, dil, tm, name):
    t, k = x2d.shape
    n_tiles, tn = w_tiles.shape[1], w_tiles.shape[3]
    n = n_tiles * tn
    if dil == 1:
        x_specs = [pl.BlockSpec((tm, k), lambda i, j: (i, 0))]
    else:
        x_specs = [pl.BlockSpec((tm, LANE), lambda i, j, c=c: (i, c)) for c in range(k // LANE)]
    scratch = [pltpu.VMEM((tm, k), BF16)]
    if dil > CHEAP_STRIDE:
        scratch.append(pltpu.VMEM((k // LANE, tm, LANE), F32))
    return pl.pallas_call(
        functools.partial(_proj_kernel, dil=dil),
        grid=(t // tm, n_tiles),
        in_specs=x_specs + [_layer_block(w_tiles, layer, pipeline_mode=pl.Buffered(1))],
        out_specs=pl.BlockSpec((tm, tn), lambda i, j: (i, j)),
        out_shape=jax.ShapeDtypeStruct((t, n), BF16),
        scratch_shapes=scratch,
        compiler_params=_cparams(2),
        name=name,
    )(*([x2d] * len(x_specs)), w_tiles)


def _t5_bucket(dist):
    n = jnp.maximum(dist, 0)
    max_exact = REL_BUCKETS // 2
    scaled = jnp.log(jnp.maximum(n, 1).astype(F32) / max_exact) / math.log(REL_MAX_DIST / max_exact)
    large = max_exact + (scaled * (REL_BUCKETS - max_exact)).astype(jnp.int32)
    return jnp.where(n < max_exact, n, jnp.minimum(large, REL_BUCKETS - 1))


def _bias_codes():
    qi = jnp.arange(BLOCK)[:, None]
    ki = jnp.arange(2 * BLOCK)[None, :]
    step = BLOCK + qi - ki
    has_prev = ki >= BLOCK
    codes = []
    for kind in range(N_KINDS):
        if kind < len(A_GROUPS):
            dil = A_GROUPS[kind][1]
            band = (step >= 0) & (step <= BLOCK)
        else:
            dil = 1
            band = (step >= 0) & (step < B_WINDOW)
        bucket = _t5_bucket(step * dil)
        codes.append(jnp.stack([jnp.where(band & has_prev, bucket, -1),
                                jnp.where(band, bucket, -1)]))
    return jnp.stack(codes).astype(jnp.int32)


def _bias_kernel(rel_ref, code_ref, o_ref):
    kind = pl.program_id(0)
    code = code_ref[0, 0]
    for slot in range(A_HEADS):
        b_head = len(A_GROUPS) * A_HEADS + slot // 2 + 4 * (slot % 2)
        col = jnp.where(kind < len(A_GROUPS), kind * A_HEADS + slot, b_head)
        acc = jnp.full(code.shape, NEG, F32)
        for b in range(REL_BUCKETS):
            acc = jnp.where(code == b, rel_ref[b, col] * LOG2E, acc)
        o_ref[0, 0, slot] = acc


def _bias_tables(rel_table):
    codes = _bias_codes()
    return pl.pallas_call(
        _bias_kernel,
        grid=(N_KINDS, 2),
        in_specs=[pl.BlockSpec(memory_space=pltpu.SMEM),
                  pl.BlockSpec((1, 1, BLOCK, 2 * BLOCK), lambda k, v: (k, v, 0, 0))],
        out_specs=pl.BlockSpec((1, 1, A_HEADS, BLOCK, 2 * BLOCK), lambda k, v: (k, v, 0, 0, 0)),
        out_shape=jax.ShapeDtypeStruct((N_KINDS, 2, A_HEADS, BLOCK, 2 * BLOCK), F32),
        compiler_params=_cparams(2),
        name="bias_tables",
    )(rel_table, codes)


BAND_QB = 32
STAT_LANES = LANE // A_HEADS // 2


def _band_kernel(sink_ref, q_ref, kp_ref, kc_ref, vp_ref, vc_ref, bias_ref, o_ref, *lse_refs,
                 shared_kv, with_sink):
    first = pl.program_id(2) == 0
    lane = lax.broadcasted_iota(jnp.int32, (BLOCK, LANE), 1)
    low = lane < HEAD_DIM
    n_qb, n_res = q_ref.shape[:2]
    for res, qb in ((res, qb) for res in range(n_res) for qb in range(n_qb)):
        variant = jnp.where(first, 0, 1) if qb == 0 else 1
        lse_tile = jnp.zeros((BLOCK, LANE), F32)
        for pair in range(4):
            cols = slice(pair * LANE, (pair + 1) * LANE)
            kv_cols = slice(0, LANE) if shared_kv else cols
            qp = q_ref[qb, res, :, cols]
            k_prev = kp_ref[res, :, kv_cols] if qb == 0 else kc_ref[qb - 1, res, :, kv_cols]
            v_prev = vp_ref[res, :, kv_cols] if qb == 0 else vc_ref[qb - 1, res, :, kv_cols]
            kcat = jnp.concatenate([k_prev, kc_ref[qb, res, :, kv_cols]], axis=0)
            vcat = jnp.concatenate([v_prev, vc_ref[qb, res, :, kv_cols]], axis=0)
            zero = jnp.zeros_like(qp)
            q2 = jnp.concatenate([jnp.where(low, qp, zero), jnp.where(low, zero, qp)], axis=0)
            s = lax.dot_general(q2, kcat, (((1,), (1,)), ((), ())), preferred_element_type=F32)
            s = s + bias_ref[variant, pair]
            m = jnp.max(s, axis=-1, keepdims=True)
            if with_sink:
                head_a = lax.broadcasted_iota(jnp.int32, (2 * BLOCK, 1), 0) < BLOCK
                sink = jnp.where(head_a, sink_ref[2 * pair], sink_ref[2 * pair + 1])
                m = jnp.maximum(m, sink)
            p = jnp.exp2(s - m)
            l = jnp.sum(p, axis=-1, keepdims=True)
            if with_sink:
                l = l + jnp.exp2(sink - m)
            o = jnp.dot(p.astype(BF16), vcat, preferred_element_type=F32)
            if lse_refs:
                for sub in range(2):
                    rows = slice(sub * BLOCK, (sub + 1) * BLOCK)
                    field = 2 * (2 * pair + sub)
                    lse_tile = jnp.where(lane // STAT_LANES == field, m[rows],
                                         jnp.where(lane // STAT_LANES == field + 1, l[rows], lse_tile))
            else:
                o = o * (1.0 / l)
            o_ref[qb, res, :, cols] = jnp.where(low, o[:BLOCK], o[BLOCK:]).astype(o_ref.dtype)
        if lse_refs:
            lse_refs[0][qb, res] = lse_tile


def _band_attention(p2, bias, sinks, *, kind, batch, seq):
    is_b = kind == len(A_GROUPS)
    dil = 1 if is_b else A_GROUPS[kind][1]
    n_span = seq // (BLOCK * dil)
    view = p2.reshape(batch, n_span, dil, BLOCK, p2.shape[-1])
    if is_b:
        q_blk, k_blk, v_blk, kv_w = COL_BQ // 512, COL_BKV // LANE, COL_BKV // LANE + 1, LANE
    else:
        q_blk = COL_A // 512 if dil == 1 else 0
        k_blk, v_blk, kv_w = q_blk + 1, q_blk + 2, 512

    n_qb = min(BAND_QB, n_span)
    n_res = BAND_QB // n_qb

    def cur(blk, width):
        return pl.BlockSpec((None, n_qb, n_res, BLOCK, width), lambda b, r, n: (b, n, r, 0, blk))

    def prev(blk, width):
        return pl.BlockSpec((None, None, n_res, BLOCK, width),
                            lambda b, r, n: (b, jnp.maximum(n * n_qb - 1, 0), r, 0, blk))

    in_specs = [
        pl.BlockSpec(memory_space=pltpu.SMEM),
        cur(q_blk, 512), prev(k_blk, kv_w), cur(k_blk, kv_w), prev(v_blk, kv_w), cur(v_blk, kv_w),
        pl.BlockSpec((None, 2, A_HEADS // 2, 2 * BLOCK, 2 * BLOCK), lambda b, r, n: (kind, 0, 0, 0, 0)),
    ]
    bias = bias.reshape(N_KINDS, 2, A_HEADS // 2, 2 * BLOCK, 2 * BLOCK)
    out_specs = [pl.BlockSpec((None, n_qb, n_res, BLOCK, 512), lambda b, r, n: (b, n, r, 0, 0))]
    out_shape = [jax.ShapeDtypeStruct((batch, n_span, dil, BLOCK, 512), BF16)]
    if not is_b:
        out_specs.append(pl.BlockSpec((None, n_qb, n_res, BLOCK, LANE), lambda b, r, n: (b, n, r, 0, 0)))
        out_shape.append(jax.ShapeDtypeStruct((batch, n_span, dil, BLOCK, LANE), F32))
    outs = pl.pallas_call(
        functools.partial(_band_kernel, shared_kv=is_b, with_sink=is_b),
        grid=(batch, dil // n_res, n_span // n_qb),
        in_specs=in_specs,
        out_specs=out_specs,
        out_shape=out_shape,
        compiler_params=_cparams(3),
        name="band_b" if is_b else f"band_a{kind}",
    )(sinks, view, view, view, view, view, bias)
    return outs[0] if is_b else outs


def _mla_prep_kernel(c_ref, cos_ref, sin_ref, gq_ref, gkv_ref, wq1_ref, wq2_ref, wk_ref, wv_ref,
                     sel_ref, q_ref, k_ref, v_ref):
    c = c_ref[...]
    cq = c[:, :C_Q_RANK].astype(F32)
    ckv = c[:, C_Q_RANK:C_Q_RANK + C_KV_RANK].astype(F32)
    nq = cq * lax.rsqrt(jnp.mean(cq * cq, axis=-1, keepdims=True) + RMS_EPS) * gq_ref[...]
    nkv = ckv * lax.rsqrt(jnp.mean(ckv * ckv, axis=-1, keepdims=True) + RMS_EPS) * gkv_ref[...]
    nq = nq.astype(BF16)
    nkv = nkv.astype(BF16)
    cos = cos_ref[...]
    sin = sin_ref[...]
    cos8 = jnp.tile(cos, (1, C_HEADS))
    sin8 = jnp.tile(sin, (1, C_HEADS))
    q = (jnp.dot(nq, wq1_ref[...], preferred_element_type=F32) * cos8
         + jnp.dot(nq, wq2_ref[...], preferred_element_type=F32) * sin8)
    q_ref[...] = q.astype(q_ref.dtype)
    picked = jnp.dot(c, sel_ref[...], preferred_element_type=F32)
    kr = picked[:, :LANE] * cos + picked[:, LANE:] * sin
    k = jnp.dot(nkv, wk_ref[...], preferred_element_type=F32) + jnp.tile(kr, (1, C_HEADS))
    k_ref[...] = k.astype(k_ref.dtype)
    lane = lax.broadcasted_iota(jnp.int32, (1, C_HEADS * LANE), 1)
    ones_col = jnp.where(lane % LANE == C_V, 1.0, 0.0).astype(F32)
    v_ref[...] = (jnp.dot(nkv, wv_ref[...], preferred_element_type=F32) + ones_col).astype(v_ref.dtype)


def _mla_prep(p2, cos_t, sin_t, sel, lw, layer, *, seq, tm=1024):
    t = p2.shape[0]
    per_seq = seq // tm
    full = lambda a: pl.BlockSpec(a.shape, lambda i: (0,) * a.ndim)
    stacked = (lw["gq"], lw["gkv"], lw["wq1"], lw["wq2"], lw["wk"], lw["wv"])
    wide = C_HEADS * LANE
    return pl.pallas_call(
        _mla_prep_kernel,
        grid=(t // tm,),
        in_specs=[pl.BlockSpec((tm, 512), lambda i: (i, COL_C // 512)),
                  pl.BlockSpec((tm, LANE), lambda i: (i % per_seq, 0)),
                  pl.BlockSpec((tm, LANE), lambda i: (i % per_seq, 0))]
                 + [_layer_block(a, layer) for a in stacked] + [full(sel)],
        out_specs=[pl.BlockSpec((tm, wide), lambda i: (i, 0))] * 3,
        out_shape=[jax.ShapeDtypeStruct((t, wide), BF16)] * 3,
        compiler_params=_cparams(1),
        name="mla_prep",
    )(p2, cos_t, sin_t, *stacked, sel)


def _mla_flash_kernel(qi_ref, ki_ref, q_ref, k_ref, v_ref, o_ref, m_ref, acc_ref, *, tq, tk, heads):
    step = pl.program_id(2)
    qi = qi_ref[step]
    ki = ki_ref[step]
    ratio = tq // tk

    @pl.when(ki == 0)
    def _():
        m_ref[...] = jnp.full(m_ref.shape, NEG, F32)
        acc_ref[...] = jnp.zeros(acc_ref.shape, F32)

    def update(h, rows, n_keys, shift):
        hc = slice(h * LANE, (h + 1) * LANE)
        s = lax.dot_general(q_ref[rows, hc], k_ref[:n_keys, hc], (((1,), (1,)), ((), ())),
                            preferred_element_type=F32)
        if shift is not None:
            row = lax.broadcasted_iota(jnp.int32, s.shape, 0) + shift
            col = lax.broadcasted_iota(jnp.int32, s.shape, 1)
            s = jnp.where(row >= col, s, NEG)
        m_prev = m_ref[h, rows]
        m_new = jnp.maximum(m_prev, jnp.max(s, axis=-1, keepdims=True))
        alpha = jnp.exp2(m_prev - m_new)
        p = jnp.exp2(s - jnp.tile(m_new, (1, n_keys // LANE)))
        acc_ref[h, rows] = alpha * acc_ref[h, rows] + jnp.dot(p.astype(BF16), v_ref[:n_keys, hc],
                                                              preferred_element_type=F32)
        m_ref[h, rows] = m_new

    @pl.when(ki < qi * ratio)
    def _():
        for h in range(heads):
            update(h, slice(0, tq), tk, None)

    for d in range(ratio):
        @pl.when(ki == qi * ratio + d)
        def _(d=d):
            for h in range(heads):
                if (d + 1) * tk < tq:
                    update(h, slice((d + 1) * tk, tq), tk, None)
                for r0 in range(0, tk, FLASH_DIAG):
                    update(h, slice(d * tk + r0, d * tk + r0 + FLASH_DIAG), r0 + FLASH_DIAG, r0)

    @pl.when(ki == (qi + 1) * ratio - 1)
    def _():
        low = lax.broadcasted_iota(jnp.int32, (tq, LANE), 1) < C_V
        for pair in range(heads // 2):
            halves = []
            for h in (2 * pair, 2 * pair + 1):
                acc = acc_ref[h]
                halves.append(acc * (1.0 / acc[:, C_V:C_V + 1]))
            odd = pltpu.roll(halves[1], C_V, axis=1)
            o_ref[:, pair * LANE:(pair + 1) * LANE] = jnp.where(low, halves[0], odd).astype(o_ref.dtype)


def _mla_flash(q, k, v, *, batch, seq, tq=FLASH_TQ, tk=FLASH_TK, heads=FLASH_HEADS):
    wide = C_HEADS * LANE
    ratio = tq // tk
    q3 = q.reshape(batch, seq, wide)
    k3 = k.reshape(batch, seq, wide)
    v3 = v.reshape(batch, seq, wide)
    pairs = [(i, j) for i in range(seq // tq) for j in range((i + 1) * ratio)]
    qi_of = jnp.asarray([i for i, _ in pairs], jnp.int32)
    ki_of = jnp.asarray([j for _, j in pairs], jnp.int32)
    q_map = lambda b, g, t, qi, ki: (b, qi[t], g)
    kv_map = lambda b, g, t, qi, ki: (b, ki[t], g)
    out = pl.pallas_call(
        functools.partial(_mla_flash_kernel, tq=tq, tk=tk, heads=heads),
        grid_spec=pltpu.PrefetchScalarGridSpec(
            num_scalar_prefetch=2,
            grid=(batch, C_HEADS // heads, len(pairs)),
            in_specs=[pl.BlockSpec((None, tq, heads * LANE), q_map),
                      pl.BlockSpec((None, tk, heads * LANE), kv_map),
                      pl.BlockSpec((None, tk, heads * LANE), kv_map)],
            out_specs=pl.BlockSpec((None, tq, heads * C_V), q_map),
            scratch_shapes=[pltpu.VMEM((heads, tq, LANE), F32),
                            pltpu.VMEM((heads, tq, LANE), F32)]),
        out_shape=jax.ShapeDtypeStruct((batch, seq, C_HEADS * C_V), BF16),
        compiler_params=_cparams(3),
        name="mla_flash",
    )(qi_of, ki_of, q3, k3, v3)
    return out.reshape(batch * seq, C_HEADS * C_V)


def _layer_norm(y, g, b):
    mu = jnp.mean(y, axis=-1, keepdims=True)
    d = y - mu
    var = jnp.mean(d * d, axis=-1, keepdims=True)
    return d * lax.rsqrt(var + LN_EPS) * g + b


def _token_order(src_ref, tmp_ref, dil):
    if dil == 1:
        return src_ref[...].astype(F32)
    n = src_ref.shape[1]
    n_slab = src_ref.shape[2] // LANE
    for r in range(dil):
        rows = src_ref[r].astype(F32)
        for c in range(n_slab):
            tmp_ref[c, pl.ds(r, n, stride=dil), :] = rows[:, c * LANE:(c + 1) * LANE]
    return jnp.concatenate([tmp_ref[c] for c in range(n_slab)], axis=1)


GATE_RING = 3


def _merge_kernel(oa0_ref, oa1_ref, oa2_ref, l0_ref, l1_ref, l2_ref, ob_ref, oc_ref, gate_hbm, x_ref,
                  e_ref, wb_ref, wo_ref, bg_ref, g_ref, b_ref, o_ref, ot1_ref, ot2_ref, lt1_ref, lt2_ref,
                  gate_buf, gate_sem, *, alpha):
    step = pl.program_id(0)
    n_steps = pl.num_programs(0)
    tm = o_ref.shape[0]

    def gate_copy(s):
        slot = s % GATE_RING
        return pltpu.make_async_copy(gate_hbm.at[pl.ds(s * tm, tm), pl.ds(COL_GATE, 3 * D_MODEL)],
                                     gate_buf.at[slot], gate_sem.at[slot])

    @pl.when(step == 0)
    def _():
        for s in range(GATE_RING - 1):
            gate_copy(s).start()

    @pl.when(step + GATE_RING - 1 < n_steps)
    def _():
        gate_copy(step + GATE_RING - 1).start()

    dils = [d for _, d in A_GROUPS]
    stats = [_token_order(ref, tmp, d)
             for ref, tmp, d in zip((l0_ref, l1_ref, l2_ref), (None, lt1_ref, lt2_ref), dils)]
    is_max = lax.broadcasted_iota(jnp.int32, stats[0].shape, 1) // STAT_LANES % 2 == 0
    top = jnp.maximum(jnp.maximum(stats[0], stats[1]), stats[2])
    es = [jnp.exp2(v - top) for v in stats]
    sums = [pltpu.roll(v, LANE - STAT_LANES, axis=1) for v in stats]
    inv = 1.0 / (es[0] * sums[0] + es[1] * sums[1] + es[2] * sums[2])
    o_a = None
    for e, oa_ref, tmp, d in zip(es, (oa0_ref, oa1_ref, oa2_ref), (None, ot1_ref, ot2_ref), dils):
        w = jnp.where(is_max, e * inv, 0.0)
        hi = w.astype(BF16)
        lo = (w - hi.astype(F32)).astype(BF16)
        wide = jnp.dot(jnp.concatenate([hi, lo], axis=1), e_ref[...], preferred_element_type=F32)
        term = wide * _token_order(oa_ref, tmp, d)
        o_a = term if o_a is None else o_a + term
    branches = (o_a.astype(BF16), ob_ref[...], oc_ref[...])
    merged = None
    gate_copy(step).wait()
    gate_ref = gate_buf.at[step % GATE_RING]
    for i, br in enumerate(branches):
        gate = jax.nn.sigmoid(gate_ref[:, i * D_MODEL:(i + 1) * D_MODEL].astype(F32)
                              + bg_ref[:, i * D_MODEL:(i + 1) * D_MODEL])
        term = gate * jnp.dot(br, wb_ref[i], preferred_element_type=F32)
        merged = term if merged is None else merged + term
    mix = jnp.dot(merged.astype(BF16), wo_ref[...], preferred_element_type=F32)
    o_ref[...] = _layer_norm(alpha * x_ref[...] + mix, g_ref[...], b_ref[...])


def _merge(oa, lse, ob, oc, p2, x2d, expand, lw, layer, *, alpha, tm=512):
    t = x2d.shape[0]
    row = lambda w: pl.BlockSpec((tm, w), lambda i: (i, 0))
    full = lambda a: pl.BlockSpec(a.shape, lambda i: (0,) * a.ndim)

    def grouped(arrs):
        views, specs = [], []
        for a, (_, dil) in zip(arrs, A_GROUPS):
            w = a.shape[-1]
            if dil == 1:
                views.append(a.reshape(t, w))
                specs.append(row(w))
                continue
            per_span = BLOCK * dil // tm
            views.append(a.reshape(-1, dil, BLOCK, w))
            specs.append(pl.BlockSpec((None, dil, tm // dil, w),
                                      lambda i, per_span=per_span: (i // per_span, 0, i % per_span, 0)))
        return views, specs

    oa_v, oa_s = grouped(oa)
    lse_v, lse_s = grouped(lse)
    stacked = (lw["wb"], lw["wo"], lw["bg"], lw["ln1_g"], lw["ln1_b"])
    slabs = lambda w: pltpu.VMEM((w // LANE, tm, LANE), F32)
    return pl.pallas_call(
        functools.partial(_merge_kernel, alpha=alpha),
        grid=(t // tm,),
        in_specs=oa_s + lse_s + [row(512), row(512), pl.BlockSpec(memory_space=pl.ANY),
                  row(D_MODEL), full(expand)] + [_layer_block(a, layer) for a in stacked],
        out_specs=row(D_MODEL),
        out_shape=jax.ShapeDtypeStruct((t, D_MODEL), F32),
        scratch_shapes=[slabs(512), slabs(512), slabs(LANE), slabs(LANE),
                        pltpu.VMEM((GATE_RING, tm, 3 * D_MODEL), BF16),
                        pltpu.SemaphoreType.DMA((GATE_RING,))],
        compiler_params=_cparams(1),
        name="merge",
    )(*oa_v, *lse_v, ob.reshape(t, 512), oc, p2, x2d, expand, *stacked)


HALO = 8
FF_CHUNK = 1024


def _ffn_kernel(halo_ref, *refs, alpha, tm, per_seq):
    n_slab = D_MODEL // LANE
    x_refs = refs[:n_slab]
    wup_ref, cw_ref, cb_ref, wdn_ref, g_ref, b_ref, o_ref, xs_ref, os_ref = refs[n_slab:]
    rows = tm + HALO
    groups = rows // SUBLANES
    i = pl.program_id(0)
    halo = jnp.where(i % per_seq == 0, jnp.zeros_like(halo_ref[...]), halo_ref[...])
    for c, x_ref in enumerate(x_refs):
        xs_ref[c, :HALO, :] = halo[:, c * LANE:(c + 1) * LANE]
        xs_ref[c, HALO:, :] = x_ref[...]
    xp = jnp.concatenate(
        [jnp.concatenate([xs_ref[c, pl.ds(j, SUBLANES, stride=groups), :] for j in range(groups)], axis=0)
         for c in range(n_slab)], axis=1)
    xh = xp.astype(BF16)
    chunks = [(c0, min(FF_CHUNK, D_FF - c0)) for c0 in range(0, D_FF, FF_CHUNK)]

    def up(c0, cf):
        return [jnp.dot(xh, wup_ref[:, base:base + cf], preferred_element_type=F32)
                for base in (c0, D_FF + c0)]

    def conv(u, base, cf):
        u3 = u.reshape(groups, SUBLANES, cf)
        wrap = pltpu.roll(u3[groups - 2:], 1, axis=1)
        prev1 = jnp.concatenate([wrap[1:], u3[:groups - 1]], axis=0)
        prev2 = jnp.concatenate([wrap, u3[:groups - 2]], axis=0)
        tap = lambda k: cw_ref[k:k + 1, base:base + cf]
        return cb_ref[:, base:base + cf] + prev2 * tap(0) + prev1 * tap(1) + u3 * tap(2)

    acc = None
    u_next = up(*chunks[0])
    for ci, (c0, cf) in enumerate(chunks):
        u_cur = u_next
        if ci + 1 < len(chunks):
            u_next = up(*chunks[ci + 1])
        gate, val = (conv(u, base, cf) for u, base in zip(u_cur, (c0, D_FF + c0)))
        act = (gate * jax.nn.sigmoid(gate) * val).reshape(rows, cf).astype(BF16)
        term = jnp.dot(act, wdn_ref[c0:c0 + cf, :], preferred_element_type=F32)
        acc = term if acc is None else acc + term
    out = _layer_norm(alpha * xp + acc, g_ref[...], b_ref[...])
    for c in range(n_slab):
        for j in range(groups):
            os_ref[c, pl.ds(j, SUBLANES, stride=groups), :] = out[j * SUBLANES:(j + 1) * SUBLANES,
                                                                  c * LANE:(c + 1) * LANE]
        o_ref[:, c * LANE:(c + 1) * LANE] = os_ref[c, HALO:, :]


def _ffn(x2d, lw, layer, *, alpha, seq, tm=512):
    t = x2d.shape[0]
    per_seq = seq // tm
    n_slab = D_MODEL // LANE
    consts = (lw["wup"], lw["cw"], lw["cb"], lw["wdn"], lw["ln2_g"], lw["ln2_b"])
    return pl.pallas_call(
        functools.partial(_ffn_kernel, alpha=alpha, tm=tm, per_seq=per_seq),
        grid=(t // tm,),
        in_specs=[pl.BlockSpec((HALO, D_MODEL), lambda i: (jnp.maximum(i * (tm // HALO) - 1, 0), 0))]
                 + [pl.BlockSpec((tm, LANE), lambda i, c=c: (i, c)) for c in range(n_slab)]
                 + [_layer_block(a, layer, pipeline_mode=pl.Buffered(1)) for a in consts],
        out_specs=pl.BlockSpec((tm, D_MODEL), lambda i: (i, 0)),
        out_shape=jax.ShapeDtypeStruct((t, D_MODEL), F32),
        scratch_shapes=[pltpu.VMEM((n_slab, tm + HALO, LANE), F32),
                        pltpu.VMEM((n_slab, tm + HALO, LANE), F32)],
        compiler_params=_cparams(1),
        name="ffn",
    )(x2d, *([x2d] * n_slab), *consts)


def _prepare_weights(w_in, b_gate, q_norm_g, kv_norm_g, w_uq, w_ukv, w_branch, w_out,
                     ln1_g, ln1_b, w_ffn_up, conv_w, conv_b, w_ffn_down, ln2_g, ln2_b):
    depth = w_in.shape[0]
    heads_w = A_HEADS * HEAD_DIM
    a_end = len(A_GROUPS) * A_COLS
    bq_end = a_end + B_Q_HEADS * HEAD_DIM
    bkv_end = bq_end + 2 * B_KV_HEADS * HEAD_DIM
    cdq_end = bkv_end + C_Q_RANK
    ckv_end = cdq_end + C_KV_RANK
    kr_end = ckv_end + C_ROPE
    col = np.arange(w_in.shape[-1])
    is_q = ((col < a_end) & (col % A_COLS < heads_w)) | ((col >= a_end) & (col < bq_end))
    col_scale = jnp.asarray(np.where(is_q, HEAD_DIM ** -0.5 * LOG2E, 1.0), F32)
    wi = (w_in * col_scale).astype(BF16)
    bq = wi[:, :, a_end:bq_end].reshape(depth, D_MODEL, B_Q_HEADS, HEAD_DIM)
    bq = jnp.stack([bq[:, :, :4], bq[:, :, 4:]], axis=3).reshape(depth, D_MODEL, heads_w)
    bkv = wi[:, :, bq_end:bkv_end]
    cdq = wi[:, :, bkv_end:cdq_end]
    ckv = wi[:, :, cdq_end:ckv_end]
    kr = wi[:, :, ckv_end:kr_end]
    kr_rot = jnp.concatenate([-kr[:, :, C_ROPE // 2:], kr[:, :, :C_ROPE // 2]], axis=2)
    gate = wi[:, :, kr_end:]
    zeros = lambda n: jnp.zeros((depth, D_MODEL, n), wi.dtype)
    rest = jnp.concatenate([wi[:, :, :A_COLS], bq, cdq, ckv, kr, kr_rot, zeros(64), bkv, zeros(256)], axis=2)
    wp = jnp.stack([gate, rest], axis=1)
    wa = []
    for g, (_, dil) in enumerate(A_GROUPS[1:], start=1):
        tn = A_COLS * 1024 // max(1024, BLOCK * dil)
        w = wi[:, :, g * A_COLS:(g + 1) * A_COLS]
        wa.append(w.reshape(depth, D_MODEL, A_COLS // tn, tn).transpose(0, 2, 1, 3))

    uq = (w_uq * ((C_NOPE + C_ROPE) ** -0.5 * LOG2E)).reshape(depth, C_Q_RANK, C_HEADS, C_NOPE + C_ROPE)
    zq = lambda n: jnp.zeros((depth, C_Q_RANK, C_HEADS, n), uq.dtype)
    half = C_ROPE // 2
    wq1 = jnp.concatenate([uq, zq(LANE - C_NOPE - C_ROPE)], axis=3)
    wq2 = jnp.concatenate([zq(C_NOPE), -uq[..., C_NOPE + half:], uq[..., C_NOPE:C_NOPE + half],
                           zq(LANE - C_NOPE - C_ROPE)], axis=3)
    ukv = w_ukv.reshape(depth, C_KV_RANK, C_HEADS, C_NOPE + C_V)
    zkv = lambda n: jnp.zeros((depth, C_KV_RANK, C_HEADS, n), ukv.dtype)
    wk = jnp.concatenate([ukv[..., :C_NOPE], zkv(LANE - C_NOPE)], axis=3)
    wv = jnp.concatenate([ukv[..., C_NOPE:], zkv(LANE - C_V)], axis=3)

    wb1 = w_branch[:, 1].reshape(depth, B_Q_HEADS, HEAD_DIM, D_MODEL)
    wb1 = jnp.stack([wb1[:, :4], wb1[:, 4:]], axis=2).reshape(depth, 512, D_MODEL)
    wb = jnp.stack([w_branch[:, 0], wb1, w_branch[:, 2]], axis=1)

    row = lambda a: a.reshape(depth, 1, -1)
    return dict(
        wp=wp, wa=wa,
        gq=row(q_norm_g), gkv=row(kv_norm_g),
        wq1=wq1.reshape(depth, C_Q_RANK, -1).astype(BF16), wq2=wq2.reshape(depth, C_Q_RANK, -1).astype(BF16),
        wk=wk.reshape(depth, C_KV_RANK, -1).astype(BF16), wv=wv.reshape(depth, C_KV_RANK, -1).astype(BF16),
        wb=wb.astype(BF16), wo=w_out.astype(BF16), bg=row(b_gate),
        ln1_g=row(ln1_g), ln1_b=row(ln1_b),
        wup=w_ffn_up.astype(BF16), cw=conv_w, cb=row(conv_b), wdn=w_ffn_down.astype(BF16),
        ln2_g=row(ln2_g), ln2_b=row(ln2_b),
    )


def _rope_selectors():
    sel = np.zeros((512, 2 * LANE), np.float32)
    for j in range(C_ROPE):
        sel[C_Q_RANK + C_KV_RANK + j, C_NOPE + j] = 1.0
        sel[C_Q_RANK + C_KV_RANK + C_ROPE + j, LANE + C_NOPE + j] = 1.0
    return jnp.asarray(sel, BF16)


def _rope_tables(seq):
    pos = jnp.arange(seq, dtype=F32)
    inv_freq = ROPE_BASE ** (-jnp.arange(0, C_ROPE, 2, dtype=F32) / C_ROPE)
    ang = pos[:, None] * inv_freq[None, :]
    cos, sin = jnp.cos(ang), jnp.sin(ang)
    pad = jnp.zeros((seq, LANE - C_NOPE - C_ROPE), F32)
    cos_t = jnp.concatenate([jnp.ones((seq, C_NOPE), F32), cos, cos, pad], axis=1)
    sin_t = jnp.concatenate([jnp.zeros((seq, C_NOPE), F32), sin, sin, pad], axis=1)
    return cos_t, sin_t


def _expand_matrix():
    e = np.zeros((LANE, A_HEADS * HEAD_DIM), np.float32)
    for c in range(A_HEADS * HEAD_DIM):
        e[(LANE // A_HEADS) * (c // HEAD_DIM), c] = 1.0
    return jnp.asarray(np.concatenate([e, e], axis=0), BF16)


def kernel(x, rel_table, w_in, b_gate, sinks, q_norm_g, kv_norm_g, w_uq, w_ukv, w_branch, w_out,
           ln1_g, ln1_b, w_ffn_up, conv_w, conv_b, w_ffn_down, ln2_g, ln2_b):
    batch, seq, d = x.shape
    depth = w_in.shape[0]
    alpha = (2 * depth) ** 0.25
    cos_t, sin_t = _rope_tables(seq)
    sel = _rope_selectors()
    expand = _expand_matrix()
    bias = _bias_tables(rel_table)
    lw = _prepare_weights(w_in, b_gate, q_norm_g, kv_norm_g, w_uq, w_ukv, w_branch, w_out,
                          ln1_g, ln1_b, w_ffn_up, conv_w, conv_b, w_ffn_down, ln2_g, ln2_b)
    x2d = x.reshape(batch * seq, d)
    for l in range(depth):
        p2 = _proj(x2d, lw["wp"], l, dil=1, tm=1024, name="proj")
        p_dil = [_proj(x2d, w, l, dil=dil, tm=max(1024, BLOCK * dil), name=f"proj_a{g + 1}")
                 for g, (w, (_, dil)) in enumerate(zip(lw["wa"], A_GROUPS[1:]))]
        sink_slots = sinks[l].reshape(2, 4).T.reshape(-1) * LOG2E
        oa, lse = [], []
        for kind in range(len(A_GROUPS)):
            src = p2 if kind == 0 else p_dil[kind - 1]
            o_g, lse_g = _band_attention(src, bias, sink_slots, kind=kind, batch=batch, seq=seq)
            oa.append(o_g)
            lse.append(lse_g)
        ob = _band_attention(p2, bias, sink_slots, kind=len(A_GROUPS), batch=batch, seq=seq)
        q, k, v = _mla_prep(p2, cos_t, sin_t, sel, lw, l, seq=seq)
        oc = _mla_flash(q, k, v, batch=batch, seq=seq)
        x2d = _merge(oa, lse, ob, oc, p2, x2d, expand, lw, l, alpha=alpha)
        x2d = _ffn(x2d, lw, l, alpha=alpha, seq=seq)
    return x2d.reshape(batch, seq, d)
```
